```python
import math
import jax
import jax.numpy as jnp
from jax import lax
import numpy as np

D_MODEL = 2048
BATCH = 8
SEQ = 2048
DEPTH = 2

CHUNK = 64
N_MEM = 256
N_A_LAYERS = DEPTH // 2
N_B_LAYERS = DEPTH - N_A_LAYERS
MIX_WIDTH = D_MODEL
MAIN_WIDTH = 3 * MIX_WIDTH // 4
MEM_WIDTH = MIX_WIDTH - MAIN_WIDTH
HEAD_DIM = 128
SSM_GROUP = 16
SSM_GROUPS = MAIN_WIDTH // SSM_GROUP
SSM_STATE = 64
FOX_HEADS = MAIN_WIDTH // HEAD_DIM
MEM_HEADS = 4
MEM_HEAD_DIM = MEM_WIDTH // MEM_HEADS
Q_BLOCK = 128
IN_WIDTH = 2 * MAIN_WIDTH + 2 * MEM_WIDTH
EPS = 1e-6
DT_MIN = 1e-3
DT_MAX = 1e-1

kernel_name = "yoco_s5_fox_memory_hybrid"


def rmsnorm(x, g):
    xf = x.astype(jnp.float32)
    y = xf * lax.rsqrt(jnp.mean(xf * xf, axis=-1, keepdims=True) + EPS) * g.astype(jnp.float32)
    return y.astype(x.dtype)


def _scan_binop(e1, e2):
    a1r, a1i, b1r, b1i = e1
    a2r, a2i, b2r, b2i = e2
    ar = a2r * a1r - a2i * a1i
    ai = a2r * a1i + a2i * a1r
    br = a2r * b1r - a2i * b1i + b2r
    bi = a2r * b1i + a2i * b1r + b2i
    return (ar, ai, br, bi)


def s5_ssm(u, lam_re, lam_im, log_step, b_re, b_im, c_re, c_im, d_skip):
    bsz, seqlen, _ = u.shape
    uf = u.astype(jnp.float32)
    ug = uf.reshape(bsz, seqlen, SSM_GROUPS, SSM_GROUP)
    lr = lam_re.astype(jnp.float32)
    li = lam_im.astype(jnp.float32)
    dt = jnp.exp(log_step.astype(jnp.float32))[:, None]
    mag = jnp.exp(lr * dt)
    ar = mag * jnp.cos(li * dt)
    ai = mag * jnp.sin(li * dt)
    den = lr * lr + li * li
    cr = ((ar - 1.0) * lr + ai * li) / den
    ci = (ai * lr - (ar - 1.0) * li) / den
    br = b_re.astype(jnp.float32)
    bi = b_im.astype(jnp.float32)
    bbar_re = cr[..., None] * br - ci[..., None] * bi
    bbar_im = cr[..., None] * bi + ci[..., None] * br
    bu_re = jnp.einsum('blgh,gph->blgp', ug, bbar_re)
    bu_im = jnp.einsum('blgh,gph->blgp', ug, bbar_im)
    a_re = jnp.broadcast_to(ar[None, None], (1, seqlen, SSM_GROUPS, SSM_STATE))
    a_im = jnp.broadcast_to(ai[None, None], (1, seqlen, SSM_GROUPS, SSM_STATE))
    _, _, x_re, x_im = lax.associative_scan(_scan_binop, (a_re, a_im, bu_re, bu_im), axis=1)
    y = (jnp.einsum('blgp,ghp->blgh', x_re, c_re.astype(jnp.float32))
         - jnp.einsum('blgp,ghp->blgh', x_im, c_im.astype(jnp.float32)))
    y = y.reshape(bsz, seqlen, MAIN_WIDTH) + d_skip.astype(jnp.float32) * uf
    return y.astype(u.dtype)


def memory_attention(qm, mem, mem_g, w_mem_kv):
    bsz, seqlen, _ = qm.shape
    memn = rmsnorm(mem, mem_g)
    kv = memn @ w_mem_kv
    km, vm = jnp.split(kv, 2, axis=-1)
    km = km.reshape(bsz, -1, MEM_HEADS, MEM_HEAD_DIM).astype(jnp.float32)
    vm = vm.reshape(bsz, -1, MEM_HEADS, MEM_HEAD_DIM).astype(jnp.float32)
    q = qm.reshape(bsz, seqlen, MEM_HEADS, MEM_HEAD_DIM).astype(jnp.float32) * (MEM_HEAD_DIM ** -0.5)
    s = jnp.einsum('blhd,bmhd->bhlm', q, km)
    p = jax.nn.softmax(s, axis=-1)
    o = jnp.einsum('bhlm,bmhd->blhd', p, vm)
    return o.reshape(bsz, seqlen, MEM_WIDTH).astype(qm.dtype)


def forgetting_attention(q, k, v, fcum):
    _, seqlen, _, dh = q.shape
    qf = q.astype(jnp.float32) * (dh ** -0.5)
    kf = k.astype(jnp.float32)
    vf = v.astype(jnp.float32)
    outs = []
    for blk in range(seqlen // Q_BLOCK):
        q0 = blk * Q_BLOCK
        q1 = q0 + Q_BLOCK
        s = jnp.einsum('bqhd,bkhd->bhqk', qf[:, q0:q1], kf[:, :q1])
        s = s + fcum[:, :, q0:q1, None] - fcum[:, :, None, :q1]
        causal = jnp.arange(q0, q1)[:, None] >= jnp.arange(q1)[None, :]
        s = jnp.where(causal, s, -jnp.inf)
        p = jax.nn.softmax(s, axis=-1)
        outs.append(jnp.einsum('bhqk,bkhd->bqhd', p, vf[:, :q1]))
    return jnp.concatenate(outs, axis=1).astype(q.dtype)


def _fwd_setup_inputs(seed: int = 0) -> dict:
    key = jax.random.key(seed)
    ks = jax.random.split(key, 32)
    f32 = jnp.float32
    D = D_MODEL
    nrm = lambda k, shape, scale: jax.random.normal(k, shape, f32) * scale
    x = jax.random.normal(ks[0], (BATCH, SEQ, D), f32)
    mem = jax.random.normal(ks[1], (BATCH, N_MEM, D), f32)
    pre_norm_g = 1.0 + nrm(ks[2], (DEPTH, D), 0.02)
    post_norm_g = 1.0 + nrm(ks[3], (DEPTH, D), 0.02)
    w_in_a = nrm(ks[4], (N_A_LAYERS, D, IN_WIDTH), D ** -0.5)
    lam_re = -0.5 + nrm(ks[5], (N_A_LAYERS, SSM_GROUPS, SSM_STATE), 0.01)
    lam_im = (math.pi * jnp.arange(SSM_STATE, dtype=f32))[None, None, :] + nrm(ks[6], (N_A_LAYERS, SSM_GROUPS, SSM_STATE), 0.01)
    log_step = jax.random.uniform(ks[7], (N_A_LAYERS, SSM_GROUPS), f32, math.log(DT_MIN), math.log(DT_MAX))
    b_re = nrm(ks[8], (N_A_LAYERS, SSM_GROUPS, SSM_STATE, SSM_GROUP), (2.0 * SSM_GROUP) ** -0.5)
    b_im = nrm(ks[9], (N_A_LAYERS, SSM_GROUPS, SSM_STATE, SSM_GROUP), (2.0 * SSM_GROUP) ** -0.5)
    c_re = nrm(ks[10], (N_A_LAYERS, SSM_GROUPS, SSM_GROUP, SSM_STATE), (2.0 * SSM_STATE) ** -0.5)
    c_im = nrm(ks[11], (N_A_LAYERS, SSM_GROUPS, SSM_GROUP, SSM_STATE), (2.0 * SSM_STATE) ** -0.5)
    d_skip = nrm(ks[12], (N_A_LAYERS, MAIN_WIDTH), 1.0)
    w_glu = nrm(ks[13], (N_A_LAYERS, MAIN_WIDTH, MAIN_WIDTH), MAIN_WIDTH ** -0.5)
    b_glu = nrm(ks[14], (N_A_LAYERS, MAIN_WIDTH), 0.01)
    kv_norm_g = 1.0 + nrm(ks[15], (D,), 0.02)
    w_kv = nrm(ks[16], (D, 2 * MAIN_WIDTH), D ** -0.5)
    w_fgate = nrm(ks[17], (D, FOX_HEADS), D ** -0.5)
    b_fgate = nrm(ks[18], (FOX_HEADS,), 0.1)
    w_in_b = nrm(ks[19], (N_B_LAYERS, D, IN_WIDTH), D ** -0.5)
    mem_norm_g = 1.0 + nrm(ks[20], (DEPTH, D), 0.02)
    w_mem_kv = nrm(ks[21], (DEPTH, D, 2 * MEM_WIDTH), D ** -0.5)
    w_out = nrm(ks[22], (DEPTH, MIX_WIDTH, D), MIX_WIDTH ** -0.5)
    return {"x": x, "mem": mem, "pre_norm_g": pre_norm_g, "post_norm_g": post_norm_g,
            "w_in_a": w_in_a, "lam_re": lam_re, "lam_im": lam_im, "log_step": log_step,
            "b_re": b_re, "b_im": b_im, "c_re": c_re, "c_im": c_im, "d_skip": d_skip,
            "w_glu": w_glu, "b_glu": b_glu, "kv_norm_g": kv_norm_g, "w_kv": w_kv,
            "w_fgate": w_fgate, "b_fgate": b_fgate, "w_in_b": w_in_b,
            "mem_norm_g": mem_norm_g, "w_mem_kv": w_mem_kv, "w_out": w_out}


def _fwd_reference(x, mem, pre_norm_g, post_norm_g, w_in_a, lam_re, lam_im, log_step,
              b_re, b_im, c_re, c_im, d_skip, w_glu, b_glu, kv_norm_g, w_kv,
              w_fgate, b_fgate, w_in_b, mem_norm_g, w_mem_kv, w_out):
    bsz, seqlen, _ = x.shape
    h = x
    k_sh = v_sh = fcum = None
    split_pts = [MAIN_WIDTH, 2 * MAIN_WIDTH, 2 * MAIN_WIDTH + MEM_WIDTH]
    for i in range(DEPTH):
        hn = rmsnorm(h, pre_norm_g[i])
        if i < N_A_LAYERS:
            la = i
            proj = hn @ w_in_a[la]
            u, z, qm, zm = jnp.split(proj, split_pts, axis=-1)
            y = s5_ssm(u, lam_re[la], lam_im[la], log_step[la], b_re[la], b_im[la],
                       c_re[la], c_im[la], d_skip[la])
            yg = jax.nn.gelu(y)
            y = yg * jax.nn.sigmoid(yg @ w_glu[la] + b_glu[la])
            main = y * jax.nn.silu(z)
        else:
            lb = i - N_A_LAYERS
            proj = hn @ w_in_b[lb]
            q, z, qm, zm = jnp.split(proj, split_pts, axis=-1)
            q = q.reshape(bsz, seqlen, FOX_HEADS, HEAD_DIM)
            att = forgetting_attention(q, k_sh, v_sh, fcum).reshape(bsz, seqlen, MAIN_WIDTH)
            main = att * jax.nn.silu(z)
        memo = memory_attention(qm, mem, mem_norm_g[i], w_mem_kv[i]) * jax.nn.silu(zm)
        o = jnp.concatenate([main, memo], axis=-1) @ w_out[i]
        h = h + rmsnorm(o, post_norm_g[i])
        if i == N_A_LAYERS - 1:
            kv_in = rmsnorm(h, kv_norm_g)
            kv = kv_in @ w_kv
            k_sh, v_sh = jnp.split(kv, 2, axis=-1)
            k_sh = k_sh.reshape(bsz, seqlen, FOX_HEADS, HEAD_DIM)
            v_sh = v_sh.reshape(bsz, seqlen, FOX_HEADS, HEAD_DIM)
            logf = jax.nn.log_sigmoid((kv_in @ w_fgate).astype(jnp.float32) + b_fgate.astype(jnp.float32))
            fcum = jnp.transpose(jnp.cumsum(logf, axis=1), (0, 2, 1))
    return h


import jax as _jax
import jax.numpy as _jnp

TWIN_FORMAT = 'train_step'
FWD_PARAMS = ['x', 'mem', 'pre_norm_g', 'post_norm_g', 'w_in_a', 'lam_re', 'lam_im', 'log_step', 'b_re', 'b_im', 'c_re', 'c_im', 'd_skip', 'w_glu', 'b_glu', 'kv_norm_g', 'w_kv', 'w_fgate', 'b_fgate', 'w_in_b', 'mem_norm_g', 'w_mem_kv', 'w_out']
TWIN_WEIGHTS = ['pre_norm_g', 'post_norm_g', 'w_in_a', 'lam_re', 'lam_im', 'log_step', 'b_re', 'b_im', 'c_re', 'c_im', 'd_skip', 'w_glu', 'b_glu', 'kv_norm_g', 'w_kv', 'w_fgate', 'b_fgate', 'w_in_b', 'mem_norm_g', 'w_mem_kv', 'w_out']
TWIN_DIFF_INPUT = 'x'
TWIN_INPUTS = ['x', 'mem', 'pre_norm_g', 'post_norm_g', 'w_in_a', 'lam_re', 'lam_im', 'log_step', 'b_re', 'b_im', 'c_re', 'c_im', 'd_skip', 'w_glu', 'b_glu', 'kv_norm_g', 'w_kv', 'w_fgate', 'b_fgate', 'w_in_b', 'mem_norm_g', 'w_mem_kv', 'w_out', 'loss_target', 'm_pre_norm_g', 'm_post_norm_g', 'm_w_in_a', 'm_lam_re', 'm_lam_im', 'm_log_step', 'm_b_re', 'm_b_im', 'm_c_re', 'm_c_im', 'm_d_skip', 'm_w_glu', 'm_b_glu', 'm_kv_norm_g', 'm_w_kv', 'm_w_fgate', 'm_b_fgate', 'm_w_in_b', 'm_mem_norm_g', 'm_w_mem_kv', 'm_w_out', 'v_pre_norm_g', 'v_post_norm_g', 'v_w_in_a', 'v_lam_re', 'v_lam_im', 'v_log_step', 'v_b_re', 'v_b_im', 'v_c_re', 'v_c_im', 'v_d_skip', 'v_w_glu', 'v_b_glu', 'v_kv_norm_g', 'v_w_kv', 'v_w_fgate', 'v_b_fgate', 'v_w_in_b', 'v_mem_norm_g', 'v_w_mem_kv', 'v_w_out']
TWIN_OUTPUTS = ['loss', 'grad_x', 'grad_pre_norm_g', 'grad_post_norm_g', 'grad_w_in_a', 'grad_lam_re', 'grad_lam_im', 'grad_log_step', 'grad_b_re', 'grad_b_im', 'grad_c_re', 'grad_c_im', 'grad_d_skip', 'grad_w_glu', 'grad_b_glu', 'grad_kv_norm_g', 'grad_w_kv', 'grad_w_fgate', 'grad_b_fgate', 'grad_w_in_b', 'grad_mem_norm_g', 'grad_w_mem_kv', 'grad_w_out', 'delta_pre_norm_g', 'delta_post_norm_g', 'delta_w_in_a', 'delta_lam_re', 'delta_lam_im', 'delta_log_step', 'delta_b_re', 'delta_b_im', 'delta_c_re', 'delta_c_im', 'delta_d_skip', 'delta_w_glu', 'delta_b_glu', 'delta_kv_norm_g', 'delta_w_kv', 'delta_w_fgate', 'delta_b_fgate', 'delta_w_in_b', 'delta_mem_norm_g', 'delta_w_mem_kv', 'delta_w_out', 'new_m_pre_norm_g', 'new_m_post_norm_g', 'new_m_w_in_a', 'new_m_lam_re', 'new_m_lam_im', 'new_m_log_step', 'new_m_b_re', 'new_m_b_im', 'new_m_c_re', 'new_m_c_im', 'new_m_d_skip', 'new_m_w_glu', 'new_m_b_glu', 'new_m_kv_norm_g', 'new_m_w_kv', 'new_m_w_fgate', 'new_m_b_fgate', 'new_m_w_in_b', 'new_m_mem_norm_g', 'new_m_w_mem_kv', 'new_m_w_out', 'new_v_pre_norm_g', 'new_v_post_norm_g', 'new_v_w_in_a', 'new_v_lam_re', 'new_v_lam_im', 'new_v_log_step', 'new_v_b_re', 'new_v_b_im', 'new_v_c_re', 'new_v_c_im', 'new_v_d_skip', 'new_v_w_glu', 'new_v_b_glu', 'new_v_kv_norm_g', 'new_v_w_kv', 'new_v_w_fgate', 'new_v_b_fgate', 'new_v_w_in_b', 'new_v_mem_norm_g', 'new_v_w_mem_kv', 'new_v_w_out']
TWIN_LEAF_KINDS = {'loss': 'loss', 'grad_x': 'grad_x', 'grad_pre_norm_g': 'grad_w', 'grad_post_norm_g': 'grad_w', 'grad_w_in_a': 'grad_w', 'grad_lam_re': 'grad_w', 'grad_lam_im': 'grad_w', 'grad_log_step': 'grad_w', 'grad_b_re': 'grad_w', 'grad_b_im': 'grad_w', 'grad_c_re': 'grad_w', 'grad_c_im': 'grad_w', 'grad_d_skip': 'grad_w', 'grad_w_glu': 'grad_w', 'grad_b_glu': 'grad_w', 'grad_kv_norm_g': 'grad_w', 'grad_w_kv': 'grad_w', 'grad_w_fgate': 'grad_w', 'grad_b_fgate': 'grad_w', 'grad_w_in_b': 'grad_w', 'grad_mem_norm_g': 'grad_w', 'grad_w_mem_kv': 'grad_w', 'grad_w_out': 'grad_w', 'delta_pre_norm_g': 'delta_w', 'delta_post_norm_g': 'delta_w', 'delta_w_in_a': 'delta_w', 'delta_lam_re': 'delta_w', 'delta_lam_im': 'delta_w', 'delta_log_step': 'delta_w', 'delta_b_re': 'delta_w', 'delta_b_im': 'delta_w', 'delta_c_re': 'delta_w', 'delta_c_im': 'delta_w', 'delta_d_skip': 'delta_w', 'delta_w_glu': 'delta_w', 'delta_b_glu': 'delta_w', 'delta_kv_norm_g': 'delta_w', 'delta_w_kv': 'delta_w', 'delta_w_fgate': 'delta_w', 'delta_b_fgate': 'delta_w', 'delta_w_in_b': 'delta_w', 'delta_mem_norm_g': 'delta_w', 'delta_w_mem_kv': 'delta_w', 'delta_w_out': 'delta_w', 'new_m_pre_norm_g': 'new_m', 'new_m_post_norm_g': 'new_m', 'new_m_w_in_a': 'new_m', 'new_m_lam_re': 'new_m', 'new_m_lam_im': 'new_m', 'new_m_log_step': 'new_m', 'new_m_b_re': 'new_m', 'new_m_b_im': 'new_m', 'new_m_c_re': 'new_m', 'new_m_c_im': 'new_m', 'new_m_d_skip': 'new_m', 'new_m_w_glu': 'new_m', 'new_m_b_glu': 'new_m', 'new_m_kv_norm_g': 'new_m', 'new_m_w_kv': 'new_m', 'new_m_w_fgate': 'new_m', 'new_m_b_fgate': 'new_m', 'new_m_w_in_b': 'new_m', 'new_m_mem_norm_g': 'new_m', 'new_m_w_mem_kv': 'new_m', 'new_m_w_out': 'new_m', 'new_v_pre_norm_g': 'new_v', 'new_v_post_norm_g': 'new_v', 'new_v_w_in_a': 'new_v', 'new_v_lam_re': 'new_v', 'new_v_lam_im': 'new_v', 'new_v_log_step': 'new_v', 'new_v_b_re': 'new_v', 'new_v_b_im': 'new_v', 'new_v_c_re': 'new_v', 'new_v_c_im': 'new_v', 'new_v_d_skip': 'new_v', 'new_v_w_glu': 'new_v', 'new_v_b_glu': 'new_v', 'new_v_kv_norm_g': 'new_v', 'new_v_w_kv': 'new_v', 'new_v_w_fgate': 'new_v', 'new_v_b_fgate': 'new_v', 'new_v_w_in_b': 'new_v', 'new_v_mem_norm_g': 'new_v', 'new_v_w_mem_kv': 'new_v', 'new_v_w_out': 'new_v'}


def _forward(args):
    return _fwd_reference(*[args[k] for k in FWD_PARAMS])


def _output_shape():
    out = _jax.eval_shape(lambda: _forward(_fwd_setup_inputs(0)))
    return out.shape, out.dtype

N_MICROBATCH = 1
ADAM_LR = 0.001
ADAM_B1 = 0.9
ADAM_B2 = 0.999
ADAM_EPS = 1e-08
ADAM_WD = 0.01
ADAM_STEP = 10
PER_EXAMPLE_BATCH_AXIS = {'x': 0, 'mem': 0, 'loss_target': 0}
SHARED_INPUTS = []
_WEIGHT_DTYPES = {'pre_norm_g': _jnp.float32, 'post_norm_g': _jnp.float32, 'w_in_a': _jnp.float32, 'lam_re': _jnp.float32, 'lam_im': _jnp.float32, 'log_step': _jnp.float32, 'b_re': _jnp.float32, 'b_im': _jnp.float32, 'c_re': _jnp.float32, 'c_im': _jnp.float32, 'd_skip': _jnp.float32, 'w_glu': _jnp.float32, 'b_glu': _jnp.float32, 'kv_norm_g': _jnp.float32, 'w_kv': _jnp.float32, 'w_fgate': _jnp.float32, 'b_fgate': _jnp.float32, 'w_in_b': _jnp.float32, 'mem_norm_g': _jnp.float32, 'w_mem_kv': _jnp.float32, 'w_out': _jnp.float32}
MOMENT_SCALE = {'pre_norm_g': 2.023468e-01, 'post_norm_g': 8.002734e+00, 'w_in_a': 1.818031e-01, 'lam_re': 9.923560e-03, 'lam_im': 1.010065e-02, 'log_step': 6.980315e+00, 'b_re': 6.666529e-03, 'b_im': 6.659851e-03, 'c_re': 1.343359e-02, 'c_im': 1.343243e-02, 'd_skip': 2.511086e-01, 'w_glu': 5.754270e-02, 'b_glu': 1.036080e-01, 'kv_norm_g': 1.596912e-01, 'w_kv': 1.250222e-01, 'w_fgate': 3.631864e-01, 'b_fgate': 5.027634e-01, 'w_in_b': 9.530191e-02, 'mem_norm_g': 3.255484e-02, 'w_mem_kv': 4.649893e-02, 'w_out': 1.787929e-01}


def _to_microbatches(a, axis):
    t = _jnp.moveaxis(a, axis, 0)
    t = t.reshape((N_MICROBATCH, t.shape[0] // N_MICROBATCH) + t.shape[1:])
    return _jnp.moveaxis(t, 1, axis + 1)


def setup_inputs(seed: int = 0) -> dict:
    inp = _fwd_setup_inputs(seed)
    key = _jax.random.fold_in(_jax.random.key(seed), 7919)
    shape, _ = _output_shape()
    out = dict(inp)
    out["loss_target"] = _jax.random.normal(_jax.random.fold_in(key, 0), shape, _jnp.float32)
    for i, name in enumerate(TWIN_WEIGHTS):
        w = inp[name].astype(_jnp.float32)
        if MOMENT_SCALE is None:
            s = _jnp.sqrt(_jnp.mean(_jnp.square(w)) + 1e-30)
        else:
            s = MOMENT_SCALE[name]
        km, kv = _jax.random.split(_jax.random.fold_in(key, i + 1))
        out[name] = w
        out["m_" + name] = s * _jax.random.normal(km, w.shape, _jnp.float32)
        out["v_" + name] = (s * s) * _jax.random.uniform(kv, w.shape, _jnp.float32, 0.5, 1.5)
    if N_MICROBATCH > 1:
        for name, axis in PER_EXAMPLE_BATCH_AXIS.items():
            out[name] = _to_microbatches(out[name], axis)
    return {'x': out['x'], 'mem': out['mem'], 'pre_norm_g': out['pre_norm_g'], 'post_norm_g': out['post_norm_g'], 'w_in_a': out['w_in_a'], 'lam_re': out['lam_re'], 'lam_im': out['lam_im'], 'log_step': out['log_step'], 'b_re': out['b_re'], 'b_im': out['b_im'], 'c_re': out['c_re'], 'c_im': out['c_im'], 'd_skip': out['d_skip'], 'w_glu': out['w_glu'], 'b_glu': out['b_glu'], 'kv_norm_g': out['kv_norm_g'], 'w_kv': out['w_kv'], 'w_fgate': out['w_fgate'], 'b_fgate': out['b_fgate'], 'w_in_b': out['w_in_b'], 'mem_norm_g': out['mem_norm_g'], 'w_mem_kv': out['w_mem_kv'], 'w_out': out['w_out'], 'loss_target': out['loss_target'], 'm_pre_norm_g': out['m_pre_norm_g'], 'm_post_norm_g': out['m_post_norm_g'], 'm_w_in_a': out['m_w_in_a'], 'm_lam_re': out['m_lam_re'], 'm_lam_im': out['m_lam_im'], 'm_log_step': out['m_log_step'], 'm_b_re': out['m_b_re'], 'm_b_im': out['m_b_im'], 'm_c_re': out['m_c_re'], 'm_c_im': out['m_c_im'], 'm_d_skip': out['m_d_skip'], 'm_w_glu': out['m_w_glu'], 'm_b_glu': out['m_b_glu'], 'm_kv_norm_g': out['m_kv_norm_g'], 'm_w_kv': out['m_w_kv'], 'm_w_fgate': out['m_w_fgate'], 'm_b_fgate': out['m_b_fgate'], 'm_w_in_b': out['m_w_in_b'], 'm_mem_norm_g': out['m_mem_norm_g'], 'm_w_mem_kv': out['m_w_mem_kv'], 'm_w_out': out['m_w_out'], 'v_pre_norm_g': out['v_pre_norm_g'], 'v_post_norm_g': out['v_post_norm_g'], 'v_w_in_a': out['v_w_in_a'], 'v_lam_re': out['v_lam_re'], 'v_lam_im': out['v_lam_im'], 'v_log_step': out['v_log_step'], 'v_b_re': out['v_b_re'], 'v_b_im': out['v_b_im'], 'v_c_re': out['v_c_re'], 'v_c_im': out['v_c_im'], 'v_d_skip': out['v_d_skip'], 'v_w_glu': out['v_w_glu'], 'v_b_glu': out['v_b_glu'], 'v_kv_norm_g': out['v_kv_norm_g'], 'v_w_kv': out['v_w_kv'], 'v_w_fgate': out['v_w_fgate'], 'v_b_fgate': out['v_b_fgate'], 'v_w_in_b': out['v_w_in_b'], 'v_mem_norm_g': out['v_mem_norm_g'], 'v_w_mem_kv': out['v_w_mem_kv'], 'v_w_out': out['v_w_out']}


def _loss(weights, diff, rest, loss_target):
    with _jax.named_scope("forward"):
        args = {**rest, TWIN_DIFF_INPUT: diff, **{k: w.astype(_WEIGHT_DTYPES[k]) for k, w in weights.items()}}
        y = _forward(args)
    with _jax.named_scope("loss_head"):
        err = _jnp.square(y.astype(_jnp.float32) - loss_target)
        return 0.5 * _jnp.sum(_jnp.mean(err, axis=-1)) if err.ndim else 0.5 * err


def _adamw(w, g, m, v):
    m = ADAM_B1 * m + (1.0 - ADAM_B1) * g
    v = ADAM_B2 * v + (1.0 - ADAM_B2) * _jnp.square(g)
    m_hat = m / (1.0 - ADAM_B1 ** ADAM_STEP)
    v_hat = v / (1.0 - ADAM_B2 ** ADAM_STEP)
    delta = -ADAM_LR * (m_hat / (_jnp.sqrt(v_hat) + ADAM_EPS) + ADAM_WD * w)
    return delta, m, v


def reference(x, mem, pre_norm_g, post_norm_g, w_in_a, lam_re, lam_im, log_step, b_re, b_im, c_re, c_im, d_skip, w_glu, b_glu, kv_norm_g, w_kv, w_fgate, b_fgate, w_in_b, mem_norm_g, w_mem_kv, w_out, loss_target, m_pre_norm_g, m_post_norm_g, m_w_in_a, m_lam_re, m_lam_im, m_log_step, m_b_re, m_b_im, m_c_re, m_c_im, m_d_skip, m_w_glu, m_b_glu, m_kv_norm_g, m_w_kv, m_w_fgate, m_b_fgate, m_w_in_b, m_mem_norm_g, m_w_mem_kv, m_w_out, v_pre_norm_g, v_post_norm_g, v_w_in_a, v_lam_re, v_lam_im, v_log_step, v_b_re, v_b_im, v_c_re, v_c_im, v_d_skip, v_w_glu, v_b_glu, v_kv_norm_g, v_w_kv, v_w_fgate, v_b_fgate, v_w_in_b, v_mem_norm_g, v_w_mem_kv, v_w_out):
    given = dict(x=x, mem=mem, pre_norm_g=pre_norm_g, post_norm_g=post_norm_g, w_in_a=w_in_a, lam_re=lam_re, lam_im=lam_im, log_step=log_step, b_re=b_re, b_im=b_im, c_re=c_re, c_im=c_im, d_skip=d_skip, w_glu=w_glu, b_glu=b_glu, kv_norm_g=kv_norm_g, w_kv=w_kv, w_fgate=w_fgate, b_fgate=b_fgate, w_in_b=w_in_b, mem_norm_g=mem_norm_g, w_mem_kv=w_mem_kv, w_out=w_out, loss_target=loss_target, m_pre_norm_g=m_pre_norm_g, m_post_norm_g=m_post_norm_g, m_w_in_a=m_w_in_a, m_lam_re=m_lam_re, m_lam_im=m_lam_im, m_log_step=m_log_step, m_b_re=m_b_re, m_b_im=m_b_im, m_c_re=m_c_re, m_c_im=m_c_im, m_d_skip=m_d_skip, m_w_glu=m_w_glu, m_b_glu=m_b_glu, m_kv_norm_g=m_kv_norm_g, m_w_kv=m_w_kv, m_w_fgate=m_w_fgate, m_b_fgate=m_b_fgate, m_w_in_b=m_w_in_b, m_mem_norm_g=m_mem_norm_g, m_w_mem_kv=m_w_mem_kv, m_w_out=m_w_out, v_pre_norm_g=v_pre_norm_g, v_post_norm_g=v_post_norm_g, v_w_in_a=v_w_in_a, v_lam_re=v_lam_re, v_lam_im=v_lam_im, v_log_step=v_log_step, v_b_re=v_b_re, v_b_im=v_b_im, v_c_re=v_c_re, v_c_im=v_c_im, v_d_skip=v_d_skip, v_w_glu=v_w_glu, v_b_glu=v_b_glu, v_kv_norm_g=v_kv_norm_g, v_w_kv=v_w_kv, v_w_fgate=v_w_fgate, v_b_fgate=v_b_fgate, v_w_in_b=v_w_in_b, v_mem_norm_g=v_mem_norm_g, v_w_mem_kv=v_w_mem_kv, v_w_out=v_w_out)
    weights = {n: given[n] for n in TWIN_WEIGHTS}
    shared = {n: given[n] for n in SHARED_INPUTS}
    per_example = {n: given[n] for n in ['x', 'mem']}
    grad_fn = _jax.value_and_grad(_loss, argnums=(0, 1))

    def one_microbatch(ex, loss_target):
        ex = dict(ex)
        diff = ex.pop(TWIN_DIFF_INPUT)
        return grad_fn(weights, diff, {**shared, **ex}, loss_target)

    if N_MICROBATCH == 1:
        loss, (grad_w, grad_x) = one_microbatch(per_example, given["loss_target"])
    else:
        def body(carry, xs):
            loss_sum, grad_sum = carry
            l_k, (gw_k, gx_k) = one_microbatch(xs[0], xs[1])
            with _jax.named_scope("update"):
                return (loss_sum + l_k, _jax.tree.map(_jnp.add, grad_sum, gw_k)), gx_k

        init = (_jnp.zeros((), _jnp.float32), _jax.tree.map(_jnp.zeros_like, weights))
        (loss, grad_w), grad_x = _jax.lax.scan(body, init, (per_example, given["loss_target"]))
    with _jax.named_scope("update"):
        delta_w, new_m, new_v = {}, {}, {}
        for n in TWIN_WEIGHTS:
            delta_w[n], new_m[n], new_v[n] = _adamw(weights[n], grad_w[n], given["m_" + n], given["v_" + n])
    return (loss, grad_x, *[grad_w[n] for n in TWIN_WEIGHTS], *[delta_w[n] for n in TWIN_WEIGHTS],
            *[new_m[n] for n in TWIN_WEIGHTS], *[new_v[n] for n in TWIN_WEIGHTS])
```

```python
import functools
import math

import jax
import jax.numpy as jnp
from jax import lax
from jax.experimental import pallas as pl
from jax.experimental.pallas import tpu as pltpu

F32 = jnp.float32
BF16 = jnp.bfloat16
HP = lax.Precision.HIGHEST
MESH_AXES = ("x", "y", "c")
N_DEV = 8
V7X_VMEM_LIMIT = 56 * 1024 * 1024
LANE = 128

EPS = 1e-6
HEAD_DIM = 128
SSM_GROUP = 16
SSM_STATE = 64
SSM_T = 16
SSM_LO = 8
FOX_BQ = 256
ADAM_LR = 0.001
ADAM_B1 = 0.9
ADAM_B2 = 0.999
ADAM_EPS = 1e-08
ADAM_WD = 0.01
ADAM_STEP = 10

WEIGHTS = ['pre_norm_g', 'post_norm_g', 'w_in_a', 'lam_re', 'lam_im', 'log_step', 'b_re', 'b_im', 'c_re', 'c_im', 'd_skip',
           'w_glu', 'b_glu', 'kv_norm_g', 'w_kv', 'w_fgate', 'b_fgate', 'w_in_b', 'mem_norm_g', 'w_mem_kv', 'w_out']
INPUTS = ['x', 'mem'] + WEIGHTS + ['loss_target'] + ['m_' + n for n in WEIGHTS] + ['v_' + n for n in WEIGHTS]
REPLICATED = ['pre_norm_g', 'post_norm_g', 'lam_re', 'lam_im', 'log_step', 'b_re', 'b_im', 'c_re', 'c_im', 'kv_norm_g',
              'b_fgate', 'mem_norm_g']
SMALL_SHARDED = ['d_skip', 'b_glu', 'w_fgate']


def _cparams(sem=None):
    return pltpu.CompilerParams(dimension_semantics=sem, vmem_limit_bytes=V7X_VMEM_LIMIT)


def _tile(n, cap):
    if n <= cap:
        return n
    best = None
    for t in range(LANE, cap + 1, LANE):
        if n % t == 0:
            best = t
    assert best is not None, (n, cap)
    return best


def _matmul(name, a, b, a_spec, b_spec, o_spec, out_shape, grid, dims, nk):
    def body(a_ref, b_ref, o_ref, acc_ref):
        k = pl.program_id(2)
        part = lax.dot_general(a_ref[...].astype(BF16), b_ref[...].astype(BF16), dims, preferred_element_type=F32)

        @pl.when(k == 0)
        def _():
            acc_ref[...] = part

        @pl.when(k > 0)
        def _():
            acc_ref[...] += part

        @pl.when(k == nk - 1)
        def _():
            o_ref[...] = acc_ref[...].astype(o_ref.dtype)

    acc_shape = tuple(d for d in o_spec.block_shape if d is not None)
    return pl.pallas_call(
        body, name=name, grid=grid, in_specs=[a_spec, b_spec], out_specs=o_spec, out_shape=out_shape,
        scratch_shapes=[pltpu.VMEM(acc_shape, F32)],
        compiler_params=_cparams(("parallel", "parallel", "arbitrary")),
    )(a, b)


NN = (((1,), (0,)), ((), ()))
NT = (((1,), (1,)), ((), ()))
TN = (((0,), (0,)), ((), ()))


def _mm_nn(name, a, b):
    (m, k), (_, n) = a.shape, b.shape
    bm, bn, bk = _tile(m, 512), _tile(n, 512), _tile(k, 2048)
    return _matmul(name, a, b, pl.BlockSpec((bm, bk), lambda i, j, kk: (i, kk)), pl.BlockSpec((bk, bn), lambda i, j, kk: (kk, j)),
                   pl.BlockSpec((bm, bn), lambda i, j, kk: (i, j)), jax.ShapeDtypeStruct((m, n), F32), (m // bm, n // bn, k // bk), NN, k // bk)


def _mm_nt(name, a, b):
    (m, c), (n, _) = a.shape, b.shape
    bm, bn, bk = _tile(m, 512), _tile(n, 512), _tile(c, 2048)
    return _matmul(name, a, b, pl.BlockSpec((bm, bk), lambda i, j, kk: (i, kk)), pl.BlockSpec((bn, bk), lambda i, j, kk: (j, kk)),
                   pl.BlockSpec((bm, bn), lambda i, j, kk: (i, j)), jax.ShapeDtypeStruct((m, n), F32), (m // bm, n // bn, c // bk), NT, c // bk)


def _mm_tn(name, a, b):
    (c, m), (_, n) = a.shape, b.shape
    bm, bn, bk = _tile(m, 512), _tile(n, 512), _tile(c, 2048)
    return _matmul(name, a, b, pl.BlockSpec((bk, bm), lambda i, j, kk: (kk, i)), pl.BlockSpec((bk, bn), lambda i, j, kk: (kk, j)),
                   pl.BlockSpec((bm, bn), lambda i, j, kk: (i, j)), jax.ShapeDtypeStruct((m, n), F32), (m // bm, n // bn, c // bk), TN, c // bk)


def _mm_nn_cb(name, a, bb):
    (m, k), (nb, _, ns) = a.shape, bb.shape
    bm, bk = _tile(m, 512), _tile(k, 2048)
    return _matmul(name, a, bb, pl.BlockSpec((bm, bk), lambda i, j, kk: (i, kk)), pl.BlockSpec((None, bk, ns), lambda i, j, kk: (j, kk, 0)),
                   pl.BlockSpec((bm, ns), lambda i, j, kk: (i, j)), jax.ShapeDtypeStruct((m, nb * ns), F32), (m // bm, nb, k // bk), NN, k // bk)


def _mm_nt_cb(name, dy, bb):
    m, (nb, k, ns) = dy.shape[0], bb.shape
    bm, bn = _tile(m, 1024), _tile(k, 1024)
    return _matmul(name, dy, bb, pl.BlockSpec((bm, ns), lambda i, j, kk: (i, kk)), pl.BlockSpec((None, bn, ns), lambda i, j, kk: (kk, j, 0)),
                   pl.BlockSpec((bm, bn), lambda i, j, kk: (i, j)), jax.ShapeDtypeStruct((m, k), F32), (m // bm, k // bn, nb), NT, nb)


def _mm_tn_cb(name, a, dy, ns):
    (c, k), nb = a.shape, dy.shape[1] // ns
    bm = _tile(k, 512)
    return _matmul(name, a, dy, pl.BlockSpec((c, bm), lambda i, j, kk: (0, i)), pl.BlockSpec((c, ns), lambda i, j, kk: (0, j)),
                   pl.BlockSpec((None, bm, ns), lambda i, j, kk: (j, i, 0)), jax.ShapeDtypeStruct((nb, k, ns), F32), (k // bm, nb, 1), TN, 1)


def make_mm(name):
    @jax.custom_vjp
    def mm(a, b):
        return _mm_nn(name, a, b)

    def fwd(a, b):
        return mm(a, b), (a, b)

    def bwd(res, dy):
        a, b = res
        return _mm_nt(name + "_da", dy, b), _mm_tn(name + "_dw", a, dy)

    mm.defvjp(fwd, bwd)
    return mm


def make_mm_cb(name):
    @jax.custom_vjp
    def mm(a, bb):
        return _mm_nn_cb(name, a, bb)

    def fwd(a, bb):
        return mm(a, bb), (a, bb)

    def bwd(res, dy):
        a, bb = res
        return _mm_nt_cb(name + "_da", dy, bb), _mm_tn_cb(name + "_dw", a, dy, bb.shape[2])

    mm.defvjp(fwd, bwd)
    return mm


def _rowop_specs(rows, params, bm, nl):
    row_specs = [pl.BlockSpec((bm, r.shape[1] // nl), lambda j, i: (i, j)) for r in rows]
    par_specs = [pl.BlockSpec((p.shape[0], p.shape[1] // nl), lambda j, i: (0, j)) for p in params]
    row_blk = [jax.ShapeDtypeStruct((bm, r.shape[1] // nl), r.dtype) for r in rows]
    par_blk = [jax.ShapeDtypeStruct((p.shape[0], p.shape[1] // nl), p.dtype) for p in params]
    return row_specs, par_specs, row_blk, par_blk


def make_rowop(f, name, bm, nl=1):
    def fwd_call(rows, params):
        n_rows = rows[0].shape[0]
        b = min(bm, n_rows)
        row_specs, par_specs, row_blk, par_blk = _rowop_specs(rows, params, b, nl)
        out_blk = jax.eval_shape(f, *row_blk, *par_blk)
        nr, npar = len(rows), len(params)

        def body(*refs):
            outs = f(*[r[...] for r in refs[:nr + npar]])
            for o_ref, o in zip(refs[nr + npar:], outs):
                o_ref[...] = o.astype(o_ref.dtype)

        return pl.pallas_call(
            body, name=name, grid=(nl, n_rows // b), in_specs=row_specs + par_specs,
            out_specs=[pl.BlockSpec(o.shape, lambda j, i: (i, j)) for o in out_blk],
            out_shape=[jax.ShapeDtypeStruct((n_rows, o.shape[1] * nl), o.dtype) for o in out_blk],
            compiler_params=_cparams(("parallel", "parallel")),
        )(*rows, *params)

    def bwd_call(rows, params, cts):
        n_rows = rows[0].shape[0]
        b = min(bm, n_rows)
        row_specs, par_specs, row_blk, par_blk = _rowop_specs(rows, params, b, nl)
        ct_specs = [pl.BlockSpec((b, c.shape[1] // nl), lambda j, i: (i, j)) for c in cts]
        nr, npar, nct = len(rows), len(params), len(cts)

        def body(*refs):
            i = pl.program_id(1)
            vals = [r[...] for r in refs[:nr + npar]]
            ct_vals = tuple(r[...] for r in refs[nr + npar:nr + npar + nct])
            _, vjp = jax.vjp(lambda *v: tuple(f(*v)), *vals)
            grads = vjp(ct_vals)
            outs = refs[nr + npar + nct:]
            for o_ref, g in zip(outs[:nr], grads[:nr]):
                o_ref[...] = g
            for o_ref, g in zip(outs[nr:], grads[nr:]):
                @pl.when(i == 0)
                def _(o_ref=o_ref, g=g):
                    o_ref[...] = g

                @pl.when(i > 0)
                def _(o_ref=o_ref, g=g):
                    o_ref[...] += g

        outs = pl.pallas_call(
            body, name=name + "_bwd", grid=(nl, n_rows // b), in_specs=row_specs + par_specs + ct_specs,
            out_specs=row_specs + par_specs,
            out_shape=[jax.ShapeDtypeStruct(a.shape, a.dtype) for a in list(rows) + list(params)],
            compiler_params=_cparams(("arbitrary", "arbitrary")),
        )(*rows, *params, *cts)
        return tuple(outs[:nr]), tuple(outs[nr:])

    @jax.custom_vjp
    def op(rows, params):
        return tuple(fwd_call(rows, params))

    def fwd(rows, params):
        return op(rows, params), (rows, params)

    def bwd(res, cts):
        rows, params = res
        return bwd_call(rows, params, tuple(cts))

    op.defvjp(fwd, bwd)
    return op


def make_groupop(f, name):
    def specs(arrs):
        return [pl.BlockSpec((None,) + a.shape[1:], lambda g: (g, 0, 0)) for a in arrs]

    def fwd_call(arrs):
        g_n = arrs[0].shape[0]
        out_blk = jax.eval_shape(f, *[jax.ShapeDtypeStruct(a.shape[1:], a.dtype) for a in arrs])
        n = len(arrs)

        def body(*refs):
            outs = f(*[r[...] for r in refs[:n]])
            for o_ref, o in zip(refs[n:], outs):
                o_ref[...] = o

        return pl.pallas_call(
            body, name=name, grid=(g_n,), in_specs=specs(arrs),
            out_specs=[pl.BlockSpec((None,) + o.shape, lambda g: (g, 0, 0)) for o in out_blk],
            out_shape=[jax.ShapeDtypeStruct((g_n,) + o.shape, o.dtype) for o in out_blk],
            compiler_params=_cparams(("parallel",)),
        )(*arrs)

    def bwd_call(arrs, cts):
        g_n = arrs[0].shape[0]
        n, nct = len(arrs), len(cts)

        def body(*refs):
            vals = [r[...] for r in refs[:n]]
            _, vjp = jax.vjp(lambda *v: tuple(f(*v)), *vals)
            grads = vjp(tuple(r[...] for r in refs[n:n + nct]))
            for o_ref, g in zip(refs[n + nct:], grads):
                o_ref[...] = g

        return pl.pallas_call(
            body, name=name + "_bwd", grid=(g_n,), in_specs=specs(arrs) + specs(cts), out_specs=specs(arrs),
            out_shape=[jax.ShapeDtypeStruct(a.shape, a.dtype) for a in arrs],
            compiler_params=_cparams(("parallel",)),
        )(*arrs, *cts)

    @jax.custom_vjp
    def op(*arrs):
        return tuple(fwd_call(arrs))

    def fwd(*arrs):
        return op(*arrs), arrs

    def bwd(arrs, cts):
        return tuple(bwd_call(arrs, tuple(cts)))

    op.defvjp(fwd, bwd)
    return op


def _rms(x, g):
    return x * lax.rsqrt(jnp.mean(x * x, axis=-1, keepdims=True) + EPS) * g


def _silu(z):
    return z * jax.nn.sigmoid(z)


def _log_sigmoid(x):
    return jnp.minimum(x, 0.0) - jnp.log(1.0 + jnp.exp(-jnp.abs(x)))


def f_norm(x, g):
    return (_rms(x, g),)


def f_gate_a1(y, u, d):
    return (jax.nn.gelu(y + d * u),)


def f_gate_a2(yg, t, z, b):
    return (yg * jax.nn.sigmoid(t + b) * _silu(z),)


def f_gate_b(att, z):
    return (att * _silu(z),)


def f_post_a(h, o, post_g, kv_g, pre_g):
    h1 = h + _rms(o, post_g)
    return h1, _rms(h1, kv_g), _rms(h1, pre_g)


def f_final(h, o, tgt, post_g):
    err = h + _rms(o, post_g) - tgt
    return (0.5 * jnp.mean(err * err, axis=-1, keepdims=True),)


def f_logf(gl, b):
    return (_log_sigmoid(gl + b),)


def f_memattn(q, zm, km, vm):
    s = lax.dot_general((q * (HEAD_DIM ** -0.5)).astype(BF16), km.astype(BF16), NT, preferred_element_type=F32)
    e = jnp.exp(s - jnp.max(s, axis=-1, keepdims=True))
    p = e / jnp.sum(e, axis=-1, keepdims=True)
    o = jnp.dot(p.astype(BF16), vm.astype(BF16), preferred_element_type=F32)
    return (o * _silu(zm),)


def _fox_block(row0, q, k, v, fc, fr):
    s = lax.dot_general((q * (HEAD_DIM ** -0.5)).astype(BF16), k.astype(BF16), NT, preferred_element_type=F32)
    s = s + fc - fr
    rows = lax.broadcasted_iota(jnp.int32, s.shape, 0) + row0
    cols = lax.broadcasted_iota(jnp.int32, s.shape, 1)
    s = jnp.where(rows >= cols, s, -1e30)
    e = jnp.exp(s - jnp.max(s, axis=-1, keepdims=True))
    p = e / jnp.sum(e, axis=-1, keepdims=True)
    return jnp.dot(p.astype(BF16), v.astype(BF16), preferred_element_type=F32)


def _cmul(ar, ai, xr, xi):
    return ar * xr - ai * xi, ar * xi + ai * xr


def _hp_dot(a, b):
    return jnp.dot(a, b, precision=HP, preferred_element_type=F32)


def f_s5(u, toep, win_r, win_i, wout_r, wout_i, coef):
    n_hi = u.shape[0] // SSM_LO
    y = _hp_dot(u, toep)
    sr, si = _hp_dot(u, win_r), _hp_dot(u, win_i)
    at_r, at_i = coef[0:1, :], coef[1:2, :]

    def lo_rows(a, lo):
        return a[lo * n_hi:(lo + 1) * n_hi]

    pr, pi = lo_rows(sr, 0), lo_rows(si, 0)
    for lo in range(1, SSM_LO):
        dr, di = _cmul(at_r, at_i, pr, pi)
        pr, pi = lo_rows(sr, lo) + dr, lo_rows(si, lo) + di
    ri = lax.broadcasted_iota(jnp.int32, (n_hi, n_hi), 0)
    ci = lax.broadcasted_iota(jnp.int32, (n_hi, n_hi), 1)
    d, step = 1, 0
    while d < n_hi:
        sh = (ri - ci == d).astype(F32)
        a_r, a_i = coef[2 + 2 * step:3 + 2 * step, :], coef[3 + 2 * step:4 + 2 * step, :]
        dr, di = _cmul(a_r, a_i, _hp_dot(sh, pr), _hp_dot(sh, pi))
        pr, pi = pr + dr, pi + di
        d, step = 2 * d, step + 1
    sh1 = (ri - ci == 1).astype(F32)
    xr, xi = [_hp_dot(sh1, pr)], [_hp_dot(sh1, pi)]
    for lo in range(1, SSM_LO):
        dr, di = _cmul(at_r, at_i, xr[-1], xi[-1])
        xr.append(lo_rows(sr, lo - 1) + dr)
        xi.append(lo_rows(si, lo - 1) + di)
    y = y + _hp_dot(jnp.concatenate(xr, axis=0), wout_r) + _hp_dot(jnp.concatenate(xi, axis=0), wout_i)
    return (y,)


def s5_operators(lam_re, lam_im, log_step, b_re, b_im, c_re, c_im, n_hi):
    t_n = SSM_T
    lr, li = lam_re, lam_im
    dt = jnp.exp(log_step)[:, None]
    mag = jnp.exp(lr * dt)
    ar, ai = mag * jnp.cos(li * dt), mag * jnp.sin(li * dt)
    den = lr * lr + li * li
    cr = ((ar - 1.0) * lr + ai * li) / den
    ci = (ai * lr - (ar - 1.0) * li) / den
    bbr = cr[..., None] * b_re - ci[..., None] * b_im
    bbi = cr[..., None] * b_im + ci[..., None] * b_re
    k = jnp.arange(t_n + 1, dtype=F32)[:, None, None]
    pm, ang = jnp.exp(k * (lr * dt)), k * (li * dt)
    pr, pi = pm * jnp.cos(ang), pm * jnp.sin(ang)
    abr = pr[..., None] * bbr - pi[..., None] * bbi
    abi = pr[..., None] * bbi + pi[..., None] * bbr
    kk = (jnp.einsum('ghp,kgpj->kghj', c_re, abr[:t_n], precision=HP)
          - jnp.einsum('ghp,kgpj->kghj', c_im, abi[:t_n], precision=HP))
    lag = jnp.arange(t_n)[None, :] - jnp.arange(t_n)[:, None]
    onehot = (lag[None] == jnp.arange(t_n)[:, None, None]).astype(F32)
    g_n, h_n = c_re.shape[0], c_re.shape[1]
    toep = jnp.einsum('kst,kghj->gsjth', onehot, kk, precision=HP).reshape(g_n, t_n * h_n, t_n * h_n)
    win_r = abr[:t_n][::-1].transpose(1, 0, 3, 2).reshape(g_n, t_n * h_n, -1)
    win_i = abi[:t_n][::-1].transpose(1, 0, 3, 2).reshape(g_n, t_n * h_n, -1)
    p1r, p1i = pr[1:, :, None, :], pi[1:, :, None, :]
    wout_r = (c_re[None] * p1r - c_im[None] * p1i).transpose(1, 3, 0, 2).reshape(g_n, -1, t_n * h_n)
    wout_i = (-(c_re[None] * p1i + c_im[None] * p1r)).transpose(1, 3, 0, 2).reshape(g_n, -1, t_n * h_n)
    rows = [pr[t_n], pi[t_n]]
    qr, qi = pr[t_n], pi[t_n]
    for _ in range(int(math.log2(SSM_LO))):
        qr, qi = qr * qr - qi * qi, 2.0 * qr * qi
    d = 1
    while d < n_hi:
        rows += [qr, qi]
        qr, qi = qr * qr - qi * qi, 2.0 * qr * qi
        d *= 2
    coef = jnp.stack(rows, axis=1)
    return toep, win_r, win_i, wout_r, wout_i, coef


def _cumsum_call(name, x, reverse):
    n_rows, w = x.shape
    bm = min(256, n_rows)
    nb = n_rows // bm

    def body(x_ref, o_ref, carry_ref):
        i = pl.program_id(0)

        @pl.when(i == 0)
        def _():
            carry_ref[...] = jnp.zeros_like(carry_ref)

        ri = lax.broadcasted_iota(jnp.int32, (bm, bm), 0)
        ci = lax.broadcasted_iota(jnp.int32, (bm, bm), 1)
        tri = ((ri <= ci) if reverse else (ri >= ci)).astype(F32)
        xb = x_ref[...]
        o_ref[...] = _hp_dot(tri, xb) + carry_ref[...]
        carry_ref[...] += jnp.sum(xb, axis=0, keepdims=True)

    idx = (lambda i: (nb - 1 - i, 0)) if reverse else (lambda i: (i, 0))
    return pl.pallas_call(
        body, name=name, grid=(nb,), in_specs=[pl.BlockSpec((bm, w), idx)], out_specs=pl.BlockSpec((bm, w), idx),
        out_shape=jax.ShapeDtypeStruct(x.shape, F32), scratch_shapes=[pltpu.VMEM((1, w), F32)],
        compiler_params=_cparams(("arbitrary",)),
    )(x)


def make_cumsum(name):
    @jax.custom_vjp
    def cs(x):
        return _cumsum_call(name, x, False)

    def fwd(x):
        return cs(x), None

    def bwd(_, dy):
        return (_cumsum_call(name + "_bwd", dy, True),)

    cs.defvjp(fwd, bwd)
    return cs


def _fox_specs(n_rows, bq):
    head = lambda h, i: (i, h)
    q_spec = pl.BlockSpec((bq, HEAD_DIM), head)
    kv_spec = pl.BlockSpec((n_rows, HEAD_DIM), lambda h, i: (0, h))
    fc_spec = pl.BlockSpec((None, bq, 1), lambda h, i: (h, i, 0))
    fr_spec = pl.BlockSpec((None, 1, n_rows), lambda h, i: (h, 0, 0))
    return q_spec, kv_spec, fc_spec, fr_spec


def _fox_fwd_call(name, q, k, v, fc, fr):
    n_rows, width = q.shape
    bq = min(FOX_BQ, n_rows)
    nq = n_rows // bq
    q_spec, kv_spec, fc_spec, fr_spec = _fox_specs(n_rows, bq)

    def body(q_ref, k_ref, v_ref, fc_ref, fr_ref, o_ref):
        i = pl.program_id(1)
        for p in range(nq):
            n_keys = (p + 1) * bq

            @pl.when(i == p)
            def _(p=p, n_keys=n_keys):
                o_ref[...] = _fox_block(p * bq, q_ref[...], k_ref[:n_keys, :], v_ref[:n_keys, :], fc_ref[...], fr_ref[:, :n_keys])

    return pl.pallas_call(
        body, name=name, grid=(width // HEAD_DIM, nq), in_specs=[q_spec, kv_spec, kv_spec, fc_spec, fr_spec], out_specs=q_spec,
        out_shape=jax.ShapeDtypeStruct(q.shape, F32), compiler_params=_cparams(("parallel", "parallel")),
    )(q, k, v, fc, fr)


def _fox_bwd_call(name, q, k, v, fc, fr, do):
    n_rows, width = q.shape
    bq = min(FOX_BQ, n_rows)
    nq = n_rows // bq
    q_spec, kv_spec, fc_spec, fr_spec = _fox_specs(n_rows, bq)

    def body(q_ref, k_ref, v_ref, fc_ref, fr_ref, do_ref, dq_ref, dk_ref, dv_ref, dfc_ref, dfr_ref):
        i = pl.program_id(1)

        @pl.when(i == 0)
        def _():
            dk_ref[...] = jnp.zeros_like(dk_ref)
            dv_ref[...] = jnp.zeros_like(dv_ref)
            dfr_ref[...] = jnp.zeros_like(dfr_ref)

        for p in range(nq):
            n_keys = (p + 1) * bq

            @pl.when(i == p)
            def _(p=p, n_keys=n_keys):
                _, vjp = jax.vjp(functools.partial(_fox_block, p * bq), q_ref[...], k_ref[:n_keys, :], v_ref[:n_keys, :],
                                 fc_ref[...], fr_ref[:, :n_keys])
                dq, dk, dv, dfc, dfr = vjp(do_ref[...])
                dq_ref[...] = dq
                dfc_ref[...] = dfc
                dk_ref[:n_keys, :] += dk
                dv_ref[:n_keys, :] += dv
                dfr_ref[:, :n_keys] += dfr

    return pl.pallas_call(
        body, name=name, grid=(width // HEAD_DIM, nq), in_specs=[q_spec, kv_spec, kv_spec, fc_spec, fr_spec, q_spec],
        out_specs=[q_spec, kv_spec, kv_spec, fc_spec, fr_spec],
        out_shape=[jax.ShapeDtypeStruct(a.shape, F32) for a in (q, k, v, fc, fr)],
        compiler_params=_cparams(("parallel", "arbitrary")),
    )(q, k, v, fc, fr, do)


def make_fox(name):
    @jax.custom_vjp
    def fox(q, k, v, fc, fr):
        return _fox_fwd_call(name, q, k, v, fc, fr)

    def fwd(q, k, v, fc, fr):
        return fox(q, k, v, fc, fr), (q, k, v, fc, fr)

    def bwd(res, do):
        return tuple(_fox_bwd_call(name + "_bwd", *res, do))

    fox.defvjp(fwd, bwd)
    return fox


def _exchange(name, src, gather):
    shard_shape = src.shape if gather else src.shape[1:]

    def body(src_ref, out_ref, send_sems, recv_sems, local_sem):
        x, y, c = lax.axis_index("x"), lax.axis_index("y"), lax.axis_index("c")
        me = 4 * x + 2 * y + c
        local = pltpu.make_async_copy(src_ref if gather else src_ref.at[me], out_ref.at[me], local_sem)
        local.start()
        sends, recvs = [], []
        for rel in range(1, N_DEV):
            px = 1 - x if rel & 4 else x
            py = 1 - y if rel & 2 else y
            pc = 1 - c if rel & 1 else c
            peer = 4 * px + 2 * py + pc
            sem = dict(send_sem=send_sems.at[rel - 1], recv_sem=recv_sems.at[rel - 1], device_id=(px, py, pc),
                       device_id_type=pl.DeviceIdType.MESH)
            send = pltpu.make_async_remote_copy(src_ref=src_ref if gather else src_ref.at[peer], dst_ref=out_ref.at[me], **sem)
            send.start()
            sends.append(send)
            recvs.append(pltpu.make_async_remote_copy(src_ref=src_ref if gather else src_ref.at[peer], dst_ref=out_ref.at[peer], **sem))
        for send, recv in zip(sends, recvs):
            recv.wait_recv()
            send.wait_send()
        local.wait()

    return pl.pallas_call(
        body, name=name, in_specs=[pl.BlockSpec(memory_space=pl.ANY)], out_specs=pl.BlockSpec(memory_space=pl.ANY),
        out_shape=jax.ShapeDtypeStruct((N_DEV,) + tuple(shard_shape), src.dtype),
        scratch_shapes=[pltpu.SemaphoreType.DMA((N_DEV - 1,)), pltpu.SemaphoreType.DMA((N_DEV - 1,)), pltpu.SemaphoreType.DMA],
    )(src)


def all_gather_blocks(name, shard):
    return _exchange(name, shard, True)


def scatter_blocks(name, blocks):
    return _exchange(name, blocks, False)


def adamw_reduce(name, partials, w, m, v):
    n_part, n_rows, n_cols = partials.shape
    br = n_rows
    for cand in (512, 256, 128, 64, 32, 16, 8):
        if n_rows % cand == 0 and n_part * cand * n_cols * 4 <= (4 << 20):
            br = cand
            break

    def body(p_ref, w_ref, m_ref, v_ref, g_ref, d_ref, nm_ref, nv_ref):
        g = p_ref[0]
        for s in range(1, n_part):
            g = g + p_ref[s]
        m_new = ADAM_B1 * m_ref[...] + (1.0 - ADAM_B1) * g
        v_new = ADAM_B2 * v_ref[...] + (1.0 - ADAM_B2) * jnp.square(g)
        m_hat = m_new / (1.0 - ADAM_B1 ** ADAM_STEP)
        v_hat = v_new / (1.0 - ADAM_B2 ** ADAM_STEP)
        g_ref[...] = g
        d_ref[...] = -ADAM_LR * (m_hat / (jnp.sqrt(v_hat) + ADAM_EPS) + ADAM_WD * w_ref[...])
        nm_ref[...] = m_new
        nv_ref[...] = v_new

    spec = pl.BlockSpec((br, n_cols), lambda i: (i, 0))
    return pl.pallas_call(
        body, name=name, grid=(n_rows // br,), in_specs=[pl.BlockSpec((n_part, br, n_cols), lambda i: (0, i, 0)), spec, spec, spec],
        out_specs=[spec] * 4, out_shape=[jax.ShapeDtypeStruct((n_rows, n_cols), F32)] * 4, compiler_params=_cparams(("parallel",)),
    )(partials, w, m, v)


def _pack(arrays, n_rows):
    flat = jnp.concatenate([a.reshape(-1) for a in arrays])
    return jnp.pad(flat, (0, n_rows * LANE - flat.shape[0])).reshape(n_rows, LANE)


def _unpack(packed, shapes):
    flat, out, off = packed.reshape(-1), [], 0
    for s in shapes:
        n = math.prod(s)
        out.append(flat[off:off + n].reshape(s))
        off += n
    return out


def _packed_rows(shapes):
    n = sum(math.prod(s) for s in shapes)
    return -(-n // (LANE * 512)) * 512


def local_loss(p, x, mem, tgt):
    n_rows, d_model = x.shape
    main_w = 3 * d_model // 4
    mem_w = d_model - main_w
    n_groups = main_w // SSM_GROUP
    n_heads = main_w // HEAD_DIM
    n_hi = n_rows // (SSM_T * SSM_LO)
    row = lambda a: a.reshape(1, -1)

    def mem_kv(i):
        memn, = make_rowop(f_norm, f"mem_norm{i}", 256)((mem,), (row(p['mem_norm_g'][i]),))
        kv = make_mm(f"mem_kv{i}")(memn, p['w_mem_kv'][i])
        return kv[:, :mem_w], kv[:, mem_w:]

    def mem_branch(i, qm, zm):
        km, vm = mem_kv(i)
        memo, = make_rowop(f_memattn, f"mem_attn{i}", 512, nl=mem_w // HEAD_DIM)((qm, zm), (km, vm))
        return memo

    def split(proj):
        return proj[:, :main_w], proj[:, main_w:2 * main_w], proj[:, 2 * main_w:2 * main_w + mem_w], proj[:, 2 * main_w + mem_w:]

    hn, = make_rowop(f_norm, "pre_norm0", 256)((x,), (row(p['pre_norm_g'][0]),))
    u, z, qm, zm = split(make_mm_cb("in_a")(hn, p['w_in_a']))
    ops = s5_operators(p['lam_re'], p['lam_im'], p['log_step'], p['b_re'], p['b_im'], p['c_re'], p['c_im'], n_hi)
    ug = u.reshape(n_hi, SSM_LO, SSM_T, n_groups, SSM_GROUP).transpose(3, 1, 0, 2, 4).reshape(n_groups, n_hi * SSM_LO, SSM_T * SSM_GROUP)
    yg, = make_groupop(f_s5, "s5")(ug, *ops)
    y = yg.reshape(n_groups, SSM_LO, n_hi, SSM_T, SSM_GROUP).transpose(2, 1, 3, 0, 4).reshape(n_rows, main_w)
    ygelu, = make_rowop(f_gate_a1, "gate_a1", 256)((y, u), (row(p['d_skip']),))
    t = make_mm("glu")(ygelu, p['w_glu'])
    main, = make_rowop(f_gate_a2, "gate_a2", 256)((ygelu, t, z), (row(p['b_glu']),))
    o = make_mm("out0")(jnp.concatenate([main, mem_branch(0, qm, zm)], axis=1), p['w_out'][0])
    h1, kv_in, hn = make_rowop(f_post_a, "post_a", 128)((x, o), (row(p['post_norm_g'][0]), row(p['kv_norm_g']), row(p['pre_norm_g'][1])))

    kv = make_mm_cb("kv")(kv_in, p['w_kv'])
    w_fg = jnp.pad(p['w_fgate'], ((0, 0), (0, LANE - n_heads)))
    b_fg = jnp.pad(p['b_fgate'], (0, LANE - n_heads)).reshape(1, LANE)
    logf, = make_rowop(f_logf, "logf", 512)((make_mm("fgate")(kv_in, w_fg),), (b_fg,))
    fcum = make_cumsum("fcum")(logf)[:, :n_heads].T

    q, z, qm, zm = split(make_mm_cb("in_b")(hn, p['w_in_b']))
    att = make_fox("fox")(q, kv[:, :main_w], kv[:, main_w:], fcum[:, :, None], fcum[:, None, :])
    main, = make_rowop(f_gate_b, "gate_b", 256)((att, z), ())
    o = make_mm("out1")(jnp.concatenate([main, mem_branch(1, qm, zm)], axis=1), p['w_out'][1])
    rowloss, = make_rowop(f_final, "final", 128)((h1, o, tgt), (row(p['post_norm_g'][1]),))
    return jnp.sum(rowloss)


def kernel(x, mem, pre_norm_g, post_norm_g, w_in_a, lam_re, lam_im, log_step, b_re, b_im, c_re, c_im, d_skip, w_glu, b_glu, kv_norm_g, w_kv, w_fgate, b_fgate, w_in_b, mem_norm_g, w_mem_kv, w_out, loss_target, m_pre_norm_g, m_post_norm_g, m_w_in_a, m_lam_re, m_lam_im, m_log_step, m_b_re, m_b_im, m_c_re, m_c_im, m_d_skip, m_w_glu, m_b_glu, m_kv_norm_g, m_w_kv, m_w_fgate, m_b_fgate, m_w_in_b, m_mem_norm_g, m_w_mem_kv, m_w_out, v_pre_norm_g, v_post_norm_g, v_w_in_a, v_lam_re, v_lam_im, v_log_step, v_b_re, v_b_im, v_c_re, v_c_im, v_d_skip, v_w_glu, v_b_glu, v_kv_norm_g, v_w_kv, v_w_fgate, v_b_fgate, v_w_in_b, v_mem_norm_g, v_w_mem_kv, v_w_out):
    a = dict(zip(INPUTS, (x, mem, pre_norm_g, post_norm_g, w_in_a, lam_re, lam_im, log_step, b_re, b_im, c_re, c_im, d_skip, w_glu, b_glu, kv_norm_g, w_kv, w_fgate, b_fgate, w_in_b, mem_norm_g, w_mem_kv, w_out, loss_target, m_pre_norm_g, m_post_norm_g, m_w_in_a, m_lam_re, m_lam_im, m_log_step, m_b_re, m_b_im, m_c_re, m_c_im, m_d_skip, m_w_glu, m_b_glu, m_kv_norm_g, m_w_kv, m_w_fgate, m_b_fgate, m_w_in_b, m_mem_norm_g, m_w_mem_kv, m_w_out, v_pre_norm_g, v_post_norm_g, v_w_in_a, v_lam_re, v_lam_im, v_log_step, v_b_re, v_b_im, v_c_re, v_c_im, v_d_skip, v_w_glu, v_b_glu, v_kv_norm_g, v_w_kv, v_w_fgate, v_b_fgate, v_w_in_b, v_mem_norm_g, v_w_mem_kv, v_w_out)))
    me = 4 * lax.axis_index("x") + 2 * lax.axis_index("y") + lax.axis_index("c")
    n_layers = w_out.shape[0]

    small_shapes = [a[n].shape for n in SMALL_SHARDED]
    small_rows = -(-sum(math.prod(s) for s in small_shapes) // (LANE * 8)) * 8
    small = all_gather_blocks("ag_small", _pack([a[n] for n in SMALL_SHARDED], small_rows))
    small = [jnp.stack(parts) for parts in zip(*[_unpack(small[b], small_shapes) for b in range(N_DEV)])]
    p = {n: a[n] for n in REPLICATED}
    for n in ('lam_re', 'lam_im', 'log_step', 'b_re', 'b_im', 'c_re', 'c_im'):
        p[n] = p[n][0]
    p['d_skip'] = small[0].reshape(-1)
    p['b_glu'] = small[1].reshape(-1)
    p['w_fgate'] = small[2].reshape(-1, w_fgate.shape[1])
    p['w_in_a'] = all_gather_blocks("ag_in_a", w_in_a[0])
    p['w_in_b'] = all_gather_blocks("ag_in_b", w_in_b[0])
    p['w_kv'] = all_gather_blocks("ag_kv", w_kv)
    p['w_glu'] = all_gather_blocks("ag_glu", w_glu[0]).reshape(-1, w_glu.shape[2])
    p['w_mem_kv'] = [all_gather_blocks(f"ag_mem_kv{i}", w_mem_kv[i]).reshape(-1, w_mem_kv.shape[2]) for i in range(n_layers)]
    p['w_out'] = [all_gather_blocks(f"ag_out{i}", w_out[i]).reshape(-1, w_out.shape[2]) for i in range(n_layers)]

    loss_local, (g, grad_x) = jax.value_and_grad(local_loss, argnums=(0, 1))(p, x[0], mem[0], loss_target[0])
    loss = lax.psum(loss_local, MESH_AXES)

    out = {}

    def update(tag, name, blocks, layer=None):
        pick = (lambda t: t) if layer is None else (lambda t: t[layer])
        shard = pick(a[name])
        two_d = lambda t: pick(t).reshape(blocks.shape[1:])
        parts = scatter_blocks("rs_" + tag, blocks)
        res = adamw_reduce("adamw_" + tag, parts, two_d(a[name]), two_d(a['m_' + name]), two_d(a['v_' + name]))
        return [r.reshape(shard.shape) for r in res]

    out['w_in_a'] = [r[None] for r in update("in_a", 'w_in_a', g['w_in_a'], 0)]
    out['w_in_b'] = [r[None] for r in update("in_b", 'w_in_b', g['w_in_b'], 0)]
    out['w_kv'] = update("kv", 'w_kv', g['w_kv'])
    out['w_glu'] = [r[None] for r in update("glu", 'w_glu', g['w_glu'].reshape((N_DEV,) + w_glu.shape[1:]), 0)]
    for name in ('w_mem_kv', 'w_out'):
        per_layer = [update(f"{name[2:]}{i}", name, g[name][i].reshape((N_DEV,) + a[name].shape[1:]), i) for i in range(n_layers)]
        out[name] = [jnp.stack(t) for t in zip(*per_layer)]

    full_shapes = [a[n].shape for n in REPLICATED] + [(1, N_DEV * d_skip.shape[1]), (1, N_DEV * b_glu.shape[1]), (N_DEV * w_fgate.shape[0], w_fgate.shape[1])]
    rows = _packed_rows(full_shapes)
    zeros = [jnp.zeros(s, F32) for s in full_shapes[len(REPLICATED):]]
    g_packed = all_gather_blocks("ag_grads", _pack([g[n] for n in REPLICATED + SMALL_SHARDED], rows))
    packed = lambda pre: _pack([a[pre + n] for n in REPLICATED] + zeros, rows)
    res = [_unpack(r, full_shapes) for r in adamw_reduce("adamw_small", g_packed, packed(''), packed('m_'), packed('v_'))]
    for i, n in enumerate(REPLICATED):
        out[n] = [r[i] for r in res]
    g_full = res[0][len(REPLICATED):]
    g_shard = [lax.dynamic_slice_in_dim(g_full[0], me * d_skip.shape[1], d_skip.shape[1], 1),
               lax.dynamic_slice_in_dim(g_full[1], me * b_glu.shape[1], b_glu.shape[1], 1),
               lax.dynamic_slice_in_dim(g_full[2], me * w_fgate.shape[0], w_fgate.shape[0], 0)]
    packed = lambda pre: _pack([a[pre + n] for n in SMALL_SHARDED], small_rows)
    res = [_unpack(r, small_shapes) for r in adamw_reduce("adamw_small_sharded", _pack(g_shard, small_rows)[None], packed(''), packed('m_'), packed('v_'))]
    for i, n in enumerate(SMALL_SHARDED):
        out[n] = [r[i] for r in res]

    return (loss, grad_x[None], *[out[n][k] for k in range(4) for n in WEIGHTS])
```

```python
import functools
import math

import jax
import jax.numpy as jnp
from jax import lax
from jax.experimental import pallas as pl
from jax.experimental.pallas import tpu as pltpu

F32 = jnp.float32
BF16 = jnp.bfloat16
HP = lax.Precision.HIGHEST
MESH_AXES = ("x", "y", "c")
N_DEV = 8
V7X_VMEM_LIMIT = 56 * 1024 * 1024
LANE = 128

EPS = 1e-6
HEAD_DIM = 128
SSM_GROUP = 16
SSM_STATE = 64
SSM_T = 16
SSM_LO = 8
SSM_GB = 4
FOX_BQ = 256
GRAD_DTYPE = BF16
ADAM_LR = 0.001
ADAM_B1 = 0.9
ADAM_B2 = 0.999
ADAM_EPS = 1e-08
ADAM_WD = 0.01
ADAM_STEP = 10

WEIGHTS = ['pre_norm_g', 'post_norm_g', 'w_in_a', 'lam_re', 'lam_im', 'log_step', 'b_re', 'b_im', 'c_re', 'c_im', 'd_skip',
           'w_glu', 'b_glu', 'kv_norm_g', 'w_kv', 'w_fgate', 'b_fgate', 'w_in_b', 'mem_norm_g', 'w_mem_kv', 'w_out']
INPUTS = ['x', 'mem'] + WEIGHTS + ['loss_target'] + ['m_' + n for n in WEIGHTS] + ['v_' + n for n in WEIGHTS]
REPLICATED = ['pre_norm_g', 'post_norm_g', 'lam_re', 'lam_im', 'log_step', 'b_re', 'b_im', 'c_re', 'c_im', 'kv_norm_g',
              'b_fgate', 'mem_norm_g']
SMALL_SHARDED = ['d_skip', 'b_glu', 'w_fgate']


def _cparams(sem=None):
    return pltpu.CompilerParams(dimension_semantics=sem, vmem_limit_bytes=V7X_VMEM_LIMIT)


def _tile(n, cap):
    if n <= cap:
        return n
    best = None
    for t in range(LANE, cap + 1, LANE):
        if n % t == 0:
            best = t
    assert best is not None, (n, cap)
    return best


def _matmul(name, a, b, a_spec, b_spec, o_spec, out_shape, grid, dims, nk):
    def body(a_ref, b_ref, o_ref, acc_ref):
        k = pl.program_id(2)
        part = lax.dot_general(a_ref[...].astype(BF16), b_ref[...].astype(BF16), dims, preferred_element_type=F32)

        @pl.when(k == 0)
        def _():
            acc_ref[...] = part

        @pl.when(k > 0)
        def _():
            acc_ref[...] += part

        @pl.when(k == nk - 1)
        def _():
            o_ref[...] = acc_ref[...].astype(o_ref.dtype)

    acc_shape = tuple(d for d in o_spec.block_shape if d is not None)
    return pl.pallas_call(
        body, name=name, grid=grid, in_specs=[a_spec, b_spec], out_specs=o_spec, out_shape=out_shape,
        scratch_shapes=[pltpu.VMEM(acc_shape, F32)],
        compiler_params=_cparams(("parallel", "parallel", "arbitrary")),
    )(a, b)


NN = (((1,), (0,)), ((), ()))
NT = (((1,), (1,)), ((), ()))
TN = (((0,), (0,)), ((), ()))


def _mm_nn(name, a, b):
    (m, k), (_, n) = a.shape, b.shape
    bm, bn, bk = _tile(m, 512), _tile(n, 512), _tile(k, 2048)
    return _matmul(name, a, b, pl.BlockSpec((bm, bk), lambda i, j, kk: (i, kk)), pl.BlockSpec((bk, bn), lambda i, j, kk: (kk, j)),
                   pl.BlockSpec((bm, bn), lambda i, j, kk: (i, j)), jax.ShapeDtypeStruct((m, n), F32), (m // bm, n // bn, k // bk), NN, k // bk)


def _mm_nt(name, a, b):
    (m, c), (n, _) = a.shape, b.shape
    bm, bn, bk = _tile(m, 512), _tile(n, 512), _tile(c, 2048)
    return _matmul(name, a, b, pl.BlockSpec((bm, bk), lambda i, j, kk: (i, kk)), pl.BlockSpec((bn, bk), lambda i, j, kk: (j, kk)),
                   pl.BlockSpec((bm, bn), lambda i, j, kk: (i, j)), jax.ShapeDtypeStruct((m, n), F32), (m // bm, n // bn, c // bk), NT, c // bk)


def _mm_tn(name, a, b, out_dtype=F32):
    (c, m), (_, n) = a.shape, b.shape
    bm, bn, bk = _tile(m, 512), _tile(n, 512), _tile(c, 2048)
    return _matmul(name, a, b, pl.BlockSpec((bk, bm), lambda i, j, kk: (kk, i)), pl.BlockSpec((bk, bn), lambda i, j, kk: (kk, j)),
                   pl.BlockSpec((bm, bn), lambda i, j, kk: (i, j)), jax.ShapeDtypeStruct((m, n), out_dtype), (m // bm, n // bn, c // bk), TN, c // bk)


def _mm_nn_cb(name, a, bb):
    (m, k), (nb, _, ns) = a.shape, bb.shape
    bm, bk = _tile(m, 512), _tile(k, 2048)
    return _matmul(name, a, bb, pl.BlockSpec((bm, bk), lambda i, j, kk: (i, kk)), pl.BlockSpec((None, bk, ns), lambda i, j, kk: (j, kk, 0)),
                   pl.BlockSpec((bm, ns), lambda i, j, kk: (i, j)), jax.ShapeDtypeStruct((m, nb * ns), F32), (m // bm, nb, k // bk), NN, k // bk)


def _mm_nt_cb(name, dy, bb):
    m, (nb, k, ns) = dy.shape[0], bb.shape
    bm, bn = _tile(m, 1024), _tile(k, 1024)
    return _matmul(name, dy, bb, pl.BlockSpec((bm, ns), lambda i, j, kk: (i, kk)), pl.BlockSpec((None, bn, ns), lambda i, j, kk: (kk, j, 0)),
                   pl.BlockSpec((bm, bn), lambda i, j, kk: (i, j)), jax.ShapeDtypeStruct((m, k), F32), (m // bm, k // bn, nb), NT, nb)


def _mm_tn_cb(name, a, dy, ns, out_dtype=F32):
    (c, k), nb = a.shape, dy.shape[1] // ns
    bm = _tile(k, 512)
    return _matmul(name, a, dy, pl.BlockSpec((c, bm), lambda i, j, kk: (0, i)), pl.BlockSpec((c, ns), lambda i, j, kk: (0, j)),
                   pl.BlockSpec((None, bm, ns), lambda i, j, kk: (j, i, 0)), jax.ShapeDtypeStruct((nb, k, ns), out_dtype), (k // bm, nb, 1), TN, 1)


def make_mm(name):
    @jax.custom_vjp
    def mm(a, b):
        return _mm_nn(name, a, b)

    def fwd(a, b):
        return mm(a, b), (a, b)

    def bwd(res, dy):
        a, b = res
        return _mm_nt(name + "_da", dy, b), _mm_tn(name + "_dw", a, dy)

    mm.defvjp(fwd, bwd)
    return mm


def make_mm_w(name, blocked=False):
    nn, nt = (_mm_nn_cb, _mm_nt_cb) if blocked else (_mm_nn, _mm_nt)

    @jax.custom_vjp
    def mm(a, w, sink):
        return nn(name, a, w)

    def fwd(a, w, sink):
        return mm(a, w, sink), (a, w)

    def bwd(res, dy):
        a, w = res
        dw = _mm_tn_cb(name + "_dw", a, dy, w.shape[2], GRAD_DTYPE) if blocked else _mm_tn(name + "_dw", a, dy, GRAD_DTYPE)
        return nt(name + "_da", dy, w), jnp.zeros_like(w), dw

    mm.defvjp(fwd, bwd)
    return mm


def _rowop_specs(rows, params, bm, nl):
    row_specs = [pl.BlockSpec((bm, r.shape[1] // nl), lambda j, i: (i, j)) for r in rows]
    par_specs = [pl.BlockSpec((p.shape[0], p.shape[1] // nl), lambda j, i: (0, j)) for p in params]
    row_blk = [jax.ShapeDtypeStruct((bm, r.shape[1] // nl), r.dtype) for r in rows]
    par_blk = [jax.ShapeDtypeStruct((p.shape[0], p.shape[1] // nl), p.dtype) for p in params]
    return row_specs, par_specs, row_blk, par_blk


def make_rowop(f, name, bm, nl=1):
    def fwd_call(rows, params):
        n_rows = rows[0].shape[0]
        b = min(bm, n_rows)
        row_specs, par_specs, row_blk, par_blk = _rowop_specs(rows, params, b, nl)
        out_blk = jax.eval_shape(f, *row_blk, *par_blk)
        nr, npar = len(rows), len(params)

        def body(*refs):
            outs = f(*[r[...] for r in refs[:nr + npar]])
            for o_ref, o in zip(refs[nr + npar:], outs):
                o_ref[...] = o.astype(o_ref.dtype)

        return pl.pallas_call(
            body, name=name, grid=(nl, n_rows // b), in_specs=row_specs + par_specs,
            out_specs=[pl.BlockSpec(o.shape, lambda j, i: (i, j)) for o in out_blk],
            out_shape=[jax.ShapeDtypeStruct((n_rows, o.shape[1] * nl), o.dtype) for o in out_blk],
            compiler_params=_cparams(("parallel", "parallel")),
        )(*rows, *params)

    def bwd_call(rows, params, cts):
        n_rows = rows[0].shape[0]
        b = min(bm, n_rows)
        row_specs, par_specs, row_blk, par_blk = _rowop_specs(rows, params, b, nl)
        ct_specs = [pl.BlockSpec((b, c.shape[1] // nl), lambda j, i: (i, j)) for c in cts]
        nr, npar, nct = len(rows), len(params), len(cts)

        def body(*refs):
            i = pl.program_id(1)
            vals = [r[...] for r in refs[:nr + npar]]
            ct_vals = tuple(r[...] for r in refs[nr + npar:nr + npar + nct])
            _, vjp = jax.vjp(lambda *v: tuple(f(*v)), *vals)
            grads = vjp(ct_vals)
            outs = refs[nr + npar + nct:]
            for o_ref, g in zip(outs[:nr], grads[:nr]):
                o_ref[...] = g
            for o_ref, g in zip(outs[nr:], grads[nr:]):
                @pl.when(i == 0)
                def _(o_ref=o_ref, g=g):
                    o_ref[...] = g

                @pl.when(i > 0)
                def _(o_ref=o_ref, g=g):
                    o_ref[...] += g

        outs = pl.pallas_call(
            body, name=name + "_bwd", grid=(nl, n_rows // b), in_specs=row_specs + par_specs + ct_specs,
            out_specs=row_specs + par_specs,
            out_shape=[jax.ShapeDtypeStruct(a.shape, a.dtype) for a in list(rows) + list(params)],
            compiler_params=_cparams(("arbitrary", "arbitrary")),
        )(*rows, *params, *cts)
        return tuple(outs[:nr]), tuple(outs[nr:])

    @jax.custom_vjp
    def op(rows, params):
        return tuple(fwd_call(rows, params))

    def fwd(rows, params):
        return op(rows, params), (rows, params)

    def bwd(res, cts):
        rows, params = res
        return bwd_call(rows, params, tuple(cts))

    op.defvjp(fwd, bwd)
    return op


def make_groupop(f, name, gb):
    def specs(arrs):
        return [pl.BlockSpec((gb,) + a.shape[1:], lambda g: (g, 0, 0)) for a in arrs]

    def fwd_call(arrs):
        g_n = arrs[0].shape[0]
        out_blk = jax.eval_shape(f, *[jax.ShapeDtypeStruct((gb,) + a.shape[1:], a.dtype) for a in arrs])
        n = len(arrs)

        def body(*refs):
            outs = f(*[r[...] for r in refs[:n]])
            for o_ref, o in zip(refs[n:], outs):
                o_ref[...] = o

        return pl.pallas_call(
            body, name=name, grid=(g_n // gb,), in_specs=specs(arrs),
            out_specs=[pl.BlockSpec(o.shape, lambda g: (g, 0, 0)) for o in out_blk],
            out_shape=[jax.ShapeDtypeStruct((g_n,) + o.shape[1:], o.dtype) for o in out_blk],
            compiler_params=_cparams(("parallel",)),
        )(*arrs)

    def bwd_call(arrs, cts):
        g_n = arrs[0].shape[0]
        n, nct = len(arrs), len(cts)

        def body(*refs):
            vals = [r[...] for r in refs[:n]]
            _, vjp = jax.vjp(lambda *v: tuple(f(*v)), *vals)
            grads = vjp(tuple(r[...] for r in refs[n:n + nct]))
            for o_ref, g in zip(refs[n + nct:], grads):
                o_ref[...] = g

        return pl.pallas_call(
            body, name=name + "_bwd", grid=(g_n // gb,), in_specs=specs(arrs) + specs(cts), out_specs=specs(arrs),
            out_shape=[jax.ShapeDtypeStruct(a.shape, a.dtype) for a in arrs],
            compiler_params=_cparams(("parallel",)),
        )(*arrs, *cts)

    @jax.custom_vjp
    def op(*arrs):
        return tuple(fwd_call(arrs))

    def fwd(*arrs):
        return op(*arrs), arrs

    def bwd(arrs, cts):
        return tuple(bwd_call(arrs, tuple(cts)))

    op.defvjp(fwd, bwd)
    return op


def _rms(x, g):
    return x * lax.rsqrt(jnp.mean(x * x, axis=-1, keepdims=True) + EPS) * g


def _silu(z):
    return z * jax.nn.sigmoid(z)


def _log_sigmoid(x):
    return jnp.minimum(x, 0.0) - jnp.log(1.0 + jnp.exp(-jnp.abs(x)))


def f_norm(x, g):
    return (_rms(x, g),)


def f_gate_a1(y, u, d):
    return (jax.nn.gelu(y + d * u),)


def f_gate_a2(yg, t, z, b):
    return (yg * jax.nn.sigmoid(t + b) * _silu(z),)


def f_gate_b(att, z):
    return (att * _silu(z),)


def f_post_a(h, o, post_g, kv_g, pre_g):
    h1 = h + _rms(o, post_g)
    return h1, _rms(h1, kv_g), _rms(h1, pre_g)


def f_final(h, o, tgt, post_g):
    err = h + _rms(o, post_g) - tgt
    return (0.5 * jnp.mean(err * err, axis=-1, keepdims=True),)


def f_logf(gl, b):
    return (_log_sigmoid(gl + b),)


def f_memattn(q, zm, km, vm):
    s = lax.dot_general((q * (HEAD_DIM ** -0.5)).astype(BF16), km.astype(BF16), NT, preferred_element_type=F32)
    e = jnp.exp(s - jnp.max(s, axis=-1, keepdims=True))
    p = e / jnp.sum(e, axis=-1, keepdims=True)
    o = jnp.dot(p.astype(BF16), vm.astype(BF16), preferred_element_type=F32)
    return (o * _silu(zm),)


def _fox_block(row0, q, k, v, fc, fr):
    s = lax.dot_general((q * (HEAD_DIM ** -0.5)).astype(BF16), k.astype(BF16), NT, preferred_element_type=F32)
    s = s + fc - fr
    rows = lax.broadcasted_iota(jnp.int32, s.shape, 0) + row0
    cols = lax.broadcasted_iota(jnp.int32, s.shape, 1)
    s = jnp.where(rows >= cols, s, -1e30)
    e = jnp.exp(s - jnp.max(s, axis=-1, keepdims=True))
    p = e / jnp.sum(e, axis=-1, keepdims=True)
    return jnp.dot(p.astype(BF16), v.astype(BF16), preferred_element_type=F32)


def _cmul(ar, ai, xr, xi):
    return ar * xr - ai * xi, ar * xi + ai * xr


def _hp_dot(a, b):
    return jnp.dot(a, b, precision=HP, preferred_element_type=F32)


def _bf_dot(a, b):
    return jnp.dot(a.astype(BF16), b.astype(BF16), preferred_element_type=F32)


def f_s5(u, *ops):
    return (jnp.stack([_s5_group(u[i], *[o[i] for o in ops]) for i in range(u.shape[0])]),)


def _s5_group(u, toep, win_r, win_i, wout_r, wout_i, coef):
    n_hi = u.shape[0] // SSM_LO
    y = _bf_dot(u, toep)
    sr, si = _bf_dot(u, win_r), _bf_dot(u, win_i)
    at_r, at_i = coef[0:1, :], coef[1:2, :]

    def lo_rows(a, lo):
        return a[lo * n_hi:(lo + 1) * n_hi]

    pr, pi = lo_rows(sr, 0), lo_rows(si, 0)
    for lo in range(1, SSM_LO):
        dr, di = _cmul(at_r, at_i, pr, pi)
        pr, pi = lo_rows(sr, lo) + dr, lo_rows(si, lo) + di
    ri = lax.broadcasted_iota(jnp.int32, (n_hi, n_hi), 0)
    ci = lax.broadcasted_iota(jnp.int32, (n_hi, n_hi), 1)
    d, step = 1, 0
    while d < n_hi:
        sh = (ri - ci == d).astype(F32)
        a_r, a_i = coef[2 + 2 * step:3 + 2 * step, :], coef[3 + 2 * step:4 + 2 * step, :]
        dr, di = _cmul(a_r, a_i, _hp_dot(sh, pr), _hp_dot(sh, pi))
        pr, pi = pr + dr, pi + di
        d, step = 2 * d, step + 1
    sh1 = (ri - ci == 1).astype(F32)
    xr, xi = [_hp_dot(sh1, pr)], [_hp_dot(sh1, pi)]
    for lo in range(1, SSM_LO):
        dr, di = _cmul(at_r, at_i, xr[-1], xi[-1])
        xr.append(lo_rows(sr, lo - 1) + dr)
        xi.append(lo_rows(si, lo - 1) + di)
    return y + _bf_dot(jnp.concatenate(xr, axis=0), wout_r) + _bf_dot(jnp.concatenate(xi, axis=0), wout_i)


def s5_operators(lam_re, lam_im, log_step, b_re, b_im, c_re, c_im, n_hi):
    t_n = SSM_T
    lr, li = lam_re, lam_im
    dt = jnp.exp(log_step)[:, None]
    mag = jnp.exp(lr * dt)
    ar, ai = mag * jnp.cos(li * dt), mag * jnp.sin(li * dt)
    den = lr * lr + li * li
    cr = ((ar - 1.0) * lr + ai * li) / den
    ci = (ai * lr - (ar - 1.0) * li) / den
    bbr = cr[..., None] * b_re - ci[..., None] * b_im
    bbi = cr[..., None] * b_im + ci[..., None] * b_re
    k = jnp.arange(t_n + 1, dtype=F32)[:, None, None]
    pm, ang = jnp.exp(k * (lr * dt)), k * (li * dt)
    pr, pi = pm * jnp.cos(ang), pm * jnp.sin(ang)
    abr = pr[..., None] * bbr - pi[..., None] * bbi
    abi = pr[..., None] * bbi + pi[..., None] * bbr
    kk = (jnp.einsum('ghp,kgpj->kghj', c_re, abr[:t_n], precision=HP)
          - jnp.einsum('ghp,kgpj->kghj', c_im, abi[:t_n], precision=HP))
    lag = jnp.arange(t_n)[None, :] - jnp.arange(t_n)[:, None]
    onehot = (lag[None] == jnp.arange(t_n)[:, None, None]).astype(F32)
    g_n, h_n = c_re.shape[0], c_re.shape[1]
    toep = jnp.einsum('kst,kghj->gsjth', onehot, kk, precision=HP).reshape(g_n, t_n * h_n, t_n * h_n)
    win_r = abr[:t_n][::-1].transpose(1, 0, 3, 2).reshape(g_n, t_n * h_n, -1)
    win_i = abi[:t_n][::-1].transpose(1, 0, 3, 2).reshape(g_n, t_n * h_n, -1)
    p1r, p1i = pr[1:, :, None, :], pi[1:, :, None, :]
    wout_r = (c_re[None] * p1r - c_im[None] * p1i).transpose(1, 3, 0, 2).reshape(g_n, -1, t_n * h_n)
    wout_i = (-(c_re[None] * p1i + c_im[None] * p1r)).transpose(1, 3, 0, 2).reshape(g_n, -1, t_n * h_n)
    rows = [pr[t_n], pi[t_n]]
    qr, qi = pr[t_n], pi[t_n]
    for _ in range(int(math.log2(SSM_LO))):
        qr, qi = qr * qr - qi * qi, 2.0 * qr * qi
    d = 1
    while d < n_hi:
        rows += [qr, qi]
        qr, qi = qr * qr - qi * qi, 2.0 * qr * qi
        d *= 2
    coef = jnp.stack(rows, axis=1)
    return toep, win_r, win_i, wout_r, wout_i, coef


def _cumsum_call(name, x, reverse):
    n_rows, w = x.shape
    bm = min(256, n_rows)
    nb = n_rows // bm

    def body(x_ref, o_ref, carry_ref):
        i = pl.program_id(0)

        @pl.when(i == 0)
        def _():
            carry_ref[...] = jnp.zeros_like(carry_ref)

        ri = lax.broadcasted_iota(jnp.int32, (bm, bm), 0)
        ci = lax.broadcasted_iota(jnp.int32, (bm, bm), 1)
        tri = ((ri <= ci) if reverse else (ri >= ci)).astype(F32)
        xb = x_ref[...]
        o_ref[...] = _hp_dot(tri, xb) + carry_ref[...]
        carry_ref[...] += jnp.sum(xb, axis=0, keepdims=True)

    idx = (lambda i: (nb - 1 - i, 0)) if reverse else (lambda i: (i, 0))
    return pl.pallas_call(
        body, name=name, grid=(nb,), in_specs=[pl.BlockSpec((bm, w), idx)], out_specs=pl.BlockSpec((bm, w), idx),
        out_shape=jax.ShapeDtypeStruct(x.shape, F32), scratch_shapes=[pltpu.VMEM((1, w), F32)],
        compiler_params=_cparams(("arbitrary",)),
    )(x)


def make_cumsum(name):
    @jax.custom_vjp
    def cs(x):
        return _cumsum_call(name, x, False)

    def fwd(x):
        return cs(x), None

    def bwd(_, dy):
        return (_cumsum_call(name + "_bwd", dy, True),)

    cs.defvjp(fwd, bwd)
    return cs


def _fox_specs(n_rows, bq):
    head = lambda h, i: (i, h)
    q_spec = pl.BlockSpec((bq, HEAD_DIM), head)
    kv_spec = pl.BlockSpec((n_rows, HEAD_DIM), lambda h, i: (0, h))
    fc_spec = pl.BlockSpec((None, bq, 1), lambda h, i: (h, i, 0))
    fr_spec = pl.BlockSpec((None, 1, n_rows), lambda h, i: (h, 0, 0))
    return q_spec, kv_spec, fc_spec, fr_spec


def _fox_fwd_call(name, q, k, v, fc, fr):
    n_rows, width = q.shape
    bq = min(FOX_BQ, n_rows)
    nq = n_rows // bq
    q_spec, kv_spec, fc_spec, fr_spec = _fox_specs(n_rows, bq)

    def body(q_ref, k_ref, v_ref, fc_ref, fr_ref, o_ref):
        i = pl.program_id(1)
        for p in range(nq):
            n_keys = (p + 1) * bq

            @pl.when(i == p)
            def _(p=p, n_keys=n_keys):
                o_ref[...] = _fox_block(p * bq, q_ref[...], k_ref[:n_keys, :], v_ref[:n_keys, :], fc_ref[...], fr_ref[:, :n_keys])

    return pl.pallas_call(
        body, name=name, grid=(width // HEAD_DIM, nq), in_specs=[q_spec, kv_spec, kv_spec, fc_spec, fr_spec], out_specs=q_spec,
        out_shape=jax.ShapeDtypeStruct(q.shape, F32), compiler_params=_cparams(("parallel", "parallel")),
    )(q, k, v, fc, fr)


def _fox_bwd_call(name, q, k, v, fc, fr, do):
    n_rows, width = q.shape
    bq = min(FOX_BQ, n_rows)
    nq = n_rows // bq
    q_spec, kv_spec, fc_spec, fr_spec = _fox_specs(n_rows, bq)

    def body(q_ref, k_ref, v_ref, fc_ref, fr_ref, do_ref, dq_ref, dk_ref, dv_ref, dfc_ref, dfr_ref):
        i = pl.program_id(1)

        @pl.when(i == 0)
        def _():
            dk_ref[...] = jnp.zeros_like(dk_ref)
            dv_ref[...] = jnp.zeros_like(dv_ref)
            dfr_ref[...] = jnp.zeros_like(dfr_ref)

        for p in range(nq):
            n_keys = (p + 1) * bq

            @pl.when(i == p)
            def _(p=p, n_keys=n_keys):
                _, vjp = jax.vjp(functools.partial(_fox_block, p * bq), q_ref[...], k_ref[:n_keys, :], v_ref[:n_keys, :],
                                 fc_ref[...], fr_ref[:, :n_keys])
                dq, dk, dv, dfc, dfr = vjp(do_ref[...])
                dq_ref[...] = dq
                dfc_ref[...] = dfc
                dk_ref[:n_keys, :] += dk
                dv_ref[:n_keys, :] += dv
                dfr_ref[:, :n_keys] += dfr

    return pl.pallas_call(
        body, name=name, grid=(width // HEAD_DIM, nq), in_specs=[q_spec, kv_spec, kv_spec, fc_spec, fr_spec, q_spec],
        out_specs=[q_spec, kv_spec, kv_spec, fc_spec, fr_spec],
        out_shape=[jax.ShapeDtypeStruct(a.shape, F32) for a in (q, k, v, fc, fr)],
        compiler_params=_cparams(("parallel", "arbitrary")),
    )(q, k, v, fc, fr, do)


def make_fox(name):
    @jax.custom_vjp
    def fox(q, k, v, fc, fr):
        return _fox_fwd_call(name, q, k, v, fc, fr)

    def fwd(q, k, v, fc, fr):
        return fox(q, k, v, fc, fr), (q, k, v, fc, fr)

    def bwd(res, do):
        return tuple(_fox_bwd_call(name + "_bwd", *res, do))

    fox.defvjp(fwd, bwd)
    return fox


def _exchange(name, src, gather):
    shard_shape = src.shape if gather else src.shape[1:]

    def body(src_ref, out_ref, send_sems, recv_sems, local_sem):
        x, y, c = lax.axis_index("x"), lax.axis_index("y"), lax.axis_index("c")
        me = 4 * x + 2 * y + c
        local = pltpu.make_async_copy(src_ref if gather else src_ref.at[me], out_ref.at[me], local_sem)
        local.start()
        sends, recvs = [], []
        for rel in range(1, N_DEV):
            px = 1 - x if rel & 4 else x
            py = 1 - y if rel & 2 else y
            pc = 1 - c if rel & 1 else c
            peer = 4 * px + 2 * py + pc
            sem = dict(send_sem=send_sems.at[rel - 1], recv_sem=recv_sems.at[rel - 1], device_id=(px, py, pc),
                       device_id_type=pl.DeviceIdType.MESH)
            send = pltpu.make_async_remote_copy(src_ref=src_ref if gather else src_ref.at[peer], dst_ref=out_ref.at[me], **sem)
            send.start()
            sends.append(send)
            recvs.append(pltpu.make_async_remote_copy(src_ref=src_ref if gather else src_ref.at[peer], dst_ref=out_ref.at[peer], **sem))
        for send, recv in zip(sends, recvs):
            recv.wait_recv()
            send.wait_send()
        local.wait()

    return pl.pallas_call(
        body, name=name, in_specs=[pl.BlockSpec(memory_space=pl.ANY)], out_specs=pl.BlockSpec(memory_space=pl.ANY),
        out_shape=jax.ShapeDtypeStruct((N_DEV,) + tuple(shard_shape), src.dtype),
        scratch_shapes=[pltpu.SemaphoreType.DMA((N_DEV - 1,)), pltpu.SemaphoreType.DMA((N_DEV - 1,)), pltpu.SemaphoreType.DMA],
    )(src)


def all_gather_blocks(name, shard):
    return _exchange(name, shard, True)


def scatter_blocks(name, blocks):
    return _exchange(name, blocks, False)


def adamw_reduce(name, partials, w, m, v):
    n_part, n_rows, n_cols = partials.shape
    br = n_rows
    for cand in (512, 256, 128, 64, 32, 16, 8):
        if n_rows % cand == 0 and n_part * cand * n_cols * partials.dtype.itemsize <= (4 << 20):
            br = cand
            break

    def body(p_ref, w_ref, m_ref, v_ref, g_ref, d_ref, nm_ref, nv_ref):
        g = p_ref[0].astype(F32)
        for s in range(1, n_part):
            g = g + p_ref[s].astype(F32)
        m_new = ADAM_B1 * m_ref[...] + (1.0 - ADAM_B1) * g
        v_new = ADAM_B2 * v_ref[...] + (1.0 - ADAM_B2) * jnp.square(g)
        m_hat = m_new / (1.0 - ADAM_B1 ** ADAM_STEP)
        v_hat = v_new / (1.0 - ADAM_B2 ** ADAM_STEP)
        g_ref[...] = g
        d_ref[...] = -ADAM_LR * (m_hat / (jnp.sqrt(v_hat) + ADAM_EPS) + ADAM_WD * w_ref[...])
        nm_ref[...] = m_new
        nv_ref[...] = v_new

    spec = pl.BlockSpec((br, n_cols), lambda i: (i, 0))
    return pl.pallas_call(
        body, name=name, grid=(n_rows // br,), in_specs=[pl.BlockSpec((n_part, br, n_cols), lambda i: (0, i, 0)), spec, spec, spec],
        out_specs=[spec] * 4, out_shape=[jax.ShapeDtypeStruct((n_rows, n_cols), F32)] * 4, compiler_params=_cparams(("parallel",)),
    )(partials, w, m, v)


def _pack(arrays, n_rows):
    flat = jnp.concatenate([a.reshape(-1) for a in arrays])
    return jnp.pad(flat, (0, n_rows * LANE - flat.shape[0])).reshape(n_rows, LANE)


def _unpack(packed, shapes):
    flat, out, off = packed.reshape(-1), [], 0
    for s in shapes:
        n = math.prod(s)
        out.append(flat[off:off + n].reshape(s))
        off += n
    return out


def _packed_rows(shapes):
    n = sum(math.prod(s) for s in shapes)
    return -(-n // (LANE * 512)) * 512


def local_loss(p, w, x, mem, tgt):
    n_rows, d_model = x.shape
    main_w = 3 * d_model // 4
    mem_w = d_model - main_w
    n_groups = main_w // SSM_GROUP
    n_heads = main_w // HEAD_DIM
    n_hi = n_rows // (SSM_T * SSM_LO)
    row = lambda a: a.reshape(1, -1)

    def mem_kv(i):
        memn, = make_rowop(f_norm, f"mem_norm{i}", 256)((mem,), (row(p['mem_norm_g'][i]),))
        kv = make_mm_w(f"mem_kv{i}")(memn, w['w_mem_kv'][i], p['w_mem_kv'][i])
        return kv[:, :mem_w], kv[:, mem_w:]

    def mem_branch(i, qm, zm):
        km, vm = mem_kv(i)
        memo, = make_rowop(f_memattn, f"mem_attn{i}", 512, nl=mem_w // HEAD_DIM)((qm, zm), (km, vm))
        return memo

    def split(proj):
        return proj[:, :main_w], proj[:, main_w:2 * main_w], proj[:, 2 * main_w:2 * main_w + mem_w], proj[:, 2 * main_w + mem_w:]

    hn, = make_rowop(f_norm, "pre_norm0", 256)((x,), (row(p['pre_norm_g'][0]),))
    u, z, qm, zm = split(make_mm_w("in_a", True)(hn, w['w_in_a'], p['w_in_a']))
    ops = s5_operators(p['lam_re'], p['lam_im'], p['log_step'], p['b_re'], p['b_im'], p['c_re'], p['c_im'], n_hi)
    ug = u.astype(BF16).reshape(n_hi, SSM_LO, SSM_T, n_groups, SSM_GROUP).transpose(3, 1, 0, 2, 4).reshape(n_groups, n_hi * SSM_LO, SSM_T * SSM_GROUP)
    yg, = make_groupop(f_s5, "s5", SSM_GB)(ug, *ops)
    y = yg.reshape(n_groups, SSM_LO, n_hi, SSM_T, SSM_GROUP).transpose(2, 1, 3, 0, 4).reshape(n_rows, main_w)
    ygelu, = make_rowop(f_gate_a1, "gate_a1", 256)((y, u), (row(p['d_skip']),))
    t = make_mm_w("glu")(ygelu, w['w_glu'], p['w_glu'])
    main, = make_rowop(f_gate_a2, "gate_a2", 256)((ygelu, t, z), (row(p['b_glu']),))
    o = make_mm_w("out0")(jnp.concatenate([main, mem_branch(0, qm, zm)], axis=1), w['w_out'][0], p['w_out'][0])
    h1, kv_in, hn = make_rowop(f_post_a, "post_a", 128)((x, o), (row(p['post_norm_g'][0]), row(p['kv_norm_g']), row(p['pre_norm_g'][1])))

    kv = make_mm_w("kv", True)(kv_in, w['w_kv'], p['w_kv'])
    w_fg = jnp.pad(p['w_fgate'], ((0, 0), (0, LANE - n_heads)))
    b_fg = jnp.pad(p['b_fgate'], (0, LANE - n_heads)).reshape(1, LANE)
    logf, = make_rowop(f_logf, "logf", 512)((make_mm("fgate")(kv_in, w_fg),), (b_fg,))
    fcum = make_cumsum("fcum")(logf)[:, :n_heads].T

    q, z, qm, zm = split(make_mm_w("in_b", True)(hn, w['w_in_b'], p['w_in_b']))
    att = make_fox("fox")(q, kv[:, :main_w], kv[:, main_w:], fcum[:, :, None], fcum[:, None, :])
    main, = make_rowop(f_gate_b, "gate_b", 256)((att, z), ())
    o = make_mm_w("out1")(jnp.concatenate([main, mem_branch(1, qm, zm)], axis=1), w['w_out'][1], p['w_out'][1])
    rowloss, = make_rowop(f_final, "final", 128)((h1, o, tgt), (row(p['post_norm_g'][1]),))
    return jnp.sum(rowloss)


def kernel(x, mem, pre_norm_g, post_norm_g, w_in_a, lam_re, lam_im, log_step, b_re, b_im, c_re, c_im, d_skip, w_glu, b_glu, kv_norm_g, w_kv, w_fgate, b_fgate, w_in_b, mem_norm_g, w_mem_kv, w_out, loss_target, m_pre_norm_g, m_post_norm_g, m_w_in_a, m_lam_re, m_lam_im, m_log_step, m_b_re, m_b_im, m_c_re, m_c_im, m_d_skip, m_w_glu, m_b_glu, m_kv_norm_g, m_w_kv, m_w_fgate, m_b_fgate, m_w_in_b, m_mem_norm_g, m_w_mem_kv, m_w_out, v_pre_norm_g, v_post_norm_g, v_w_in_a, v_lam_re, v_lam_im, v_log_step, v_b_re, v_b_im, v_c_re, v_c_im, v_d_skip, v_w_glu, v_b_glu, v_kv_norm_g, v_w_kv, v_w_fgate, v_b_fgate, v_w_in_b, v_mem_norm_g, v_w_mem_kv, v_w_out):
    a = dict(zip(INPUTS, (x, mem, pre_norm_g, post_norm_g, w_in_a, lam_re, lam_im, log_step, b_re, b_im, c_re, c_im, d_skip, w_glu, b_glu, kv_norm_g, w_kv, w_fgate, b_fgate, w_in_b, mem_norm_g, w_mem_kv, w_out, loss_target, m_pre_norm_g, m_post_norm_g, m_w_in_a, m_lam_re, m_lam_im, m_log_step, m_b_re, m_b_im, m_c_re, m_c_im, m_d_skip, m_w_glu, m_b_glu, m_kv_norm_g, m_w_kv, m_w_fgate, m_b_fgate, m_w_in_b, m_mem_norm_g, m_w_mem_kv, m_w_out, v_pre_norm_g, v_post_norm_g, v_w_in_a, v_lam_re, v_lam_im, v_log_step, v_b_re, v_b_im, v_c_re, v_c_im, v_d_skip, v_w_glu, v_b_glu, v_kv_norm_g, v_w_kv, v_w_fgate, v_b_fgate, v_w_in_b, v_mem_norm_g, v_w_mem_kv, v_w_out)))
    me = 4 * lax.axis_index("x") + 2 * lax.axis_index("y") + lax.axis_index("c")
    n_layers = w_out.shape[0]

    small_shapes = [a[n].shape for n in SMALL_SHARDED]
    small_rows = -(-sum(math.prod(s) for s in small_shapes) // (LANE * 8)) * 8
    small = all_gather_blocks("ag_small", _pack([a[n] for n in SMALL_SHARDED], small_rows))
    small = [jnp.stack(parts) for parts in zip(*[_unpack(small[b], small_shapes) for b in range(N_DEV)])]
    p = {n: a[n] for n in REPLICATED}
    for n in ('lam_re', 'lam_im', 'log_step', 'b_re', 'b_im', 'c_re', 'c_im'):
        p[n] = p[n][0]
    p['d_skip'] = small[0].reshape(-1)
    p['b_glu'] = small[1].reshape(-1)
    p['w_fgate'] = small[2].reshape(-1, w_fgate.shape[1])
    w = {}
    w['w_in_a'] = all_gather_blocks("ag_in_a", w_in_a[0].astype(BF16))
    w['w_in_b'] = all_gather_blocks("ag_in_b", w_in_b[0].astype(BF16))
    w['w_kv'] = all_gather_blocks("ag_kv", w_kv.astype(BF16))
    w['w_glu'] = all_gather_blocks("ag_glu", w_glu[0].astype(BF16)).reshape(-1, w_glu.shape[2])
    w['w_mem_kv'] = [all_gather_blocks(f"ag_mem_kv{i}", w_mem_kv[i].astype(BF16)).reshape(-1, w_mem_kv.shape[2]) for i in range(n_layers)]
    w['w_out'] = [all_gather_blocks(f"ag_out{i}", w_out[i].astype(BF16)).reshape(-1, w_out.shape[2]) for i in range(n_layers)]
    p.update(jax.tree.map(lambda t: jnp.zeros(t.shape, GRAD_DTYPE), w))

    loss_local, (g, grad_x) = jax.value_and_grad(local_loss, argnums=(0, 2))(p, w, x[0], mem[0], loss_target[0])
    loss = lax.psum(loss_local, MESH_AXES)

    out = {}

    def update(tag, name, blocks, layer=None):
        pick = (lambda t: t) if layer is None else (lambda t: t[layer])
        shard = pick(a[name])
        two_d = lambda t: pick(t).reshape(blocks.shape[1:])
        parts = scatter_blocks("rs_" + tag, blocks)
        res = adamw_reduce("adamw_" + tag, parts, two_d(a[name]), two_d(a['m_' + name]), two_d(a['v_' + name]))
        return [r.reshape(shard.shape) for r in res]

    out['w_in_a'] = [r[None] for r in update("in_a", 'w_in_a', g['w_in_a'], 0)]
    out['w_in_b'] = [r[None] for r in update("in_b", 'w_in_b', g['w_in_b'], 0)]
    out['w_kv'] = update("kv", 'w_kv', g['w_kv'])
    out['w_glu'] = [r[None] for r in update("glu", 'w_glu', g['w_glu'].reshape((N_DEV,) + w_glu.shape[1:]), 0)]
    for name in ('w_mem_kv', 'w_out'):
        per_layer = [update(f"{name[2:]}{i}", name, g[name][i].reshape((N_DEV,) + a[name].shape[1:]), i) for i in range(n_layers)]
        out[name] = [jnp.stack(t) for t in zip(*per_layer)]

    full_shapes = [a[n].shape for n in REPLICATED] + [(1, N_DEV * d_skip.shape[1]), (1, N_DEV * b_glu.shape[1]), (N_DEV * w_fgate.shape[0], w_fgate.shape[1])]
    rows = _packed_rows(full_shapes)
    zeros = [jnp.zeros(s, F32) for s in full_shapes[len(REPLICATED):]]
    g_packed = all_gather_blocks("ag_grads", _pack([g[n] for n in REPLICATED + SMALL_SHARDED], rows))
    packed = lambda pre: _pack([a[pre + n] for n in REPLICATED] + zeros, rows)
    res = [_unpack(r, full_shapes) for r in adamw_reduce("adamw_small", g_packed, packed(''), packed('m_'), packed('v_'))]
    for i, n in enumerate(REPLICATED):
        out[n] = [r[i] for r in res]
    g_full = res[0][len(REPLICATED):]
    g_shard = [lax.dynamic_slice_in_dim(g_full[0], me * d_skip.shape[1], d_skip.shape[1], 1),
               lax.dynamic_slice_in_dim(g_full[1], me * b_glu.shape[1], b_glu.shape[1], 1),
               lax.dynamic_slice_in_dim(g_full[2], me * w_fgate.shape[0], w_fgate.shape[0], 0)]
    packed = lambda pre: _pack([a[pre + n] for n in SMALL_SHARDED], small_rows)
    res = [_unpack(r, small_shapes) for r in adamw_reduce("adamw_small_sharded", _pack(g_shard, small_rows)[None], packed(''), packed('m_'), packed('v_'))]
    for i, n in enumerate(SMALL_SHARDED):
        out[n] = [r[i] for r in res]

    return (loss, grad_x[None], *[out[n][k] for k in range(4) for n in WEIGHTS])
```

```python
import functools
import math

import jax
import jax.numpy as jnp
from jax import lax
from jax.experimental import pallas as pl
from jax.experimental.pallas import tpu as pltpu

F32 = jnp.float32
BF16 = jnp.bfloat16
HP = lax.Precision.HIGHEST
MESH_AXES = ("x", "y", "c")
N_DEV = 8
V7X_VMEM_LIMIT = 56 * 1024 * 1024
LANE = 128

EPS = 1e-6
HEAD_DIM = 128
SSM_GROUP = 16
SSM_STATE = 64
SSM_T = 16
SSM_LO = 8
SSM_GB = 4
FOX_BQ = 256
GRAD_DTYPE = BF16
ADAM_LR = 0.001
ADAM_B1 = 0.9
ADAM_B2 = 0.999
ADAM_EPS = 1e-08
ADAM_WD = 0.01
ADAM_STEP = 10

WEIGHTS = ['pre_norm_g', 'post_norm_g', 'w_in_a', 'lam_re', 'lam_im', 'log_step', 'b_re', 'b_im', 'c_re', 'c_im', 'd_skip',
           'w_glu', 'b_glu', 'kv_norm_g', 'w_kv', 'w_fgate', 'b_fgate', 'w_in_b', 'mem_norm_g', 'w_mem_kv', 'w_out']
INPUTS = ['x', 'mem'] + WEIGHTS + ['loss_target'] + ['m_' + n for n in WEIGHTS] + ['v_' + n for n in WEIGHTS]
REPLICATED = ['pre_norm_g', 'post_norm_g', 'lam_re', 'lam_im', 'log_step', 'b_re', 'b_im', 'c_re', 'c_im', 'kv_norm_g',
              'b_fgate', 'mem_norm_g']
SMALL_SHARDED = ['d_skip', 'b_glu', 'w_fgate']


def _cparams(sem=None):
    return pltpu.CompilerParams(dimension_semantics=sem, vmem_limit_bytes=V7X_VMEM_LIMIT)


def _tile(n, cap):
    if n <= cap:
        return n
    best = None
    for t in range(LANE, cap + 1, LANE):
        if n % t == 0:
            best = t
    assert best is not None, (n, cap)
    return best


def _matmul(name, a, b, a_spec, b_spec, o_spec, out_shape, grid, dims, nk, after=None):
    def body(a_ref, b_ref, *rest):
        o_ref, acc_ref = rest[-2:]
        k = pl.program_id(2)
        part = lax.dot_general(a_ref[...].astype(BF16), b_ref[...].astype(BF16), dims, preferred_element_type=F32)

        @pl.when(k == 0)
        def _():
            acc_ref[...] = part

        @pl.when(k > 0)
        def _():
            acc_ref[...] += part

        @pl.when(k == nk - 1)
        def _():
            o_ref[...] = acc_ref[...].astype(o_ref.dtype)

    acc_shape = tuple(d for d in o_spec.block_shape if d is not None)
    extra = [] if after is None else [after]
    return pl.pallas_call(
        body, name=name, grid=grid, in_specs=[a_spec, b_spec] + [pl.BlockSpec(memory_space=pl.ANY)] * len(extra), out_specs=o_spec,
        out_shape=out_shape, scratch_shapes=[pltpu.VMEM(acc_shape, F32)],
        compiler_params=_cparams(("parallel", "parallel", "arbitrary")),
    )(a, b, *extra)


NN = (((1,), (0,)), ((), ()))
NT = (((1,), (1,)), ((), ()))
TN = (((0,), (0,)), ((), ()))


def _mm_nn(name, a, b):
    (m, k), (_, n) = a.shape, b.shape
    bm, bn, bk = _tile(m, 512), _tile(n, 512), _tile(k, 2048)
    return _matmul(name, a, b, pl.BlockSpec((bm, bk), lambda i, j, kk: (i, kk)), pl.BlockSpec((bk, bn), lambda i, j, kk: (kk, j)),
                   pl.BlockSpec((bm, bn), lambda i, j, kk: (i, j)), jax.ShapeDtypeStruct((m, n), F32), (m // bm, n // bn, k // bk), NN, k // bk)


def _mm_nt(name, a, b, after=None):
    (m, c), (n, _) = a.shape, b.shape
    bm, bn, bk = _tile(m, 512), _tile(n, 512), _tile(c, 2048)
    return _matmul(name, a, b, pl.BlockSpec((bm, bk), lambda i, j, kk: (i, kk)), pl.BlockSpec((bn, bk), lambda i, j, kk: (j, kk)),
                   pl.BlockSpec((bm, bn), lambda i, j, kk: (i, j)), jax.ShapeDtypeStruct((m, n), F32), (m // bm, n // bn, c // bk), NT, c // bk, after)


def _mm_tn(name, a, b, out_dtype=F32):
    (c, m), (_, n) = a.shape, b.shape
    bm, bn, bk = _tile(m, 512), _tile(n, 512), _tile(c, 2048)
    return _matmul(name, a, b, pl.BlockSpec((bk, bm), lambda i, j, kk: (kk, i)), pl.BlockSpec((bk, bn), lambda i, j, kk: (kk, j)),
                   pl.BlockSpec((bm, bn), lambda i, j, kk: (i, j)), jax.ShapeDtypeStruct((m, n), out_dtype), (m // bm, n // bn, c // bk), TN, c // bk)


def _mm_nn_cb(name, a, bb):
    (m, k), (nb, _, ns) = a.shape, bb.shape
    bm, bk = _tile(m, 512), _tile(k, 2048)
    return _matmul(name, a, bb, pl.BlockSpec((bm, bk), lambda i, j, kk: (i, kk)), pl.BlockSpec((None, bk, ns), lambda i, j, kk: (j, kk, 0)),
                   pl.BlockSpec((bm, ns), lambda i, j, kk: (i, j)), jax.ShapeDtypeStruct((m, nb * ns), F32), (m // bm, nb, k // bk), NN, k // bk)


def _mm_nt_cb(name, dy, bb, after=None):
    m, (nb, k, ns) = dy.shape[0], bb.shape
    bm, bn = _tile(m, 1024), _tile(k, 1024)
    return _matmul(name, dy, bb, pl.BlockSpec((bm, ns), lambda i, j, kk: (i, kk)), pl.BlockSpec((None, bn, ns), lambda i, j, kk: (kk, j, 0)),
                   pl.BlockSpec((bm, bn), lambda i, j, kk: (i, j)), jax.ShapeDtypeStruct((m, k), F32), (m // bm, k // bn, nb), NT, nb, after)


def _mm_tn_cb(name, a, dy, ns, out_dtype=F32):
    (c, k), nb = a.shape, dy.shape[1] // ns
    bm = _tile(k, 512)
    return _matmul(name, a, dy, pl.BlockSpec((c, bm), lambda i, j, kk: (0, i)), pl.BlockSpec((c, ns), lambda i, j, kk: (0, j)),
                   pl.BlockSpec((None, bm, ns), lambda i, j, kk: (j, i, 0)), jax.ShapeDtypeStruct((nb, k, ns), out_dtype), (k // bm, nb, 1), TN, 1)


def make_mm(name):
    @jax.custom_vjp
    def mm(a, b):
        return _mm_nn(name, a, b)

    def fwd(a, b):
        return mm(a, b), (a, b)

    def bwd(res, dy):
        a, b = res
        return _mm_nt(name + "_da", dy, b), _mm_tn(name + "_dw", a, dy)

    mm.defvjp(fwd, bwd)
    return mm


def _rowop_specs(rows, params, bm, nl):
    row_specs = [pl.BlockSpec((bm, r.shape[1] // nl), lambda j, i: (i, j)) for r in rows]
    par_specs = [pl.BlockSpec((p.shape[0], p.shape[1] // nl), lambda j, i: (0, j)) for p in params]
    row_blk = [jax.ShapeDtypeStruct((bm, r.shape[1] // nl), r.dtype) for r in rows]
    par_blk = [jax.ShapeDtypeStruct((p.shape[0], p.shape[1] // nl), p.dtype) for p in params]
    return row_specs, par_specs, row_blk, par_blk


def make_rowop(f, name, bm, nl=1):
    def fwd_call(rows, params):
        n_rows = rows[0].shape[0]
        b = min(bm, n_rows)
        row_specs, par_specs, row_blk, par_blk = _rowop_specs(rows, params, b, nl)
        out_blk = jax.eval_shape(f, *row_blk, *par_blk)
        nr, npar = len(rows), len(params)

        def body(*refs):
            outs = f(*[r[...] for r in refs[:nr + npar]])
            for o_ref, o in zip(refs[nr + npar:], outs):
                o_ref[...] = o.astype(o_ref.dtype)

        return pl.pallas_call(
            body, name=name, grid=(nl, n_rows // b), in_specs=row_specs + par_specs,
            out_specs=[pl.BlockSpec(o.shape, lambda j, i: (i, j)) for o in out_blk],
            out_shape=[jax.ShapeDtypeStruct((n_rows, o.shape[1] * nl), o.dtype) for o in out_blk],
            compiler_params=_cparams(("parallel", "parallel")),
        )(*rows, *params)

    def bwd_call(rows, params, cts):
        n_rows = rows[0].shape[0]
        b = min(bm, n_rows)
        row_specs, par_specs, row_blk, par_blk = _rowop_specs(rows, params, b, nl)
        ct_specs = [pl.BlockSpec((b, c.shape[1] // nl), lambda j, i: (i, j)) for c in cts]
        nr, npar, nct = len(rows), len(params), len(cts)

        def body(*refs):
            i = pl.program_id(1)
            vals = [r[...] for r in refs[:nr + npar]]
            ct_vals = tuple(r[...] for r in refs[nr + npar:nr + npar + nct])
            _, vjp = jax.vjp(lambda *v: tuple(f(*v)), *vals)
            grads = vjp(ct_vals)
            outs = refs[nr + npar + nct:]
            for o_ref, g in zip(outs[:nr], grads[:nr]):
                o_ref[...] = g
            for o_ref, g in zip(outs[nr:], grads[nr:]):
                @pl.when(i == 0)
                def _(o_ref=o_ref, g=g):
                    o_ref[...] = g

                @pl.when(i > 0)
                def _(o_ref=o_ref, g=g):
                    o_ref[...] += g

        outs = pl.pallas_call(
            body, name=name + "_bwd", grid=(nl, n_rows // b), in_specs=row_specs + par_specs + ct_specs,
            out_specs=row_specs + par_specs,
            out_shape=[jax.ShapeDtypeStruct(a.shape, a.dtype) for a in list(rows) + list(params)],
            compiler_params=_cparams(("arbitrary", "arbitrary")),
        )(*rows, *params, *cts)
        return tuple(outs[:nr]), tuple(outs[nr:])

    @jax.custom_vjp
    def op(rows, params):
        return tuple(fwd_call(rows, params))

    def fwd(rows, params):
        return op(rows, params), (rows, params)

    def bwd(res, cts):
        rows, params = res
        return bwd_call(rows, params, tuple(cts))

    op.defvjp(fwd, bwd)
    return op


def make_groupop(f, name, gb):
    def specs(arrs):
        return [pl.BlockSpec((gb,) + a.shape[1:], lambda g: (g, 0, 0)) for a in arrs]

    def fwd_call(arrs):
        g_n = arrs[0].shape[0]
        out_blk = jax.eval_shape(f, *[jax.ShapeDtypeStruct(a.shape[1:], a.dtype) for a in arrs])
        n = len(arrs)

        def body(*refs):
            for i in range(gb):
                outs = f(*[r[i] for r in refs[:n]])
                for o_ref, o in zip(refs[n:], outs):
                    o_ref[i] = o

        return pl.pallas_call(
            body, name=name, grid=(g_n // gb,), in_specs=specs(arrs),
            out_specs=[pl.BlockSpec((gb,) + o.shape, lambda g: (g, 0, 0)) for o in out_blk],
            out_shape=[jax.ShapeDtypeStruct((g_n,) + o.shape, o.dtype) for o in out_blk],
            compiler_params=_cparams(("parallel",)),
        )(*arrs)

    def bwd_call(arrs, cts):
        g_n = arrs[0].shape[0]
        n, nct = len(arrs), len(cts)

        def body(*refs):
            for i in range(gb):
                _, vjp = jax.vjp(lambda *v: tuple(f(*v)), *[r[i] for r in refs[:n]])
                grads = vjp(tuple(r[i] for r in refs[n:n + nct]))
                for o_ref, g in zip(refs[n + nct:], grads):
                    o_ref[i] = g

        return pl.pallas_call(
            body, name=name + "_bwd", grid=(g_n // gb,), in_specs=specs(arrs) + specs(cts), out_specs=specs(arrs),
            out_shape=[jax.ShapeDtypeStruct(a.shape, a.dtype) for a in arrs],
            compiler_params=_cparams(("parallel",)),
        )(*arrs, *cts)

    @jax.custom_vjp
    def op(*arrs):
        return tuple(fwd_call(arrs))

    def fwd(*arrs):
        return op(*arrs), arrs

    def bwd(arrs, cts):
        return tuple(bwd_call(arrs, tuple(cts)))

    op.defvjp(fwd, bwd)
    return op


def _rms(x, g):
    return x * lax.rsqrt(jnp.mean(x * x, axis=-1, keepdims=True) + EPS) * g


def _silu(z):
    return z * jax.nn.sigmoid(z)


def _log_sigmoid(x):
    return jnp.minimum(x, 0.0) - jnp.log(1.0 + jnp.exp(-jnp.abs(x)))


def f_norm(x, g):
    return (_rms(x, g),)


def f_gate_a1(y, u, d):
    return (jax.nn.gelu(y + d * u),)


def f_gate_a2(yg, t, z, b):
    return (yg * jax.nn.sigmoid(t + b) * _silu(z),)


def f_gate_b(att, z):
    return (att * _silu(z),)


def f_post_a(h, o, post_g, kv_g, pre_g):
    h1 = h + _rms(o, post_g)
    return h1, _rms(h1, kv_g), _rms(h1, pre_g)


def f_final(h, o, tgt, post_g):
    err = h + _rms(o, post_g) - tgt
    return (0.5 * jnp.mean(err * err, axis=-1, keepdims=True),)


def f_logf(gl, b):
    return (_log_sigmoid(gl + b),)


def f_memattn(q, zm, km, vm):
    s = lax.dot_general((q * (HEAD_DIM ** -0.5)).astype(BF16), km.astype(BF16), NT, preferred_element_type=F32)
    e = jnp.exp(s - jnp.max(s, axis=-1, keepdims=True))
    p = e / jnp.sum(e, axis=-1, keepdims=True)
    o = jnp.dot(p.astype(BF16), vm.astype(BF16), preferred_element_type=F32)
    return (o * _silu(zm),)


def _fox_block(row0, q, k, v, fc, fr):
    s = lax.dot_general((q * (HEAD_DIM ** -0.5)).astype(BF16), k.astype(BF16), NT, preferred_element_type=F32)
    s = s + fc - fr
    rows = lax.broadcasted_iota(jnp.int32, s.shape, 0) + row0
    cols = lax.broadcasted_iota(jnp.int32, s.shape, 1)
    s = jnp.where(rows >= cols, s, -1e30)
    e = jnp.exp(s - jnp.max(s, axis=-1, keepdims=True))
    p = e / jnp.sum(e, axis=-1, keepdims=True)
    return jnp.dot(p.astype(BF16), v.astype(BF16), preferred_element_type=F32)


def _cmul(ar, ai, xr, xi):
    return ar * xr - ai * xi, ar * xi + ai * xr


def _hp_dot(a, b):
    return jnp.dot(a, b, precision=HP, preferred_element_type=F32)


def _bf_dot(a, b):
    return jnp.dot(a.astype(BF16), b.astype(BF16), preferred_element_type=F32)


def f_s5(u, toep, win_r, win_i, wout_r, wout_i, coef):
    n_hi = u.shape[0] // SSM_LO
    y = _bf_dot(u, toep)
    sr, si = _bf_dot(u, win_r), _bf_dot(u, win_i)
    at_r, at_i = coef[0:1, :], coef[1:2, :]

    def lo_rows(a, lo):
        return a[lo * n_hi:(lo + 1) * n_hi]

    pr, pi = lo_rows(sr, 0), lo_rows(si, 0)
    for lo in range(1, SSM_LO):
        dr, di = _cmul(at_r, at_i, pr, pi)
        pr, pi = lo_rows(sr, lo) + dr, lo_rows(si, lo) + di
    ri = lax.broadcasted_iota(jnp.int32, (n_hi, n_hi), 0)
    ci = lax.broadcasted_iota(jnp.int32, (n_hi, n_hi), 1)
    d, step = 1, 0
    while d < n_hi:
        sh = (ri - ci == d).astype(F32)
        a_r, a_i = coef[2 + 2 * step:3 + 2 * step, :], coef[3 + 2 * step:4 + 2 * step, :]
        dr, di = _cmul(a_r, a_i, _hp_dot(sh, pr), _hp_dot(sh, pi))
        pr, pi = pr + dr, pi + di
        d, step = 2 * d, step + 1
    sh1 = (ri - ci == 1).astype(F32)
    xr, xi = [_hp_dot(sh1, pr)], [_hp_dot(sh1, pi)]
    for lo in range(1, SSM_LO):
        dr, di = _cmul(at_r, at_i, xr[-1], xi[-1])
        xr.append(lo_rows(sr, lo - 1) + dr)
        xi.append(lo_rows(si, lo - 1) + di)
    return (y + _bf_dot(jnp.concatenate(xr, axis=0), wout_r) + _bf_dot(jnp.concatenate(xi, axis=0), wout_i),)


def s5_operators(lam_re, lam_im, log_step, b_re, b_im, c_re, c_im, n_hi):
    t_n = SSM_T
    lr, li = lam_re, lam_im
    dt = jnp.exp(log_step)[:, None]
    mag = jnp.exp(lr * dt)
    ar, ai = mag * jnp.cos(li * dt), mag * jnp.sin(li * dt)
    den = lr * lr + li * li
    cr = ((ar - 1.0) * lr + ai * li) / den
    ci = (ai * lr - (ar - 1.0) * li) / den
    bbr = cr[..., None] * b_re - ci[..., None] * b_im
    bbi = cr[..., None] * b_im + ci[..., None] * b_re
    k = jnp.arange(t_n + 1, dtype=F32)[:, None, None]
    pm, ang = jnp.exp(k * (lr * dt)), k * (li * dt)
    pr, pi = pm * jnp.cos(ang), pm * jnp.sin(ang)
    abr = pr[..., None] * bbr - pi[..., None] * bbi
    abi = pr[..., None] * bbi + pi[..., None] * bbr
    kk = (jnp.einsum('ghp,kgpj->kghj', c_re, abr[:t_n], precision=HP)
          - jnp.einsum('ghp,kgpj->kghj', c_im, abi[:t_n], precision=HP))
    lag = jnp.arange(t_n)[None, :] - jnp.arange(t_n)[:, None]
    onehot = (lag[None] == jnp.arange(t_n)[:, None, None]).astype(F32)
    g_n, h_n = c_re.shape[0], c_re.shape[1]
    toep = jnp.einsum('kst,kghj->gsjth', onehot, kk, precision=HP).reshape(g_n, t_n * h_n, t_n * h_n)
    win_r = abr[:t_n][::-1].transpose(1, 0, 3, 2).reshape(g_n, t_n * h_n, -1)
    win_i = abi[:t_n][::-1].transpose(1, 0, 3, 2).reshape(g_n, t_n * h_n, -1)
    p1r, p1i = pr[1:, :, None, :], pi[1:, :, None, :]
    wout_r = (c_re[None] * p1r - c_im[None] * p1i).transpose(1, 3, 0, 2).reshape(g_n, -1, t_n * h_n)
    wout_i = (-(c_re[None] * p1i + c_im[None] * p1r)).transpose(1, 3, 0, 2).reshape(g_n, -1, t_n * h_n)
    rows = [pr[t_n], pi[t_n]]
    qr, qi = pr[t_n], pi[t_n]
    for _ in range(int(math.log2(SSM_LO))):
        qr, qi = qr * qr - qi * qi, 2.0 * qr * qi
    d = 1
    while d < n_hi:
        rows += [qr, qi]
        qr, qi = qr * qr - qi * qi, 2.0 * qr * qi
        d *= 2
    coef = jnp.stack(rows, axis=1)
    return toep, win_r, win_i, wout_r, wout_i, coef


def _cumsum_call(name, x, reverse):
    n_rows, w = x.shape
    bm = min(256, n_rows)
    nb = n_rows // bm

    def body(x_ref, o_ref, carry_ref):
        i = pl.program_id(0)

        @pl.when(i == 0)
        def _():
            carry_ref[...] = jnp.zeros_like(carry_ref)

        ri = lax.broadcasted_iota(jnp.int32, (bm, bm), 0)
        ci = lax.broadcasted_iota(jnp.int32, (bm, bm), 1)
        tri = ((ri <= ci) if reverse else (ri >= ci)).astype(F32)
        xb = x_ref[...]
        o_ref[...] = _hp_dot(tri, xb) + carry_ref[...]
        carry_ref[...] += jnp.sum(xb, axis=0, keepdims=True)

    idx = (lambda i: (nb - 1 - i, 0)) if reverse else (lambda i: (i, 0))
    return pl.pallas_call(
        body, name=name, grid=(nb,), in_specs=[pl.BlockSpec((bm, w), idx)], out_specs=pl.BlockSpec((bm, w), idx),
        out_shape=jax.ShapeDtypeStruct(x.shape, F32), scratch_shapes=[pltpu.VMEM((1, w), F32)],
        compiler_params=_cparams(("arbitrary",)),
    )(x)


def make_cumsum(name):
    @jax.custom_vjp
    def cs(x):
        return _cumsum_call(name, x, False)

    def fwd(x):
        return cs(x), None

    def bwd(_, dy):
        return (_cumsum_call(name + "_bwd", dy, True),)

    cs.defvjp(fwd, bwd)
    return cs


def _fox_specs(n_rows, bq):
    head = lambda h, i: (i, h)
    q_spec = pl.BlockSpec((bq, HEAD_DIM), head)
    kv_spec = pl.BlockSpec((n_rows, HEAD_DIM), lambda h, i: (0, h))
    fc_spec = pl.BlockSpec((None, bq, 1), lambda h, i: (h, i, 0))
    fr_spec = pl.BlockSpec((None, 1, n_rows), lambda h, i: (h, 0, 0))
    return q_spec, kv_spec, fc_spec, fr_spec


def _fox_fwd_call(name, q, k, v, fc, fr):
    n_rows, width = q.shape
    bq = min(FOX_BQ, n_rows)
    nq = n_rows // bq
    q_spec, kv_spec, fc_spec, fr_spec = _fox_specs(n_rows, bq)

    def body(q_ref, k_ref, v_ref, fc_ref, fr_ref, o_ref):
        i = pl.program_id(1)
        for p in range(nq):
            n_keys = (p + 1) * bq

            @pl.when(i == p)
            def _(p=p, n_keys=n_keys):
                o_ref[...] = _fox_block(p * bq, q_ref[...], k_ref[:n_keys, :], v_ref[:n_keys, :], fc_ref[...], fr_ref[:, :n_keys])

    return pl.pallas_call(
        body, name=name, grid=(width // HEAD_DIM, nq), in_specs=[q_spec, kv_spec, kv_spec, fc_spec, fr_spec], out_specs=q_spec,
        out_shape=jax.ShapeDtypeStruct(q.shape, F32), compiler_params=_cparams(("parallel", "parallel")),
    )(q, k, v, fc, fr)


def _fox_bwd_call(name, q, k, v, fc, fr, do):
    n_rows, width = q.shape
    bq = min(FOX_BQ, n_rows)
    nq = n_rows // bq
    q_spec, kv_spec, fc_spec, fr_spec = _fox_specs(n_rows, bq)

    def body(q_ref, k_ref, v_ref, fc_ref, fr_ref, do_ref, dq_ref, dk_ref, dv_ref, dfc_ref, dfr_ref):
        i = pl.program_id(1)

        @pl.when(i == 0)
        def _():
            dk_ref[...] = jnp.zeros_like(dk_ref)
            dv_ref[...] = jnp.zeros_like(dv_ref)
            dfr_ref[...] = jnp.zeros_like(dfr_ref)

        for p in range(nq):
            n_keys = (p + 1) * bq

            @pl.when(i == p)
            def _(p=p, n_keys=n_keys):
                _, vjp = jax.vjp(functools.partial(_fox_block, p * bq), q_ref[...], k_ref[:n_keys, :], v_ref[:n_keys, :],
                                 fc_ref[...], fr_ref[:, :n_keys])
                dq, dk, dv, dfc, dfr = vjp(do_ref[...])
                dq_ref[...] = dq
                dfc_ref[...] = dfc
                dk_ref[:n_keys, :] += dk
                dv_ref[:n_keys, :] += dv
                dfr_ref[:, :n_keys] += dfr

    return pl.pallas_call(
        body, name=name, grid=(width // HEAD_DIM, nq), in_specs=[q_spec, kv_spec, kv_spec, fc_spec, fr_spec, q_spec],
        out_specs=[q_spec, kv_spec, kv_spec, fc_spec, fr_spec],
        out_shape=[jax.ShapeDtypeStruct(a.shape, F32) for a in (q, k, v, fc, fr)],
        compiler_params=_cparams(("parallel", "arbitrary")),
    )(q, k, v, fc, fr, do)


def make_fox(name):
    @jax.custom_vjp
    def fox(q, k, v, fc, fr):
        return _fox_fwd_call(name, q, k, v, fc, fr)

    def fwd(q, k, v, fc, fr):
        return fox(q, k, v, fc, fr), (q, k, v, fc, fr)

    def bwd(res, do):
        return tuple(_fox_bwd_call(name + "_bwd", *res, do))

    fox.defvjp(fwd, bwd)
    return fox


def _exchange(name, src, gather):
    shard_shape = src.shape if gather else src.shape[1:]

    def body(src_ref, out_ref, send_sems, recv_sems, local_sem):
        x, y, c = lax.axis_index("x"), lax.axis_index("y"), lax.axis_index("c")
        me = 4 * x + 2 * y + c
        local = pltpu.make_async_copy(src_ref if gather else src_ref.at[me], out_ref.at[me], local_sem)
        local.start()
        sends, recvs = [], []
        for rel in range(1, N_DEV):
            px = 1 - x if rel & 4 else x
            py = 1 - y if rel & 2 else y
            pc = 1 - c if rel & 1 else c
            peer = 4 * px + 2 * py + pc
            sem = dict(send_sem=send_sems.at[rel - 1], recv_sem=recv_sems.at[rel - 1], device_id=(px, py, pc),
                       device_id_type=pl.DeviceIdType.MESH)
            send = pltpu.make_async_remote_copy(src_ref=src_ref if gather else src_ref.at[peer], dst_ref=out_ref.at[me], **sem)
            send.start()
            sends.append(send)
            recvs.append(pltpu.make_async_remote_copy(src_ref=src_ref if gather else src_ref.at[peer], dst_ref=out_ref.at[peer], **sem))
        for send, recv in zip(sends, recvs):
            recv.wait_recv()
            send.wait_send()
        local.wait()

    return pl.pallas_call(
        body, name=name, in_specs=[pl.BlockSpec(memory_space=pl.ANY)], out_specs=pl.BlockSpec(memory_space=pl.ANY),
        out_shape=jax.ShapeDtypeStruct((N_DEV,) + tuple(shard_shape), src.dtype),
        scratch_shapes=[pltpu.SemaphoreType.DMA((N_DEV - 1,)), pltpu.SemaphoreType.DMA((N_DEV - 1,)), pltpu.SemaphoreType.DMA],
    )(src)


def _relations(x, y, c):
    out = []
    for rel in range(1, N_DEV):
        px = 1 - x if rel & 4 else x
        py = 1 - y if rel & 2 else y
        pc = 1 - c if rel & 1 else c
        out.append((rel, (px, py, pc), 4 * px + 2 * py + pc))
    return out


_HBM_SPEC = pl.BlockSpec(memory_space=pltpu.HBM)
_SEM_SPEC = pl.BlockSpec(memory_space=pltpu.SEMAPHORE)
_DATAFLOW = pltpu.SideEffectType.DATAFLOW_SIDE_EFFECTING


def exchange_start(name, srcs, gather):
    n = len(srcs)
    me = 4 * lax.axis_index("x") + 2 * lax.axis_index("y") + lax.axis_index("c")
    lands = []
    for s in srcs:
        own = s if gather else lax.dynamic_index_in_dim(s, me, 0, keepdims=False)
        land = lax.empty((N_DEV,) + own.shape, s.dtype)
        lands.append(lax.dynamic_update_index_in_dim(land, own, me, 0))

    def body(*refs):
        src_refs, land_refs = refs[:n], refs[n:2 * n]
        send_sems, recv_sems, token = refs[2 * n], refs[2 * n + 1], refs[-1]
        x, y, c = lax.axis_index("x"), lax.axis_index("y"), lax.axis_index("c")
        mine = 4 * x + 2 * y + c
        for k in range(n):
            for rel, peer, peer_blk in _relations(x, y, c):
                pltpu.make_async_remote_copy(
                    src_ref=src_refs[k] if gather else src_refs[k].at[peer_blk], dst_ref=land_refs[k].at[mine],
                    send_sem=send_sems.at[7 * k + rel - 1], recv_sem=recv_sems.at[7 * k + rel - 1],
                    device_id=peer, device_id_type=pl.DeviceIdType.MESH).start()
        token[...] = jnp.zeros_like(token)

    sem = pltpu.SemaphoreType.DMA((7 * n,))
    hbm = lambda t: pltpu.HBM(t.shape, t.dtype)
    outs = pl.pallas_call(
        body, name=name,
        out_shape=(sem, sem, *[hbm(s) for s in srcs], *[hbm(l) for l in lands], jax.ShapeDtypeStruct((8, LANE), F32)),
        in_specs=[_HBM_SPEC] * (2 * n),
        out_specs=(_SEM_SPEC, _SEM_SPEC, *[_HBM_SPEC] * (2 * n), pl.BlockSpec(memory_space=pltpu.VMEM)),
        input_output_aliases={i: 2 + i for i in range(2 * n)},
        compiler_params=pltpu.CompilerParams(has_side_effects=_DATAFLOW),
    )(*[pltpu.with_memory_space_constraint(t, pltpu.HBM) for t in list(srcs) + lands])
    return dict(name=name, gather=gather, send=outs[0], recv=outs[1], srcs=list(outs[2:2 + n]), lands=list(outs[2 + n:2 + 2 * n]), token=outs[-1])


def exchange_wait(handle, k, after):
    gather = handle['gather']

    def body(src_ref, land_ref, send_sems, recv_sems, after_ref, src_out, land_out):
        x, y, c = lax.axis_index("x"), lax.axis_index("y"), lax.axis_index("c")
        for rel, peer, peer_blk in _relations(x, y, c):
            copy = pltpu.make_async_remote_copy(
                src_ref=src_ref if gather else src_ref.at[peer_blk], dst_ref=land_ref.at[peer_blk],
                send_sem=send_sems.at[7 * k + rel - 1], recv_sem=recv_sems.at[7 * k + rel - 1],
                device_id=peer, device_id_type=pl.DeviceIdType.MESH)
            copy.wait_send()
            copy.wait_recv()

    src, land = handle['srcs'][k], handle['lands'][k]
    return pl.pallas_call(
        body, name=f"{handle['name']}_wait{k}", out_shape=(pltpu.HBM(src.shape, src.dtype), pltpu.HBM(land.shape, land.dtype)),
        in_specs=[_HBM_SPEC, _HBM_SPEC, _SEM_SPEC, _SEM_SPEC, pl.BlockSpec(memory_space=pl.ANY)], out_specs=(_HBM_SPEC, _HBM_SPEC),
        input_output_aliases={0: 0, 1: 1}, compiler_params=pltpu.CompilerParams(has_side_effects=_DATAFLOW),
    )(src, land, handle['send'], handle['recv'], after)[1]


def all_gather_blocks(name, shard):
    return _exchange(name, shard, True)


def scatter_blocks(name, blocks):
    return _exchange(name, blocks, False)


def adamw_reduce(name, partials, w, m, v):
    n_part, n_rows, n_cols = partials.shape
    br = n_rows
    for cand in (512, 256, 128, 64, 32, 16, 8):
        if n_rows % cand == 0 and n_part * cand * n_cols * partials.dtype.itemsize <= (4 << 20):
            br = cand
            break

    def body(p_ref, w_ref, m_ref, v_ref, g_ref, d_ref, nm_ref, nv_ref):
        g = p_ref[0].astype(F32)
        for s in range(1, n_part):
            g = g + p_ref[s].astype(F32)
        m_new = ADAM_B1 * m_ref[...] + (1.0 - ADAM_B1) * g
        v_new = ADAM_B2 * v_ref[...] + (1.0 - ADAM_B2) * jnp.square(g)
        m_hat = m_new / (1.0 - ADAM_B1 ** ADAM_STEP)
        v_hat = v_new / (1.0 - ADAM_B2 ** ADAM_STEP)
        g_ref[...] = g
        d_ref[...] = -ADAM_LR * (m_hat / (jnp.sqrt(v_hat) + ADAM_EPS) + ADAM_WD * w_ref[...])
        nm_ref[...] = m_new
        nv_ref[...] = v_new

    spec = pl.BlockSpec((br, n_cols), lambda i: (i, 0))
    return pl.pallas_call(
        body, name=name, grid=(n_rows // br,), in_specs=[pl.BlockSpec((n_part, br, n_cols), lambda i: (0, i, 0)), spec, spec, spec],
        out_specs=[spec] * 4, out_shape=[jax.ShapeDtypeStruct((n_rows, n_cols), F32)] * 4, compiler_params=_cparams(("parallel",)),
    )(partials, w, m, v)


def _pack(arrays, n_rows):
    flat = jnp.concatenate([a.reshape(-1) for a in arrays])
    return jnp.pad(flat, (0, n_rows * LANE - flat.shape[0])).reshape(n_rows, LANE)


def _unpack(packed, shapes):
    flat, out, off = packed.reshape(-1), [], 0
    for s in shapes:
        n = math.prod(s)
        out.append(flat[off:off + n].reshape(s))
        off += n
    return out


def _packed_rows(shapes):
    n = sum(math.prod(s) for s in shapes)
    return -(-n // (LANE * 512)) * 512


GATHER_ORDER = ['in_a', 'mem_kv0', 'glu', 'out0', 'kv', 'in_b', 'mem_kv1', 'out1']


BLOCKED = ('in_a', 'kv', 'in_b')


def local_step(p, gathered, x, mem, tgt):
    n_rows, d_model = x.shape
    main_w = 3 * d_model // 4
    mem_w = d_model - main_w
    n_groups = main_w // SSM_GROUP
    n_heads = main_w // HEAD_DIM
    n_hi = n_rows // (SSM_T * SSM_LO)
    row = lambda a: a.reshape(1, -1)
    tape, pending = {}, {}

    def mm_fwd(name, act):
        w = exchange_wait(gathered, GATHER_ORDER.index(name), act)
        if name not in BLOCKED:
            w = w.reshape(-1, w.shape[2])
        tape[name] = (act, w)
        return (_mm_nn_cb if name in BLOCKED else _mm_nn)(name, act, w)

    def mm_bwd(name, dy):
        act, w = tape[name]
        if name in BLOCKED:
            dw = _mm_tn_cb(name + "_dw", act, dy, w.shape[2], GRAD_DTYPE)
        else:
            dw = _mm_tn(name + "_dw", act, dy, GRAD_DTYPE)
            dw = dw.reshape(N_DEV, dw.shape[0] // N_DEV, dw.shape[1])
        pending[name] = exchange_start("rs_" + name, [dw], gather=False)
        return (_mm_nt_cb if name in BLOCKED else _mm_nt)(name + "_da", dy, w, pending[name]['token'])

    def split(proj):
        return proj[:, :main_w], proj[:, main_w:2 * main_w], proj[:, 2 * main_w:2 * main_w + mem_w], proj[:, 2 * main_w + mem_w:]

    def mem_attn(i, qm, zm, kvm):
        memo, = make_rowop(f_memattn, f"mem_attn{i}", 512, nl=mem_w // HEAD_DIM)((qm, zm), (kvm[:, :mem_w], kvm[:, mem_w:]))
        return memo

    def seg_norms(x_, pre_g0, mem_g0, mem_g1):
        hn, = make_rowop(f_norm, "pre_norm0", 256)((x_,), (row(pre_g0),))
        memn0, = make_rowop(f_norm, "mem_norm0", 256)((mem,), (row(mem_g0),))
        memn1, = make_rowop(f_norm, "mem_norm1", 256)((mem,), (row(mem_g1),))
        return hn, memn0, memn1

    def seg_a1(proj, kvm, lam_re, lam_im, log_step, b_re, b_im, c_re, c_im, d_skip):
        u, z, qm, zm = split(proj)
        ops = s5_operators(lam_re, lam_im, log_step, b_re, b_im, c_re, c_im, n_hi)
        ug = u.astype(BF16).reshape(n_hi, SSM_LO, SSM_T, n_groups, SSM_GROUP).transpose(3, 1, 0, 2, 4).reshape(n_groups, n_hi * SSM_LO, SSM_T * SSM_GROUP)
        yg, = make_groupop(f_s5, "s5", SSM_GB)(ug, *ops)
        y = yg.reshape(n_groups, SSM_LO, n_hi, SSM_T, SSM_GROUP).transpose(2, 1, 3, 0, 4).reshape(n_rows, main_w)
        ygelu, = make_rowop(f_gate_a1, "gate_a1", 256)((y, u), (row(d_skip),))
        return ygelu, z, mem_attn(0, qm, zm, kvm)

    def seg_a2(ygelu, t, z, memo, b_glu):
        main, = make_rowop(f_gate_a2, "gate_a2", 256)((ygelu, t, z), (row(b_glu),))
        return jnp.concatenate([main, memo], axis=1)

    def seg_post_a(x_, o, post_g0, kv_g, pre_g1):
        return make_rowop(f_post_a, "post_a", 128)((x_, o), (row(post_g0), row(kv_g), row(pre_g1)))

    def seg_b(proj, kv, kv_in, kvm, w_fgate, b_fgate):
        w_fg = jnp.pad(w_fgate, ((0, 0), (0, LANE - n_heads)))
        b_fg = jnp.pad(b_fgate, (0, LANE - n_heads)).reshape(1, LANE)
        logf, = make_rowop(f_logf, "logf", 512)((make_mm("fgate")(kv_in, w_fg),), (b_fg,))
        fcum = make_cumsum("fcum")(logf)[:, :n_heads].T
        q, z, qm, zm = split(proj)
        att = make_fox("fox")(q, kv[:, :main_w], kv[:, main_w:], fcum[:, :, None], fcum[:, None, :])
        main, = make_rowop(f_gate_b, "gate_b", 256)((att, z), ())
        return jnp.concatenate([main, mem_attn(1, qm, zm, kvm)], axis=1)

    def seg_final(h1, o, post_g1):
        rowloss, = make_rowop(f_final, "final", 128)((h1, o, tgt), (row(post_g1),))
        return jnp.sum(rowloss)

    (hn0, memn0, memn1), vjp_norms = jax.vjp(seg_norms, x, p['pre_norm_g'][0], p['mem_norm_g'][0], p['mem_norm_g'][1])
    proj_a = mm_fwd("in_a", hn0)
    kvm0 = mm_fwd("mem_kv0", memn0)
    s5_names = ('lam_re', 'lam_im', 'log_step', 'b_re', 'b_im', 'c_re', 'c_im', 'd_skip')
    (ygelu, z_a, memo0), vjp_a1 = jax.vjp(seg_a1, proj_a, kvm0, *[p[n] for n in s5_names])
    t = mm_fwd("glu", ygelu)
    cat0, vjp_a2 = jax.vjp(seg_a2, ygelu, t, z_a, memo0, p['b_glu'])
    o0 = mm_fwd("out0", cat0)
    (h1, kv_in, hn1), vjp_post_a = jax.vjp(seg_post_a, x, o0, p['post_norm_g'][0], p['kv_norm_g'], p['pre_norm_g'][1])
    kv = mm_fwd("kv", kv_in)
    proj_b = mm_fwd("in_b", hn1)
    kvm1 = mm_fwd("mem_kv1", memn1)
    cat1, vjp_b = jax.vjp(seg_b, proj_b, kv, kv_in, kvm1, p['w_fgate'], p['b_fgate'])
    o1 = mm_fwd("out1", cat1)
    loss, vjp_final = jax.vjp(seg_final, h1, o1, p['post_norm_g'][1])

    g = {}
    d_h1, d_o1, g_post_g1 = vjp_final(jnp.ones((), F32))
    d_proj_b, d_kv, d_kv_in, d_kvm1, g['w_fgate'], g['b_fgate'] = vjp_b(mm_bwd("out1", d_o1))
    d_memn1 = mm_bwd("mem_kv1", d_kvm1)
    d_hn1 = mm_bwd("in_b", d_proj_b)
    d_kv_in = d_kv_in + mm_bwd("kv", d_kv)
    d_x, d_o0, g_post_g0, g['kv_norm_g'], g_pre_g1 = vjp_post_a((d_h1, d_kv_in, d_hn1))
    d_ygelu, d_t, d_z, d_memo0, g['b_glu'] = vjp_a2(mm_bwd("out0", d_o0))
    d_ygelu = d_ygelu + mm_bwd("glu", d_t)
    d_proj_a, d_kvm0, *g_s5 = vjp_a1((d_ygelu, d_z, d_memo0))
    g.update(zip(s5_names, g_s5))
    d_memn0 = mm_bwd("mem_kv0", d_kvm0)
    d_x2, g_pre_g0, g_mem_g0, g_mem_g1 = vjp_norms((mm_bwd("in_a", d_proj_a), d_memn0, d_memn1))
    g['pre_norm_g'] = jnp.stack([g_pre_g0, g_pre_g1])
    g['post_norm_g'] = jnp.stack([g_post_g0, g_post_g1])
    g['mem_norm_g'] = jnp.stack([g_mem_g0, g_mem_g1])
    return loss, g, d_x + d_x2, pending


def kernel(x, mem, pre_norm_g, post_norm_g, w_in_a, lam_re, lam_im, log_step, b_re, b_im, c_re, c_im, d_skip, w_glu, b_glu, kv_norm_g, w_kv, w_fgate, b_fgate, w_in_b, mem_norm_g, w_mem_kv, w_out, loss_target, m_pre_norm_g, m_post_norm_g, m_w_in_a, m_lam_re, m_lam_im, m_log_step, m_b_re, m_b_im, m_c_re, m_c_im, m_d_skip, m_w_glu, m_b_glu, m_kv_norm_g, m_w_kv, m_w_fgate, m_b_fgate, m_w_in_b, m_mem_norm_g, m_w_mem_kv, m_w_out, v_pre_norm_g, v_post_norm_g, v_w_in_a, v_lam_re, v_lam_im, v_log_step, v_b_re, v_b_im, v_c_re, v_c_im, v_d_skip, v_w_glu, v_b_glu, v_kv_norm_g, v_w_kv, v_w_fgate, v_b_fgate, v_w_in_b, v_mem_norm_g, v_w_mem_kv, v_w_out):
    a = dict(zip(INPUTS, (x, mem, pre_norm_g, post_norm_g, w_in_a, lam_re, lam_im, log_step, b_re, b_im, c_re, c_im, d_skip, w_glu, b_glu, kv_norm_g, w_kv, w_fgate, b_fgate, w_in_b, mem_norm_g, w_mem_kv, w_out, loss_target, m_pre_norm_g, m_post_norm_g, m_w_in_a, m_lam_re, m_lam_im, m_log_step, m_b_re, m_b_im, m_c_re, m_c_im, m_d_skip, m_w_glu, m_b_glu, m_kv_norm_g, m_w_kv, m_w_fgate, m_b_fgate, m_w_in_b, m_mem_norm_g, m_w_mem_kv, m_w_out, v_pre_norm_g, v_post_norm_g, v_w_in_a, v_lam_re, v_lam_im, v_log_step, v_b_re, v_b_im, v_c_re, v_c_im, v_d_skip, v_w_glu, v_b_glu, v_kv_norm_g, v_w_kv, v_w_fgate, v_b_fgate, v_w_in_b, v_mem_norm_g, v_w_mem_kv, v_w_out)))
    me = 4 * lax.axis_index("x") + 2 * lax.axis_index("y") + lax.axis_index("c")
    n_layers = w_out.shape[0]

    small_shapes = [a[n].shape for n in SMALL_SHARDED]
    small_rows = -(-sum(math.prod(s) for s in small_shapes) // (LANE * 8)) * 8
    small = all_gather_blocks("ag_small", _pack([a[n] for n in SMALL_SHARDED], small_rows))
    small = [jnp.stack(parts) for parts in zip(*[_unpack(small[b], small_shapes) for b in range(N_DEV)])]
    p = {n: a[n] for n in REPLICATED}
    for n in ('lam_re', 'lam_im', 'log_step', 'b_re', 'b_im', 'c_re', 'c_im'):
        p[n] = p[n][0]
    p['d_skip'] = small[0].reshape(-1)
    p['b_glu'] = small[1].reshape(-1)
    p['w_fgate'] = small[2].reshape(-1, w_fgate.shape[1])
    shards = dict(in_a=w_in_a[0], mem_kv0=w_mem_kv[0], glu=w_glu[0], out0=w_out[0], kv=w_kv, in_b=w_in_b[0], mem_kv1=w_mem_kv[1], out1=w_out[1])
    gathered = exchange_start("ag_weights", [shards[n].astype(BF16) for n in GATHER_ORDER], gather=True)
    p['pre_norm_g'] = p['pre_norm_g'] + gathered['token'][0, 0]

    loss_local, g, grad_x, pending = local_step(p, gathered, x[0], mem[0], loss_target[0])
    loss = lax.psum(loss_local, MESH_AXES)

    out = {}

    def update(tag, name, layer=None):
        pick = (lambda t: t) if layer is None else (lambda t: t[layer])
        parts = exchange_wait(pending[tag], 0, grad_x)
        two_d = lambda t: pick(t).reshape(parts.shape[1:])
        res = adamw_reduce("adamw_" + tag, parts, two_d(a[name]), two_d(a['m_' + name]), two_d(a['v_' + name]))
        return [r.reshape(pick(a[name]).shape) for r in res]

    per_layer = {n: [None] * n_layers for n in ('w_out', 'w_mem_kv')}
    per_layer['w_out'][1] = update("out1", 'w_out', 1)
    per_layer['w_mem_kv'][1] = update("mem_kv1", 'w_mem_kv', 1)
    out['w_in_b'] = [r[None] for r in update("in_b", 'w_in_b', 0)]
    out['w_kv'] = update("kv", 'w_kv')
    per_layer['w_out'][0] = update("out0", 'w_out', 0)
    per_layer['w_mem_kv'][0] = update("mem_kv0", 'w_mem_kv', 0)
    out['w_glu'] = [r[None] for r in update("glu", 'w_glu', 0)]
    out['w_in_a'] = [r[None] for r in update("in_a", 'w_in_a', 0)]
    for name in per_layer:
        out[name] = [jnp.stack(t) for t in zip(*per_layer[name])]

    full_shapes = [a[n].shape for n in REPLICATED] + [(1, N_DEV * d_skip.shape[1]), (1, N_DEV * b_glu.shape[1]), (N_DEV * w_fgate.shape[0], w_fgate.shape[1])]
    rows = _packed_rows(full_shapes)
    zeros = [jnp.zeros(s, F32) for s in full_shapes[len(REPLICATED):]]
    g_packed = all_gather_blocks("ag_grads", _pack([g[n] for n in REPLICATED + SMALL_SHARDED], rows))
    packed = lambda pre: _pack([a[pre + n] for n in REPLICATED] + zeros, rows)
    res = [_unpack(r, full_shapes) for r in adamw_reduce("adamw_small", g_packed, packed(''), packed('m_'), packed('v_'))]
    for i, n in enumerate(REPLICATED):
        out[n] = [r[i] for r in res]
    g_full = res[0][len(REPLICATED):]
    g_shard = [lax.dynamic_slice_in_dim(g_full[0], me * d_skip.shape[1], d_skip.shape[1], 1),
               lax.dynamic_slice_in_dim(g_full[1], me * b_glu.shape[1], b_glu.shape[1], 1),
               lax.dynamic_slice_in_dim(g_full[2], me * w_fgate.shape[0], w_fgate.shape[0], 0)]
    packed = lambda pre: _pack([a[pre + n] for n in SMALL_SHARDED], small_rows)
    res = [_unpack(r, small_shapes) for r in adamw_reduce("adamw_small_sharded", _pack(g_shard, small_rows)[None], packed(''), packed('m_'), packed('v_'))]
    for i, n in enumerate(SMALL_SHARDED):
        out[n] = [r[i] for r in res]

    return (loss, grad_x[None], *[out[n][k] for k in range(4) for n in WEIGHTS])
```

```python
import functools
import math

import jax
import jax.numpy as jnp
from jax import lax
from jax.experimental import pallas as pl
from jax.experimental.pallas import tpu as pltpu

F32 = jnp.float32
BF16 = jnp.bfloat16
HP = lax.Precision.HIGHEST
MESH_AXES = ("x", "y", "c")
N_DEV = 8
V7X_VMEM_LIMIT = 56 * 1024 * 1024
LANE = 128

EPS = 1e-6
HEAD_DIM = 128
SSM_GROUP = 16
SSM_STATE = 64
SSM_T = 16
SSM_LO = 8
SSM_GB = 4
FOX_BQ = 256
GRAD_DTYPE = BF16
ADAM_LR = 0.001
ADAM_B1 = 0.9
ADAM_B2 = 0.999
ADAM_EPS = 1e-08
ADAM_WD = 0.01
ADAM_STEP = 10

WEIGHTS = ['pre_norm_g', 'post_norm_g', 'w_in_a', 'lam_re', 'lam_im', 'log_step', 'b_re', 'b_im', 'c_re', 'c_im', 'd_skip',
           'w_glu', 'b_glu', 'kv_norm_g', 'w_kv', 'w_fgate', 'b_fgate', 'w_in_b', 'mem_norm_g', 'w_mem_kv', 'w_out']
INPUTS = ['x', 'mem'] + WEIGHTS + ['loss_target'] + ['m_' + n for n in WEIGHTS] + ['v_' + n for n in WEIGHTS]
REPLICATED = ['pre_norm_g', 'post_norm_g', 'lam_re', 'lam_im', 'log_step', 'b_re', 'b_im', 'c_re', 'c_im', 'kv_norm_g',
              'b_fgate', 'mem_norm_g']
SMALL_SHARDED = ['d_skip', 'b_glu', 'w_fgate']


def _cparams(sem=None):
    return pltpu.CompilerParams(dimension_semantics=sem, vmem_limit_bytes=V7X_VMEM_LIMIT)


def _tile(n, cap):
    if n <= cap:
        return n
    best = None
    for t in range(LANE, cap + 1, LANE):
        if n % t == 0:
            best = t
    assert best is not None, (n, cap)
    return best


def _matmul(name, a, b, a_spec, b_spec, o_spec, out_shape, grid, dims, nk, after=None):
    def body(a_ref, b_ref, *rest):
        o_ref, acc_ref = rest[-2:]
        k = pl.program_id(2)
        part = lax.dot_general(a_ref[...].astype(BF16), b_ref[...].astype(BF16), dims, preferred_element_type=F32)

        @pl.when(k == 0)
        def _():
            acc_ref[...] = part

        @pl.when(k > 0)
        def _():
            acc_ref[...] += part

        @pl.when(k == nk - 1)
        def _():
            o_ref[...] = acc_ref[...].astype(o_ref.dtype)

    acc_shape = tuple(d for d in o_spec.block_shape if d is not None)
    extra = [] if after is None else [after]
    return pl.pallas_call(
        body, name=name, grid=grid, in_specs=[a_spec, b_spec] + [pl.BlockSpec(memory_space=pl.ANY)] * len(extra), out_specs=o_spec,
        out_shape=out_shape, scratch_shapes=[pltpu.VMEM(acc_shape, F32)],
        compiler_params=_cparams(("parallel", "parallel", "arbitrary")),
    )(a, b, *extra)


NN = (((1,), (0,)), ((), ()))
NT = (((1,), (1,)), ((), ()))
TN = (((0,), (0,)), ((), ()))


def _mm_nn(name, a, b):
    (m, k), (_, n) = a.shape, b.shape
    bm, bn, bk = _tile(m, 512), _tile(n, 512), _tile(k, 2048)
    return _matmul(name, a, b, pl.BlockSpec((bm, bk), lambda i, j, kk: (i, kk)), pl.BlockSpec((bk, bn), lambda i, j, kk: (kk, j)),
                   pl.BlockSpec((bm, bn), lambda i, j, kk: (i, j)), jax.ShapeDtypeStruct((m, n), F32), (m // bm, n // bn, k // bk), NN, k // bk)


def _mm_nt(name, a, b, after=None):
    (m, c), (n, _) = a.shape, b.shape
    bm, bn, bk = _tile(m, 512), _tile(n, 512), _tile(c, 2048)
    return _matmul(name, a, b, pl.BlockSpec((bm, bk), lambda i, j, kk: (i, kk)), pl.BlockSpec((bn, bk), lambda i, j, kk: (j, kk)),
                   pl.BlockSpec((bm, bn), lambda i, j, kk: (i, j)), jax.ShapeDtypeStruct((m, n), F32), (m // bm, n // bn, c // bk), NT, c // bk, after)


def _mm_tn(name, a, b, out_dtype=F32, after=None):
    (c, m), (_, n) = a.shape, b.shape
    bm, bn, bk = _tile(m, 512), _tile(n, 512), _tile(c, 2048)
    return _matmul(name, a, b, pl.BlockSpec((bk, bm), lambda i, j, kk: (kk, i)), pl.BlockSpec((bk, bn), lambda i, j, kk: (kk, j)),
                   pl.BlockSpec((bm, bn), lambda i, j, kk: (i, j)), jax.ShapeDtypeStruct((m, n), out_dtype), (m // bm, n // bn, c // bk), TN, c // bk, after)


def _mm_nn_cb(name, a, bb):
    (m, k), (nb, _, ns) = a.shape, bb.shape
    bm, bk = _tile(m, 512), _tile(k, 2048)
    return _matmul(name, a, bb, pl.BlockSpec((bm, bk), lambda i, j, kk: (i, kk)), pl.BlockSpec((None, bk, ns), lambda i, j, kk: (j, kk, 0)),
                   pl.BlockSpec((bm, ns), lambda i, j, kk: (i, j)), jax.ShapeDtypeStruct((m, nb * ns), F32), (m // bm, nb, k // bk), NN, k // bk)


def _mm_nt_cb(name, dy, bb, after=None):
    m, (nb, k, ns) = dy.shape[0], bb.shape
    bm, bn = _tile(m, 1024), _tile(k, 1024)
    return _matmul(name, dy, bb, pl.BlockSpec((bm, ns), lambda i, j, kk: (i, kk)), pl.BlockSpec((None, bn, ns), lambda i, j, kk: (kk, j, 0)),
                   pl.BlockSpec((bm, bn), lambda i, j, kk: (i, j)), jax.ShapeDtypeStruct((m, k), F32), (m // bm, k // bn, nb), NT, nb, after)


def _mm_tn_cb(name, a, dy, ns, out_dtype=F32, after=None):
    (c, k), nb = a.shape, dy.shape[1] // ns
    bm = _tile(k, 512)
    return _matmul(name, a, dy, pl.BlockSpec((c, bm), lambda i, j, kk: (0, i)), pl.BlockSpec((c, ns), lambda i, j, kk: (0, j)),
                   pl.BlockSpec((None, bm, ns), lambda i, j, kk: (j, i, 0)), jax.ShapeDtypeStruct((nb, k, ns), out_dtype), (k // bm, nb, 1), TN, 1, after)


def make_mm(name):
    @jax.custom_vjp
    def mm(a, b):
        return _mm_nn(name, a, b)

    def fwd(a, b):
        return mm(a, b), (a, b)

    def bwd(res, dy):
        a, b = res
        return _mm_nt(name + "_da", dy, b), _mm_tn(name + "_dw", a, dy)

    mm.defvjp(fwd, bwd)
    return mm


def _rowop_specs(rows, params, bm, nl):
    row_specs = [pl.BlockSpec((bm, r.shape[1] // nl), lambda j, i: (i, j)) for r in rows]
    par_specs = [pl.BlockSpec((p.shape[0], p.shape[1] // nl), lambda j, i: (0, j)) for p in params]
    row_blk = [jax.ShapeDtypeStruct((bm, r.shape[1] // nl), r.dtype) for r in rows]
    par_blk = [jax.ShapeDtypeStruct((p.shape[0], p.shape[1] // nl), p.dtype) for p in params]
    return row_specs, par_specs, row_blk, par_blk


def make_rowop(f, name, bm, nl=1):
    def fwd_call(rows, params):
        n_rows = rows[0].shape[0]
        b = min(bm, n_rows)
        row_specs, par_specs, row_blk, par_blk = _rowop_specs(rows, params, b, nl)
        out_blk = jax.eval_shape(f, *row_blk, *par_blk)
        nr, npar = len(rows), len(params)

        def body(*refs):
            outs = f(*[r[...] for r in refs[:nr + npar]])
            for o_ref, o in zip(refs[nr + npar:], outs):
                o_ref[...] = o.astype(o_ref.dtype)

        return pl.pallas_call(
            body, name=name, grid=(nl, n_rows // b), in_specs=row_specs + par_specs,
            out_specs=[pl.BlockSpec(o.shape, lambda j, i: (i, j)) for o in out_blk],
            out_shape=[jax.ShapeDtypeStruct((n_rows, o.shape[1] * nl), o.dtype) for o in out_blk],
            compiler_params=_cparams(("parallel", "parallel")),
        )(*rows, *params)

    def bwd_call(rows, params, cts):
        n_rows = rows[0].shape[0]
        b = min(bm, n_rows)
        row_specs, par_specs, row_blk, par_blk = _rowop_specs(rows, params, b, nl)
        ct_specs = [pl.BlockSpec((b, c.shape[1] // nl), lambda j, i: (i, j)) for c in cts]
        nr, npar, nct = len(rows), len(params), len(cts)

        def body(*refs):
            i = pl.program_id(1)
            vals = [r[...] for r in refs[:nr + npar]]
            ct_vals = tuple(r[...] for r in refs[nr + npar:nr + npar + nct])
            _, vjp = jax.vjp(lambda *v: tuple(f(*v)), *vals)
            grads = vjp(ct_vals)
            outs = refs[nr + npar + nct:]
            for o_ref, g in zip(outs[:nr], grads[:nr]):
                o_ref[...] = g
            for o_ref, g in zip(outs[nr:], grads[nr:]):
                @pl.when(i == 0)
                def _(o_ref=o_ref, g=g):
                    o_ref[...] = g

                @pl.when(i > 0)
                def _(o_ref=o_ref, g=g):
                    o_ref[...] += g

        outs = pl.pallas_call(
            body, name=name + "_bwd", grid=(nl, n_rows // b), in_specs=row_specs + par_specs + ct_specs,
            out_specs=row_specs + par_specs,
            out_shape=[jax.ShapeDtypeStruct(a.shape, a.dtype) for a in list(rows) + list(params)],
            compiler_params=_cparams(("arbitrary", "arbitrary")),
        )(*rows, *params, *cts)
        return tuple(outs[:nr]), tuple(outs[nr:])

    @jax.custom_vjp
    def op(rows, params):
        return tuple(fwd_call(rows, params))

    def fwd(rows, params):
        return op(rows, params), (rows, params)

    def bwd(res, cts):
        rows, params = res
        return bwd_call(rows, params, tuple(cts))

    op.defvjp(fwd, bwd)
    return op


def make_groupop(f, name, gb):
    def specs(arrs):
        return [pl.BlockSpec((gb,) + a.shape[1:], lambda g: (g, 0, 0)) for a in arrs]

    def fwd_call(arrs):
        g_n = arrs[0].shape[0]
        out_blk = jax.eval_shape(f, *[jax.ShapeDtypeStruct(a.shape[1:], a.dtype) for a in arrs])
        n = len(arrs)

        def body(*refs):
            for i in range(gb):
                outs = f(*[r[i] for r in refs[:n]])
                for o_ref, o in zip(refs[n:], outs):
                    o_ref[i] = o

        return pl.pallas_call(
            body, name=name, grid=(g_n // gb,), in_specs=specs(arrs),
            out_specs=[pl.BlockSpec((gb,) + o.shape, lambda g: (g, 0, 0)) for o in out_blk],
            out_shape=[jax.ShapeDtypeStruct((g_n,) + o.shape, o.dtype) for o in out_blk],
            compiler_params=_cparams(("parallel",)),
        )(*arrs)

    def bwd_call(arrs, cts):
        g_n = arrs[0].shape[0]
        n, nct = len(arrs), len(cts)

        def body(*refs):
            for i in range(gb):
                _, vjp = jax.vjp(lambda *v: tuple(f(*v)), *[r[i] for r in refs[:n]])
                grads = vjp(tuple(r[i] for r in refs[n:n + nct]))
                for o_ref, g in zip(refs[n + nct:], grads):
                    o_ref[i] = g

        return pl.pallas_call(
            body, name=name + "_bwd", grid=(g_n // gb,), in_specs=specs(arrs) + specs(cts), out_specs=specs(arrs),
            out_shape=[jax.ShapeDtypeStruct(a.shape, a.dtype) for a in arrs],
            compiler_params=_cparams(("parallel",)),
        )(*arrs, *cts)

    @jax.custom_vjp
    def op(*arrs):
        return tuple(fwd_call(arrs))

    def fwd(*arrs):
        return op(*arrs), arrs

    def bwd(arrs, cts):
        return tuple(bwd_call(arrs, tuple(cts)))

    op.defvjp(fwd, bwd)
    return op


def _rms(x, g):
    return x * lax.rsqrt(jnp.mean(x * x, axis=-1, keepdims=True) + EPS) * g


def _silu(z):
    return z * jax.nn.sigmoid(z)


def _log_sigmoid(x):
    return jnp.minimum(x, 0.0) - jnp.log(1.0 + jnp.exp(-jnp.abs(x)))


def f_norm(x, g):
    return (_rms(x, g),)


def f_gate_a1(y, u, d):
    return (jax.nn.gelu(y + d * u),)


def f_gate_a2(yg, t, z, b):
    return (yg * jax.nn.sigmoid(t + b) * _silu(z),)


def f_gate_b(att, z):
    return (att * _silu(z),)


def f_post_a(h, o, post_g, kv_g, pre_g):
    h1 = h + _rms(o, post_g)
    return h1, _rms(h1, kv_g), _rms(h1, pre_g)


def f_final(h, o, tgt, post_g):
    err = h + _rms(o, post_g) - tgt
    return (0.5 * jnp.mean(err * err, axis=-1, keepdims=True),)


def f_logf(gl, b):
    return (_log_sigmoid(gl + b),)


def f_memattn(q, zm, km, vm):
    s = lax.dot_general((q * (HEAD_DIM ** -0.5)).astype(BF16), km.astype(BF16), NT, preferred_element_type=F32)
    e = jnp.exp(s - jnp.max(s, axis=-1, keepdims=True))
    p = e / jnp.sum(e, axis=-1, keepdims=True)
    o = jnp.dot(p.astype(BF16), vm.astype(BF16), preferred_element_type=F32)
    return (o * _silu(zm),)


def _fox_block(row0, q, k, v, fc, fr):
    s = lax.dot_general((q * (HEAD_DIM ** -0.5)).astype(BF16), k.astype(BF16), NT, preferred_element_type=F32)
    s = s + fc - fr
    rows = lax.broadcasted_iota(jnp.int32, s.shape, 0) + row0
    cols = lax.broadcasted_iota(jnp.int32, s.shape, 1)
    s = jnp.where(rows >= cols, s, -1e30)
    e = jnp.exp(s - jnp.max(s, axis=-1, keepdims=True))
    p = e / jnp.sum(e, axis=-1, keepdims=True)
    return jnp.dot(p.astype(BF16), v.astype(BF16), preferred_element_type=F32)


def _cmul(ar, ai, xr, xi):
    return ar * xr - ai * xi, ar * xi + ai * xr


def _hp_dot(a, b):
    return jnp.dot(a, b, precision=HP, preferred_element_type=F32)


def _bf_dot(a, b):
    return jnp.dot(a.astype(BF16), b.astype(BF16), preferred_element_type=F32)


def f_s5(u, toep, win_r, win_i, wout_r, wout_i, coef):
    n_hi = u.shape[0] // SSM_LO
    y = _bf_dot(u, toep)
    sr, si = _bf_dot(u, win_r), _bf_dot(u, win_i)
    at_r, at_i = coef[0:1, :], coef[1:2, :]

    def lo_rows(a, lo):
        return a[lo * n_hi:(lo + 1) * n_hi]

    pr, pi = lo_rows(sr, 0), lo_rows(si, 0)
    for lo in range(1, SSM_LO):
        dr, di = _cmul(at_r, at_i, pr, pi)
        pr, pi = lo_rows(sr, lo) + dr, lo_rows(si, lo) + di
    ri = lax.broadcasted_iota(jnp.int32, (n_hi, n_hi), 0)
    ci = lax.broadcasted_iota(jnp.int32, (n_hi, n_hi), 1)
    d, step = 1, 0
    while d < n_hi:
        sh = (ri - ci == d).astype(F32)
        a_r, a_i = coef[2 + 2 * step:3 + 2 * step, :], coef[3 + 2 * step:4 + 2 * step, :]
        dr, di = _cmul(a_r, a_i, _hp_dot(sh, pr), _hp_dot(sh, pi))
        pr, pi = pr + dr, pi + di
        d, step = 2 * d, step + 1
    sh1 = (ri - ci == 1).astype(F32)
    xr, xi = [_hp_dot(sh1, pr)], [_hp_dot(sh1, pi)]
    for lo in range(1, SSM_LO):
        dr, di = _cmul(at_r, at_i, xr[-1], xi[-1])
        xr.append(lo_rows(sr, lo - 1) + dr)
        xi.append(lo_rows(si, lo - 1) + di)
    return (y + _bf_dot(jnp.concatenate(xr, axis=0), wout_r) + _bf_dot(jnp.concatenate(xi, axis=0), wout_i),)


def s5_operators(lam_re, lam_im, log_step, b_re, b_im, c_re, c_im, n_hi):
    t_n = SSM_T
    lr, li = lam_re, lam_im
    dt = jnp.exp(log_step)[:, None]
    mag = jnp.exp(lr * dt)
    ar, ai = mag * jnp.cos(li * dt), mag * jnp.sin(li * dt)
    den = lr * lr + li * li
    cr = ((ar - 1.0) * lr + ai * li) / den
    ci = (ai * lr - (ar - 1.0) * li) / den
    bbr = cr[..., None] * b_re - ci[..., None] * b_im
    bbi = cr[..., None] * b_im + ci[..., None] * b_re
    k = jnp.arange(t_n + 1, dtype=F32)[:, None, None]
    pm, ang = jnp.exp(k * (lr * dt)), k * (li * dt)
    pr, pi = pm * jnp.cos(ang), pm * jnp.sin(ang)
    abr = pr[..., None] * bbr - pi[..., None] * bbi
    abi = pr[..., None] * bbi + pi[..., None] * bbr
    kk = (jnp.einsum('ghp,kgpj->kghj', c_re, abr[:t_n], precision=HP)
          - jnp.einsum('ghp,kgpj->kghj', c_im, abi[:t_n], precision=HP))
    lag = jnp.arange(t_n)[None, :] - jnp.arange(t_n)[:, None]
    onehot = (lag[None] == jnp.arange(t_n)[:, None, None]).astype(F32)
    g_n, h_n = c_re.shape[0], c_re.shape[1]
    toep = jnp.einsum('kst,kghj->gsjth', onehot, kk, precision=HP).reshape(g_n, t_n * h_n, t_n * h_n)
    win_r = abr[:t_n][::-1].transpose(1, 0, 3, 2).reshape(g_n, t_n * h_n, -1)
    win_i = abi[:t_n][::-1].transpose(1, 0, 3, 2).reshape(g_n, t_n * h_n, -1)
    p1r, p1i = pr[1:, :, None, :], pi[1:, :, None, :]
    wout_r = (c_re[None] * p1r - c_im[None] * p1i).transpose(1, 3, 0, 2).reshape(g_n, -1, t_n * h_n)
    wout_i = (-(c_re[None] * p1i + c_im[None] * p1r)).transpose(1, 3, 0, 2).reshape(g_n, -1, t_n * h_n)
    rows = [pr[t_n], pi[t_n]]
    qr, qi = pr[t_n], pi[t_n]
    for _ in range(int(math.log2(SSM_LO))):
        qr, qi = qr * qr - qi * qi, 2.0 * qr * qi
    d = 1
    while d < n_hi:
        rows += [qr, qi]
        qr, qi = qr * qr - qi * qi, 2.0 * qr * qi
        d *= 2
    coef = jnp.stack(rows, axis=1)
    return toep, win_r, win_i, wout_r, wout_i, coef


def _cumsum_call(name, x, reverse):
    n_rows, w = x.shape
    bm = min(256, n_rows)
    nb = n_rows // bm

    def body(x_ref, o_ref, carry_ref):
        i = pl.program_id(0)

        @pl.when(i == 0)
        def _():
            carry_ref[...] = jnp.zeros_like(carry_ref)

        ri = lax.broadcasted_iota(jnp.int32, (bm, bm), 0)
        ci = lax.broadcasted_iota(jnp.int32, (bm, bm), 1)
        tri = ((ri <= ci) if reverse else (ri >= ci)).astype(F32)
        xb = x_ref[...]
        o_ref[...] = _hp_dot(tri, xb) + carry_ref[...]
        carry_ref[...] += jnp.sum(xb, axis=0, keepdims=True)

    idx = (lambda i: (nb - 1 - i, 0)) if reverse else (lambda i: (i, 0))
    return pl.pallas_call(
        body, name=name, grid=(nb,), in_specs=[pl.BlockSpec((bm, w), idx)], out_specs=pl.BlockSpec((bm, w), idx),
        out_shape=jax.ShapeDtypeStruct(x.shape, F32), scratch_shapes=[pltpu.VMEM((1, w), F32)],
        compiler_params=_cparams(("arbitrary",)),
    )(x)


def make_cumsum(name):
    @jax.custom_vjp
    def cs(x):
        return _cumsum_call(name, x, False)

    def fwd(x):
        return cs(x), None

    def bwd(_, dy):
        return (_cumsum_call(name + "_bwd", dy, True),)

    cs.defvjp(fwd, bwd)
    return cs


def _fox_specs(n_rows, bq):
    head = lambda h, i: (i, h)
    q_spec = pl.BlockSpec((bq, HEAD_DIM), head)
    kv_spec = pl.BlockSpec((n_rows, HEAD_DIM), lambda h, i: (0, h))
    fc_spec = pl.BlockSpec((None, bq, 1), lambda h, i: (h, i, 0))
    fr_spec = pl.BlockSpec((None, 1, n_rows), lambda h, i: (h, 0, 0))
    return q_spec, kv_spec, fc_spec, fr_spec


def _fox_fwd_call(name, q, k, v, fc, fr):
    n_rows, width = q.shape
    bq = min(FOX_BQ, n_rows)
    nq = n_rows // bq
    q_spec, kv_spec, fc_spec, fr_spec = _fox_specs(n_rows, bq)

    def body(q_ref, k_ref, v_ref, fc_ref, fr_ref, o_ref):
        i = pl.program_id(1)
        for p in range(nq):
            n_keys = (p + 1) * bq

            @pl.when(i == p)
            def _(p=p, n_keys=n_keys):
                o_ref[...] = _fox_block(p * bq, q_ref[...], k_ref[:n_keys, :], v_ref[:n_keys, :], fc_ref[...], fr_ref[:, :n_keys])

    return pl.pallas_call(
        body, name=name, grid=(width // HEAD_DIM, nq), in_specs=[q_spec, kv_spec, kv_spec, fc_spec, fr_spec], out_specs=q_spec,
        out_shape=jax.ShapeDtypeStruct(q.shape, F32), compiler_params=_cparams(("parallel", "parallel")),
    )(q, k, v, fc, fr)


def _fox_bwd_call(name, q, k, v, fc, fr, do):
    n_rows, width = q.shape
    bq = min(FOX_BQ, n_rows)
    nq = n_rows // bq
    q_spec, kv_spec, fc_spec, fr_spec = _fox_specs(n_rows, bq)

    def body(q_ref, k_ref, v_ref, fc_ref, fr_ref, do_ref, dq_ref, dk_ref, dv_ref, dfc_ref, dfr_ref):
        i = pl.program_id(1)

        @pl.when(i == 0)
        def _():
            dk_ref[...] = jnp.zeros_like(dk_ref)
            dv_ref[...] = jnp.zeros_like(dv_ref)
            dfr_ref[...] = jnp.zeros_like(dfr_ref)

        for p in range(nq):
            n_keys = (p + 1) * bq

            @pl.when(i == p)
            def _(p=p, n_keys=n_keys):
                _, vjp = jax.vjp(functools.partial(_fox_block, p * bq), q_ref[...], k_ref[:n_keys, :], v_ref[:n_keys, :],
                                 fc_ref[...], fr_ref[:, :n_keys])
                dq, dk, dv, dfc, dfr = vjp(do_ref[...])
                dq_ref[...] = dq
                dfc_ref[...] = dfc
                dk_ref[:n_keys, :] += dk
                dv_ref[:n_keys, :] += dv
                dfr_ref[:, :n_keys] += dfr

    return pl.pallas_call(
        body, name=name, grid=(width // HEAD_DIM, nq), in_specs=[q_spec, kv_spec, kv_spec, fc_spec, fr_spec, q_spec],
        out_specs=[q_spec, kv_spec, kv_spec, fc_spec, fr_spec],
        out_shape=[jax.ShapeDtypeStruct(a.shape, F32) for a in (q, k, v, fc, fr)],
        compiler_params=_cparams(("parallel", "arbitrary")),
    )(q, k, v, fc, fr, do)


def make_fox(name):
    @jax.custom_vjp
    def fox(q, k, v, fc, fr):
        return _fox_fwd_call(name, q, k, v, fc, fr)

    def fwd(q, k, v, fc, fr):
        return fox(q, k, v, fc, fr), (q, k, v, fc, fr)

    def bwd(res, do):
        return tuple(_fox_bwd_call(name + "_bwd", *res, do))

    fox.defvjp(fwd, bwd)
    return fox


def _relations(x, y, c):
    out = []
    for rel in range(1, N_DEV):
        px = 1 - x if rel & 4 else x
        py = 1 - y if rel & 2 else y
        pc = 1 - c if rel & 1 else c
        out.append((rel, (px, py, pc), 4 * px + 2 * py + pc))
    return out


_HBM_SPEC = pl.BlockSpec(memory_space=pltpu.HBM)
_SEM_SPEC = pl.BlockSpec(memory_space=pltpu.SEMAPHORE)
_DATAFLOW = pltpu.SideEffectType.DATAFLOW_SIDE_EFFECTING


def exchange_start(name, srcs, gather):
    n = len(srcs)
    me = 4 * lax.axis_index("x") + 2 * lax.axis_index("y") + lax.axis_index("c")
    lands = []
    for s in srcs:
        own = s if gather else lax.dynamic_index_in_dim(s, me, 0, keepdims=False)
        land = lax.empty((N_DEV,) + own.shape, s.dtype)
        lands.append(lax.dynamic_update_index_in_dim(land, own, me, 0))

    def body(*refs):
        src_refs, land_refs = refs[:n], refs[n:2 * n]
        send_sems, recv_sems, token = refs[2 * n], refs[2 * n + 1], refs[-1]
        x, y, c = lax.axis_index("x"), lax.axis_index("y"), lax.axis_index("c")
        mine = 4 * x + 2 * y + c
        for k in range(n):
            for rel, peer, peer_blk in _relations(x, y, c):
                pltpu.make_async_remote_copy(
                    src_ref=src_refs[k] if gather else src_refs[k].at[peer_blk], dst_ref=land_refs[k].at[mine],
                    send_sem=send_sems.at[7 * k + rel - 1], recv_sem=recv_sems.at[7 * k + rel - 1],
                    device_id=peer, device_id_type=pl.DeviceIdType.MESH).start()
        token[...] = jnp.zeros_like(token)

    sem = pltpu.SemaphoreType.DMA((7 * n,))
    hbm = lambda t: pltpu.HBM(t.shape, t.dtype)
    outs = pl.pallas_call(
        body, name=name,
        out_shape=(sem, sem, *[hbm(s) for s in srcs], *[hbm(l) for l in lands], jax.ShapeDtypeStruct((8, LANE), F32)),
        in_specs=[_HBM_SPEC] * (2 * n),
        out_specs=(_SEM_SPEC, _SEM_SPEC, *[_HBM_SPEC] * (2 * n), pl.BlockSpec(memory_space=pltpu.VMEM)),
        input_output_aliases={i: 2 + i for i in range(2 * n)},
        compiler_params=pltpu.CompilerParams(has_side_effects=_DATAFLOW),
    )(*[pltpu.with_memory_space_constraint(t, pltpu.HBM) for t in list(srcs) + lands])
    return dict(name=name, gather=gather, send=outs[0], recv=outs[1], srcs=list(outs[2:2 + n]), lands=list(outs[2 + n:2 + 2 * n]), token=outs[-1])


def exchange_wait(handle, k, after):
    gather = handle['gather']

    def body(src_ref, land_ref, send_sems, recv_sems, after_ref, src_out, land_out):
        x, y, c = lax.axis_index("x"), lax.axis_index("y"), lax.axis_index("c")
        for rel, peer, peer_blk in _relations(x, y, c):
            copy = pltpu.make_async_remote_copy(
                src_ref=src_ref if gather else src_ref.at[peer_blk], dst_ref=land_ref.at[peer_blk],
                send_sem=send_sems.at[7 * k + rel - 1], recv_sem=recv_sems.at[7 * k + rel - 1],
                device_id=peer, device_id_type=pl.DeviceIdType.MESH)
            copy.wait_send()
            copy.wait_recv()

    src, land = handle['srcs'][k], handle['lands'][k]
    return pl.pallas_call(
        body, name=f"{handle['name']}_wait{k}", out_shape=(pltpu.HBM(src.shape, src.dtype), pltpu.HBM(land.shape, land.dtype)),
        in_specs=[_HBM_SPEC, _HBM_SPEC, _SEM_SPEC, _SEM_SPEC, pl.BlockSpec(memory_space=pl.ANY)], out_specs=(_HBM_SPEC, _HBM_SPEC),
        input_output_aliases={0: 0, 1: 1}, compiler_params=pltpu.CompilerParams(has_side_effects=_DATAFLOW),
    )(src, land, handle['send'], handle['recv'], after)[1]


def adamw_reduce(name, partials, w, m, v):
    n_part, n_rows, n_cols = partials.shape
    br = n_rows
    for cand in (512, 256, 128, 64, 32, 16, 8):
        if n_rows % cand == 0 and n_part * cand * n_cols * partials.dtype.itemsize <= (4 << 20):
            br = cand
            break

    def body(p_ref, w_ref, m_ref, v_ref, g_ref, d_ref, nm_ref, nv_ref):
        g = p_ref[0].astype(F32)
        for s in range(1, n_part):
            g = g + p_ref[s].astype(F32)
        m_new = ADAM_B1 * m_ref[...] + (1.0 - ADAM_B1) * g
        v_new = ADAM_B2 * v_ref[...] + (1.0 - ADAM_B2) * jnp.square(g)
        m_hat = m_new / (1.0 - ADAM_B1 ** ADAM_STEP)
        v_hat = v_new / (1.0 - ADAM_B2 ** ADAM_STEP)
        g_ref[...] = g
        d_ref[...] = -ADAM_LR * (m_hat / (jnp.sqrt(v_hat) + ADAM_EPS) + ADAM_WD * w_ref[...])
        nm_ref[...] = m_new
        nv_ref[...] = v_new

    spec = pl.BlockSpec((br, n_cols), lambda i: (i, 0))
    return pl.pallas_call(
        body, name=name, grid=(n_rows // br,), in_specs=[pl.BlockSpec((n_part, br, n_cols), lambda i: (0, i, 0)), spec, spec, spec],
        out_specs=[spec] * 4, out_shape=[jax.ShapeDtypeStruct((n_rows, n_cols), F32)] * 4, compiler_params=_cparams(("parallel",)),
    )(partials, w, m, v)


def _pack(arrays, n_rows):
    flat = jnp.concatenate([a.reshape(-1) for a in arrays])
    return jnp.pad(flat, (0, n_rows * LANE - flat.shape[0])).reshape(n_rows, LANE)


def _unpack(packed, shapes):
    flat, out, off = packed.reshape(-1), [], 0
    for s in shapes:
        n = math.prod(s)
        out.append(flat[off:off + n].reshape(s))
        off += n
    return out


def _packed_rows(shapes):
    n = sum(math.prod(s) for s in shapes)
    return -(-n // (LANE * 512)) * 512


GATHER_ORDER = ['small', 'in_a', 'mem_kv0', 'glu', 'out0', 'kv', 'in_b', 'mem_kv1', 'out1']


BLOCKED = ('in_a', 'kv', 'in_b')


def local_step(p, gathered, x, mem, tgt, start_early, start_late):
    n_rows, d_model = x.shape
    main_w = 3 * d_model // 4
    mem_w = d_model - main_w
    n_groups = main_w // SSM_GROUP
    n_heads = main_w // HEAD_DIM
    n_hi = n_rows // (SSM_T * SSM_LO)
    row = lambda a: a.reshape(1, -1)
    tape, pending = {}, {}

    def mm_fwd(name, act):
        w = exchange_wait(gathered, GATHER_ORDER.index(name), act)
        if name not in BLOCKED:
            w = w.reshape(-1, w.shape[2])
        tape[name] = (act, w)
        return (_mm_nn_cb if name in BLOCKED else _mm_nn)(name, act, w)

    def mm_dw(name, dy, after=None):
        act, w = tape[name]
        if name in BLOCKED:
            dw = _mm_tn_cb(name + "_dw", act, dy, w.shape[2], GRAD_DTYPE, after)
        else:
            dw = _mm_tn(name + "_dw", act, dy, GRAD_DTYPE, after)
            dw = dw.reshape(N_DEV, dw.shape[0] // N_DEV, dw.shape[1])
        pending[name] = exchange_start("rs_" + name, [dw], gather=False)

    def mm_da(name, dy, after=None):
        return (_mm_nt_cb if name in BLOCKED else _mm_nt)(name + "_da", dy, tape[name][1], after)

    def mm_bwd(name, dy):
        mm_dw(name, dy)
        return mm_da(name, dy, pending[name]['token'])

    def split(proj):
        return proj[:, :main_w], proj[:, main_w:2 * main_w], proj[:, 2 * main_w:2 * main_w + mem_w], proj[:, 2 * main_w + mem_w:]

    def mem_attn(i, qm, zm, kvm):
        memo, = make_rowop(f_memattn, f"mem_attn{i}", 512, nl=mem_w // HEAD_DIM)((qm, zm), (kvm[:, :mem_w], kvm[:, mem_w:]))
        return memo

    def seg_norms(x_, pre_g0, mem_g0, mem_g1):
        hn, = make_rowop(f_norm, "pre_norm0", 256)((x_,), (row(pre_g0),))
        memn0, = make_rowop(f_norm, "mem_norm0", 256)((mem,), (row(mem_g0),))
        memn1, = make_rowop(f_norm, "mem_norm1", 256)((mem,), (row(mem_g1),))
        return hn, memn0, memn1

    def seg_a1(proj, kvm, lam_re, lam_im, log_step, b_re, b_im, c_re, c_im, d_skip):
        u, z, qm, zm = split(proj)
        ops = s5_operators(lam_re, lam_im, log_step, b_re, b_im, c_re, c_im, n_hi)
        ug = u.astype(BF16).reshape(n_hi, SSM_LO, SSM_T, n_groups, SSM_GROUP).transpose(3, 1, 0, 2, 4).reshape(n_groups, n_hi * SSM_LO, SSM_T * SSM_GROUP)
        yg, = make_groupop(f_s5, "s5", SSM_GB)(ug, *ops)
        y = yg.reshape(n_groups, SSM_LO, n_hi, SSM_T, SSM_GROUP).transpose(2, 1, 3, 0, 4).reshape(n_rows, main_w)
        ygelu, = make_rowop(f_gate_a1, "gate_a1", 256)((y, u), (row(d_skip),))
        return ygelu, z, mem_attn(0, qm, zm, kvm)

    def seg_a2(ygelu, t, z, memo, b_glu):
        main, = make_rowop(f_gate_a2, "gate_a2", 256)((ygelu, t, z), (row(b_glu),))
        return jnp.concatenate([main, memo], axis=1)

    def seg_post_a(x_, o, post_g0, kv_g, pre_g1):
        return make_rowop(f_post_a, "post_a", 128)((x_, o), (row(post_g0), row(kv_g), row(pre_g1)))

    def seg_b(proj, kv, kv_in, kvm, w_fgate, b_fgate):
        w_fg = jnp.pad(w_fgate, ((0, 0), (0, LANE - n_heads)))
        b_fg = jnp.pad(b_fgate, (0, LANE - n_heads)).reshape(1, LANE)
        logf, = make_rowop(f_logf, "logf", 512)((make_mm("fgate")(kv_in, w_fg),), (b_fg,))
        fcum = make_cumsum("fcum")(logf)[:, :n_heads].T
        q, z, qm, zm = split(proj)
        att = make_fox("fox")(q, kv[:, :main_w], kv[:, main_w:], fcum[:, :, None], fcum[:, None, :])
        main, = make_rowop(f_gate_b, "gate_b", 256)((att, z), ())
        return jnp.concatenate([main, mem_attn(1, qm, zm, kvm)], axis=1)

    def seg_final(h1, o, post_g1):
        rowloss, = make_rowop(f_final, "final", 128)((h1, o, tgt), (row(post_g1),))
        return jnp.sum(rowloss)

    (hn0, memn0, memn1), vjp_norms = jax.vjp(seg_norms, x, p['pre_norm_g'][0], p['mem_norm_g'][0], p['mem_norm_g'][1])
    proj_a = mm_fwd("in_a", hn0)
    kvm0 = mm_fwd("mem_kv0", memn0)
    s5_names = ('lam_re', 'lam_im', 'log_step', 'b_re', 'b_im', 'c_re', 'c_im', 'd_skip')
    (ygelu, z_a, memo0), vjp_a1 = jax.vjp(seg_a1, proj_a, kvm0, *[p[n] for n in s5_names])
    t = mm_fwd("glu", ygelu)
    cat0, vjp_a2 = jax.vjp(seg_a2, ygelu, t, z_a, memo0, p['b_glu'])
    o0 = mm_fwd("out0", cat0)
    (h1, kv_in, hn1), vjp_post_a = jax.vjp(seg_post_a, x, o0, p['post_norm_g'][0], p['kv_norm_g'], p['pre_norm_g'][1])
    kv = mm_fwd("kv", kv_in)
    proj_b = mm_fwd("in_b", hn1)
    kvm1 = mm_fwd("mem_kv1", memn1)
    cat1, vjp_b = jax.vjp(seg_b, proj_b, kv, kv_in, kvm1, p['w_fgate'], p['b_fgate'])
    o1 = mm_fwd("out1", cat1)
    loss, vjp_final = jax.vjp(seg_final, h1, o1, p['post_norm_g'][1])

    g = {}
    d_h1, d_o1, g_post_g1 = vjp_final(jnp.ones((), F32))
    d_proj_b, d_kv, d_kv_in, d_kvm1, g['w_fgate'], g['b_fgate'] = vjp_b(mm_bwd("out1", d_o1))
    d_memn1 = mm_bwd("mem_kv1", d_kvm1)
    d_hn1 = mm_bwd("in_b", d_proj_b)
    d_kv_in = d_kv_in + mm_bwd("kv", d_kv)
    d_x, d_o0, g_post_g0, g['kv_norm_g'], g_pre_g1 = vjp_post_a((d_h1, d_kv_in, d_hn1))
    d_ygelu, d_t, d_z, d_memo0, g['b_glu'] = vjp_a2(mm_bwd("out0", d_o0))
    d_ygelu = d_ygelu + mm_bwd("glu", d_t)
    d_proj_a, d_kvm0, *g_s5 = vjp_a1((d_ygelu, d_z, d_memo0))
    g.update(zip(s5_names, g_s5))
    g['pre_norm_g'] = jnp.stack([jnp.zeros_like(g_pre_g1), g_pre_g1])
    g['post_norm_g'] = jnp.stack([g_post_g0, g_post_g1])
    g['mem_norm_g'] = jnp.zeros_like(p['mem_norm_g'])
    early = start_early(g)
    d_memn0 = mm_bwd("mem_kv0", d_kvm0)
    d_x2, *g_late = vjp_norms((mm_da("in_a", d_proj_a, early['token']), d_memn0, d_memn1))
    late = start_late(g_late)
    mm_dw("in_a", d_proj_a, late['token'])
    return loss, d_x + d_x2, pending, early, late


def kernel(x, mem, pre_norm_g, post_norm_g, w_in_a, lam_re, lam_im, log_step, b_re, b_im, c_re, c_im, d_skip, w_glu, b_glu, kv_norm_g, w_kv, w_fgate, b_fgate, w_in_b, mem_norm_g, w_mem_kv, w_out, loss_target, m_pre_norm_g, m_post_norm_g, m_w_in_a, m_lam_re, m_lam_im, m_log_step, m_b_re, m_b_im, m_c_re, m_c_im, m_d_skip, m_w_glu, m_b_glu, m_kv_norm_g, m_w_kv, m_w_fgate, m_b_fgate, m_w_in_b, m_mem_norm_g, m_w_mem_kv, m_w_out, v_pre_norm_g, v_post_norm_g, v_w_in_a, v_lam_re, v_lam_im, v_log_step, v_b_re, v_b_im, v_c_re, v_c_im, v_d_skip, v_w_glu, v_b_glu, v_kv_norm_g, v_w_kv, v_w_fgate, v_b_fgate, v_w_in_b, v_mem_norm_g, v_w_mem_kv, v_w_out):
    a = dict(zip(INPUTS, (x, mem, pre_norm_g, post_norm_g, w_in_a, lam_re, lam_im, log_step, b_re, b_im, c_re, c_im, d_skip, w_glu, b_glu, kv_norm_g, w_kv, w_fgate, b_fgate, w_in_b, mem_norm_g, w_mem_kv, w_out, loss_target, m_pre_norm_g, m_post_norm_g, m_w_in_a, m_lam_re, m_lam_im, m_log_step, m_b_re, m_b_im, m_c_re, m_c_im, m_d_skip, m_w_glu, m_b_glu, m_kv_norm_g, m_w_kv, m_w_fgate, m_b_fgate, m_w_in_b, m_mem_norm_g, m_w_mem_kv, m_w_out, v_pre_norm_g, v_post_norm_g, v_w_in_a, v_lam_re, v_lam_im, v_log_step, v_b_re, v_b_im, v_c_re, v_c_im, v_d_skip, v_w_glu, v_b_glu, v_kv_norm_g, v_w_kv, v_w_fgate, v_b_fgate, v_w_in_b, v_mem_norm_g, v_w_mem_kv, v_w_out)))
    me = 4 * lax.axis_index("x") + 2 * lax.axis_index("y") + lax.axis_index("c")
    n_layers = w_out.shape[0]

    small_shapes = [a[n].shape for n in SMALL_SHARDED]
    small_rows = -(-sum(math.prod(s) for s in small_shapes) // (LANE * 8)) * 8
    shards = dict(in_a=w_in_a[0], mem_kv0=w_mem_kv[0], glu=w_glu[0], out0=w_out[0], kv=w_kv, in_b=w_in_b[0], mem_kv1=w_mem_kv[1], out1=w_out[1])
    operands = [_pack([a[n] for n in SMALL_SHARDED], small_rows)] + [shards[n].astype(BF16) for n in GATHER_ORDER[1:]]
    gathered = exchange_start("ag_weights", operands, gather=True)
    small = exchange_wait(gathered, 0, gathered['token'])
    small = [jnp.stack(parts) for parts in zip(*[_unpack(small[b], small_shapes) for b in range(N_DEV)])]
    p = {n: a[n] for n in REPLICATED}
    for n in ('lam_re', 'lam_im', 'log_step', 'b_re', 'b_im', 'c_re', 'c_im'):
        p[n] = p[n][0]
    p['d_skip'] = small[0].reshape(-1)
    p['b_glu'] = small[1].reshape(-1)
    p['w_fgate'] = small[2].reshape(-1, w_fgate.shape[1])
    p['pre_norm_g'] = p['pre_norm_g'] + gathered['token'][0, 0]

    full_shapes = [a[n].shape for n in REPLICATED] + [(1, N_DEV * d_skip.shape[1]), (1, N_DEV * b_glu.shape[1]), (N_DEV * w_fgate.shape[0], w_fgate.shape[1])]
    rows = _packed_rows(full_shapes)
    late_rows = 3 * pre_norm_g.shape[1] // LANE
    start_early = lambda g: exchange_start("ag_grads", [_pack([g[n] for n in REPLICATED + SMALL_SHARDED], rows)], gather=True)
    start_late = lambda g_late: exchange_start("ag_grads_late", [_pack(g_late, late_rows)], gather=True)

    loss_local, grad_x, pending, early, late = local_step(p, gathered, x[0], mem[0], loss_target[0], start_early, start_late)
    loss = lax.psum(loss_local, MESH_AXES)

    out = {}

    def update(tag, name, layer=None):
        pick = (lambda t: t) if layer is None else (lambda t: t[layer])
        parts = exchange_wait(pending[tag], 0, grad_x)
        two_d = lambda t: pick(t).reshape(parts.shape[1:])
        res = adamw_reduce("adamw_" + tag, parts, two_d(a[name]), two_d(a['m_' + name]), two_d(a['v_' + name]))
        return [r.reshape(pick(a[name]).shape) for r in res]

    per_layer = {n: [None] * n_layers for n in ('w_out', 'w_mem_kv')}
    per_layer['w_out'][1] = update("out1", 'w_out', 1)
    per_layer['w_mem_kv'][1] = update("mem_kv1", 'w_mem_kv', 1)
    out['w_in_b'] = [r[None] for r in update("in_b", 'w_in_b', 0)]
    out['w_kv'] = update("kv", 'w_kv')
    per_layer['w_out'][0] = update("out0", 'w_out', 0)
    per_layer['w_mem_kv'][0] = update("mem_kv0", 'w_mem_kv', 0)
    out['w_glu'] = [r[None] for r in update("glu", 'w_glu', 0)]
    for name in per_layer:
        out[name] = [jnp.stack(t) for t in zip(*per_layer[name])]

    zeros = [jnp.zeros(s, F32) for s in full_shapes[len(REPLICATED):]]
    packed = lambda pre: _pack([a[pre + n] for n in REPLICATED] + zeros, rows)
    res = [_unpack(r, full_shapes) for r in adamw_reduce("adamw_small", exchange_wait(early, 0, grad_x), packed(''), packed('m_'), packed('v_'))]
    for i, n in enumerate(REPLICATED):
        out[n] = [r[i] for r in res]
    g_full = res[0][len(REPLICATED):]
    g_shard = [lax.dynamic_slice_in_dim(g_full[0], me * d_skip.shape[1], d_skip.shape[1], 1),
               lax.dynamic_slice_in_dim(g_full[1], me * b_glu.shape[1], b_glu.shape[1], 1),
               lax.dynamic_slice_in_dim(g_full[2], me * w_fgate.shape[0], w_fgate.shape[0], 0)]
    packed = lambda pre: _pack([a[pre + n] for n in SMALL_SHARDED], small_rows)
    res = [_unpack(r, small_shapes) for r in adamw_reduce("adamw_small_sharded", _pack(g_shard, small_rows)[None], packed(''), packed('m_'), packed('v_'))]
    for i, n in enumerate(SMALL_SHARDED):
        out[n] = [r[i] for r in res]
    packed = lambda pre: _pack([a[pre + 'pre_norm_g'][0], a[pre + 'mem_norm_g'][0], a[pre + 'mem_norm_g'][1]], late_rows)
    res = [_unpack(r, [pre_norm_g.shape[1:]] * 3) for r in adamw_reduce("adamw_small_late", exchange_wait(late, 0, grad_x), packed(''), packed('m_'), packed('v_'))]
    out['pre_norm_g'] = [jnp.stack([r[0], o[1]]) for r, o in zip(res, out['pre_norm_g'])]
    out['mem_norm_g'] = [jnp.stack([r[1], r[2]]) for r in res]
    out['w_in_a'] = [r[None] for r in update("in_a", 'w_in_a', 0)]

    return (loss, grad_x[None], *[out[n][k] for k in range(4) for n in WEIGHTS])
```

```python
import functools
import math

import jax
import jax.numpy as jnp
from jax import lax
from jax.experimental import pallas as pl
from jax.experimental.pallas import tpu as pltpu

F32 = jnp.float32
BF16 = jnp.bfloat16
HP = lax.Precision.HIGHEST
MESH_AXES = ("x", "y", "c")
N_DEV = 8
V7X_VMEM_LIMIT = 56 * 1024 * 1024
LANE = 128

EPS = 1e-6
HEAD_DIM = 128
SSM_GROUP = 16
SSM_STATE = 64
SSM_T = 16
SSM_LO = 8
SSM_GB = 8
FOX_BQ = 256
GRAD_DTYPE = BF16
ADAM_LR = 0.001
ADAM_B1 = 0.9
ADAM_B2 = 0.999
ADAM_EPS = 1e-08
ADAM_WD = 0.01
ADAM_STEP = 10

WEIGHTS = ['pre_norm_g', 'post_norm_g', 'w_in_a', 'lam_re', 'lam_im', 'log_step', 'b_re', 'b_im', 'c_re', 'c_im', 'd_skip',
           'w_glu', 'b_glu', 'kv_norm_g', 'w_kv', 'w_fgate', 'b_fgate', 'w_in_b', 'mem_norm_g', 'w_mem_kv', 'w_out']
INPUTS = ['x', 'mem'] + WEIGHTS + ['loss_target'] + ['m_' + n for n in WEIGHTS] + ['v_' + n for n in WEIGHTS]
REPLICATED = ['pre_norm_g', 'post_norm_g', 'lam_re', 'lam_im', 'log_step', 'b_re', 'b_im', 'c_re', 'c_im', 'kv_norm_g',
              'b_fgate', 'mem_norm_g']
SMALL_SHARDED = ['d_skip', 'b_glu', 'w_fgate']


def _cparams(sem=None):
    return pltpu.CompilerParams(dimension_semantics=sem, vmem_limit_bytes=V7X_VMEM_LIMIT)


def _tile(n, cap):
    if n <= cap:
        return n
    best = None
    for t in range(LANE, cap + 1, LANE):
        if n % t == 0:
            best = t
    assert best is not None, (n, cap)
    return best


def _matmul(name, a, b, a_spec, b_spec, o_spec, out_shape, grid, dims, nk, after=None):
    def body(a_ref, b_ref, *rest):
        o_ref, acc_ref = rest[-2:]
        k = pl.program_id(2)
        part = lax.dot_general(a_ref[...].astype(BF16), b_ref[...].astype(BF16), dims, preferred_element_type=F32)

        @pl.when(k == 0)
        def _():
            acc_ref[...] = part

        @pl.when(k > 0)
        def _():
            acc_ref[...] += part

        @pl.when(k == nk - 1)
        def _():
            o_ref[...] = acc_ref[...].astype(o_ref.dtype)

    acc_shape = tuple(d for d in o_spec.block_shape if d is not None)
    extra = [] if after is None else [after]
    return pl.pallas_call(
        body, name=name, grid=grid, in_specs=[a_spec, b_spec] + [pl.BlockSpec(memory_space=pl.ANY)] * len(extra), out_specs=o_spec,
        out_shape=out_shape, scratch_shapes=[pltpu.VMEM(acc_shape, F32)],
        compiler_params=_cparams(("parallel", "parallel", "arbitrary")),
    )(a, b, *extra)


NN = (((1,), (0,)), ((), ()))
NT = (((1,), (1,)), ((), ()))
TN = (((0,), (0,)), ((), ()))


def _mm_nn(name, a, b):
    (m, k), (_, n) = a.shape, b.shape
    bm, bn, bk = _tile(m, 512), _tile(n, 512), _tile(k, 2048)
    return _matmul(name, a, b, pl.BlockSpec((bm, bk), lambda i, j, kk: (i, kk)), pl.BlockSpec((bk, bn), lambda i, j, kk: (kk, j)),
                   pl.BlockSpec((bm, bn), lambda i, j, kk: (i, j)), jax.ShapeDtypeStruct((m, n), F32), (m // bm, n // bn, k // bk), NN, k // bk)


def _mm_nt(name, a, b, after=None):
    (m, c), (n, _) = a.shape, b.shape
    bm, bn, bk = _tile(m, 512), _tile(n, 512), _tile(c, 2048)
    return _matmul(name, a, b, pl.BlockSpec((bm, bk), lambda i, j, kk: (i, kk)), pl.BlockSpec((bn, bk), lambda i, j, kk: (j, kk)),
                   pl.BlockSpec((bm, bn), lambda i, j, kk: (i, j)), jax.ShapeDtypeStruct((m, n), F32), (m // bm, n // bn, c // bk), NT, c // bk, after)


def _mm_tn(name, a, b, out_dtype=F32, after=None):
    (c, m), (_, n) = a.shape, b.shape
    bm, bn, bk = _tile(m, 512), _tile(n, 512), _tile(c, 2048)
    return _matmul(name, a, b, pl.BlockSpec((bk, bm), lambda i, j, kk: (kk, i)), pl.BlockSpec((bk, bn), lambda i, j, kk: (kk, j)),
                   pl.BlockSpec((bm, bn), lambda i, j, kk: (i, j)), jax.ShapeDtypeStruct((m, n), out_dtype), (m // bm, n // bn, c // bk), TN, c // bk, after)


def _mm_nn_cb(name, a, bb):
    (m, k), (nb, _, ns) = a.shape, bb.shape
    bm, bk = _tile(m, 512), _tile(k, 2048)
    return _matmul(name, a, bb, pl.BlockSpec((bm, bk), lambda i, j, kk: (i, kk)), pl.BlockSpec((None, bk, ns), lambda i, j, kk: (j, kk, 0)),
                   pl.BlockSpec((bm, ns), lambda i, j, kk: (i, j)), jax.ShapeDtypeStruct((m, nb * ns), F32), (m // bm, nb, k // bk), NN, k // bk)


def _mm_nt_cb(name, dy, bb, after=None):
    m, (nb, k, ns) = dy.shape[0], bb.shape
    bm, bn = _tile(m, 1024), _tile(k, 1024)
    return _matmul(name, dy, bb, pl.BlockSpec((bm, ns), lambda i, j, kk: (i, kk)), pl.BlockSpec((None, bn, ns), lambda i, j, kk: (kk, j, 0)),
                   pl.BlockSpec((bm, bn), lambda i, j, kk: (i, j)), jax.ShapeDtypeStruct((m, k), F32), (m // bm, k // bn, nb), NT, nb, after)


def _mm_tn_cb(name, a, dy, ns, out_dtype=F32, after=None):
    (c, k), nb = a.shape, dy.shape[1] // ns
    bm = _tile(k, 512)
    return _matmul(name, a, dy, pl.BlockSpec((c, bm), lambda i, j, kk: (0, i)), pl.BlockSpec((c, ns), lambda i, j, kk: (0, j)),
                   pl.BlockSpec((None, bm, ns), lambda i, j, kk: (j, i, 0)), jax.ShapeDtypeStruct((nb, k, ns), out_dtype), (k // bm, nb, 1), TN, 1, after)


def make_mm(name):
    @jax.custom_vjp
    def mm(a, b):
        return _mm_nn(name, a, b)

    def fwd(a, b):
        return mm(a, b), (a, b)

    def bwd(res, dy):
        a, b = res
        return _mm_nt(name + "_da", dy, b), _mm_tn(name + "_dw", a, dy)

    mm.defvjp(fwd, bwd)
    return mm


def _rowop_specs(rows, params, bm, nl):
    row_specs = [pl.BlockSpec((bm, r.shape[1] // nl), lambda j, i: (i, j)) for r in rows]
    par_specs = [pl.BlockSpec((p.shape[0], p.shape[1] // nl), lambda j, i: (0, j)) for p in params]
    row_blk = [jax.ShapeDtypeStruct((bm, r.shape[1] // nl), r.dtype) for r in rows]
    par_blk = [jax.ShapeDtypeStruct((p.shape[0], p.shape[1] // nl), p.dtype) for p in params]
    return row_specs, par_specs, row_blk, par_blk


def make_rowop(f, name, bm, nl=1):
    def fwd_call(rows, params):
        n_rows = rows[0].shape[0]
        b = min(bm, n_rows)
        row_specs, par_specs, row_blk, par_blk = _rowop_specs(rows, params, b, nl)
        out_blk = jax.eval_shape(f, *row_blk, *par_blk)
        nr, npar = len(rows), len(params)

        def body(*refs):
            outs = f(*[r[...] for r in refs[:nr + npar]])
            for o_ref, o in zip(refs[nr + npar:], outs):
                o_ref[...] = o.astype(o_ref.dtype)

        return pl.pallas_call(
            body, name=name, grid=(nl, n_rows // b), in_specs=row_specs + par_specs,
            out_specs=[pl.BlockSpec(o.shape, lambda j, i: (i, j)) for o in out_blk],
            out_shape=[jax.ShapeDtypeStruct((n_rows, o.shape[1] * nl), o.dtype) for o in out_blk],
            compiler_params=_cparams(("parallel", "parallel")),
        )(*rows, *params)

    def bwd_call(rows, params, cts):
        n_rows = rows[0].shape[0]
        b = min(bm, n_rows)
        row_specs, par_specs, row_blk, par_blk = _rowop_specs(rows, params, b, nl)
        ct_specs = [pl.BlockSpec((b, c.shape[1] // nl), lambda j, i: (i, j)) for c in cts]
        nr, npar, nct = len(rows), len(params), len(cts)

        def body(*refs):
            i = pl.program_id(1)
            vals = [r[...] for r in refs[:nr + npar]]
            ct_vals = tuple(r[...] for r in refs[nr + npar:nr + npar + nct])
            _, vjp = jax.vjp(lambda *v: tuple(f(*v)), *vals)
            grads = vjp(ct_vals)
            outs = refs[nr + npar + nct:]
            for o_ref, g in zip(outs[:nr], grads[:nr]):
                o_ref[...] = g
            for o_ref, g in zip(outs[nr:], grads[nr:]):
                @pl.when(i == 0)
                def _(o_ref=o_ref, g=g):
                    o_ref[...] = g

                @pl.when(i > 0)
                def _(o_ref=o_ref, g=g):
                    o_ref[...] += g

        outs = pl.pallas_call(
            body, name=name + "_bwd", grid=(nl, n_rows // b), in_specs=row_specs + par_specs + ct_specs,
            out_specs=row_specs + par_specs,
            out_shape=[jax.ShapeDtypeStruct(a.shape, a.dtype) for a in list(rows) + list(params)],
            compiler_params=_cparams(("arbitrary", "arbitrary")),
        )(*rows, *params, *cts)
        return tuple(outs[:nr]), tuple(outs[nr:])

    @jax.custom_vjp
    def op(rows, params):
        return tuple(fwd_call(rows, params))

    def fwd(rows, params):
        return op(rows, params), (rows, params)

    def bwd(res, cts):
        rows, params = res
        return bwd_call(rows, params, tuple(cts))

    op.defvjp(fwd, bwd)
    return op


def make_groupop(f, name, gb):
    def specs(arrs):
        return [pl.BlockSpec((gb,) + a.shape[1:], lambda g: (g, 0, 0)) for a in arrs]

    def fwd_call(arrs):
        g_n = arrs[0].shape[0]
        out_blk = jax.eval_shape(f, *[jax.ShapeDtypeStruct((gb,) + a.shape[1:], a.dtype) for a in arrs])[0]
        n = len(arrs)

        def body(*refs):
            for i, o in enumerate(f(*[r[...] for r in refs[:n]])):
                refs[n][i] = o

        return pl.pallas_call(
            body, name=name, grid=(g_n // gb,), in_specs=specs(arrs),
            out_specs=pl.BlockSpec((gb,) + out_blk.shape, lambda g: (g, 0, 0)),
            out_shape=jax.ShapeDtypeStruct((g_n,) + out_blk.shape, out_blk.dtype),
            compiler_params=_cparams(("parallel",)),
        )(*arrs)

    def bwd_call(arrs, ct):
        g_n = arrs[0].shape[0]
        n = len(arrs)

        def body(*refs):
            _, vjp = jax.vjp(lambda *v: tuple(f(*v)), *[r[...] for r in refs[:n]])
            for o_ref, g in zip(refs[n + 1:], vjp(tuple(refs[n][i] for i in range(gb)))):
                o_ref[...] = g

        return pl.pallas_call(
            body, name=name + "_bwd", grid=(g_n // gb,), in_specs=specs(arrs) + specs([ct]), out_specs=specs(arrs),
            out_shape=[jax.ShapeDtypeStruct(a.shape, a.dtype) for a in arrs],
            compiler_params=_cparams(("parallel",)),
        )(*arrs, ct)

    @jax.custom_vjp
    def op(*arrs):
        return fwd_call(arrs)

    def fwd(*arrs):
        return op(*arrs), arrs

    def bwd(arrs, ct):
        return tuple(bwd_call(arrs, ct))

    op.defvjp(fwd, bwd)
    return op


def _rms(x, g):
    return x * lax.rsqrt(jnp.mean(x * x, axis=-1, keepdims=True) + EPS) * g


def _silu(z):
    return z * jax.nn.sigmoid(z)


def _log_sigmoid(x):
    return jnp.minimum(x, 0.0) - jnp.log(1.0 + jnp.exp(-jnp.abs(x)))


def f_norm(x, g):
    return (_rms(x, g),)


def f_gate_a1(y, u, d):
    return (jax.nn.gelu(y + d * u),)


def f_gate_a2(yg, t, z, b):
    return (yg * jax.nn.sigmoid(t + b) * _silu(z),)


def f_gate_b(att, z):
    return (att * _silu(z),)


def f_post_a(h, o, post_g, kv_g, pre_g):
    h1 = h + _rms(o, post_g)
    return h1, _rms(h1, kv_g), _rms(h1, pre_g)


def f_final(h, o, tgt, post_g):
    err = h + _rms(o, post_g) - tgt
    return (0.5 * jnp.mean(err * err, axis=-1, keepdims=True),)


def f_logf(gl, b):
    return (_log_sigmoid(gl + b),)


def _mxu(a, b, dims):
    return lax.dot_general(a.astype(BF16), b.astype(BF16), dims, preferred_element_type=F32)


@jax.custom_vjp
def _mxu_nn(a, b):
    return _mxu(a, b, NN)


def _mxu_nn_bwd(res, g):
    a, b = res
    return _mxu(g, b, NT).astype(a.dtype), _mxu(a, g, TN).astype(b.dtype)


_mxu_nn.defvjp(lambda a, b: (_mxu(a, b, NN), (a, b)), _mxu_nn_bwd)


@jax.custom_vjp
def _mxu_nt(a, b):
    return _mxu(a, b, NT)


def _mxu_nt_bwd(res, g):
    a, b = res
    return _mxu(g, b, NN).astype(a.dtype), _mxu(g, a, TN).astype(b.dtype)


_mxu_nt.defvjp(lambda a, b: (_mxu(a, b, NT), (a, b)), _mxu_nt_bwd)


def f_memattn(q, zm, km, vm):
    o = _mxu_nn(_softmax(_mxu_nt(q * (HEAD_DIM ** -0.5), km)), vm)
    return (o * _silu(zm),)


@jax.custom_vjp
def _softmax(s):
    e = jnp.exp(s - jnp.max(s, axis=-1, keepdims=True))
    return e * (1.0 / jnp.sum(e, axis=-1, keepdims=True))


def _softmax_fwd(s):
    p = _softmax(s)
    return p, p


def _softmax_bwd(p, dp):
    return (p * (dp - jnp.sum(dp * p, axis=-1, keepdims=True)),)


_softmax.defvjp(_softmax_fwd, _softmax_bwd)


def _fox_block(q, k, v, fr):
    bq = q.shape[0]
    s = _mxu_nt(q * (HEAD_DIM ** -0.5), k) - fr
    tri = lax.broadcasted_iota(jnp.int32, (bq, bq), 0) >= lax.broadcasted_iota(jnp.int32, (bq, bq), 1)
    diag = jnp.where(tri, s[:, -bq:], -1e30)
    s = diag if k.shape[0] == bq else jnp.concatenate([s[:, :-bq], diag], axis=1)
    return _mxu_nn(_softmax(s), v)


def _cmul(ar, ai, xr, xi):
    return ar * xr - ai * xi, ar * xi + ai * xr


def _hp_dot(a, b):
    return jnp.dot(a, b, precision=HP, preferred_element_type=F32)


_bf_dot = _mxu_nn


def f_s5(u, toep, win_r, win_i, wout_r, wout_i, coef):
    gb = u.shape[0]
    n_hi = u.shape[1] // SSM_LO
    sr = [_bf_dot(u[i], win_r[i]) for i in range(gb)]
    si = [_bf_dot(u[i], win_i[i]) for i in range(gb)]

    def stacked(parts, lo):
        return jnp.concatenate([a[lo * n_hi:(lo + 1) * n_hi] for a in parts], axis=0)

    def coef_rows(k):
        return jnp.concatenate([jnp.broadcast_to(coef[i, k:k + 1, :], (n_hi, coef.shape[2])) for i in range(gb)], axis=0)

    at_r, at_i = coef_rows(0), coef_rows(1)
    pr, pi = stacked(sr, 0), stacked(si, 0)
    for lo in range(1, SSM_LO):
        dr, di = _cmul(at_r, at_i, pr, pi)
        pr, pi = stacked(sr, lo) + dr, stacked(si, lo) + di
    n = 2 * gb * n_hi
    ri = lax.broadcasted_iota(jnp.int32, (n, n), 0)
    ci = lax.broadcasted_iota(jnp.int32, (n, n), 1)
    same = ri // n_hi == ci // n_hi

    def shifted(d, zr, zi):
        z = _hp_dot((same & (ri - ci == d)).astype(F32), jnp.concatenate([zr, zi], axis=0))
        return z[:n // 2], z[n // 2:]

    d, step = 1, 0
    while d < n_hi:
        dr, di = _cmul(coef_rows(2 + 2 * step), coef_rows(3 + 2 * step), *shifted(d, pr, pi))
        pr, pi = pr + dr, pi + di
        d, step = 2 * d, step + 1
    er, ei = shifted(1, pr, pi)
    xr, xi = [er], [ei]
    for lo in range(1, SSM_LO):
        dr, di = _cmul(at_r, at_i, xr[-1], xi[-1])
        xr.append(stacked(sr, lo - 1) + dr)
        xi.append(stacked(si, lo - 1) + di)

    def group_rows(parts, i):
        return jnp.concatenate([a[i * n_hi:(i + 1) * n_hi] for a in parts], axis=0)

    return tuple(_bf_dot(u[i], toep[i]) + _bf_dot(group_rows(xr, i), wout_r[i]) + _bf_dot(group_rows(xi, i), wout_i[i])
                 for i in range(gb))


def s5_operators(lam_re, lam_im, log_step, b_re, b_im, c_re, c_im, n_hi):
    t_n = SSM_T
    lr, li = lam_re, lam_im
    dt = jnp.exp(log_step)[:, None]
    mag = jnp.exp(lr * dt)
    ar, ai = mag * jnp.cos(li * dt), mag * jnp.sin(li * dt)
    den = lr * lr + li * li
    cr = ((ar - 1.0) * lr + ai * li) / den
    ci = (ai * lr - (ar - 1.0) * li) / den
    bbr = cr[..., None] * b_re - ci[..., None] * b_im
    bbi = cr[..., None] * b_im + ci[..., None] * b_re
    k = jnp.arange(t_n + 1, dtype=F32)[:, None, None]
    pm, ang = jnp.exp(k * (lr * dt)), k * (li * dt)
    pr, pi = pm * jnp.cos(ang), pm * jnp.sin(ang)
    abr = pr[..., None] * bbr - pi[..., None] * bbi
    abi = pr[..., None] * bbi + pi[..., None] * bbr
    kk = (jnp.einsum('ghp,kgpj->kghj', c_re, abr[:t_n], precision=HP)
          - jnp.einsum('ghp,kgpj->kghj', c_im, abi[:t_n], precision=HP))
    lag = jnp.arange(t_n)[None, :] - jnp.arange(t_n)[:, None]
    onehot = (lag[None] == jnp.arange(t_n)[:, None, None]).astype(F32)
    g_n, h_n = c_re.shape[0], c_re.shape[1]
    toep = jnp.einsum('kst,kghj->gsjth', onehot, kk, precision=HP).reshape(g_n, t_n * h_n, t_n * h_n)
    win_r = abr[:t_n][::-1].transpose(1, 0, 3, 2).reshape(g_n, t_n * h_n, -1)
    win_i = abi[:t_n][::-1].transpose(1, 0, 3, 2).reshape(g_n, t_n * h_n, -1)
    p1r, p1i = pr[1:, :, None, :], pi[1:, :, None, :]
    wout_r = (c_re[None] * p1r - c_im[None] * p1i).transpose(1, 3, 0, 2).reshape(g_n, -1, t_n * h_n)
    wout_i = (-(c_re[None] * p1i + c_im[None] * p1r)).transpose(1, 3, 0, 2).reshape(g_n, -1, t_n * h_n)
    rows = [pr[t_n], pi[t_n]]
    qr, qi = pr[t_n], pi[t_n]
    for _ in range(int(math.log2(SSM_LO))):
        qr, qi = qr * qr - qi * qi, 2.0 * qr * qi
    d = 1
    while d < n_hi:
        rows += [qr, qi]
        qr, qi = qr * qr - qi * qi, 2.0 * qr * qi
        d *= 2
    coef = jnp.stack(rows, axis=1)
    return toep, win_r, win_i, wout_r, wout_i, coef


def _cumsum_call(name, x, reverse):
    n_rows, w = x.shape
    bm = min(256, n_rows)
    nb = n_rows // bm

    def body(x_ref, o_ref, carry_ref):
        i = pl.program_id(0)

        @pl.when(i == 0)
        def _():
            carry_ref[...] = jnp.zeros_like(carry_ref)

        ri = lax.broadcasted_iota(jnp.int32, (bm, bm), 0)
        ci = lax.broadcasted_iota(jnp.int32, (bm, bm), 1)
        tri = ((ri <= ci) if reverse else (ri >= ci)).astype(F32)
        xb = x_ref[...]
        o_ref[...] = _hp_dot(tri, xb) + carry_ref[...]
        carry_ref[...] += jnp.sum(xb, axis=0, keepdims=True)

    idx = (lambda i: (nb - 1 - i, 0)) if reverse else (lambda i: (i, 0))
    return pl.pallas_call(
        body, name=name, grid=(nb,), in_specs=[pl.BlockSpec((bm, w), idx)], out_specs=pl.BlockSpec((bm, w), idx),
        out_shape=jax.ShapeDtypeStruct(x.shape, F32), scratch_shapes=[pltpu.VMEM((1, w), F32)],
        compiler_params=_cparams(("arbitrary",)),
    )(x)


def make_cumsum(name):
    @jax.custom_vjp
    def cs(x):
        return _cumsum_call(name, x, False)

    def fwd(x):
        return cs(x), None

    def bwd(_, dy):
        return (_cumsum_call(name + "_bwd", dy, True),)

    cs.defvjp(fwd, bwd)
    return cs


def _fox_specs(n_rows, bq):
    q_spec = pl.BlockSpec((bq, HEAD_DIM), lambda h, i: (i, h))
    kv_spec = pl.BlockSpec((n_rows, HEAD_DIM), lambda h, i: (0, h))
    fr_spec = pl.BlockSpec((None, 1, n_rows), lambda h, i: (h, 0, 0))
    return q_spec, kv_spec, fr_spec


def _fox_fwd_call(name, q, k, v, fr):
    n_rows, width = q.shape
    bq = min(FOX_BQ, n_rows)
    nq = n_rows // bq
    q_spec, kv_spec, fr_spec = _fox_specs(n_rows, bq)

    def body(q_ref, k_ref, v_ref, fr_ref, o_ref):
        i = pl.program_id(1)
        for p in range(nq):
            n_keys = (p + 1) * bq

            @pl.when(i == p)
            def _(n_keys=n_keys):
                o_ref[...] = _fox_block(q_ref[...], k_ref[:n_keys, :], v_ref[:n_keys, :], fr_ref[:, :n_keys])

    return pl.pallas_call(
        body, name=name, grid=(width // HEAD_DIM, nq), in_specs=[q_spec, kv_spec, kv_spec, fr_spec], out_specs=q_spec,
        out_shape=jax.ShapeDtypeStruct(q.shape, F32), compiler_params=_cparams(("parallel", "parallel")),
    )(q, k, v, fr)


def _fox_bwd_call(name, q, k, v, fr, do):
    n_rows, width = q.shape
    bq = min(FOX_BQ, n_rows)
    nq = n_rows // bq
    q_spec, kv_spec, fr_spec = _fox_specs(n_rows, bq)

    def body(q_ref, k_ref, v_ref, fr_ref, do_ref, dq_ref, dk_ref, dv_ref, dfr_ref):
        i = pl.program_id(1)

        @pl.when(i == 0)
        def _():
            dk_ref[...] = jnp.zeros_like(dk_ref)
            dv_ref[...] = jnp.zeros_like(dv_ref)
            dfr_ref[...] = jnp.zeros_like(dfr_ref)

        for p in range(nq):
            n_keys = (p + 1) * bq

            @pl.when(i == p)
            def _(n_keys=n_keys):
                _, vjp = jax.vjp(_fox_block, q_ref[...], k_ref[:n_keys, :], v_ref[:n_keys, :], fr_ref[:, :n_keys])
                dq, dk, dv, dfr = vjp(do_ref[...])
                dq_ref[...] = dq
                dk_ref[:n_keys, :] += dk
                dv_ref[:n_keys, :] += dv
                dfr_ref[:, :n_keys] += dfr

    return pl.pallas_call(
        body, name=name, grid=(width // HEAD_DIM, nq), in_specs=[q_spec, kv_spec, kv_spec, fr_spec, q_spec],
        out_specs=[q_spec, kv_spec, kv_spec, fr_spec], out_shape=[jax.ShapeDtypeStruct(a.shape, F32) for a in (q, k, v, fr)],
        compiler_params=_cparams(("parallel", "arbitrary")),
    )(q, k, v, fr, do)


def make_fox(name):
    @jax.custom_vjp
    def fox(q, k, v, fr):
        return _fox_fwd_call(name, q, k, v, fr)

    def fwd(q, k, v, fr):
        return fox(q, k, v, fr), (q, k, v, fr)

    def bwd(res, do):
        return tuple(_fox_bwd_call(name + "_bwd", *res, do))

    fox.defvjp(fwd, bwd)
    return fox


def _relations(x, y, c):
    out = []
    for rel in range(1, N_DEV):
        px = 1 - x if rel & 4 else x
        py = 1 - y if rel & 2 else y
        pc = 1 - c if rel & 1 else c
        out.append((rel, (px, py, pc), 4 * px + 2 * py + pc))
    return out


_HBM_SPEC = pl.BlockSpec(memory_space=pltpu.HBM)
_SEM_SPEC = pl.BlockSpec(memory_space=pltpu.SEMAPHORE)
_DATAFLOW = pltpu.SideEffectType.DATAFLOW_SIDE_EFFECTING


def exchange_start(name, srcs, gather):
    n = len(srcs)
    me = 4 * lax.axis_index("x") + 2 * lax.axis_index("y") + lax.axis_index("c")
    lands = []
    for s in srcs:
        own = s if gather else lax.dynamic_index_in_dim(s, me, 0, keepdims=False)
        land = lax.empty((N_DEV,) + own.shape, s.dtype)
        lands.append(lax.dynamic_update_index_in_dim(land, own, me, 0))

    def body(*refs):
        src_refs, land_refs = refs[:n], refs[n:2 * n]
        send_sems, recv_sems, token = refs[2 * n], refs[2 * n + 1], refs[-1]
        x, y, c = lax.axis_index("x"), lax.axis_index("y"), lax.axis_index("c")
        mine = 4 * x + 2 * y + c
        for k in range(n):
            for rel, peer, peer_blk in _relations(x, y, c):
                pltpu.make_async_remote_copy(
                    src_ref=src_refs[k] if gather else src_refs[k].at[peer_blk], dst_ref=land_refs[k].at[mine],
                    send_sem=send_sems.at[7 * k + rel - 1], recv_sem=recv_sems.at[7 * k + rel - 1],
                    device_id=peer, device_id_type=pl.DeviceIdType.MESH).start()
        token[...] = jnp.zeros_like(token)

    sem = pltpu.SemaphoreType.DMA((7 * n,))
    hbm = lambda t: pltpu.HBM(t.shape, t.dtype)
    outs = pl.pallas_call(
        body, name=name,
        out_shape=(sem, sem, *[hbm(s) for s in srcs], *[hbm(l) for l in lands], jax.ShapeDtypeStruct((8, LANE), F32)),
        in_specs=[_HBM_SPEC] * (2 * n),
        out_specs=(_SEM_SPEC, _SEM_SPEC, *[_HBM_SPEC] * (2 * n), pl.BlockSpec(memory_space=pltpu.VMEM)),
        input_output_aliases={i: 2 + i for i in range(2 * n)},
        compiler_params=pltpu.CompilerParams(has_side_effects=_DATAFLOW),
    )(*[pltpu.with_memory_space_constraint(t, pltpu.HBM) for t in list(srcs) + lands])
    return dict(name=name, gather=gather, send=outs[0], recv=outs[1], srcs=list(outs[2:2 + n]), lands=list(outs[2 + n:2 + 2 * n]), token=outs[-1])


def exchange_wait(handle, k, after):
    gather = handle['gather']

    def body(src_ref, land_ref, send_sems, recv_sems, after_ref, src_out, land_out):
        x, y, c = lax.axis_index("x"), lax.axis_index("y"), lax.axis_index("c")
        for rel, peer, peer_blk in _relations(x, y, c):
            copy = pltpu.make_async_remote_copy(
                src_ref=src_ref if gather else src_ref.at[peer_blk], dst_ref=land_ref.at[peer_blk],
                send_sem=send_sems.at[7 * k + rel - 1], recv_sem=recv_sems.at[7 * k + rel - 1],
                device_id=peer, device_id_type=pl.DeviceIdType.MESH)
            copy.wait_send()
            copy.wait_recv()

    src, land = handle['srcs'][k], handle['lands'][k]
    return pl.pallas_call(
        body, name=f"{handle['name']}_wait{k}", out_shape=(pltpu.HBM(src.shape, src.dtype), pltpu.HBM(land.shape, land.dtype)),
        in_specs=[_HBM_SPEC, _HBM_SPEC, _SEM_SPEC, _SEM_SPEC, pl.BlockSpec(memory_space=pl.ANY)], out_specs=(_HBM_SPEC, _HBM_SPEC),
        input_output_aliases={0: 0, 1: 1}, compiler_params=pltpu.CompilerParams(has_side_effects=_DATAFLOW),
    )(src, land, handle['send'], handle['recv'], after)[1]


def adamw_reduce(name, partials, w, m, v):
    n_part, n_rows, n_cols = partials.shape
    br = n_rows
    for cand in (512, 256, 128, 64, 32, 16, 8):
        if n_rows % cand == 0 and n_part * cand * n_cols * partials.dtype.itemsize <= (4 << 20):
            br = cand
            break

    def body(p_ref, w_ref, m_ref, v_ref, g_ref, d_ref, nm_ref, nv_ref):
        g = p_ref[0].astype(F32)
        for s in range(1, n_part):
            g = g + p_ref[s].astype(F32)
        m_new = ADAM_B1 * m_ref[...] + (1.0 - ADAM_B1) * g
        v_new = ADAM_B2 * v_ref[...] + (1.0 - ADAM_B2) * jnp.square(g)
        m_hat = m_new / (1.0 - ADAM_B1 ** ADAM_STEP)
        v_hat = v_new / (1.0 - ADAM_B2 ** ADAM_STEP)
        g_ref[...] = g
        d_ref[...] = -ADAM_LR * (m_hat / (jnp.sqrt(v_hat) + ADAM_EPS) + ADAM_WD * w_ref[...])
        nm_ref[...] = m_new
        nv_ref[...] = v_new

    spec = pl.BlockSpec((br, n_cols), lambda i: (i, 0))
    return pl.pallas_call(
        body, name=name, grid=(n_rows // br,), in_specs=[pl.BlockSpec((n_part, br, n_cols), lambda i: (0, i, 0)), spec, spec, spec],
        out_specs=[spec] * 4, out_shape=[jax.ShapeDtypeStruct((n_rows, n_cols), F32)] * 4, compiler_params=_cparams(("parallel",)),
    )(partials, w, m, v)


def _pack(arrays, n_rows):
    flat = jnp.concatenate([a.reshape(-1) for a in arrays])
    return jnp.pad(flat, (0, n_rows * LANE - flat.shape[0])).reshape(n_rows, LANE)


def _unpack(packed, shapes):
    flat, out, off = packed.reshape(-1), [], 0
    for s in shapes:
        n = math.prod(s)
        out.append(flat[off:off + n].reshape(s))
        off += n
    return out


def _packed_rows(shapes):
    n = sum(math.prod(s) for s in shapes)
    return -(-n // (LANE * 512)) * 512


GATHER_ORDER = ['small', 'in_a', 'mem_kv0', 'glu', 'out0', 'kv', 'in_b', 'mem_kv1', 'out1']


BLOCKED = ('in_a', 'kv', 'in_b')


def local_step(p, gathered, x, mem, tgt, start_early, start_late):
    n_rows, d_model = x.shape
    main_w = 3 * d_model // 4
    mem_w = d_model - main_w
    n_groups = main_w // SSM_GROUP
    n_heads = main_w // HEAD_DIM
    n_hi = n_rows // (SSM_T * SSM_LO)
    row = lambda a: a.reshape(1, -1)
    tape, pending = {}, {}

    def mm_fwd(name, act, after=None):
        w = exchange_wait(gathered, GATHER_ORDER.index(name), act if after is None else after)
        if name not in BLOCKED:
            w = w.reshape(-1, w.shape[2])
        tape[name] = (act, w)
        return (_mm_nn_cb if name in BLOCKED else _mm_nn)(name, act, w)

    def mm_dw(name, dy, after=None):
        act, w = tape[name]
        if name in BLOCKED:
            dw = _mm_tn_cb(name + "_dw", act, dy, w.shape[2], GRAD_DTYPE, after)
        else:
            dw = _mm_tn(name + "_dw", act, dy, GRAD_DTYPE, after)
            dw = dw.reshape(N_DEV, dw.shape[0] // N_DEV, dw.shape[1])
        pending[name] = exchange_start("rs_" + name, [dw], gather=False)

    def mm_da(name, dy, after=None):
        return (_mm_nt_cb if name in BLOCKED else _mm_nt)(name + "_da", dy, tape[name][1], after)

    def mm_bwd(name, dy):
        mm_dw(name, dy)
        return mm_da(name, dy, pending[name]['token'])

    def split(proj):
        return proj[:, :main_w], proj[:, main_w:2 * main_w], proj[:, 2 * main_w:2 * main_w + mem_w], proj[:, 2 * main_w + mem_w:]

    def mem_attn(i, qm, zm, kvm):
        memo, = make_rowop(f_memattn, f"mem_attn{i}", 512, nl=mem_w // HEAD_DIM)((qm, zm), (kvm[:, :mem_w], kvm[:, mem_w:]))
        return memo

    def seg_norms(x_, pre_g0, mem_g0, mem_g1):
        hn, = make_rowop(f_norm, "pre_norm0", 256)((x_,), (row(pre_g0),))
        memn0, = make_rowop(f_norm, "mem_norm0", 256)((mem,), (row(mem_g0),))
        memn1, = make_rowop(f_norm, "mem_norm1", 256)((mem,), (row(mem_g1),))
        return hn, memn0, memn1

    def seg_a1(proj, kvm, lam_re, lam_im, log_step, b_re, b_im, c_re, c_im, d_skip):
        u, z, qm, zm = split(proj)
        ops = s5_operators(lam_re, lam_im, log_step, b_re, b_im, c_re, c_im, n_hi)
        ug = u.astype(BF16).reshape(n_hi, SSM_LO, SSM_T, n_groups, SSM_GROUP).transpose(3, 1, 0, 2, 4).reshape(n_groups, n_hi * SSM_LO, SSM_T * SSM_GROUP)
        yg = make_groupop(f_s5, "s5", SSM_GB)(ug, *ops)
        y = yg.reshape(n_groups, SSM_LO, n_hi, SSM_T, SSM_GROUP).transpose(2, 1, 3, 0, 4).reshape(n_rows, main_w)
        ygelu, = make_rowop(f_gate_a1, "gate_a1", 256)((y, u), (row(d_skip),))
        return ygelu, z, mem_attn(0, qm, zm, kvm)

    def seg_a2(ygelu, t, z, memo, b_glu):
        main, = make_rowop(f_gate_a2, "gate_a2", 256)((ygelu, t, z), (row(b_glu),))
        return jnp.concatenate([main, memo], axis=1)

    def seg_post_a(x_, o, post_g0, kv_g, pre_g1):
        return make_rowop(f_post_a, "post_a", 128)((x_, o), (row(post_g0), row(kv_g), row(pre_g1)))

    def seg_b(proj, kv, kv_in, kvm, w_fgate, b_fgate):
        w_fg = jnp.pad(w_fgate, ((0, 0), (0, LANE - n_heads)))
        b_fg = jnp.pad(b_fgate, (0, LANE - n_heads)).reshape(1, LANE)
        logf, = make_rowop(f_logf, "logf", 512)((make_mm("fgate")(kv_in, w_fg),), (b_fg,))
        fcum = make_cumsum("fcum")(logf)[:, :n_heads].T
        q, z, qm, zm = split(proj)
        att = make_fox("fox")(q, kv[:, :main_w], kv[:, main_w:], fcum[:, None, :])
        main, = make_rowop(f_gate_b, "gate_b", 256)((att, z), ())
        return jnp.concatenate([main, mem_attn(1, qm, zm, kvm)], axis=1)

    def seg_final(h1, o, post_g1):
        rowloss, = make_rowop(f_final, "final", 128)((h1, o, tgt), (row(post_g1),))
        return jnp.sum(rowloss)

    (hn0, memn0, memn1), vjp_norms = jax.vjp(seg_norms, x, p['pre_norm_g'][0], p['mem_norm_g'][0], p['mem_norm_g'][1])
    proj_a = mm_fwd("in_a", hn0)
    kvm0 = mm_fwd("mem_kv0", memn0, proj_a)
    s5_names = ('lam_re', 'lam_im', 'log_step', 'b_re', 'b_im', 'c_re', 'c_im', 'd_skip')
    (ygelu, z_a, memo0), vjp_a1 = jax.vjp(seg_a1, proj_a, kvm0, *[p[n] for n in s5_names])
    t = mm_fwd("glu", ygelu)
    cat0, vjp_a2 = jax.vjp(seg_a2, ygelu, t, z_a, memo0, p['b_glu'])
    o0 = mm_fwd("out0", cat0)
    (h1, kv_in, hn1), vjp_post_a = jax.vjp(seg_post_a, x, o0, p['post_norm_g'][0], p['kv_norm_g'], p['pre_norm_g'][1])
    kv = mm_fwd("kv", kv_in)
    proj_b = mm_fwd("in_b", hn1)
    kvm1 = mm_fwd("mem_kv1", memn1, proj_b)
    cat1, vjp_b = jax.vjp(seg_b, proj_b, kv, kv_in, kvm1, p['w_fgate'], p['b_fgate'])
    o1 = mm_fwd("out1", cat1)
    loss, vjp_final = jax.vjp(seg_final, h1, o1, p['post_norm_g'][1])

    g = {}
    d_h1, d_o1, g_post_g1 = vjp_final(jnp.ones((), F32))
    d_proj_b, d_kv, d_kv_in, d_kvm1, g['w_fgate'], g['b_fgate'] = vjp_b(mm_bwd("out1", d_o1))
    d_memn1 = mm_bwd("mem_kv1", d_kvm1)
    d_hn1 = mm_bwd("in_b", d_proj_b)
    d_kv_in = d_kv_in + mm_bwd("kv", d_kv)
    d_x, d_o0, g_post_g0, g['kv_norm_g'], g_pre_g1 = vjp_post_a((d_h1, d_kv_in, d_hn1))
    d_ygelu, d_t, d_z, d_memo0, g['b_glu'] = vjp_a2(mm_bwd("out0", d_o0))
    d_ygelu = d_ygelu + mm_bwd("glu", d_t)
    d_proj_a, d_kvm0, *g_s5 = vjp_a1((d_ygelu, d_z, d_memo0))
    g.update(zip(s5_names, g_s5))
    g['pre_norm_g'] = jnp.stack([jnp.zeros_like(g_pre_g1), g_pre_g1])
    g['post_norm_g'] = jnp.stack([g_post_g0, g_post_g1])
    g['mem_norm_g'] = jnp.zeros_like(p['mem_norm_g'])
    early = start_early(g)
    d_memn0 = mm_bwd("mem_kv0", d_kvm0)
    d_x2, *g_late = vjp_norms((mm_da("in_a", d_proj_a, early['token']), d_memn0, d_memn1))
    late = start_late(g_late)
    mm_dw("in_a", d_proj_a, late['token'])
    return loss, d_x + d_x2, pending, early, late


def kernel(x, mem, pre_norm_g, post_norm_g, w_in_a, lam_re, lam_im, log_step, b_re, b_im, c_re, c_im, d_skip, w_glu, b_glu, kv_norm_g, w_kv, w_fgate, b_fgate, w_in_b, mem_norm_g, w_mem_kv, w_out, loss_target, m_pre_norm_g, m_post_norm_g, m_w_in_a, m_lam_re, m_lam_im, m_log_step, m_b_re, m_b_im, m_c_re, m_c_im, m_d_skip, m_w_glu, m_b_glu, m_kv_norm_g, m_w_kv, m_w_fgate, m_b_fgate, m_w_in_b, m_mem_norm_g, m_w_mem_kv, m_w_out, v_pre_norm_g, v_post_norm_g, v_w_in_a, v_lam_re, v_lam_im, v_log_step, v_b_re, v_b_im, v_c_re, v_c_im, v_d_skip, v_w_glu, v_b_glu, v_kv_norm_g, v_w_kv, v_w_fgate, v_b_fgate, v_w_in_b, v_mem_norm_g, v_w_mem_kv, v_w_out):
    a = dict(zip(INPUTS, (x, mem, pre_norm_g, post_norm_g, w_in_a, lam_re, lam_im, log_step, b_re, b_im, c_re, c_im, d_skip, w_glu, b_glu, kv_norm_g, w_kv, w_fgate, b_fgate, w_in_b, mem_norm_g, w_mem_kv, w_out, loss_target, m_pre_norm_g, m_post_norm_g, m_w_in_a, m_lam_re, m_lam_im, m_log_step, m_b_re, m_b_im, m_c_re, m_c_im, m_d_skip, m_w_glu, m_b_glu, m_kv_norm_g, m_w_kv, m_w_fgate, m_b_fgate, m_w_in_b, m_mem_norm_g, m_w_mem_kv, m_w_out, v_pre_norm_g, v_post_norm_g, v_w_in_a, v_lam_re, v_lam_im, v_log_step, v_b_re, v_b_im, v_c_re, v_c_im, v_d_skip, v_w_glu, v_b_glu, v_kv_norm_g, v_w_kv, v_w_fgate, v_b_fgate, v_w_in_b, v_mem_norm_g, v_w_mem_kv, v_w_out)))
    me = 4 * lax.axis_index("x") + 2 * lax.axis_index("y") + lax.axis_index("c")
    n_layers = w_out.shape[0]

    small_shapes = [a[n].shape for n in SMALL_SHARDED]
    small_rows = -(-sum(math.prod(s) for s in small_shapes) // (LANE * 8)) * 8
    shards = dict(in_a=w_in_a[0], mem_kv0=w_mem_kv[0], glu=w_glu[0], out0=w_out[0], kv=w_kv, in_b=w_in_b[0], mem_kv1=w_mem_kv[1], out1=w_out[1])
    operands = [_pack([a[n] for n in SMALL_SHARDED], small_rows)] + [shards[n].astype(BF16) for n in GATHER_ORDER[1:]]
    gathered = exchange_start("ag_weights", operands, gather=True)
    small = exchange_wait(gathered, 0, gathered['token'])
    small = [jnp.stack(parts) for parts in zip(*[_unpack(small[b], small_shapes) for b in range(N_DEV)])]
    p = {n: a[n] for n in REPLICATED}
    for n in ('lam_re', 'lam_im', 'log_step', 'b_re', 'b_im', 'c_re', 'c_im'):
        p[n] = p[n][0]
    p['d_skip'] = small[0].reshape(-1)
    p['b_glu'] = small[1].reshape(-1)
    p['w_fgate'] = small[2].reshape(-1, w_fgate.shape[1])
    p['pre_norm_g'] = p['pre_norm_g'] + gathered['token'][0, 0]

    full_shapes = [a[n].shape for n in REPLICATED] + [(1, N_DEV * d_skip.shape[1]), (1, N_DEV * b_glu.shape[1]), (N_DEV * w_fgate.shape[0], w_fgate.shape[1])]
    rows = _packed_rows(full_shapes)
    late_rows = 3 * pre_norm_g.shape[1] // LANE
    start_early = lambda g: exchange_start("ag_grads", [_pack([g[n] for n in REPLICATED + SMALL_SHARDED], rows)], gather=True)
    start_late = lambda g_late: exchange_start("ag_grads_late", [_pack(g_late, late_rows)], gather=True)

    loss_local, grad_x, pending, early, late = local_step(p, gathered, x[0], mem[0], loss_target[0], start_early, start_late)
    loss = lax.psum(loss_local, MESH_AXES)

    out = {}

    def update(tag, name, layer=None):
        pick = (lambda t: t) if layer is None else (lambda t: t[layer])
        parts = exchange_wait(pending[tag], 0, grad_x)
        two_d = lambda t: pick(t).reshape(parts.shape[1:])
        res = adamw_reduce("adamw_" + tag, parts, two_d(a[name]), two_d(a['m_' + name]), two_d(a['v_' + name]))
        return [r.reshape(pick(a[name]).shape) for r in res]

    per_layer = {n: [None] * n_layers for n in ('w_out', 'w_mem_kv')}
    per_layer['w_out'][1] = update("out1", 'w_out', 1)
    per_layer['w_mem_kv'][1] = update("mem_kv1", 'w_mem_kv', 1)
    out['w_in_b'] = [r[None] for r in update("in_b", 'w_in_b', 0)]
    out['w_kv'] = update("kv", 'w_kv')
    per_layer['w_out'][0] = update("out0", 'w_out', 0)
    per_layer['w_mem_kv'][0] = update("mem_kv0", 'w_mem_kv', 0)
    out['w_glu'] = [r[None] for r in update("glu", 'w_glu', 0)]
    for name in per_layer:
        out[name] = [jnp.stack(t) for t in zip(*per_layer[name])]

    zeros = [jnp.zeros(s, F32) for s in full_shapes[len(REPLICATED):]]
    packed = lambda pre: _pack([a[pre + n] for n in REPLICATED] + zeros, rows)
    res = [_unpack(r, full_shapes) for r in adamw_reduce("adamw_small", exchange_wait(early, 0, grad_x), packed(''), packed('m_'), packed('v_'))]
    for i, n in enumerate(REPLICATED):
        out[n] = [r[i] for r in res]
    g_full = res[0][len(REPLICATED):]
    g_shard = [lax.dynamic_slice_in_dim(g_full[0], me * d_skip.shape[1], d_skip.shape[1], 1),
               lax.dynamic_slice_in_dim(g_full[1], me * b_glu.shape[1], b_glu.shape[1], 1),
               lax.dynamic_slice_in_dim(g_full[2], me * w_fgate.shape[0], w_fgate.shape[0], 0)]
    packed = lambda pre: _pack([a[pre + n] for n in SMALL_SHARDED], small_rows)
    res = [_unpack(r, small_shapes) for r in adamw_reduce("adamw_small_sharded", _pack(g_shard, small_rows)[None], packed(''), packed('m_'), packed('v_'))]
    for i, n in enumerate(SMALL_SHARDED):
        out[n] = [r[i] for r in res]
    packed = lambda pre: _pack([a[pre + 'pre_norm_g'][0], a[pre + 'mem_norm_g'][0], a[pre + 'mem_norm_g'][1]], late_rows)
    res = [_unpack(r, [pre_norm_g.shape[1:]] * 3) for r in adamw_reduce("adamw_small_late", exchange_wait(late, 0, grad_x), packed(''), packed('m_'), packed('v_'))]
    out['pre_norm_g'] = [jnp.stack([r[0], o[1]]) for r, o in zip(res, out['pre_norm_g'])]
    out['mem_norm_g'] = [jnp.stack([r[1], r[2]]) for r in res]
    out['w_in_a'] = [r[None] for r in update("in_a", 'w_in_a', 0)]

    return (loss, grad_x[None], *[out[n][k] for k in range(4) for n in WEIGHTS])
```

```python
import functools
import math

import jax
import jax.numpy as jnp
from jax import lax
from jax.experimental import pallas as pl
from jax.experimental.pallas import tpu as pltpu

F32 = jnp.float32
BF16 = jnp.bfloat16
HP = lax.Precision.HIGHEST
MESH_AXES = ("x", "y", "c")
N_DEV = 8
V7X_VMEM_LIMIT = 56 * 1024 * 1024
LANE = 128

EPS = 1e-6
HEAD_DIM = 128
SSM_GROUP = 16
SSM_STATE = 64
SSM_T = 16
SSM_LO = 8
SSM_GB = 8
FOX_BQ = 256
GRAD_DTYPE = BF16
ACT_DTYPE = BF16
ADAM_LR = 0.001
ADAM_B1 = 0.9
ADAM_B2 = 0.999
ADAM_EPS = 1e-08
ADAM_WD = 0.01
ADAM_STEP = 10

WEIGHTS = ['pre_norm_g', 'post_norm_g', 'w_in_a', 'lam_re', 'lam_im', 'log_step', 'b_re', 'b_im', 'c_re', 'c_im', 'd_skip',
           'w_glu', 'b_glu', 'kv_norm_g', 'w_kv', 'w_fgate', 'b_fgate', 'w_in_b', 'mem_norm_g', 'w_mem_kv', 'w_out']
INPUTS = ['x', 'mem'] + WEIGHTS + ['loss_target'] + ['m_' + n for n in WEIGHTS] + ['v_' + n for n in WEIGHTS]
REPLICATED = ['pre_norm_g', 'post_norm_g', 'lam_re', 'lam_im', 'log_step', 'b_re', 'b_im', 'c_re', 'c_im', 'kv_norm_g',
              'b_fgate', 'mem_norm_g']
SMALL_SHARDED = ['d_skip', 'b_glu', 'w_fgate']


def _cparams(sem=None):
    return pltpu.CompilerParams(dimension_semantics=sem, vmem_limit_bytes=V7X_VMEM_LIMIT)


def _tile(n, cap):
    if n <= cap:
        return n
    best = None
    for t in range(LANE, cap + 1, LANE):
        if n % t == 0:
            best = t
    assert best is not None, (n, cap)
    return best


def _matmul(name, a, b, a_spec, b_spec, o_spec, out_shape, grid, dims, nk, after=None):
    def body(a_ref, b_ref, *rest):
        o_ref, acc_ref = rest[-2:]
        k = pl.program_id(2)
        part = lax.dot_general(a_ref[...].astype(BF16), b_ref[...].astype(BF16), dims, preferred_element_type=F32)

        @pl.when(k == 0)
        def _():
            acc_ref[...] = part

        @pl.when(k > 0)
        def _():
            acc_ref[...] += part

        @pl.when(k == nk - 1)
        def _():
            o_ref[...] = acc_ref[...].astype(o_ref.dtype)

    acc_shape = tuple(d for d in o_spec.block_shape if d is not None)
    extra = [] if after is None else [after]
    return pl.pallas_call(
        body, name=name, grid=grid, in_specs=[a_spec, b_spec] + [pl.BlockSpec(memory_space=pl.ANY)] * len(extra), out_specs=o_spec,
        out_shape=out_shape, scratch_shapes=[pltpu.VMEM(acc_shape, F32)],
        compiler_params=_cparams(("parallel", "parallel", "arbitrary")),
    )(a, b, *extra)


NN = (((1,), (0,)), ((), ()))
NT = (((1,), (1,)), ((), ()))
TN = (((0,), (0,)), ((), ()))


def _mm_nn(name, a, b, out_dtype=ACT_DTYPE):
    (m, k), (_, n) = a.shape, b.shape
    bm, bn, bk = _tile(m, 512), _tile(n, 512), _tile(k, 2048)
    return _matmul(name, a, b, pl.BlockSpec((bm, bk), lambda i, j, kk: (i, kk)), pl.BlockSpec((bk, bn), lambda i, j, kk: (kk, j)),
                   pl.BlockSpec((bm, bn), lambda i, j, kk: (i, j)), jax.ShapeDtypeStruct((m, n), out_dtype), (m // bm, n // bn, k // bk), NN, k // bk)


def _mm_nt(name, a, b, after=None, out_dtype=ACT_DTYPE):
    (m, c), (n, _) = a.shape, b.shape
    bm, bn, bk = _tile(m, 512), _tile(n, 512), _tile(c, 2048)
    return _matmul(name, a, b, pl.BlockSpec((bm, bk), lambda i, j, kk: (i, kk)), pl.BlockSpec((bn, bk), lambda i, j, kk: (j, kk)),
                   pl.BlockSpec((bm, bn), lambda i, j, kk: (i, j)), jax.ShapeDtypeStruct((m, n), out_dtype), (m // bm, n // bn, c // bk), NT, c // bk, after)


def _mm_tn(name, a, b, out_dtype=F32, after=None):
    (c, m), (_, n) = a.shape, b.shape
    bm, bn, bk = _tile(m, 512), _tile(n, 512), _tile(c, 2048)
    return _matmul(name, a, b, pl.BlockSpec((bk, bm), lambda i, j, kk: (kk, i)), pl.BlockSpec((bk, bn), lambda i, j, kk: (kk, j)),
                   pl.BlockSpec((bm, bn), lambda i, j, kk: (i, j)), jax.ShapeDtypeStruct((m, n), out_dtype), (m // bm, n // bn, c // bk), TN, c // bk, after)


def _mm_nn_cb(name, a, bb):
    (m, k), (nb, _, ns) = a.shape, bb.shape
    bm, bk = _tile(m, 512), _tile(k, 2048)
    return _matmul(name, a, bb, pl.BlockSpec((bm, bk), lambda i, j, kk: (i, kk)), pl.BlockSpec((None, bk, ns), lambda i, j, kk: (j, kk, 0)),
                   pl.BlockSpec((bm, ns), lambda i, j, kk: (i, j)), jax.ShapeDtypeStruct((m, nb * ns), ACT_DTYPE), (m // bm, nb, k // bk), NN, k // bk)


def _mm_nt_cb(name, dy, bb, after=None):
    m, (nb, k, ns) = dy.shape[0], bb.shape
    bm, bn = _tile(m, 1024), _tile(k, 1024)
    return _matmul(name, dy, bb, pl.BlockSpec((bm, ns), lambda i, j, kk: (i, kk)), pl.BlockSpec((None, bn, ns), lambda i, j, kk: (kk, j, 0)),
                   pl.BlockSpec((bm, bn), lambda i, j, kk: (i, j)), jax.ShapeDtypeStruct((m, k), ACT_DTYPE), (m // bm, k // bn, nb), NT, nb, after)


def _mm_tn_cb(name, a, dy, ns, out_dtype=F32, after=None):
    (c, k), nb = a.shape, dy.shape[1] // ns
    bm = _tile(k, 512)
    return _matmul(name, a, dy, pl.BlockSpec((c, bm), lambda i, j, kk: (0, i)), pl.BlockSpec((c, ns), lambda i, j, kk: (0, j)),
                   pl.BlockSpec((None, bm, ns), lambda i, j, kk: (j, i, 0)), jax.ShapeDtypeStruct((nb, k, ns), out_dtype), (k // bm, nb, 1), TN, 1, after)


def make_mm(name):
    @jax.custom_vjp
    def mm(a, b):
        return _mm_nn(name, a, b, F32)

    def fwd(a, b):
        return mm(a, b), (a, b)

    def bwd(res, dy):
        a, b = res
        return _mm_nt(name + "_da", dy, b, out_dtype=a.dtype), _mm_tn(name + "_dw", a, dy, b.dtype)

    mm.defvjp(fwd, bwd)
    return mm


def _rowop_specs(rows, params, bm, nl):
    row_specs = [pl.BlockSpec((bm, r.shape[1] // nl), lambda j, i: (i, j)) for r in rows]
    par_specs = [pl.BlockSpec((p.shape[0], p.shape[1] // nl), lambda j, i: (0, j)) for p in params]
    row_blk = [jax.ShapeDtypeStruct((bm, r.shape[1] // nl), r.dtype) for r in rows]
    par_blk = [jax.ShapeDtypeStruct((p.shape[0], p.shape[1] // nl), p.dtype) for p in params]
    return row_specs, par_specs, row_blk, par_blk


def make_rowop(f, name, bm, nl=1, out_dtypes=None):
    def loaded(refs):
        return [r[...].astype(F32) for r in refs]

    def fwd_call(rows, params):
        n_rows = rows[0].shape[0]
        b = min(bm, n_rows)
        row_specs, par_specs, row_blk, par_blk = _rowop_specs(rows, params, b, nl)
        out_blk = jax.eval_shape(f, *[jax.ShapeDtypeStruct(t.shape, F32) for t in row_blk + par_blk])
        dts = out_dtypes or [F32] * len(out_blk)
        nr, npar = len(rows), len(params)

        def body(*refs):
            for o_ref, o in zip(refs[nr + npar:], f(*loaded(refs[:nr + npar]))):
                o_ref[...] = o.astype(o_ref.dtype)

        return pl.pallas_call(
            body, name=name, grid=(nl, n_rows // b), in_specs=row_specs + par_specs,
            out_specs=[pl.BlockSpec(o.shape, lambda j, i: (i, j)) for o in out_blk],
            out_shape=[jax.ShapeDtypeStruct((n_rows, o.shape[1] * nl), dt) for o, dt in zip(out_blk, dts)],
            compiler_params=_cparams(("parallel", "parallel")),
        )(*rows, *params)

    def bwd_call(rows, params, cts):
        n_rows = rows[0].shape[0]
        b = min(bm, n_rows)
        row_specs, par_specs, row_blk, par_blk = _rowop_specs(rows, params, b, nl)
        ct_specs = [pl.BlockSpec((b, c.shape[1] // nl), lambda j, i: (i, j)) for c in cts]
        nr, npar, nct = len(rows), len(params), len(cts)

        def body(*refs):
            i = pl.program_id(1)
            _, vjp = jax.vjp(lambda *v: tuple(f(*v)), *loaded(refs[:nr + npar]))
            grads = vjp(tuple(loaded(refs[nr + npar:nr + npar + nct])))
            outs = refs[nr + npar + nct:]
            for o_ref, g in zip(outs[:nr], grads[:nr]):
                o_ref[...] = g.astype(o_ref.dtype)
            for o_ref, g in zip(outs[nr:], grads[nr:]):
                @pl.when(i == 0)
                def _(o_ref=o_ref, g=g):
                    o_ref[...] = g

                @pl.when(i > 0)
                def _(o_ref=o_ref, g=g):
                    o_ref[...] += g

        outs = pl.pallas_call(
            body, name=name + "_bwd", grid=(nl, n_rows // b), in_specs=row_specs + par_specs + ct_specs,
            out_specs=row_specs + par_specs,
            out_shape=[jax.ShapeDtypeStruct(a.shape, a.dtype) for a in rows] + [jax.ShapeDtypeStruct(a.shape, F32) for a in params],
            compiler_params=_cparams(("arbitrary", "arbitrary")),
        )(*rows, *params, *cts)
        return tuple(outs[:nr]), tuple(g.astype(a.dtype) for g, a in zip(outs[nr:], params))

    @jax.custom_vjp
    def op(rows, params):
        return tuple(fwd_call(rows, params))

    def fwd(rows, params):
        return op(rows, params), (rows, params)

    def bwd(res, cts):
        rows, params = res
        return bwd_call(rows, params, tuple(cts))

    op.defvjp(fwd, bwd)
    return op


def make_groupop(f, name, gb):
    def specs(arrs):
        return [pl.BlockSpec((gb,) + a.shape[1:], lambda g: (g, 0, 0)) for a in arrs]

    def fwd_call(arrs):
        g_n = arrs[0].shape[0]
        out_blk = jax.eval_shape(f, *[jax.ShapeDtypeStruct((gb,) + a.shape[1:], a.dtype) for a in arrs])[0]
        n = len(arrs)

        def body(*refs):
            for i, o in enumerate(f(*[r[...] for r in refs[:n]])):
                refs[n][i] = o.astype(refs[n].dtype)

        return pl.pallas_call(
            body, name=name, grid=(g_n // gb,), in_specs=specs(arrs),
            out_specs=pl.BlockSpec((gb,) + out_blk.shape, lambda g: (g, 0, 0)),
            out_shape=jax.ShapeDtypeStruct((g_n,) + out_blk.shape, ACT_DTYPE),
            compiler_params=_cparams(("parallel",)),
        )(*arrs)

    def bwd_call(arrs, ct):
        g_n = arrs[0].shape[0]
        n = len(arrs)

        def body(*refs):
            _, vjp = jax.vjp(lambda *v: tuple(f(*v)), *[r[...] for r in refs[:n]])
            for o_ref, g in zip(refs[n + 1:], vjp(tuple(refs[n][i].astype(F32) for i in range(gb)))):
                o_ref[...] = g.astype(o_ref.dtype)

        return pl.pallas_call(
            body, name=name + "_bwd", grid=(g_n // gb,), in_specs=specs(arrs) + specs([ct]), out_specs=specs(arrs),
            out_shape=[jax.ShapeDtypeStruct(a.shape, a.dtype) for a in arrs],
            compiler_params=_cparams(("parallel",)),
        )(*arrs, ct)

    @jax.custom_vjp
    def op(*arrs):
        return fwd_call(arrs)

    def fwd(*arrs):
        return op(*arrs), arrs

    def bwd(arrs, ct):
        return tuple(bwd_call(arrs, ct))

    op.defvjp(fwd, bwd)
    return op


def _rms(x, g):
    return x * lax.rsqrt(jnp.mean(x * x, axis=-1, keepdims=True) + EPS) * g


def _silu(z):
    return z * jax.nn.sigmoid(z)


def _log_sigmoid(x):
    return jnp.minimum(x, 0.0) - jnp.log(1.0 + jnp.exp(-jnp.abs(x)))


def f_norm(x, g):
    return (_rms(x, g),)


def f_gate_a1(y, u, d):
    return (jax.nn.gelu(y + d * u),)


def f_gate_a2(yg, t, z, b):
    return (yg * jax.nn.sigmoid(t + b) * _silu(z),)


def f_gate_b(att, z):
    return (att * _silu(z),)


def f_post_a(h, o, post_g, kv_g, pre_g):
    h1 = h + _rms(o, post_g)
    return h1, _rms(h1, kv_g), _rms(h1, pre_g)


def f_final(h, o, tgt, post_g):
    err = h + _rms(o, post_g) - tgt
    return (0.5 * jnp.mean(err * err, axis=-1, keepdims=True),)


def f_logf(gl, b):
    return (_log_sigmoid(gl + b),)


def _mxu(a, b, dims):
    return lax.dot_general(a.astype(BF16), b.astype(BF16), dims, preferred_element_type=F32)


@jax.custom_vjp
def _mxu_nn(a, b):
    return _mxu(a, b, NN)


def _mxu_nn_bwd(res, g):
    a, b = res
    return _mxu(g, b, NT).astype(a.dtype), _mxu(a, g, TN).astype(b.dtype)


_mxu_nn.defvjp(lambda a, b: (_mxu(a, b, NN), (a, b)), _mxu_nn_bwd)


@jax.custom_vjp
def _mxu_nt(a, b):
    return _mxu(a, b, NT)


def _mxu_nt_bwd(res, g):
    a, b = res
    return _mxu(g, b, NN).astype(a.dtype), _mxu(g, a, TN).astype(b.dtype)


_mxu_nt.defvjp(lambda a, b: (_mxu(a, b, NT), (a, b)), _mxu_nt_bwd)


def f_memattn(q, zm, km, vm):
    o = _mxu_nn(_softmax(_mxu_nt(q * (HEAD_DIM ** -0.5), km)), vm)
    return (o * _silu(zm),)


@jax.custom_vjp
def _softmax(s):
    e = jnp.exp(s - jnp.max(s, axis=-1, keepdims=True))
    return e * (1.0 / jnp.sum(e, axis=-1, keepdims=True))


def _softmax_fwd(s):
    p = _softmax(s)
    return p, p


def _softmax_bwd(p, dp):
    return (p * (dp - jnp.sum(dp * p, axis=-1, keepdims=True)),)


_softmax.defvjp(_softmax_fwd, _softmax_bwd)


def _fox_block(q, k, v, fr):
    bq = q.shape[0]
    s = _mxu_nt(q * (HEAD_DIM ** -0.5), k) - fr
    tri = lax.broadcasted_iota(jnp.int32, (bq, bq), 0) >= lax.broadcasted_iota(jnp.int32, (bq, bq), 1)
    diag = jnp.where(tri, s[:, -bq:], -1e30)
    s = diag if k.shape[0] == bq else jnp.concatenate([s[:, :-bq], diag], axis=1)
    return _mxu_nn(_softmax(s), v)


def _cmul(ar, ai, xr, xi):
    return ar * xr - ai * xi, ar * xi + ai * xr


def _hp_dot(a, b):
    return jnp.dot(a, b, precision=HP, preferred_element_type=F32)


_bf_dot = _mxu_nn


def f_s5(u, toep, win_r, win_i, wout_r, wout_i, coef):
    gb = u.shape[0]
    n_hi = u.shape[1] // SSM_LO
    sr = [_bf_dot(u[i], win_r[i]) for i in range(gb)]
    si = [_bf_dot(u[i], win_i[i]) for i in range(gb)]

    def stacked(parts, lo):
        return jnp.concatenate([a[lo * n_hi:(lo + 1) * n_hi] for a in parts], axis=0)

    def coef_rows(k):
        return jnp.concatenate([jnp.broadcast_to(coef[i, k:k + 1, :], (n_hi, coef.shape[2])) for i in range(gb)], axis=0)

    at_r, at_i = coef_rows(0), coef_rows(1)
    pr, pi = stacked(sr, 0), stacked(si, 0)
    for lo in range(1, SSM_LO):
        dr, di = _cmul(at_r, at_i, pr, pi)
        pr, pi = stacked(sr, lo) + dr, stacked(si, lo) + di
    n = 2 * gb * n_hi
    ri = lax.broadcasted_iota(jnp.int32, (n, n), 0)
    ci = lax.broadcasted_iota(jnp.int32, (n, n), 1)
    same = ri // n_hi == ci // n_hi

    def shifted(d, zr, zi):
        z = _hp_dot((same & (ri - ci == d)).astype(F32), jnp.concatenate([zr, zi], axis=0))
        return z[:n // 2], z[n // 2:]

    d, step = 1, 0
    while d < n_hi:
        dr, di = _cmul(coef_rows(2 + 2 * step), coef_rows(3 + 2 * step), *shifted(d, pr, pi))
        pr, pi = pr + dr, pi + di
        d, step = 2 * d, step + 1
    er, ei = shifted(1, pr, pi)
    xr, xi = [er], [ei]
    for lo in range(1, SSM_LO):
        dr, di = _cmul(at_r, at_i, xr[-1], xi[-1])
        xr.append(stacked(sr, lo - 1) + dr)
        xi.append(stacked(si, lo - 1) + di)

    def group_rows(parts, i):
        return jnp.concatenate([a[i * n_hi:(i + 1) * n_hi] for a in parts], axis=0)

    return tuple(_bf_dot(u[i], toep[i]) + _bf_dot(group_rows(xr, i), wout_r[i]) + _bf_dot(group_rows(xi, i), wout_i[i])
                 for i in range(gb))


def s5_operators(lam_re, lam_im, log_step, b_re, b_im, c_re, c_im, n_hi):
    t_n = SSM_T
    lr, li = lam_re, lam_im
    dt = jnp.exp(log_step)[:, None]
    mag = jnp.exp(lr * dt)
    ar, ai = mag * jnp.cos(li * dt), mag * jnp.sin(li * dt)
    den = lr * lr + li * li
    cr = ((ar - 1.0) * lr + ai * li) / den
    ci = (ai * lr - (ar - 1.0) * li) / den
    bbr = cr[..., None] * b_re - ci[..., None] * b_im
    bbi = cr[..., None] * b_im + ci[..., None] * b_re
    k = jnp.arange(t_n + 1, dtype=F32)[:, None, None]
    pm, ang = jnp.exp(k * (lr * dt)), k * (li * dt)
    pr, pi = pm * jnp.cos(ang), pm * jnp.sin(ang)
    abr = pr[..., None] * bbr - pi[..., None] * bbi
    abi = pr[..., None] * bbi + pi[..., None] * bbr
    kk = (jnp.einsum('ghp,kgpj->kghj', c_re, abr[:t_n], precision=HP)
          - jnp.einsum('ghp,kgpj->kghj', c_im, abi[:t_n], precision=HP))
    lag = jnp.arange(t_n)[None, :] - jnp.arange(t_n)[:, None]
    onehot = (lag[None] == jnp.arange(t_n)[:, None, None]).astype(F32)
    g_n, h_n = c_re.shape[0], c_re.shape[1]
    toep = jnp.einsum('kst,kghj->gsjth', onehot, kk, precision=HP).reshape(g_n, t_n * h_n, t_n * h_n)
    win_r = abr[:t_n][::-1].transpose(1, 0, 3, 2).reshape(g_n, t_n * h_n, -1)
    win_i = abi[:t_n][::-1].transpose(1, 0, 3, 2).reshape(g_n, t_n * h_n, -1)
    p1r, p1i = pr[1:, :, None, :], pi[1:, :, None, :]
    wout_r = (c_re[None] * p1r - c_im[None] * p1i).transpose(1, 3, 0, 2).reshape(g_n, -1, t_n * h_n)
    wout_i = (-(c_re[None] * p1i + c_im[None] * p1r)).transpose(1, 3, 0, 2).reshape(g_n, -1, t_n * h_n)
    rows = [pr[t_n], pi[t_n]]
    qr, qi = pr[t_n], pi[t_n]
    for _ in range(int(math.log2(SSM_LO))):
        qr, qi = qr * qr - qi * qi, 2.0 * qr * qi
    d = 1
    while d < n_hi:
        rows += [qr, qi]
        qr, qi = qr * qr - qi * qi, 2.0 * qr * qi
        d *= 2
    coef = jnp.stack(rows, axis=1)
    return toep, win_r, win_i, wout_r, wout_i, coef


def _cumsum_call(name, x, reverse):
    n_rows, w = x.shape
    bm = min(256, n_rows)
    nb = n_rows // bm

    def body(x_ref, o_ref, carry_ref):
        i = pl.program_id(0)

        @pl.when(i == 0)
        def _():
            carry_ref[...] = jnp.zeros_like(carry_ref)

        ri = lax.broadcasted_iota(jnp.int32, (bm, bm), 0)
        ci = lax.broadcasted_iota(jnp.int32, (bm, bm), 1)
        tri = ((ri <= ci) if reverse else (ri >= ci)).astype(F32)
        xb = x_ref[...]
        o_ref[...] = _hp_dot(tri, xb) + carry_ref[...]
        carry_ref[...] += jnp.sum(xb, axis=0, keepdims=True)

    idx = (lambda i: (nb - 1 - i, 0)) if reverse else (lambda i: (i, 0))
    return pl.pallas_call(
        body, name=name, grid=(nb,), in_specs=[pl.BlockSpec((bm, w), idx)], out_specs=pl.BlockSpec((bm, w), idx),
        out_shape=jax.ShapeDtypeStruct(x.shape, F32), scratch_shapes=[pltpu.VMEM((1, w), F32)],
        compiler_params=_cparams(("arbitrary",)),
    )(x)


def make_cumsum(name):
    @jax.custom_vjp
    def cs(x):
        return _cumsum_call(name, x, False)

    def fwd(x):
        return cs(x), None

    def bwd(_, dy):
        return (_cumsum_call(name + "_bwd", dy, True),)

    cs.defvjp(fwd, bwd)
    return cs


def _fox_specs(n_rows, bq):
    q_spec = pl.BlockSpec((bq, HEAD_DIM), lambda h, i: (i, h))
    kv_spec = pl.BlockSpec((n_rows, HEAD_DIM), lambda h, i: (0, h))
    fr_spec = pl.BlockSpec((None, 1, n_rows), lambda h, i: (h, 0, 0))
    return q_spec, kv_spec, fr_spec


def _fox_fwd_call(name, q, k, v, fr):
    n_rows, width = q.shape
    bq = min(FOX_BQ, n_rows)
    nq = n_rows // bq
    q_spec, kv_spec, fr_spec = _fox_specs(n_rows, bq)

    def body(q_ref, k_ref, v_ref, fr_ref, o_ref):
        i = pl.program_id(1)
        for p in range(nq):
            n_keys = (p + 1) * bq

            @pl.when(i == p)
            def _(n_keys=n_keys):
                o = _fox_block(q_ref[...].astype(F32), k_ref[:n_keys, :], v_ref[:n_keys, :], fr_ref[:, :n_keys])
                o_ref[...] = o.astype(o_ref.dtype)

    return pl.pallas_call(
        body, name=name, grid=(width // HEAD_DIM, nq), in_specs=[q_spec, kv_spec, kv_spec, fr_spec], out_specs=q_spec,
        out_shape=jax.ShapeDtypeStruct(q.shape, ACT_DTYPE), compiler_params=_cparams(("parallel", "parallel")),
    )(q, k, v, fr)


def _fox_bwd_call(name, q, k, v, fr, do):
    n_rows, width = q.shape
    bq = min(FOX_BQ, n_rows)
    nq = n_rows // bq
    q_spec, kv_spec, fr_spec = _fox_specs(n_rows, bq)

    def body(q_ref, k_ref, v_ref, fr_ref, do_ref, dq_ref, dk_ref, dv_ref, dfr_ref, dk_acc, dv_acc):
        i = pl.program_id(1)

        @pl.when(i == 0)
        def _():
            dk_acc[...] = jnp.zeros_like(dk_acc)
            dv_acc[...] = jnp.zeros_like(dv_acc)
            dfr_ref[...] = jnp.zeros_like(dfr_ref)

        for p in range(nq):
            n_keys = (p + 1) * bq

            @pl.when(i == p)
            def _(n_keys=n_keys):
                _, vjp = jax.vjp(_fox_block, q_ref[...].astype(F32), k_ref[:n_keys, :].astype(F32), v_ref[:n_keys, :].astype(F32),
                                 fr_ref[:, :n_keys])
                dq, dk, dv, dfr = vjp(do_ref[...].astype(F32))
                dq_ref[...] = dq.astype(dq_ref.dtype)
                dk_acc[:n_keys, :] += dk
                dv_acc[:n_keys, :] += dv
                dfr_ref[:, :n_keys] += dfr

        @pl.when(i == nq - 1)
        def _():
            dk_ref[...] = dk_acc[...].astype(dk_ref.dtype)
            dv_ref[...] = dv_acc[...].astype(dv_ref.dtype)

    return pl.pallas_call(
        body, name=name, grid=(width // HEAD_DIM, nq), in_specs=[q_spec, kv_spec, kv_spec, fr_spec, q_spec],
        out_specs=[q_spec, kv_spec, kv_spec, fr_spec], out_shape=[jax.ShapeDtypeStruct(a.shape, a.dtype) for a in (q, k, v, fr)],
        scratch_shapes=[pltpu.VMEM((n_rows, HEAD_DIM), F32), pltpu.VMEM((n_rows, HEAD_DIM), F32)],
        compiler_params=_cparams(("parallel", "arbitrary")),
    )(q, k, v, fr, do)


def make_fox(name):
    @jax.custom_vjp
    def fox(q, k, v, fr):
        return _fox_fwd_call(name, q, k, v, fr)

    def fwd(q, k, v, fr):
        return fox(q, k, v, fr), (q, k, v, fr)

    def bwd(res, do):
        return tuple(_fox_bwd_call(name + "_bwd", *res, do))

    fox.defvjp(fwd, bwd)
    return fox


def _relations(x, y, c):
    out = []
    for rel in range(1, N_DEV):
        px = 1 - x if rel & 4 else x
        py = 1 - y if rel & 2 else y
        pc = 1 - c if rel & 1 else c
        out.append((rel, (px, py, pc), 4 * px + 2 * py + pc))
    return out


_HBM_SPEC = pl.BlockSpec(memory_space=pltpu.HBM)
_SEM_SPEC = pl.BlockSpec(memory_space=pltpu.SEMAPHORE)
_DATAFLOW = pltpu.SideEffectType.DATAFLOW_SIDE_EFFECTING


def exchange_start(name, srcs, gather):
    n = len(srcs)
    me = 4 * lax.axis_index("x") + 2 * lax.axis_index("y") + lax.axis_index("c")
    lands = []
    for s in srcs:
        own = s if gather else lax.dynamic_index_in_dim(s, me, 0, keepdims=False)
        land = lax.empty((N_DEV,) + own.shape, s.dtype)
        lands.append(lax.dynamic_update_index_in_dim(land, own, me, 0))

    def body(*refs):
        src_refs, land_refs = refs[:n], refs[n:2 * n]
        send_sems, recv_sems, token = refs[2 * n], refs[2 * n + 1], refs[-1]
        x, y, c = lax.axis_index("x"), lax.axis_index("y"), lax.axis_index("c")
        mine = 4 * x + 2 * y + c
        for k in range(n):
            for rel, peer, peer_blk in _relations(x, y, c):
                pltpu.make_async_remote_copy(
                    src_ref=src_refs[k] if gather else src_refs[k].at[peer_blk], dst_ref=land_refs[k].at[mine],
                    send_sem=send_sems.at[7 * k + rel - 1], recv_sem=recv_sems.at[7 * k + rel - 1],
                    device_id=peer, device_id_type=pl.DeviceIdType.MESH).start()
        token[...] = jnp.zeros_like(token)

    sem = pltpu.SemaphoreType.DMA((7 * n,))
    hbm = lambda t: pltpu.HBM(t.shape, t.dtype)
    outs = pl.pallas_call(
        body, name=name,
        out_shape=(sem, sem, *[hbm(s) for s in srcs], *[hbm(l) for l in lands], jax.ShapeDtypeStruct((8, LANE), F32)),
        in_specs=[_HBM_SPEC] * (2 * n),
        out_specs=(_SEM_SPEC, _SEM_SPEC, *[_HBM_SPEC] * (2 * n), pl.BlockSpec(memory_space=pltpu.VMEM)),
        input_output_aliases={i: 2 + i for i in range(2 * n)},
        compiler_params=pltpu.CompilerParams(has_side_effects=_DATAFLOW),
    )(*[pltpu.with_memory_space_constraint(t, pltpu.HBM) for t in list(srcs) + lands])
    return dict(name=name, gather=gather, send=outs[0], recv=outs[1], srcs=list(outs[2:2 + n]), lands=list(outs[2 + n:2 + 2 * n]), token=outs[-1])


def exchange_wait(handle, k, after):
    gather = handle['gather']

    def body(src_ref, land_ref, send_sems, recv_sems, after_ref, src_out, land_out):
        x, y, c = lax.axis_index("x"), lax.axis_index("y"), lax.axis_index("c")
        for rel, peer, peer_blk in _relations(x, y, c):
            copy = pltpu.make_async_remote_copy(
                src_ref=src_ref if gather else src_ref.at[peer_blk], dst_ref=land_ref.at[peer_blk],
                send_sem=send_sems.at[7 * k + rel - 1], recv_sem=recv_sems.at[7 * k + rel - 1],
                device_id=peer, device_id_type=pl.DeviceIdType.MESH)
            copy.wait_send()
            copy.wait_recv()

    src, land = handle['srcs'][k], handle['lands'][k]
    return pl.pallas_call(
        body, name=f"{handle['name']}_wait{k}", out_shape=(pltpu.HBM(src.shape, src.dtype), pltpu.HBM(land.shape, land.dtype)),
        in_specs=[_HBM_SPEC, _HBM_SPEC, _SEM_SPEC, _SEM_SPEC, pl.BlockSpec(memory_space=pl.ANY)], out_specs=(_HBM_SPEC, _HBM_SPEC),
        input_output_aliases={0: 0, 1: 1}, compiler_params=pltpu.CompilerParams(has_side_effects=_DATAFLOW),
    )(src, land, handle['send'], handle['recv'], after)[1]


def adamw_reduce(name, partials, w, m, v):
    n_part, n_rows, n_cols = partials.shape
    br = n_rows
    for cand in (512, 256, 128, 64, 32, 16, 8):
        if n_rows % cand == 0 and n_part * cand * n_cols * partials.dtype.itemsize <= (4 << 20):
            br = cand
            break

    def body(p_ref, w_ref, m_ref, v_ref, g_ref, d_ref, nm_ref, nv_ref):
        g = p_ref[0].astype(F32)
        for s in range(1, n_part):
            g = g + p_ref[s].astype(F32)
        m_new = ADAM_B1 * m_ref[...] + (1.0 - ADAM_B1) * g
        v_new = ADAM_B2 * v_ref[...] + (1.0 - ADAM_B2) * jnp.square(g)
        m_hat = m_new / (1.0 - ADAM_B1 ** ADAM_STEP)
        v_hat = v_new / (1.0 - ADAM_B2 ** ADAM_STEP)
        g_ref[...] = g
        d_ref[...] = -ADAM_LR * (m_hat / (jnp.sqrt(v_hat) + ADAM_EPS) + ADAM_WD * w_ref[...])
        nm_ref[...] = m_new
        nv_ref[...] = v_new

    spec = pl.BlockSpec((br, n_cols), lambda i: (i, 0))
    return pl.pallas_call(
        body, name=name, grid=(n_rows // br,), in_specs=[pl.BlockSpec((n_part, br, n_cols), lambda i: (0, i, 0)), spec, spec, spec],
        out_specs=[spec] * 4, out_shape=[jax.ShapeDtypeStruct((n_rows, n_cols), F32)] * 4, compiler_params=_cparams(("parallel",)),
    )(partials, w, m, v)


def _pack(arrays, n_rows):
    flat = jnp.concatenate([a.reshape(-1) for a in arrays])
    return jnp.pad(flat, (0, n_rows * LANE - flat.shape[0])).reshape(n_rows, LANE)


def _unpack(packed, shapes):
    flat, out, off = packed.reshape(-1), [], 0
    for s in shapes:
        n = math.prod(s)
        out.append(flat[off:off + n].reshape(s))
        off += n
    return out


def _packed_rows(shapes):
    n = sum(math.prod(s) for s in shapes)
    return -(-n // (LANE * 512)) * 512


GATHER_ORDER = ['small', 'in_a', 'mem_kv0', 'glu', 'out0', 'kv', 'in_b', 'mem_kv1', 'out1']


BLOCKED = ('in_a', 'kv', 'in_b')


def local_step(p, gathered, x, mem, tgt, start_early, start_late):
    n_rows, d_model = x.shape
    main_w = 3 * d_model // 4
    mem_w = d_model - main_w
    n_groups = main_w // SSM_GROUP
    n_heads = main_w // HEAD_DIM
    n_hi = n_rows // (SSM_T * SSM_LO)
    row = lambda a: a.reshape(1, -1)
    tape, pending = {}, {}

    def mm_fwd(name, act, after=None):
        w = exchange_wait(gathered, GATHER_ORDER.index(name), act if after is None else after)
        if name not in BLOCKED:
            w = w.reshape(-1, w.shape[2])
        tape[name] = (act, w)
        return (_mm_nn_cb if name in BLOCKED else _mm_nn)(name, act, w)

    def mm_dw(name, dy, after=None):
        act, w = tape[name]
        if name in BLOCKED:
            dw = _mm_tn_cb(name + "_dw", act, dy, w.shape[2], GRAD_DTYPE, after)
        else:
            dw = _mm_tn(name + "_dw", act, dy, GRAD_DTYPE, after)
            dw = dw.reshape(N_DEV, dw.shape[0] // N_DEV, dw.shape[1])
        pending[name] = exchange_start("rs_" + name, [dw], gather=False)

    def mm_da(name, dy, after=None):
        return (_mm_nt_cb if name in BLOCKED else _mm_nt)(name + "_da", dy, tape[name][1], after)

    def mm_bwd(name, dy):
        mm_dw(name, dy)
        return mm_da(name, dy, pending[name]['token'])

    def split(proj):
        return proj[:, :main_w], proj[:, main_w:2 * main_w], proj[:, 2 * main_w:2 * main_w + mem_w], proj[:, 2 * main_w + mem_w:]

    def mem_attn(i, qm, zm, kvm):
        memo, = make_rowop(f_memattn, f"mem_attn{i}", 512, mem_w // HEAD_DIM, [ACT_DTYPE])((qm, zm), (kvm[:, :mem_w], kvm[:, mem_w:]))
        return memo

    def seg_norms(x_, pre_g0, mem_g0, mem_g1):
        hn, = make_rowop(f_norm, "pre_norm0", 256, out_dtypes=[ACT_DTYPE])((x_,), (row(pre_g0),))
        memn0, = make_rowop(f_norm, "mem_norm0", 256, out_dtypes=[ACT_DTYPE])((mem,), (row(mem_g0),))
        memn1, = make_rowop(f_norm, "mem_norm1", 256, out_dtypes=[ACT_DTYPE])((mem,), (row(mem_g1),))
        return hn, memn0, memn1

    def seg_a1(proj, kvm, lam_re, lam_im, log_step, b_re, b_im, c_re, c_im, d_skip):
        u, z, qm, zm = split(proj)
        ops = s5_operators(lam_re, lam_im, log_step, b_re, b_im, c_re, c_im, n_hi)
        ug = u.astype(BF16).reshape(n_hi, SSM_LO, SSM_T, n_groups, SSM_GROUP).transpose(3, 1, 0, 2, 4).reshape(n_groups, n_hi * SSM_LO, SSM_T * SSM_GROUP)
        yg = make_groupop(f_s5, "s5", SSM_GB)(ug, *ops)
        y = yg.reshape(n_groups, SSM_LO, n_hi, SSM_T, SSM_GROUP).transpose(2, 1, 3, 0, 4).reshape(n_rows, main_w)
        ygelu, = make_rowop(f_gate_a1, "gate_a1", 256, out_dtypes=[ACT_DTYPE])((y, u), (row(d_skip),))
        return ygelu, z, mem_attn(0, qm, zm, kvm)

    def seg_a2(ygelu, t, z, memo, b_glu):
        main, = make_rowop(f_gate_a2, "gate_a2", 256, out_dtypes=[ACT_DTYPE])((ygelu, t, z), (row(b_glu),))
        return jnp.concatenate([main, memo], axis=1)

    def seg_post_a(x_, o, post_g0, kv_g, pre_g1):
        return make_rowop(f_post_a, "post_a", 128, out_dtypes=[F32, ACT_DTYPE, ACT_DTYPE])((x_, o), (row(post_g0), row(kv_g), row(pre_g1)))

    def seg_b(proj, kv, kv_in, kvm, w_fgate, b_fgate):
        w_fg = jnp.pad(w_fgate, ((0, 0), (0, LANE - n_heads)))
        b_fg = jnp.pad(b_fgate, (0, LANE - n_heads)).reshape(1, LANE)
        logf, = make_rowop(f_logf, "logf", 512)((make_mm("fgate")(kv_in, w_fg),), (b_fg,))
        fcum = make_cumsum("fcum")(logf)[:, :n_heads].T
        q, z, qm, zm = split(proj)
        att = make_fox("fox")(q, kv[:, :main_w], kv[:, main_w:], fcum[:, None, :])
        main, = make_rowop(f_gate_b, "gate_b", 256, out_dtypes=[ACT_DTYPE])((att, z), ())
        return jnp.concatenate([main, mem_attn(1, qm, zm, kvm)], axis=1)

    def seg_final(h1, o, post_g1):
        rowloss, = make_rowop(f_final, "final", 128)((h1, o, tgt), (row(post_g1),))
        return jnp.sum(rowloss)

    (hn0, memn0, memn1), vjp_norms = jax.vjp(seg_norms, x, p['pre_norm_g'][0], p['mem_norm_g'][0], p['mem_norm_g'][1])
    proj_a = mm_fwd("in_a", hn0)
    kvm0 = mm_fwd("mem_kv0", memn0, proj_a)
    s5_names = ('lam_re', 'lam_im', 'log_step', 'b_re', 'b_im', 'c_re', 'c_im', 'd_skip')
    (ygelu, z_a, memo0), vjp_a1 = jax.vjp(seg_a1, proj_a, kvm0, *[p[n] for n in s5_names])
    t = mm_fwd("glu", ygelu)
    cat0, vjp_a2 = jax.vjp(seg_a2, ygelu, t, z_a, memo0, p['b_glu'])
    o0 = mm_fwd("out0", cat0)
    (h1, kv_in, hn1), vjp_post_a = jax.vjp(seg_post_a, x, o0, p['post_norm_g'][0], p['kv_norm_g'], p['pre_norm_g'][1])
    kv = mm_fwd("kv", kv_in)
    proj_b = mm_fwd("in_b", hn1)
    kvm1 = mm_fwd("mem_kv1", memn1, proj_b)
    cat1, vjp_b = jax.vjp(seg_b, proj_b, kv, kv_in, kvm1, p['w_fgate'], p['b_fgate'])
    o1 = mm_fwd("out1", cat1)
    loss, vjp_final = jax.vjp(seg_final, h1, o1, p['post_norm_g'][1])

    g = {}
    d_h1, d_o1, g_post_g1 = vjp_final(jnp.ones((), F32))
    d_proj_b, d_kv, d_kv_in, d_kvm1, g['w_fgate'], g['b_fgate'] = vjp_b(mm_bwd("out1", d_o1))
    d_memn1 = mm_bwd("mem_kv1", d_kvm1)
    d_hn1 = mm_bwd("in_b", d_proj_b)
    d_kv_in = d_kv_in + mm_bwd("kv", d_kv)
    d_x, d_o0, g_post_g0, g['kv_norm_g'], g_pre_g1 = vjp_post_a((d_h1, d_kv_in, d_hn1))
    d_ygelu, d_t, d_z, d_memo0, g['b_glu'] = vjp_a2(mm_bwd("out0", d_o0))
    d_ygelu = d_ygelu + mm_bwd("glu", d_t)
    d_proj_a, d_kvm0, *g_s5 = vjp_a1((d_ygelu, d_z, d_memo0))
    g.update(zip(s5_names, g_s5))
    g['pre_norm_g'] = jnp.stack([jnp.zeros_like(g_pre_g1), g_pre_g1])
    g['post_norm_g'] = jnp.stack([g_post_g0, g_post_g1])
    g['mem_norm_g'] = jnp.zeros_like(p['mem_norm_g'])
    early = start_early(g)
    d_memn0 = mm_bwd("mem_kv0", d_kvm0)
    d_x2, *g_late = vjp_norms((mm_da("in_a", d_proj_a, early['token']), d_memn0, d_memn1))
    late = start_late(g_late)
    mm_dw("in_a", d_proj_a, late['token'])
    return loss, d_x + d_x2, pending, early, late


def kernel(x, mem, pre_norm_g, post_norm_g, w_in_a, lam_re, lam_im, log_step, b_re, b_im, c_re, c_im, d_skip, w_glu, b_glu, kv_norm_g, w_kv, w_fgate, b_fgate, w_in_b, mem_norm_g, w_mem_kv, w_out, loss_target, m_pre_norm_g, m_post_norm_g, m_w_in_a, m_lam_re, m_lam_im, m_log_step, m_b_re, m_b_im, m_c_re, m_c_im, m_d_skip, m_w_glu, m_b_glu, m_kv_norm_g, m_w_kv, m_w_fgate, m_b_fgate, m_w_in_b, m_mem_norm_g, m_w_mem_kv, m_w_out, v_pre_norm_g, v_post_norm_g, v_w_in_a, v_lam_re, v_lam_im, v_log_step, v_b_re, v_b_im, v_c_re, v_c_im, v_d_skip, v_w_glu, v_b_glu, v_kv_norm_g, v_w_kv, v_w_fgate, v_b_fgate, v_w_in_b, v_mem_norm_g, v_w_mem_kv, v_w_out):
    a = dict(zip(INPUTS, (x, mem, pre_norm_g, post_norm_g, w_in_a, lam_re, lam_im, log_step, b_re, b_im, c_re, c_im, d_skip, w_glu, b_glu, kv_norm_g, w_kv, w_fgate, b_fgate, w_in_b, mem_norm_g, w_mem_kv, w_out, loss_target, m_pre_norm_g, m_post_norm_g, m_w_in_a, m_lam_re, m_lam_im, m_log_step, m_b_re, m_b_im, m_c_re, m_c_im, m_d_skip, m_w_glu, m_b_glu, m_kv_norm_g, m_w_kv, m_w_fgate, m_b_fgate, m_w_in_b, m_mem_norm_g, m_w_mem_kv, m_w_out, v_pre_norm_g, v_post_norm_g, v_w_in_a, v_lam_re, v_lam_im, v_log_step, v_b_re, v_b_im, v_c_re, v_c_im, v_d_skip, v_w_glu, v_b_glu, v_kv_norm_g, v_w_kv, v_w_fgate, v_b_fgate, v_w_in_b, v_mem_norm_g, v_w_mem_kv, v_w_out)))
    me = 4 * lax.axis_index("x") + 2 * lax.axis_index("y") + lax.axis_index("c")
    n_layers = w_out.shape[0]

    small_shapes = [a[n].shape for n in SMALL_SHARDED]
    small_rows = -(-sum(math.prod(s) for s in small_shapes) // (LANE * 8)) * 8
    shards = dict(in_a=w_in_a[0], mem_kv0=w_mem_kv[0], glu=w_glu[0], out0=w_out[0], kv=w_kv, in_b=w_in_b[0], mem_kv1=w_mem_kv[1], out1=w_out[1])
    operands = [_pack([a[n] for n in SMALL_SHARDED], small_rows)] + [shards[n].astype(BF16) for n in GATHER_ORDER[1:]]
    gathered = exchange_start("ag_weights", operands, gather=True)
    small = exchange_wait(gathered, 0, gathered['token'])
    small = [jnp.stack(parts) for parts in zip(*[_unpack(small[b], small_shapes) for b in range(N_DEV)])]
    p = {n: a[n] for n in REPLICATED}
    for n in ('lam_re', 'lam_im', 'log_step', 'b_re', 'b_im', 'c_re', 'c_im'):
        p[n] = p[n][0]
    p['d_skip'] = small[0].reshape(-1)
    p['b_glu'] = small[1].reshape(-1)
    p['w_fgate'] = small[2].reshape(-1, w_fgate.shape[1])
    p['pre_norm_g'] = p['pre_norm_g'] + gathered['token'][0, 0]

    full_shapes = [a[n].shape for n in REPLICATED] + [(1, N_DEV * d_skip.shape[1]), (1, N_DEV * b_glu.shape[1]), (N_DEV * w_fgate.shape[0], w_fgate.shape[1])]
    rows = _packed_rows(full_shapes)
    late_rows = 3 * pre_norm_g.shape[1] // LANE
    start_early = lambda g: exchange_start("ag_grads", [_pack([g[n] for n in REPLICATED + SMALL_SHARDED], rows)], gather=True)
    start_late = lambda g_late: exchange_start("ag_grads_late", [_pack(g_late, late_rows)], gather=True)

    loss_local, grad_x, pending, early, late = local_step(p, gathered, x[0], mem[0], loss_target[0], start_early, start_late)
    loss = lax.psum(loss_local, MESH_AXES)

    out = {}

    def update(tag, name, layer=None):
        pick = (lambda t: t) if layer is None else (lambda t: t[layer])
        parts = exchange_wait(pending[tag], 0, grad_x)
        two_d = lambda t: pick(t).reshape(parts.shape[1:])
        res = adamw_reduce("adamw_" + tag, parts, two_d(a[name]), two_d(a['m_' + name]), two_d(a['v_' + name]))
        return [r.reshape(pick(a[name]).shape) for r in res]

    per_layer = {n: [None] * n_layers for n in ('w_out', 'w_mem_kv')}
    per_layer['w_out'][1] = update("out1", 'w_out', 1)
    per_layer['w_mem_kv'][1] = update("mem_kv1", 'w_mem_kv', 1)
    out['w_in_b'] = [r[None] for r in update("in_b", 'w_in_b', 0)]
    out['w_kv'] = update("kv", 'w_kv')
    per_layer['w_out'][0] = update("out0", 'w_out', 0)
    per_layer['w_mem_kv'][0] = update("mem_kv0", 'w_mem_kv', 0)
    out['w_glu'] = [r[None] for r in update("glu", 'w_glu', 0)]
    for name in per_layer:
        out[name] = [jnp.stack(t) for t in zip(*per_layer[name])]

    zeros = [jnp.zeros(s, F32) for s in full_shapes[len(REPLICATED):]]
    packed = lambda pre: _pack([a[pre + n] for n in REPLICATED] + zeros, rows)
    res = [_unpack(r, full_shapes) for r in adamw_reduce("adamw_small", exchange_wait(early, 0, grad_x), packed(''), packed('m_'), packed('v_'))]
    for i, n in enumerate(REPLICATED):
        out[n] = [r[i] for r in res]
    g_full = res[0][len(REPLICATED):]
    g_shard = [lax.dynamic_slice_in_dim(g_full[0], me * d_skip.shape[1], d_skip.shape[1], 1),
               lax.dynamic_slice_in_dim(g_full[1], me * b_glu.shape[1], b_glu.shape[1], 1),
               lax.dynamic_slice_in_dim(g_full[2], me * w_fgate.shape[0], w_fgate.shape[0], 0)]
    packed = lambda pre: _pack([a[pre + n] for n in SMALL_SHARDED], small_rows)
    res = [_unpack(r, small_shapes) for r in adamw_reduce("adamw_small_sharded", _pack(g_shard, small_rows)[None], packed(''), packed('m_'), packed('v_'))]
    for i, n in enumerate(SMALL_SHARDED):
        out[n] = [r[i] for r in res]
    packed = lambda pre: _pack([a[pre + 'pre_norm_g'][0], a[pre + 'mem_norm_g'][0], a[pre + 'mem_norm_g'][1]], late_rows)
    res = [_unpack(r, [pre_norm_g.shape[1:]] * 3) for r in adamw_reduce("adamw_small_late", exchange_wait(late, 0, grad_x), packed(''), packed('m_'), packed('v_'))]
    out['pre_norm_g'] = [jnp.stack([r[0], o[1]]) for r, o in zip(res, out['pre_norm_g'])]
    out['mem_norm_g'] = [jnp.stack([r[1], r[2]]) for r in res]
    out['w_in_a'] = [r[None] for r in update("in_a", 'w_in_a', 0)]

    return (loss, grad_x[None], *[out[n][k] for k in range(4) for n in WEIGHTS])
```

```python
import functools
import math

import jax
import jax.numpy as jnp
from jax import lax
from jax.experimental import pallas as pl
from jax.experimental.pallas import tpu as pltpu

F32 = jnp.float32
BF16 = jnp.bfloat16
HP = lax.Precision.HIGHEST
MESH_AXES = ("x", "y", "c")
N_DEV = 8
V7X_VMEM_LIMIT = 56 * 1024 * 1024
LANE = 128

EPS = 1e-6
HEAD_DIM = 128
SSM_GROUP = 16
SSM_STATE = 64
SSM_T = 8
SSM_LO = 16
SSM_GB = 8
FOX_BQ = 256
GRAD_DTYPE = BF16
ACT_DTYPE = BF16
ADAM_LR = 0.001
ADAM_B1 = 0.9
ADAM_B2 = 0.999
ADAM_EPS = 1e-08
ADAM_WD = 0.01
ADAM_STEP = 10

WEIGHTS = ['pre_norm_g', 'post_norm_g', 'w_in_a', 'lam_re', 'lam_im', 'log_step', 'b_re', 'b_im', 'c_re', 'c_im', 'd_skip',
           'w_glu', 'b_glu', 'kv_norm_g', 'w_kv', 'w_fgate', 'b_fgate', 'w_in_b', 'mem_norm_g', 'w_mem_kv', 'w_out']
INPUTS = ['x', 'mem'] + WEIGHTS + ['loss_target'] + ['m_' + n for n in WEIGHTS] + ['v_' + n for n in WEIGHTS]
REPLICATED = ['pre_norm_g', 'post_norm_g', 'lam_re', 'lam_im', 'log_step', 'b_re', 'b_im', 'c_re', 'c_im', 'kv_norm_g',
              'b_fgate', 'mem_norm_g']
SMALL_SHARDED = ['d_skip', 'b_glu', 'w_fgate']


def _cparams(sem=None):
    return pltpu.CompilerParams(dimension_semantics=sem, vmem_limit_bytes=V7X_VMEM_LIMIT)


def _tile(n, cap):
    if n <= cap:
        return n
    best = None
    for t in range(LANE, cap + 1, LANE):
        if n % t == 0:
            best = t
    assert best is not None, (n, cap)
    return best


def _matmul(name, a, b, a_spec, b_spec, o_spec, out_shape, grid, dims, nk, after=None):
    def body(a_ref, b_ref, *rest):
        o_ref, acc_ref = rest[-2:]
        k = pl.program_id(2)
        part = lax.dot_general(a_ref[...].astype(BF16), b_ref[...].astype(BF16), dims, preferred_element_type=F32)

        @pl.when(k == 0)
        def _():
            acc_ref[...] = part

        @pl.when(k > 0)
        def _():
            acc_ref[...] += part

        @pl.when(k == nk - 1)
        def _():
            o_ref[...] = acc_ref[...].astype(o_ref.dtype)

    acc_shape = tuple(d for d in o_spec.block_shape if d is not None)
    extra = [] if after is None else [after]
    return pl.pallas_call(
        body, name=name, grid=grid, in_specs=[a_spec, b_spec] + [pl.BlockSpec(memory_space=pl.ANY)] * len(extra), out_specs=o_spec,
        out_shape=out_shape, scratch_shapes=[pltpu.VMEM(acc_shape, F32)],
        compiler_params=_cparams(("parallel", "parallel", "arbitrary")),
    )(a, b, *extra)


NN = (((1,), (0,)), ((), ()))
NT = (((1,), (1,)), ((), ()))
TN = (((0,), (0,)), ((), ()))


def _mm_nn(name, a, b, out_dtype=ACT_DTYPE):
    (m, k), (_, n) = a.shape, b.shape
    bm, bn, bk = _tile(m, 1024), _tile(n, 512), _tile(k, 2048)
    return _matmul(name, a, b, pl.BlockSpec((bm, bk), lambda i, j, kk: (i, kk)), pl.BlockSpec((bk, bn), lambda i, j, kk: (kk, j)),
                   pl.BlockSpec((bm, bn), lambda i, j, kk: (i, j)), jax.ShapeDtypeStruct((m, n), out_dtype), (m // bm, n // bn, k // bk), NN, k // bk)


def _mm_nt(name, a, b, after=None, out_dtype=ACT_DTYPE):
    (m, c), (n, _) = a.shape, b.shape
    bm, bn, bk = _tile(m, 1024), _tile(n, 512), _tile(c, 2048)
    return _matmul(name, a, b, pl.BlockSpec((bm, bk), lambda i, j, kk: (i, kk)), pl.BlockSpec((bn, bk), lambda i, j, kk: (j, kk)),
                   pl.BlockSpec((bm, bn), lambda i, j, kk: (i, j)), jax.ShapeDtypeStruct((m, n), out_dtype), (m // bm, n // bn, c // bk), NT, c // bk, after)


def _mm_tn(name, a, b, out_dtype=F32, after=None):
    (c, m), (_, n) = a.shape, b.shape
    bm, bn, bk = _tile(m, 1024), _tile(n, 512), _tile(c, 2048)
    return _matmul(name, a, b, pl.BlockSpec((bk, bm), lambda i, j, kk: (kk, i)), pl.BlockSpec((bk, bn), lambda i, j, kk: (kk, j)),
                   pl.BlockSpec((bm, bn), lambda i, j, kk: (i, j)), jax.ShapeDtypeStruct((m, n), out_dtype), (m // bm, n // bn, c // bk), TN, c // bk, after)


def _mm_nn_cb(name, a, bb):
    (m, k), (nb, _, ns) = a.shape, bb.shape
    bm, bk = _tile(m, 1024), _tile(k, 2048)
    return _matmul(name, a, bb, pl.BlockSpec((bm, bk), lambda i, j, kk: (i, kk)), pl.BlockSpec((None, bk, ns), lambda i, j, kk: (j, kk, 0)),
                   pl.BlockSpec((bm, ns), lambda i, j, kk: (i, j)), jax.ShapeDtypeStruct((m, nb * ns), ACT_DTYPE), (m // bm, nb, k // bk), NN, k // bk)


def _mm_nt_cb(name, dy, bb, after=None):
    m, (nb, k, ns) = dy.shape[0], bb.shape
    bm, bn = _tile(m, 1024), _tile(k, 1024)
    return _matmul(name, dy, bb, pl.BlockSpec((bm, ns), lambda i, j, kk: (i, kk)), pl.BlockSpec((None, bn, ns), lambda i, j, kk: (kk, j, 0)),
                   pl.BlockSpec((bm, bn), lambda i, j, kk: (i, j)), jax.ShapeDtypeStruct((m, k), ACT_DTYPE), (m // bm, k // bn, nb), NT, nb, after)


def _mm_tn_cb(name, a, dy, ns, out_dtype=F32, after=None):
    (c, k), nb = a.shape, dy.shape[1] // ns
    bm = _tile(k, 1024)
    return _matmul(name, a, dy, pl.BlockSpec((c, bm), lambda i, j, kk: (0, i)), pl.BlockSpec((c, ns), lambda i, j, kk: (0, j)),
                   pl.BlockSpec((None, bm, ns), lambda i, j, kk: (j, i, 0)), jax.ShapeDtypeStruct((nb, k, ns), out_dtype), (k // bm, nb, 1), TN, 1, after)


def make_mm(name):
    @jax.custom_vjp
    def mm(a, b):
        return _mm_nn(name, a, b, F32)

    def fwd(a, b):
        return mm(a, b), (a, b)

    def bwd(res, dy):
        a, b = res
        return _mm_nt(name + "_da", dy, b, out_dtype=a.dtype), _mm_tn(name + "_dw", a, dy, b.dtype)

    mm.defvjp(fwd, bwd)
    return mm


def _rowop_specs(rows, params, bm, nl):
    row_specs = [pl.BlockSpec((bm, r.shape[1] // nl), lambda j, i: (i, j)) for r in rows]
    par_specs = [pl.BlockSpec((p.shape[0], p.shape[1] // nl), lambda j, i: (0, j)) for p in params]
    row_blk = [jax.ShapeDtypeStruct((bm, r.shape[1] // nl), r.dtype) for r in rows]
    par_blk = [jax.ShapeDtypeStruct((p.shape[0], p.shape[1] // nl), p.dtype) for p in params]
    return row_specs, par_specs, row_blk, par_blk


def make_rowop(f, name, bm, nl=1, out_dtypes=None):
    def loaded(refs):
        return [r[...].astype(F32) for r in refs]

    def fwd_call(rows, params):
        n_rows = rows[0].shape[0]
        b = min(bm, n_rows)
        row_specs, par_specs, row_blk, par_blk = _rowop_specs(rows, params, b, nl)
        out_blk = jax.eval_shape(f, *[jax.ShapeDtypeStruct(t.shape, F32) for t in row_blk + par_blk])
        dts = out_dtypes or [F32] * len(out_blk)
        nr, npar = len(rows), len(params)

        def body(*refs):
            for o_ref, o in zip(refs[nr + npar:], f(*loaded(refs[:nr + npar]))):
                o_ref[...] = o.astype(o_ref.dtype)

        return pl.pallas_call(
            body, name=name, grid=(nl, n_rows // b), in_specs=row_specs + par_specs,
            out_specs=[pl.BlockSpec(o.shape, lambda j, i: (i, j)) for o in out_blk],
            out_shape=[jax.ShapeDtypeStruct((n_rows, o.shape[1] * nl), dt) for o, dt in zip(out_blk, dts)],
            compiler_params=_cparams(("parallel", "parallel")),
        )(*rows, *params)

    def bwd_call(rows, params, cts):
        n_rows = rows[0].shape[0]
        b = min(bm, n_rows)
        row_specs, par_specs, row_blk, par_blk = _rowop_specs(rows, params, b, nl)
        ct_specs = [pl.BlockSpec((b, c.shape[1] // nl), lambda j, i: (i, j)) for c in cts]
        nr, npar, nct = len(rows), len(params), len(cts)

        def body(*refs):
            i = pl.program_id(1)
            _, vjp = jax.vjp(lambda *v: tuple(f(*v)), *loaded(refs[:nr + npar]))
            grads = vjp(tuple(loaded(refs[nr + npar:nr + npar + nct])))
            outs = refs[nr + npar + nct:]
            for o_ref, g in zip(outs[:nr], grads[:nr]):
                o_ref[...] = g.astype(o_ref.dtype)
            for o_ref, g in zip(outs[nr:], grads[nr:]):
                @pl.when(i == 0)
                def _(o_ref=o_ref, g=g):
                    o_ref[...] = g

                @pl.when(i > 0)
                def _(o_ref=o_ref, g=g):
                    o_ref[...] += g

        outs = pl.pallas_call(
            body, name=name + "_bwd", grid=(nl, n_rows // b), in_specs=row_specs + par_specs + ct_specs,
            out_specs=row_specs + par_specs,
            out_shape=[jax.ShapeDtypeStruct(a.shape, a.dtype) for a in rows] + [jax.ShapeDtypeStruct(a.shape, F32) for a in params],
            compiler_params=_cparams(("arbitrary", "arbitrary")),
        )(*rows, *params, *cts)
        return tuple(outs[:nr]), tuple(g.astype(a.dtype) for g, a in zip(outs[nr:], params))

    @jax.custom_vjp
    def op(rows, params):
        return tuple(fwd_call(rows, params))

    def fwd(rows, params):
        return op(rows, params), (rows, params)

    def bwd(res, cts):
        rows, params = res
        return bwd_call(rows, params, tuple(cts))

    op.defvjp(fwd, bwd)
    return op


def make_groupop(f, name, gb):
    def specs(arrs):
        return [pl.BlockSpec((gb,) + a.shape[1:], lambda g: (g, 0, 0)) for a in arrs]

    def fwd_call(arrs):
        g_n = arrs[0].shape[0]
        out_blk = jax.eval_shape(f, *[jax.ShapeDtypeStruct((gb,) + a.shape[1:], a.dtype) for a in arrs])[0]
        n = len(arrs)

        def body(*refs):
            for i, o in enumerate(f(*[r[...] for r in refs[:n]])):
                refs[n][i] = o.astype(refs[n].dtype)

        return pl.pallas_call(
            body, name=name, grid=(g_n // gb,), in_specs=specs(arrs),
            out_specs=pl.BlockSpec((gb,) + out_blk.shape, lambda g: (g, 0, 0)),
            out_shape=jax.ShapeDtypeStruct((g_n,) + out_blk.shape, ACT_DTYPE),
            compiler_params=_cparams(("parallel",)),
        )(*arrs)

    def bwd_call(arrs, ct):
        g_n = arrs[0].shape[0]
        n = len(arrs)

        def body(*refs):
            _, vjp = jax.vjp(lambda *v: tuple(f(*v)), *[r[...] for r in refs[:n]])
            for o_ref, g in zip(refs[n + 1:], vjp(tuple(refs[n][i].astype(F32) for i in range(gb)))):
                o_ref[...] = g.astype(o_ref.dtype)

        return pl.pallas_call(
            body, name=name + "_bwd", grid=(g_n // gb,), in_specs=specs(arrs) + specs([ct]), out_specs=specs(arrs),
            out_shape=[jax.ShapeDtypeStruct(a.shape, a.dtype) for a in arrs],
            compiler_params=_cparams(("parallel",)),
        )(*arrs, ct)

    @jax.custom_vjp
    def op(*arrs):
        return fwd_call(arrs)

    def fwd(*arrs):
        return op(*arrs), arrs

    def bwd(arrs, ct):
        return tuple(bwd_call(arrs, ct))

    op.defvjp(fwd, bwd)
    return op


def _rms(x, g):
    return x * lax.rsqrt(jnp.mean(x * x, axis=-1, keepdims=True) + EPS) * g


def _silu(z):
    return z * jax.nn.sigmoid(z)


def _log_sigmoid(x):
    return jnp.minimum(x, 0.0) - jnp.log(1.0 + jnp.exp(-jnp.abs(x)))


def f_norm(x, g):
    return (_rms(x, g),)


def f_gate_a1(y, u, d):
    return (jax.nn.gelu(y + d * u),)


def f_gate_a2(yg, t, z, b):
    return (yg * jax.nn.sigmoid(t + b) * _silu(z),)


def f_gate_b(att, z):
    return (att * _silu(z),)


def f_post_a(h, o, post_g, kv_g, pre_g):
    h1 = h + _rms(o, post_g)
    return h1, _rms(h1, kv_g), _rms(h1, pre_g)


def f_final(h, o, tgt, post_g):
    err = h + _rms(o, post_g) - tgt
    return (0.5 * jnp.mean(err * err, axis=-1, keepdims=True),)


def f_logf(gl, b):
    return (_log_sigmoid(gl + b),)


def _mxu(a, b, dims):
    return lax.dot_general(a.astype(BF16), b.astype(BF16), dims, preferred_element_type=F32)


@jax.custom_vjp
def _mxu_nn(a, b):
    return _mxu(a, b, NN)


def _mxu_nn_bwd(res, g):
    a, b = res
    return _mxu(g, b, NT).astype(a.dtype), _mxu(a, g, TN).astype(b.dtype)


_mxu_nn.defvjp(lambda a, b: (_mxu(a, b, NN), (a, b)), _mxu_nn_bwd)


@jax.custom_vjp
def _mxu_nt(a, b):
    return _mxu(a, b, NT)


def _mxu_nt_bwd(res, g):
    a, b = res
    return _mxu(g, b, NN).astype(a.dtype), _mxu(g, a, TN).astype(b.dtype)


_mxu_nt.defvjp(lambda a, b: (_mxu(a, b, NT), (a, b)), _mxu_nt_bwd)


def f_memattn(q, zm, km, vm):
    o = _mxu_nn(_softmax(_mxu_nt(q * (HEAD_DIM ** -0.5), km)), vm)
    return (o * _silu(zm),)


@jax.custom_vjp
def _softmax(s):
    e = jnp.exp(s - jnp.max(s, axis=-1, keepdims=True))
    return e * (1.0 / jnp.sum(e, axis=-1, keepdims=True))


def _softmax_fwd(s):
    p = _softmax(s)
    return p, p


def _softmax_bwd(p, dp):
    return (p * (dp - jnp.sum(dp * p, axis=-1, keepdims=True)),)


_softmax.defvjp(_softmax_fwd, _softmax_bwd)


def _fox_block(q, k, v, fr):
    bq = q.shape[0]
    s = _mxu_nt(q * (HEAD_DIM ** -0.5), k) - fr
    tri = lax.broadcasted_iota(jnp.int32, (bq, bq), 0) >= lax.broadcasted_iota(jnp.int32, (bq, bq), 1)
    diag = jnp.where(tri, s[:, -bq:], -1e30)
    s = diag if k.shape[0] == bq else jnp.concatenate([s[:, :-bq], diag], axis=1)
    return _mxu_nn(_softmax(s), v)


def _cmul(ar, ai, xr, xi):
    return ar * xr - ai * xi, ar * xi + ai * xr


def _hp_dot(a, b):
    return jnp.dot(a, b, precision=HP, preferred_element_type=F32)


_bf_dot = _mxu_nn


def f_s5(u, toep, win_r, win_i, wout_r, wout_i, coef):
    gb = u.shape[0]
    n_hi = u.shape[1] // SSM_LO
    sr = [_bf_dot(u[i], win_r[i]) for i in range(gb)]
    si = [_bf_dot(u[i], win_i[i]) for i in range(gb)]

    def stacked(parts, lo):
        return jnp.concatenate([a[lo * n_hi:(lo + 1) * n_hi] for a in parts], axis=0)

    def coef_rows(k):
        return jnp.concatenate([jnp.broadcast_to(coef[i, k:k + 1, :], (n_hi, coef.shape[2])) for i in range(gb)], axis=0)

    at_r, at_i = coef_rows(0), coef_rows(1)
    pr, pi = stacked(sr, 0), stacked(si, 0)
    for lo in range(1, SSM_LO):
        dr, di = _cmul(at_r, at_i, pr, pi)
        pr, pi = stacked(sr, lo) + dr, stacked(si, lo) + di
    n = 2 * gb * n_hi
    ri = lax.broadcasted_iota(jnp.int32, (n, n), 0)
    ci = lax.broadcasted_iota(jnp.int32, (n, n), 1)
    same = ri // n_hi == ci // n_hi

    def shifted(d, zr, zi):
        z = _hp_dot((same & (ri - ci == d)).astype(F32), jnp.concatenate([zr, zi], axis=0))
        return z[:n // 2], z[n // 2:]

    d, step = 1, 0
    while d < n_hi:
        dr, di = _cmul(coef_rows(2 + 2 * step), coef_rows(3 + 2 * step), *shifted(d, pr, pi))
        pr, pi = pr + dr, pi + di
        d, step = 2 * d, step + 1
    er, ei = shifted(1, pr, pi)
    xr, xi = [er], [ei]
    for lo in range(1, SSM_LO):
        dr, di = _cmul(at_r, at_i, xr[-1], xi[-1])
        xr.append(stacked(sr, lo - 1) + dr)
        xi.append(stacked(si, lo - 1) + di)

    def group_rows(parts, i):
        return jnp.concatenate([a[i * n_hi:(i + 1) * n_hi] for a in parts], axis=0)

    return tuple(_bf_dot(u[i], toep[i]) + _bf_dot(group_rows(xr, i), wout_r[i]) + _bf_dot(group_rows(xi, i), wout_i[i])
                 for i in range(gb))


def s5_operators(lam_re, lam_im, log_step, b_re, b_im, c_re, c_im, n_hi):
    t_n = SSM_T
    lr, li = lam_re, lam_im
    dt = jnp.exp(log_step)[:, None]
    mag = jnp.exp(lr * dt)
    ar, ai = mag * jnp.cos(li * dt), mag * jnp.sin(li * dt)
    den = lr * lr + li * li
    cr = ((ar - 1.0) * lr + ai * li) / den
    ci = (ai * lr - (ar - 1.0) * li) / den
    bbr = cr[..., None] * b_re - ci[..., None] * b_im
    bbi = cr[..., None] * b_im + ci[..., None] * b_re
    k = jnp.arange(t_n + 1, dtype=F32)[:, None, None]
    pm, ang = jnp.exp(k * (lr * dt)), k * (li * dt)
    pr, pi = pm * jnp.cos(ang), pm * jnp.sin(ang)
    abr = pr[..., None] * bbr - pi[..., None] * bbi
    abi = pr[..., None] * bbi + pi[..., None] * bbr
    kk = (jnp.einsum('ghp,kgpj->kghj', c_re, abr[:t_n], precision=HP)
          - jnp.einsum('ghp,kgpj->kghj', c_im, abi[:t_n], precision=HP))
    lag = jnp.arange(t_n)[None, :] - jnp.arange(t_n)[:, None]
    onehot = (lag[None] == jnp.arange(t_n)[:, None, None]).astype(F32)
    g_n, h_n = c_re.shape[0], c_re.shape[1]
    toep = jnp.einsum('kst,kghj->gsjth', onehot, kk, precision=HP).reshape(g_n, t_n * h_n, t_n * h_n)
    win_r = abr[:t_n][::-1].transpose(1, 0, 3, 2).reshape(g_n, t_n * h_n, -1)
    win_i = abi[:t_n][::-1].transpose(1, 0, 3, 2).reshape(g_n, t_n * h_n, -1)
    p1r, p1i = pr[1:, :, None, :], pi[1:, :, None, :]
    wout_r = (c_re[None] * p1r - c_im[None] * p1i).transpose(1, 3, 0, 2).reshape(g_n, -1, t_n * h_n)
    wout_i = (-(c_re[None] * p1i + c_im[None] * p1r)).transpose(1, 3, 0, 2).reshape(g_n, -1, t_n * h_n)
    rows = [pr[t_n], pi[t_n]]
    qr, qi = pr[t_n], pi[t_n]
    for _ in range(int(math.log2(SSM_LO))):
        qr, qi = qr * qr - qi * qi, 2.0 * qr * qi
    d = 1
    while d < n_hi:
        rows += [qr, qi]
        qr, qi = qr * qr - qi * qi, 2.0 * qr * qi
        d *= 2
    coef = jnp.stack(rows, axis=1)
    return toep, win_r, win_i, wout_r, wout_i, coef


def _cumsum_call(name, x, reverse):
    n_rows, w = x.shape
    bm = min(256, n_rows)
    nb = n_rows // bm

    def body(x_ref, o_ref, carry_ref):
        i = pl.program_id(0)

        @pl.when(i == 0)
        def _():
            carry_ref[...] = jnp.zeros_like(carry_ref)

        ri = lax.broadcasted_iota(jnp.int32, (bm, bm), 0)
        ci = lax.broadcasted_iota(jnp.int32, (bm, bm), 1)
        tri = ((ri <= ci) if reverse else (ri >= ci)).astype(F32)
        xb = x_ref[...]
        o_ref[...] = _hp_dot(tri, xb) + carry_ref[...]
        carry_ref[...] += jnp.sum(xb, axis=0, keepdims=True)

    idx = (lambda i: (nb - 1 - i, 0)) if reverse else (lambda i: (i, 0))
    return pl.pallas_call(
        body, name=name, grid=(nb,), in_specs=[pl.BlockSpec((bm, w), idx)], out_specs=pl.BlockSpec((bm, w), idx),
        out_shape=jax.ShapeDtypeStruct(x.shape, F32), scratch_shapes=[pltpu.VMEM((1, w), F32)],
        compiler_params=_cparams(("arbitrary",)),
    )(x)


def make_cumsum(name):
    @jax.custom_vjp
    def cs(x):
        return _cumsum_call(name, x, False)

    def fwd(x):
        return cs(x), None

    def bwd(_, dy):
        return (_cumsum_call(name + "_bwd", dy, True),)

    cs.defvjp(fwd, bwd)
    return cs


def _fox_specs(n_rows, bq):
    q_spec = pl.BlockSpec((bq, HEAD_DIM), lambda h, i: (i, h))
    kv_spec = pl.BlockSpec((n_rows, HEAD_DIM), lambda h, i: (0, h))
    fr_spec = pl.BlockSpec((None, 1, n_rows), lambda h, i: (h, 0, 0))
    return q_spec, kv_spec, fr_spec


def _fox_fwd_call(name, q, k, v, fr):
    n_rows, width = q.shape
    bq = min(FOX_BQ, n_rows)
    nq = n_rows // bq
    q_spec, kv_spec, fr_spec = _fox_specs(n_rows, bq)

    def body(q_ref, k_ref, v_ref, fr_ref, o_ref):
        i = pl.program_id(1)
        for p in range(nq):
            n_keys = (p + 1) * bq

            @pl.when(i == p)
            def _(n_keys=n_keys):
                o = _fox_block(q_ref[...].astype(F32), k_ref[:n_keys, :], v_ref[:n_keys, :], fr_ref[:, :n_keys])
                o_ref[...] = o.astype(o_ref.dtype)

    return pl.pallas_call(
        body, name=name, grid=(width // HEAD_DIM, nq), in_specs=[q_spec, kv_spec, kv_spec, fr_spec], out_specs=q_spec,
        out_shape=jax.ShapeDtypeStruct(q.shape, ACT_DTYPE), compiler_params=_cparams(("parallel", "parallel")),
    )(q, k, v, fr)


def _fox_bwd_call(name, q, k, v, fr, do):
    n_rows, width = q.shape
    bq = min(FOX_BQ, n_rows)
    nq = n_rows // bq
    q_spec, kv_spec, fr_spec = _fox_specs(n_rows, bq)

    def body(q_ref, k_ref, v_ref, fr_ref, do_ref, dq_ref, dk_ref, dv_ref, dfr_ref, dk_acc, dv_acc):
        i = pl.program_id(1)

        @pl.when(i == 0)
        def _():
            dk_acc[...] = jnp.zeros_like(dk_acc)
            dv_acc[...] = jnp.zeros_like(dv_acc)
            dfr_ref[...] = jnp.zeros_like(dfr_ref)

        for p in range(nq):
            n_keys = (p + 1) * bq

            @pl.when(i == p)
            def _(n_keys=n_keys):
                _, vjp = jax.vjp(_fox_block, q_ref[...].astype(F32), k_ref[:n_keys, :].astype(F32), v_ref[:n_keys, :].astype(F32),
                                 fr_ref[:, :n_keys])
                dq, dk, dv, dfr = vjp(do_ref[...].astype(F32))
                dq_ref[...] = dq.astype(dq_ref.dtype)
                dk_acc[:n_keys, :] += dk
                dv_acc[:n_keys, :] += dv
                dfr_ref[:, :n_keys] += dfr

        @pl.when(i == nq - 1)
        def _():
            dk_ref[...] = dk_acc[...].astype(dk_ref.dtype)
            dv_ref[...] = dv_acc[...].astype(dv_ref.dtype)

    return pl.pallas_call(
        body, name=name, grid=(width // HEAD_DIM, nq), in_specs=[q_spec, kv_spec, kv_spec, fr_spec, q_spec],
        out_specs=[q_spec, kv_spec, kv_spec, fr_spec], out_shape=[jax.ShapeDtypeStruct(a.shape, a.dtype) for a in (q, k, v, fr)],
        scratch_shapes=[pltpu.VMEM((n_rows, HEAD_DIM), F32), pltpu.VMEM((n_rows, HEAD_DIM), F32)],
        compiler_params=_cparams(("parallel", "arbitrary")),
    )(q, k, v, fr, do)


def make_fox(name):
    @jax.custom_vjp
    def fox(q, k, v, fr):
        return _fox_fwd_call(name, q, k, v, fr)

    def fwd(q, k, v, fr):
        return fox(q, k, v, fr), (q, k, v, fr)

    def bwd(res, do):
        return tuple(_fox_bwd_call(name + "_bwd", *res, do))

    fox.defvjp(fwd, bwd)
    return fox


def _relations(x, y, c):
    out = []
    for rel in range(1, N_DEV):
        px = 1 - x if rel & 4 else x
        py = 1 - y if rel & 2 else y
        pc = 1 - c if rel & 1 else c
        out.append((rel, (px, py, pc), 4 * px + 2 * py + pc))
    return out


_HBM_SPEC = pl.BlockSpec(memory_space=pltpu.HBM)
_SEM_SPEC = pl.BlockSpec(memory_space=pltpu.SEMAPHORE)
_DATAFLOW = pltpu.SideEffectType.DATAFLOW_SIDE_EFFECTING


def exchange_start(name, srcs, gather):
    n = len(srcs)
    me = 4 * lax.axis_index("x") + 2 * lax.axis_index("y") + lax.axis_index("c")
    lands = []
    for s in srcs:
        own = s if gather else lax.dynamic_index_in_dim(s, me, 0, keepdims=False)
        land = lax.empty((N_DEV,) + own.shape, s.dtype)
        lands.append(lax.dynamic_update_index_in_dim(land, own, me, 0))

    def body(*refs):
        src_refs, land_refs = refs[:n], refs[n:2 * n]
        send_sems, recv_sems, token = refs[2 * n], refs[2 * n + 1], refs[-1]
        x, y, c = lax.axis_index("x"), lax.axis_index("y"), lax.axis_index("c")
        mine = 4 * x + 2 * y + c
        for k in range(n):
            for rel, peer, peer_blk in _relations(x, y, c):
                pltpu.make_async_remote_copy(
                    src_ref=src_refs[k] if gather else src_refs[k].at[peer_blk], dst_ref=land_refs[k].at[mine],
                    send_sem=send_sems.at[7 * k + rel - 1], recv_sem=recv_sems.at[7 * k + rel - 1],
                    device_id=peer, device_id_type=pl.DeviceIdType.MESH).start()
        token[...] = jnp.zeros_like(token)

    sem = pltpu.SemaphoreType.DMA((7 * n,))
    hbm = lambda t: pltpu.HBM(t.shape, t.dtype)
    outs = pl.pallas_call(
        body, name=name,
        out_shape=(sem, sem, *[hbm(s) for s in srcs], *[hbm(l) for l in lands], jax.ShapeDtypeStruct((8, LANE), F32)),
        in_specs=[_HBM_SPEC] * (2 * n),
        out_specs=(_SEM_SPEC, _SEM_SPEC, *[_HBM_SPEC] * (2 * n), pl.BlockSpec(memory_space=pltpu.VMEM)),
        input_output_aliases={i: 2 + i for i in range(2 * n)},
        compiler_params=pltpu.CompilerParams(has_side_effects=_DATAFLOW),
    )(*[pltpu.with_memory_space_constraint(t, pltpu.HBM) for t in list(srcs) + lands])
    return dict(name=name, gather=gather, send=outs[0], recv=outs[1], srcs=list(outs[2:2 + n]), lands=list(outs[2 + n:2 + 2 * n]), token=outs[-1])


def exchange_wait(handle, k, after):
    gather = handle['gather']

    def body(src_ref, land_ref, send_sems, recv_sems, after_ref, src_out, land_out):
        x, y, c = lax.axis_index("x"), lax.axis_index("y"), lax.axis_index("c")
        for rel, peer, peer_blk in _relations(x, y, c):
            copy = pltpu.make_async_remote_copy(
                src_ref=src_ref if gather else src_ref.at[peer_blk], dst_ref=land_ref.at[peer_blk],
                send_sem=send_sems.at[7 * k + rel - 1], recv_sem=recv_sems.at[7 * k + rel - 1],
                device_id=peer, device_id_type=pl.DeviceIdType.MESH)
            copy.wait_send()
            copy.wait_recv()

    src, land = handle['srcs'][k], handle['lands'][k]
    return pl.pallas_call(
        body, name=f"{handle['name']}_wait{k}", out_shape=(pltpu.HBM(src.shape, src.dtype), pltpu.HBM(land.shape, land.dtype)),
        in_specs=[_HBM_SPEC, _HBM_SPEC, _SEM_SPEC, _SEM_SPEC, pl.BlockSpec(memory_space=pl.ANY)], out_specs=(_HBM_SPEC, _HBM_SPEC),
        input_output_aliases={0: 0, 1: 1}, compiler_params=pltpu.CompilerParams(has_side_effects=_DATAFLOW),
    )(src, land, handle['send'], handle['recv'], after)[1]


def adamw_reduce(name, partials, w, m, v):
    n_part, n_rows, n_cols = partials.shape
    br = n_rows
    for cand in (512, 256, 128, 64, 32, 16, 8):
        if n_rows % cand == 0 and n_part * cand * n_cols * partials.dtype.itemsize <= (4 << 20):
            br = cand
            break

    def body(p_ref, w_ref, m_ref, v_ref, g_ref, d_ref, nm_ref, nv_ref):
        g = p_ref[0].astype(F32)
        for s in range(1, n_part):
            g = g + p_ref[s].astype(F32)
        m_new = ADAM_B1 * m_ref[...] + (1.0 - ADAM_B1) * g
        v_new = ADAM_B2 * v_ref[...] + (1.0 - ADAM_B2) * jnp.square(g)
        m_hat = m_new / (1.0 - ADAM_B1 ** ADAM_STEP)
        v_hat = v_new / (1.0 - ADAM_B2 ** ADAM_STEP)
        g_ref[...] = g
        d_ref[...] = -ADAM_LR * (m_hat / (jnp.sqrt(v_hat) + ADAM_EPS) + ADAM_WD * w_ref[...])
        nm_ref[...] = m_new
        nv_ref[...] = v_new

    spec = pl.BlockSpec((br, n_cols), lambda i: (i, 0))
    return pl.pallas_call(
        body, name=name, grid=(n_rows // br,), in_specs=[pl.BlockSpec((n_part, br, n_cols), lambda i: (0, i, 0)), spec, spec, spec],
        out_specs=[spec] * 4, out_shape=[jax.ShapeDtypeStruct((n_rows, n_cols), F32)] * 4, compiler_params=_cparams(("parallel",)),
    )(partials, w, m, v)


def _pack(arrays, n_rows):
    flat = jnp.concatenate([a.reshape(-1) for a in arrays])
    return jnp.pad(flat, (0, n_rows * LANE - flat.shape[0])).reshape(n_rows, LANE)


def _unpack(packed, shapes):
    flat, out, off = packed.reshape(-1), [], 0
    for s in shapes:
        n = math.prod(s)
        out.append(flat[off:off + n].reshape(s))
        off += n
    return out


def _packed_rows(shapes):
    n = sum(math.prod(s) for s in shapes)
    return -(-n // (LANE * 512)) * 512


GATHER_ORDER = ['small', 'in_a', 'mem_kv0', 'glu', 'out0', 'kv', 'in_b', 'mem_kv1', 'out1']


BLOCKED = ('in_a', 'kv', 'in_b')


def local_step(p, gathered, x, mem, tgt, start_early, start_late):
    n_rows, d_model = x.shape
    main_w = 3 * d_model // 4
    mem_w = d_model - main_w
    n_groups = main_w // SSM_GROUP
    n_heads = main_w // HEAD_DIM
    n_hi = n_rows // (SSM_T * SSM_LO)
    row = lambda a: a.reshape(1, -1)
    tape, pending = {}, {}

    def mm_fwd(name, act, after=None):
        w = exchange_wait(gathered, GATHER_ORDER.index(name), act if after is None else after)
        if name not in BLOCKED:
            w = w.reshape(-1, w.shape[2])
        tape[name] = (act, w)
        return (_mm_nn_cb if name in BLOCKED else _mm_nn)(name, act, w)

    def mm_dw(name, dy, after=None):
        act, w = tape[name]
        if name in BLOCKED:
            dw = _mm_tn_cb(name + "_dw", act, dy, w.shape[2], GRAD_DTYPE, after)
        else:
            dw = _mm_tn(name + "_dw", act, dy, GRAD_DTYPE, after)
            dw = dw.reshape(N_DEV, dw.shape[0] // N_DEV, dw.shape[1])
        pending[name] = exchange_start("rs_" + name, [dw], gather=False)

    def mm_da(name, dy, after=None):
        return (_mm_nt_cb if name in BLOCKED else _mm_nt)(name + "_da", dy, tape[name][1], after)

    def mm_bwd(name, dy):
        mm_dw(name, dy)
        return mm_da(name, dy, pending[name]['token'])

    def split(proj):
        return proj[:, :main_w], proj[:, main_w:2 * main_w], proj[:, 2 * main_w:2 * main_w + mem_w], proj[:, 2 * main_w + mem_w:]

    def mem_attn(i, qm, zm, kvm):
        memo, = make_rowop(f_memattn, f"mem_attn{i}", 512, mem_w // HEAD_DIM, [ACT_DTYPE])((qm, zm), (kvm[:, :mem_w], kvm[:, mem_w:]))
        return memo

    def seg_norms(x_, pre_g0, mem_g0, mem_g1):
        hn, = make_rowop(f_norm, "pre_norm0", 256, out_dtypes=[ACT_DTYPE])((x_,), (row(pre_g0),))
        memn0, = make_rowop(f_norm, "mem_norm0", 256, out_dtypes=[ACT_DTYPE])((mem,), (row(mem_g0),))
        memn1, = make_rowop(f_norm, "mem_norm1", 256, out_dtypes=[ACT_DTYPE])((mem,), (row(mem_g1),))
        return hn, memn0, memn1

    def seg_a1(proj, kvm, d_skip, *ops):
        u, z, qm, zm = split(proj)
        ug = u.astype(BF16).reshape(n_hi, SSM_LO, SSM_T, n_groups, SSM_GROUP).transpose(3, 1, 0, 2, 4).reshape(n_groups, n_hi * SSM_LO, SSM_T * SSM_GROUP)
        yg = make_groupop(f_s5, "s5", SSM_GB)(ug, *ops)
        y = yg.reshape(n_groups, SSM_LO, n_hi, SSM_T, SSM_GROUP).transpose(2, 1, 3, 0, 4).reshape(n_rows, main_w)
        ygelu, = make_rowop(f_gate_a1, "gate_a1", 256, out_dtypes=[ACT_DTYPE])((y, u), (row(d_skip),))
        return ygelu, z, mem_attn(0, qm, zm, kvm)

    def seg_a2(ygelu, t, z, memo, b_glu):
        main, = make_rowop(f_gate_a2, "gate_a2", 256, out_dtypes=[ACT_DTYPE])((ygelu, t, z), (row(b_glu),))
        return jnp.concatenate([main, memo], axis=1)

    def seg_post_a(x_, o, post_g0, kv_g, pre_g1):
        return make_rowop(f_post_a, "post_a", 128, out_dtypes=[F32, ACT_DTYPE, ACT_DTYPE])((x_, o), (row(post_g0), row(kv_g), row(pre_g1)))

    def seg_b(proj, kv, kv_in, kvm, w_fgate, b_fgate):
        w_fg = jnp.pad(w_fgate, ((0, 0), (0, LANE - n_heads)))
        b_fg = jnp.pad(b_fgate, (0, LANE - n_heads)).reshape(1, LANE)
        logf, = make_rowop(f_logf, "logf", 512)((make_mm("fgate")(kv_in, w_fg),), (b_fg,))
        fcum = make_cumsum("fcum")(logf)[:, :n_heads].T
        q, z, qm, zm = split(proj)
        att = make_fox("fox")(q, kv[:, :main_w], kv[:, main_w:], fcum[:, None, :])
        main, = make_rowop(f_gate_b, "gate_b", 256, out_dtypes=[ACT_DTYPE])((att, z), ())
        return jnp.concatenate([main, mem_attn(1, qm, zm, kvm)], axis=1)

    def seg_final(h1, o, post_g1):
        rowloss, = make_rowop(f_final, "final", 128)((h1, o, tgt), (row(post_g1),))
        return jnp.sum(rowloss)

    (hn0, memn0, memn1), vjp_norms = jax.vjp(seg_norms, x, p['pre_norm_g'][0], p['mem_norm_g'][0], p['mem_norm_g'][1])
    s5_names = ('lam_re', 'lam_im', 'log_step', 'b_re', 'b_im', 'c_re', 'c_im')
    ops, vjp_ops = jax.vjp(lambda *a: s5_operators(*a, n_hi), *[p[n] for n in s5_names])
    proj_a = mm_fwd("in_a", hn0, ops[0])
    kvm0 = mm_fwd("mem_kv0", memn0, proj_a)
    (ygelu, z_a, memo0), vjp_a1 = jax.vjp(seg_a1, proj_a, kvm0, p['d_skip'], *ops)
    t = mm_fwd("glu", ygelu)
    cat0, vjp_a2 = jax.vjp(seg_a2, ygelu, t, z_a, memo0, p['b_glu'])
    o0 = mm_fwd("out0", cat0)
    (h1, kv_in, hn1), vjp_post_a = jax.vjp(seg_post_a, x, o0, p['post_norm_g'][0], p['kv_norm_g'], p['pre_norm_g'][1])
    kv = mm_fwd("kv", kv_in)
    proj_b = mm_fwd("in_b", hn1)
    kvm1 = mm_fwd("mem_kv1", memn1, proj_b)
    cat1, vjp_b = jax.vjp(seg_b, proj_b, kv, kv_in, kvm1, p['w_fgate'], p['b_fgate'])
    o1 = mm_fwd("out1", cat1)
    loss, vjp_final = jax.vjp(seg_final, h1, o1, p['post_norm_g'][1])

    g = {}
    d_h1, d_o1, g_post_g1 = vjp_final(jnp.ones((), F32))
    d_proj_b, d_kv, d_kv_in, d_kvm1, g['w_fgate'], g['b_fgate'] = vjp_b(mm_bwd("out1", d_o1))
    d_memn1 = mm_bwd("mem_kv1", d_kvm1)
    d_hn1 = mm_bwd("in_b", d_proj_b)
    d_kv_in = d_kv_in + mm_bwd("kv", d_kv)
    d_x, d_o0, g_post_g0, g['kv_norm_g'], g_pre_g1 = vjp_post_a((d_h1, d_kv_in, d_hn1))
    d_ygelu, d_t, d_z, d_memo0, g['b_glu'] = vjp_a2(mm_bwd("out0", d_o0))
    d_ygelu = d_ygelu + mm_bwd("glu", d_t)
    d_proj_a, d_kvm0, g['d_skip'], *d_ops = vjp_a1((d_ygelu, d_z, d_memo0))
    g.update(zip(s5_names, vjp_ops(tuple(d_ops))))
    g['pre_norm_g'] = jnp.stack([jnp.zeros_like(g_pre_g1), g_pre_g1])
    g['post_norm_g'] = jnp.stack([g_post_g0, g_post_g1])
    g['mem_norm_g'] = jnp.zeros_like(p['mem_norm_g'])
    early = start_early(g)
    d_memn0 = mm_bwd("mem_kv0", d_kvm0)
    d_x2, *g_late = vjp_norms((mm_da("in_a", d_proj_a, early['token']), d_memn0, d_memn1))
    late = start_late(g_late)
    mm_dw("in_a", d_proj_a, late['token'])
    return loss, d_x + d_x2, pending, early, late


def kernel(x, mem, pre_norm_g, post_norm_g, w_in_a, lam_re, lam_im, log_step, b_re, b_im, c_re, c_im, d_skip, w_glu, b_glu, kv_norm_g, w_kv, w_fgate, b_fgate, w_in_b, mem_norm_g, w_mem_kv, w_out, loss_target, m_pre_norm_g, m_post_norm_g, m_w_in_a, m_lam_re, m_lam_im, m_log_step, m_b_re, m_b_im, m_c_re, m_c_im, m_d_skip, m_w_glu, m_b_glu, m_kv_norm_g, m_w_kv, m_w_fgate, m_b_fgate, m_w_in_b, m_mem_norm_g, m_w_mem_kv, m_w_out, v_pre_norm_g, v_post_norm_g, v_w_in_a, v_lam_re, v_lam_im, v_log_step, v_b_re, v_b_im, v_c_re, v_c_im, v_d_skip, v_w_glu, v_b_glu, v_kv_norm_g, v_w_kv, v_w_fgate, v_b_fgate, v_w_in_b, v_mem_norm_g, v_w_mem_kv, v_w_out):
    a = dict(zip(INPUTS, (x, mem, pre_norm_g, post_norm_g, w_in_a, lam_re, lam_im, log_step, b_re, b_im, c_re, c_im, d_skip, w_glu, b_glu, kv_norm_g, w_kv, w_fgate, b_fgate, w_in_b, mem_norm_g, w_mem_kv, w_out, loss_target, m_pre_norm_g, m_post_norm_g, m_w_in_a, m_lam_re, m_lam_im, m_log_step, m_b_re, m_b_im, m_c_re, m_c_im, m_d_skip, m_w_glu, m_b_glu, m_kv_norm_g, m_w_kv, m_w_fgate, m_b_fgate, m_w_in_b, m_mem_norm_g, m_w_mem_kv, m_w_out, v_pre_norm_g, v_post_norm_g, v_w_in_a, v_lam_re, v_lam_im, v_log_step, v_b_re, v_b_im, v_c_re, v_c_im, v_d_skip, v_w_glu, v_b_glu, v_kv_norm_g, v_w_kv, v_w_fgate, v_b_fgate, v_w_in_b, v_mem_norm_g, v_w_mem_kv, v_w_out)))
    me = 4 * lax.axis_index("x") + 2 * lax.axis_index("y") + lax.axis_index("c")
    n_layers = w_out.shape[0]

    small_shapes = [a[n].shape for n in SMALL_SHARDED]
    small_rows = -(-sum(math.prod(s) for s in small_shapes) // (LANE * 8)) * 8
    shards = dict(in_a=w_in_a[0], mem_kv0=w_mem_kv[0], glu=w_glu[0], out0=w_out[0], kv=w_kv, in_b=w_in_b[0], mem_kv1=w_mem_kv[1], out1=w_out[1])
    operands = [_pack([a[n] for n in SMALL_SHARDED], small_rows)] + [shards[n].astype(BF16) for n in GATHER_ORDER[1:]]
    gathered = exchange_start("ag_weights", operands, gather=True)
    small = exchange_wait(gathered, 0, gathered['token'])
    small = [jnp.stack(parts) for parts in zip(*[_unpack(small[b], small_shapes) for b in range(N_DEV)])]
    p = {n: a[n] for n in REPLICATED}
    for n in ('lam_re', 'lam_im', 'log_step', 'b_re', 'b_im', 'c_re', 'c_im'):
        p[n] = p[n][0]
    p['d_skip'] = small[0].reshape(-1)
    p['b_glu'] = small[1].reshape(-1)
    p['w_fgate'] = small[2].reshape(-1, w_fgate.shape[1])
    p['pre_norm_g'] = p['pre_norm_g'] + gathered['token'][0, 0]

    full_shapes = [a[n].shape for n in REPLICATED] + [(1, N_DEV * d_skip.shape[1]), (1, N_DEV * b_glu.shape[1]), (N_DEV * w_fgate.shape[0], w_fgate.shape[1])]
    rows = _packed_rows(full_shapes)
    late_rows = 3 * pre_norm_g.shape[1] // LANE
    start_early = lambda g: exchange_start("ag_grads", [_pack([g[n] for n in REPLICATED + SMALL_SHARDED], rows)], gather=True)
    start_late = lambda g_late: exchange_start("ag_grads_late", [_pack(g_late, late_rows)], gather=True)

    loss_local, grad_x, pending, early, late = local_step(p, gathered, x[0], mem[0], loss_target[0], start_early, start_late)
    loss = lax.psum(loss_local, MESH_AXES)

    out = {}

    def update(tag, name, layer=None):
        pick = (lambda t: t) if layer is None else (lambda t: t[layer])
        parts = exchange_wait(pending[tag], 0, grad_x)
        two_d = lambda t: pick(t).reshape(parts.shape[1:])
        res = adamw_reduce("adamw_" + tag, parts, two_d(a[name]), two_d(a['m_' + name]), two_d(a['v_' + name]))
        return [r.reshape(pick(a[name]).shape) for r in res]

    per_layer = {n: [None] * n_layers for n in ('w_out', 'w_mem_kv')}
    per_layer['w_out'][1] = update("out1", 'w_out', 1)
    per_layer['w_mem_kv'][1] = update("mem_kv1", 'w_mem_kv', 1)
    out['w_in_b'] = [r[None] for r in update("in_b", 'w_in_b', 0)]
    out['w_kv'] = update("kv", 'w_kv')
    per_layer['w_out'][0] = update("out0", 'w_out', 0)
    per_layer['w_mem_kv'][0] = update("mem_kv0", 'w_mem_kv', 0)
    out['w_glu'] = [r[None] for r in update("glu", 'w_glu', 0)]
    for name in per_layer:
        out[name] = [jnp.stack(t) for t in zip(*per_layer[name])]

    zeros = [jnp.zeros(s, F32) for s in full_shapes[len(REPLICATED):]]
    packed = lambda pre: _pack([a[pre + n] for n in REPLICATED] + zeros, rows)
    res = [_unpack(r, full_shapes) for r in adamw_reduce("adamw_small", exchange_wait(early, 0, grad_x), packed(''), packed('m_'), packed('v_'))]
    for i, n in enumerate(REPLICATED):
        out[n] = [r[i] for r in res]
    g_full = res[0][len(REPLICATED):]
    g_shard = [lax.dynamic_slice_in_dim(g_full[0], me * d_skip.shape[1], d_skip.shape[1], 1),
               lax.dynamic_slice_in_dim(g_full[1], me * b_glu.shape[1], b_glu.shape[1], 1),
               lax.dynamic_slice_in_dim(g_full[2], me * w_fgate.shape[0], w_fgate.shape[0], 0)]
    packed = lambda pre: _pack([a[pre + n] for n in SMALL_SHARDED], small_rows)
    res = [_unpack(r, small_shapes) for r in adamw_reduce("adamw_small_sharded", _pack(g_shard, small_rows)[None], packed(''), packed('m_'), packed('v_'))]
    for i, n in enumerate(SMALL_SHARDED):
        out[n] = [r[i] for r in res]
    packed = lambda pre: _pack([a[pre + 'pre_norm_g'][0], a[pre + 'mem_norm_g'][0], a[pre + 'mem_norm_g'][1]], late_rows)
    res = [_unpack(r, [pre_norm_g.shape[1:]] * 3) for r in adamw_reduce("adamw_small_late", exchange_wait(late, 0, grad_x), packed(''), packed('m_'), packed('v_'))]
    out['pre_norm_g'] = [jnp.stack([r[0], o[1]]) for r, o in zip(res, out['pre_norm_g'])]
    out['mem_norm_g'] = [jnp.stack([r[1], r[2]]) for r in res]
    out['w_in_a'] = [r[None] for r in update("in_a", 'w_in_a', 0)]

    return (loss, grad_x[None], *[out[n][k] for k in range(4) for n in WEIGHTS])
```

```python
import functools
import math

import jax
import jax.numpy as jnp
from jax import lax
from jax.experimental import pallas as pl
from jax.experimental.pallas import tpu as pltpu

F32 = jnp.float32
BF16 = jnp.bfloat16
HP = lax.Precision.HIGHEST
MESH_AXES = ("x", "y", "c")
N_DEV = 8
V7X_VMEM_LIMIT = 56 * 1024 * 1024
LANE = 128

EPS = 1e-6
HEAD_DIM = 128
SSM_GROUP = 16
SSM_STATE = 64
SSM_T = 8
SSM_LO = 16
SSM_GB = 8
FOX_BQ = 256
GRAD_DTYPE = BF16
ACT_DTYPE = BF16
ADAM_LR = 0.001
ADAM_B1 = 0.9
ADAM_B2 = 0.999
ADAM_EPS = 1e-08
ADAM_WD = 0.01
ADAM_STEP = 10

WEIGHTS = ['pre_norm_g', 'post_norm_g', 'w_in_a', 'lam_re', 'lam_im', 'log_step', 'b_re', 'b_im', 'c_re', 'c_im', 'd_skip',
           'w_glu', 'b_glu', 'kv_norm_g', 'w_kv', 'w_fgate', 'b_fgate', 'w_in_b', 'mem_norm_g', 'w_mem_kv', 'w_out']
INPUTS = ['x', 'mem'] + WEIGHTS + ['loss_target'] + ['m_' + n for n in WEIGHTS] + ['v_' + n for n in WEIGHTS]
REPLICATED = ['pre_norm_g', 'post_norm_g', 'lam_re', 'lam_im', 'log_step', 'b_re', 'b_im', 'c_re', 'c_im', 'kv_norm_g',
              'b_fgate', 'mem_norm_g']
SMALL_SHARDED = ['d_skip', 'b_glu', 'w_fgate']


def _cparams(sem=None):
    return pltpu.CompilerParams(dimension_semantics=sem, vmem_limit_bytes=V7X_VMEM_LIMIT)


def _tile(n, cap):
    if n <= cap:
        return n
    best = None
    for t in range(LANE, cap + 1, LANE):
        if n % t == 0:
            best = t
    assert best is not None, (n, cap)
    return best


def _matmul(name, a, b, a_spec, b_spec, o_spec, out_shape, grid, dims, nk, after=None):
    def body(a_ref, b_ref, *rest):
        o_ref, acc_ref = rest[-2:]
        k = pl.program_id(2)
        part = lax.dot_general(a_ref[...].astype(BF16), b_ref[...].astype(BF16), dims, preferred_element_type=F32)

        @pl.when(k == 0)
        def _():
            acc_ref[...] = part

        @pl.when(k > 0)
        def _():
            acc_ref[...] += part

        @pl.when(k == nk - 1)
        def _():
            o_ref[...] = acc_ref[...].astype(o_ref.dtype)

    acc_shape = tuple(d for d in o_spec.block_shape if d is not None)
    extra = [] if after is None else [after]
    return pl.pallas_call(
        body, name=name, grid=grid, in_specs=[a_spec, b_spec] + [pl.BlockSpec(memory_space=pl.ANY)] * len(extra), out_specs=o_spec,
        out_shape=out_shape, scratch_shapes=[pltpu.VMEM(acc_shape, F32)],
        compiler_params=_cparams(("parallel", "parallel", "arbitrary")),
    )(a, b, *extra)


NN = (((1,), (0,)), ((), ()))
NT = (((1,), (1,)), ((), ()))
TN = (((0,), (0,)), ((), ()))


def _mm_nn(name, a, b, out_dtype=ACT_DTYPE):
    (m, k), (_, n) = a.shape, b.shape
    bm, bn, bk = _tile(m, 1024), _tile(n, 512), _tile(k, 2048)
    return _matmul(name, a, b, pl.BlockSpec((bm, bk), lambda i, j, kk: (i, kk)), pl.BlockSpec((bk, bn), lambda i, j, kk: (kk, j)),
                   pl.BlockSpec((bm, bn), lambda i, j, kk: (i, j)), jax.ShapeDtypeStruct((m, n), out_dtype), (m // bm, n // bn, k // bk), NN, k // bk)


def _mm_nt(name, a, b, after=None, out_dtype=ACT_DTYPE):
    (m, c), (n, _) = a.shape, b.shape
    bm, bn, bk = _tile(m, 1024), _tile(n, 512), _tile(c, 2048)
    return _matmul(name, a, b, pl.BlockSpec((bm, bk), lambda i, j, kk: (i, kk)), pl.BlockSpec((bn, bk), lambda i, j, kk: (j, kk)),
                   pl.BlockSpec((bm, bn), lambda i, j, kk: (i, j)), jax.ShapeDtypeStruct((m, n), out_dtype), (m // bm, n // bn, c // bk), NT, c // bk, after)


def _mm_tn(name, a, b, out_dtype=F32, after=None):
    (c, m), (_, n) = a.shape, b.shape
    bm, bn, bk = _tile(m, 1024), _tile(n, 512), _tile(c, 2048)
    return _matmul(name, a, b, pl.BlockSpec((bk, bm), lambda i, j, kk: (kk, i)), pl.BlockSpec((bk, bn), lambda i, j, kk: (kk, j)),
                   pl.BlockSpec((bm, bn), lambda i, j, kk: (i, j)), jax.ShapeDtypeStruct((m, n), out_dtype), (m // bm, n // bn, c // bk), TN, c // bk, after)


def _mm_nn_cb(name, a, bb):
    (m, k), (nb, _, ns) = a.shape, bb.shape
    bm, bk = _tile(m, 1024), _tile(k, 2048)
    return _matmul(name, a, bb, pl.BlockSpec((bm, bk), lambda i, j, kk: (i, kk)), pl.BlockSpec((None, bk, ns), lambda i, j, kk: (j, kk, 0)),
                   pl.BlockSpec((bm, ns), lambda i, j, kk: (i, j)), jax.ShapeDtypeStruct((m, nb * ns), ACT_DTYPE), (m // bm, nb, k // bk), NN, k // bk)


def _mm_nt_cb(name, dy, bb, after=None):
    m, (nb, k, ns) = dy.shape[0], bb.shape
    bm, bn = _tile(m, 1024), _tile(k, 512)

    def body(dy_ref, b_ref, *rest):
        o_ref = rest[-1]
        acc = lax.dot_general(dy_ref[:, 0:ns], b_ref[0], NT, preferred_element_type=F32)
        for j in range(1, nb):
            acc += lax.dot_general(dy_ref[:, j * ns:(j + 1) * ns], b_ref[j], NT, preferred_element_type=F32)
        o_ref[...] = acc.astype(o_ref.dtype)

    extra = [] if after is None else [after]
    return pl.pallas_call(
        body, name=name, grid=(m // bm, k // bn),
        in_specs=[pl.BlockSpec((bm, nb * ns), lambda i, j: (i, 0)), pl.BlockSpec((nb, bn, ns), lambda i, j: (0, j, 0))]
        + [pl.BlockSpec(memory_space=pl.ANY)] * len(extra),
        out_specs=pl.BlockSpec((bm, bn), lambda i, j: (i, j)), out_shape=jax.ShapeDtypeStruct((m, k), ACT_DTYPE),
        compiler_params=_cparams(("parallel", "parallel")),
    )(dy, bb, *extra)


def _mm_tn_cb(name, a, dy, ns, out_dtype=F32, after=None):
    (c, k), nb = a.shape, dy.shape[1] // ns
    bm = _tile(k, 1024)
    return _matmul(name, a, dy, pl.BlockSpec((c, bm), lambda i, j, kk: (0, i)), pl.BlockSpec((c, ns), lambda i, j, kk: (0, j)),
                   pl.BlockSpec((None, bm, ns), lambda i, j, kk: (j, i, 0)), jax.ShapeDtypeStruct((nb, k, ns), out_dtype), (k // bm, nb, 1), TN, 1, after)


def make_mm(name):
    @jax.custom_vjp
    def mm(a, b):
        return _mm_nn(name, a, b, F32)

    def fwd(a, b):
        return mm(a, b), (a, b)

    def bwd(res, dy):
        a, b = res
        return _mm_nt(name + "_da", dy, b, out_dtype=a.dtype), _mm_tn(name + "_dw", a, dy, b.dtype)

    mm.defvjp(fwd, bwd)
    return mm


def _rowop_specs(rows, params, bm, nl):
    row_specs = [pl.BlockSpec((bm, r.shape[1] // nl), lambda j, i: (i, j)) for r in rows]
    par_specs = [pl.BlockSpec((p.shape[0], p.shape[1] // nl), lambda j, i: (0, j)) for p in params]
    row_blk = [jax.ShapeDtypeStruct((bm, r.shape[1] // nl), r.dtype) for r in rows]
    par_blk = [jax.ShapeDtypeStruct((p.shape[0], p.shape[1] // nl), p.dtype) for p in params]
    return row_specs, par_specs, row_blk, par_blk


def make_rowop(f, name, bm, nl=1, out_dtypes=None):
    def loaded(refs):
        return [r[...].astype(F32) for r in refs]

    def fwd_call(rows, params):
        n_rows = rows[0].shape[0]
        b = min(bm, n_rows)
        row_specs, par_specs, row_blk, par_blk = _rowop_specs(rows, params, b, nl)
        out_blk = jax.eval_shape(f, *[jax.ShapeDtypeStruct(t.shape, F32) for t in row_blk + par_blk])
        dts = out_dtypes or [F32] * len(out_blk)
        nr, npar = len(rows), len(params)

        def body(*refs):
            for o_ref, o in zip(refs[nr + npar:], f(*loaded(refs[:nr + npar]))):
                o_ref[...] = o.astype(o_ref.dtype)

        return pl.pallas_call(
            body, name=name, grid=(nl, n_rows // b), in_specs=row_specs + par_specs,
            out_specs=[pl.BlockSpec(o.shape, lambda j, i: (i, j)) for o in out_blk],
            out_shape=[jax.ShapeDtypeStruct((n_rows, o.shape[1] * nl), dt) for o, dt in zip(out_blk, dts)],
            compiler_params=_cparams(("parallel", "parallel")),
        )(*rows, *params)

    def bwd_call(rows, params, cts):
        n_rows = rows[0].shape[0]
        b = min(bm, n_rows)
        row_specs, par_specs, row_blk, par_blk = _rowop_specs(rows, params, b, nl)
        ct_specs = [pl.BlockSpec((b, c.shape[1] // nl), lambda j, i: (i, j)) for c in cts]
        nr, npar, nct = len(rows), len(params), len(cts)

        def body(*refs):
            i = pl.program_id(1)
            _, vjp = jax.vjp(lambda *v: tuple(f(*v)), *loaded(refs[:nr + npar]))
            grads = vjp(tuple(loaded(refs[nr + npar:nr + npar + nct])))
            outs = refs[nr + npar + nct:]
            for o_ref, g in zip(outs[:nr], grads[:nr]):
                o_ref[...] = g.astype(o_ref.dtype)
            for o_ref, g in zip(outs[nr:], grads[nr:]):
                @pl.when(i == 0)
                def _(o_ref=o_ref, g=g):
                    o_ref[...] = g

                @pl.when(i > 0)
                def _(o_ref=o_ref, g=g):
                    o_ref[...] += g

        outs = pl.pallas_call(
            body, name=name + "_bwd", grid=(nl, n_rows // b), in_specs=row_specs + par_specs + ct_specs,
            out_specs=row_specs + par_specs,
            out_shape=[jax.ShapeDtypeStruct(a.shape, a.dtype) for a in rows] + [jax.ShapeDtypeStruct(a.shape, F32) for a in params],
            compiler_params=_cparams(("arbitrary", "arbitrary")),
        )(*rows, *params, *cts)
        return tuple(outs[:nr]), tuple(g.astype(a.dtype) for g, a in zip(outs[nr:], params))

    @jax.custom_vjp
    def op(rows, params):
        return tuple(fwd_call(rows, params))

    def fwd(rows, params):
        return op(rows, params), (rows, params)

    def bwd(res, cts):
        rows, params = res
        return bwd_call(rows, params, tuple(cts))

    op.defvjp(fwd, bwd)
    return op


def make_groupop(f, name, gb):
    def specs(arrs):
        return [pl.BlockSpec((gb,) + a.shape[1:], lambda g: (g, 0, 0)) for a in arrs]

    def fwd_call(arrs):
        g_n = arrs[0].shape[0]
        out_blk = jax.eval_shape(f, *[jax.ShapeDtypeStruct((gb,) + a.shape[1:], a.dtype) for a in arrs])[0]
        n = len(arrs)

        def body(*refs):
            for i, o in enumerate(f(*[r[...] for r in refs[:n]])):
                refs[n][i] = o.astype(refs[n].dtype)

        return pl.pallas_call(
            body, name=name, grid=(g_n // gb,), in_specs=specs(arrs),
            out_specs=pl.BlockSpec((gb,) + out_blk.shape, lambda g: (g, 0, 0)),
            out_shape=jax.ShapeDtypeStruct((g_n,) + out_blk.shape, ACT_DTYPE),
            compiler_params=_cparams(("parallel",)),
        )(*arrs)

    def bwd_call(arrs, ct):
        g_n = arrs[0].shape[0]
        n = len(arrs)

        def body(*refs):
            _, vjp = jax.vjp(lambda *v: tuple(f(*v)), *[r[...] for r in refs[:n]])
            for o_ref, g in zip(refs[n + 1:], vjp(tuple(refs[n][i].astype(F32) for i in range(gb)))):
                o_ref[...] = g.astype(o_ref.dtype)

        return pl.pallas_call(
            body, name=name + "_bwd", grid=(g_n // gb,), in_specs=specs(arrs) + specs([ct]), out_specs=specs(arrs),
            out_shape=[jax.ShapeDtypeStruct(a.shape, a.dtype) for a in arrs],
            compiler_params=_cparams(("parallel",)),
        )(*arrs, ct)

    @jax.custom_vjp
    def op(*arrs):
        return fwd_call(arrs)

    def fwd(*arrs):
        return op(*arrs), arrs

    def bwd(arrs, ct):
        return tuple(bwd_call(arrs, ct))

    op.defvjp(fwd, bwd)
    return op


def _rms(x, g):
    return x * lax.rsqrt(jnp.mean(x * x, axis=-1, keepdims=True) + EPS) * g


def _silu(z):
    return z * jax.nn.sigmoid(z)


def _log_sigmoid(x):
    return jnp.minimum(x, 0.0) - jnp.log(1.0 + jnp.exp(-jnp.abs(x)))


def f_norm(x, g):
    return (_rms(x, g),)


def f_gate_a1(y, u, d):
    return (jax.nn.gelu(y + d * u),)


def f_gate_a2(yg, t, z, b):
    return (yg * jax.nn.sigmoid(t + b) * _silu(z),)


def f_gate_b(att, z):
    return (att * _silu(z),)


def f_post_a(h, o, post_g, kv_g, pre_g):
    h1 = h + _rms(o, post_g)
    return h1, _rms(h1, kv_g), _rms(h1, pre_g)


def f_final(h, o, tgt, post_g):
    err = h + _rms(o, post_g) - tgt
    return (0.5 * jnp.mean(err * err, axis=-1, keepdims=True),)


def f_logf(gl, b):
    return (_log_sigmoid(gl + b),)


def _mxu(a, b, dims):
    return lax.dot_general(a.astype(BF16), b.astype(BF16), dims, preferred_element_type=F32)


@jax.custom_vjp
def _mxu_nn(a, b):
    return _mxu(a, b, NN)


def _mxu_nn_bwd(res, g):
    a, b = res
    return _mxu(g, b, NT).astype(a.dtype), _mxu(a, g, TN).astype(b.dtype)


_mxu_nn.defvjp(lambda a, b: (_mxu(a, b, NN), (a, b)), _mxu_nn_bwd)


@jax.custom_vjp
def _mxu_nt(a, b):
    return _mxu(a, b, NT)


def _mxu_nt_bwd(res, g):
    a, b = res
    return _mxu(g, b, NN).astype(a.dtype), _mxu(g, a, TN).astype(b.dtype)


_mxu_nt.defvjp(lambda a, b: (_mxu(a, b, NT), (a, b)), _mxu_nt_bwd)


def f_memattn(q, zm, km, vm):
    o = _mxu_nn(_softmax(_mxu_nt(q * (HEAD_DIM ** -0.5), km)), vm)
    return (o * _silu(zm),)


@jax.custom_vjp
def _softmax(s):
    e = jnp.exp(s - jnp.max(s, axis=-1, keepdims=True))
    return e * (1.0 / jnp.sum(e, axis=-1, keepdims=True))


def _softmax_fwd(s):
    p = _softmax(s)
    return p, p


def _softmax_bwd(p, dp):
    return (p * (dp - jnp.sum(dp * p, axis=-1, keepdims=True)),)


_softmax.defvjp(_softmax_fwd, _softmax_bwd)


def _fox_block(q, k, v, fr):
    bq = q.shape[0]
    s = _mxu_nt(q * (HEAD_DIM ** -0.5), k) - fr
    tri = lax.broadcasted_iota(jnp.int32, (bq, bq), 0) >= lax.broadcasted_iota(jnp.int32, (bq, bq), 1)
    diag = jnp.where(tri, s[:, -bq:], -1e30)
    s = diag if k.shape[0] == bq else jnp.concatenate([s[:, :-bq], diag], axis=1)
    return _mxu_nn(_softmax(s), v)


def _cmul(ar, ai, xr, xi):
    return ar * xr - ai * xi, ar * xi + ai * xr


def _hp_dot(a, b):
    return jnp.dot(a, b, precision=HP, preferred_element_type=F32)


_bf_dot = _mxu_nn


def f_s5(u, toep, win_r, win_i, wout_r, wout_i, coef):
    gb = u.shape[0]
    n_hi = u.shape[1] // SSM_LO
    sr = [_bf_dot(u[i], win_r[i]) for i in range(gb)]
    si = [_bf_dot(u[i], win_i[i]) for i in range(gb)]

    def stacked(parts, lo):
        return jnp.concatenate([a[lo * n_hi:(lo + 1) * n_hi] for a in parts], axis=0)

    def coef_rows(k):
        return jnp.concatenate([jnp.broadcast_to(coef[i, k:k + 1, :], (n_hi, coef.shape[2])) for i in range(gb)], axis=0)

    at_r, at_i = coef_rows(0), coef_rows(1)
    pr, pi = stacked(sr, 0), stacked(si, 0)
    for lo in range(1, SSM_LO):
        dr, di = _cmul(at_r, at_i, pr, pi)
        pr, pi = stacked(sr, lo) + dr, stacked(si, lo) + di
    n = 2 * gb * n_hi
    ri = lax.broadcasted_iota(jnp.int32, (n, n), 0)
    ci = lax.broadcasted_iota(jnp.int32, (n, n), 1)
    same = ri // n_hi == ci // n_hi

    def shifted(d, zr, zi):
        z = _hp_dot((same & (ri - ci == d)).astype(F32), jnp.concatenate([zr, zi], axis=0))
        return z[:n // 2], z[n // 2:]

    d, step = 1, 0
    while d < n_hi:
        dr, di = _cmul(coef_rows(2 + 2 * step), coef_rows(3 + 2 * step), *shifted(d, pr, pi))
        pr, pi = pr + dr, pi + di
        d, step = 2 * d, step + 1
    er, ei = shifted(1, pr, pi)
    xr, xi = [er], [ei]
    for lo in range(1, SSM_LO):
        dr, di = _cmul(at_r, at_i, xr[-1], xi[-1])
        xr.append(stacked(sr, lo - 1) + dr)
        xi.append(stacked(si, lo - 1) + di)

    def group_rows(parts, i):
        return jnp.concatenate([a[i * n_hi:(i + 1) * n_hi] for a in parts], axis=0)

    return tuple(_bf_dot(u[i], toep[i]) + _bf_dot(group_rows(xr, i), wout_r[i]) + _bf_dot(group_rows(xi, i), wout_i[i])
                 for i in range(gb))


def s5_operators(lam_re, lam_im, log_step, b_re, b_im, c_re, c_im, n_hi):
    t_n = SSM_T
    lr, li = lam_re, lam_im
    dt = jnp.exp(log_step)[:, None]
    mag = jnp.exp(lr * dt)
    ar, ai = mag * jnp.cos(li * dt), mag * jnp.sin(li * dt)
    den = lr * lr + li * li
    cr = ((ar - 1.0) * lr + ai * li) / den
    ci = (ai * lr - (ar - 1.0) * li) / den
    bbr = cr[..., None] * b_re - ci[..., None] * b_im
    bbi = cr[..., None] * b_im + ci[..., None] * b_re
    k = jnp.arange(t_n + 1, dtype=F32)[:, None, None]
    pm, ang = jnp.exp(k * (lr * dt)), k * (li * dt)
    pr, pi = pm * jnp.cos(ang), pm * jnp.sin(ang)
    abr = pr[..., None] * bbr - pi[..., None] * bbi
    abi = pr[..., None] * bbi + pi[..., None] * bbr
    kk = (jnp.einsum('ghp,kgpj->kghj', c_re, abr[:t_n], precision=HP)
          - jnp.einsum('ghp,kgpj->kghj', c_im, abi[:t_n], precision=HP))
    lag = jnp.arange(t_n)[None, :] - jnp.arange(t_n)[:, None]
    onehot = (lag[None] == jnp.arange(t_n)[:, None, None]).astype(F32)
    g_n, h_n = c_re.shape[0], c_re.shape[1]
    toep = jnp.einsum('kst,kghj->gsjth', onehot, kk, precision=HP).reshape(g_n, t_n * h_n, t_n * h_n)
    win_r = abr[:t_n][::-1].transpose(1, 0, 3, 2).reshape(g_n, t_n * h_n, -1)
    win_i = abi[:t_n][::-1].transpose(1, 0, 3, 2).reshape(g_n, t_n * h_n, -1)
    p1r, p1i = pr[1:, :, None, :], pi[1:, :, None, :]
    wout_r = (c_re[None] * p1r - c_im[None] * p1i).transpose(1, 3, 0, 2).reshape(g_n, -1, t_n * h_n)
    wout_i = (-(c_re[None] * p1i + c_im[None] * p1r)).transpose(1, 3, 0, 2).reshape(g_n, -1, t_n * h_n)
    rows = [pr[t_n], pi[t_n]]
    qr, qi = pr[t_n], pi[t_n]
    for _ in range(int(math.log2(SSM_LO))):
        qr, qi = qr * qr - qi * qi, 2.0 * qr * qi
    d = 1
    while d < n_hi:
        rows += [qr, qi]
        qr, qi = qr * qr - qi * qi, 2.0 * qr * qi
        d *= 2
    coef = jnp.stack(rows, axis=1)
    return toep, win_r, win_i, wout_r, wout_i, coef


def _cumsum_call(name, x, reverse):
    n_rows, w = x.shape
    bm = min(256, n_rows)
    nb = n_rows // bm

    def body(x_ref, o_ref, carry_ref):
        i = pl.program_id(0)

        @pl.when(i == 0)
        def _():
            carry_ref[...] = jnp.zeros_like(carry_ref)

        ri = lax.broadcasted_iota(jnp.int32, (bm, bm), 0)
        ci = lax.broadcasted_iota(jnp.int32, (bm, bm), 1)
        tri = ((ri <= ci) if reverse else (ri >= ci)).astype(F32)
        xb = x_ref[...]
        o_ref[...] = _hp_dot(tri, xb) + carry_ref[...]
        carry_ref[...] += jnp.sum(xb, axis=0, keepdims=True)

    idx = (lambda i: (nb - 1 - i, 0)) if reverse else (lambda i: (i, 0))
    return pl.pallas_call(
        body, name=name, grid=(nb,), in_specs=[pl.BlockSpec((bm, w), idx)], out_specs=pl.BlockSpec((bm, w), idx),
        out_shape=jax.ShapeDtypeStruct(x.shape, F32), scratch_shapes=[pltpu.VMEM((1, w), F32)],
        compiler_params=_cparams(("arbitrary",)),
    )(x)


def make_cumsum(name):
    @jax.custom_vjp
    def cs(x):
        return _cumsum_call(name, x, False)

    def fwd(x):
        return cs(x), None

    def bwd(_, dy):
        return (_cumsum_call(name + "_bwd", dy, True),)

    cs.defvjp(fwd, bwd)
    return cs


def _fox_specs(n_rows, bq):
    q_spec = pl.BlockSpec((bq, HEAD_DIM), lambda h, i: (i, h))
    kv_spec = pl.BlockSpec((n_rows, HEAD_DIM), lambda h, i: (0, h))
    fr_spec = pl.BlockSpec((None, 1, n_rows), lambda h, i: (h, 0, 0))
    return q_spec, kv_spec, fr_spec


def _fox_fwd_call(name, q, k, v, fr):
    n_rows, width = q.shape
    bq = min(FOX_BQ, n_rows)
    nq = n_rows // bq
    q_spec, kv_spec, fr_spec = _fox_specs(n_rows, bq)

    def body(q_ref, k_ref, v_ref, fr_ref, o_ref):
        i = pl.program_id(1)
        for p in range(nq):
            n_keys = (p + 1) * bq

            @pl.when(i == p)
            def _(n_keys=n_keys):
                o = _fox_block(q_ref[...].astype(F32), k_ref[:n_keys, :], v_ref[:n_keys, :], fr_ref[:, :n_keys])
                o_ref[...] = o.astype(o_ref.dtype)

    return pl.pallas_call(
        body, name=name, grid=(width // HEAD_DIM, nq), in_specs=[q_spec, kv_spec, kv_spec, fr_spec], out_specs=q_spec,
        out_shape=jax.ShapeDtypeStruct(q.shape, ACT_DTYPE), compiler_params=_cparams(("parallel", "parallel")),
    )(q, k, v, fr)


def _fox_bwd_call(name, q, k, v, fr, do):
    n_rows, width = q.shape
    bq = min(FOX_BQ, n_rows)
    nq = n_rows // bq
    q_spec, kv_spec, fr_spec = _fox_specs(n_rows, bq)

    def body(q_ref, k_ref, v_ref, fr_ref, do_ref, dq_ref, dk_ref, dv_ref, dfr_ref, dk_acc, dv_acc):
        i = pl.program_id(1)

        @pl.when(i == 0)
        def _():
            dk_acc[...] = jnp.zeros_like(dk_acc)
            dv_acc[...] = jnp.zeros_like(dv_acc)
            dfr_ref[...] = jnp.zeros_like(dfr_ref)

        for p in range(nq):
            n_keys = (p + 1) * bq

            @pl.when(i == p)
            def _(n_keys=n_keys):
                _, vjp = jax.vjp(_fox_block, q_ref[...].astype(F32), k_ref[:n_keys, :].astype(F32), v_ref[:n_keys, :].astype(F32),
                                 fr_ref[:, :n_keys])
                dq, dk, dv, dfr = vjp(do_ref[...].astype(F32))
                dq_ref[...] = dq.astype(dq_ref.dtype)
                dk_acc[:n_keys, :] += dk
                dv_acc[:n_keys, :] += dv
                dfr_ref[:, :n_keys] += dfr

        @pl.when(i == nq - 1)
        def _():
            dk_ref[...] = dk_acc[...].astype(dk_ref.dtype)
            dv_ref[...] = dv_acc[...].astype(dv_ref.dtype)

    return pl.pallas_call(
        body, name=name, grid=(width // HEAD_DIM, nq), in_specs=[q_spec, kv_spec, kv_spec, fr_spec, q_spec],
        out_specs=[q_spec, kv_spec, kv_spec, fr_spec], out_shape=[jax.ShapeDtypeStruct(a.shape, a.dtype) for a in (q, k, v, fr)],
        scratch_shapes=[pltpu.VMEM((n_rows, HEAD_DIM), F32), pltpu.VMEM((n_rows, HEAD_DIM), F32)],
        compiler_params=_cparams(("parallel", "arbitrary")),
    )(q, k, v, fr, do)


def make_fox(name):
    @jax.custom_vjp
    def fox(q, k, v, fr):
        return _fox_fwd_call(name, q, k, v, fr)

    def fwd(q, k, v, fr):
        return fox(q, k, v, fr), (q, k, v, fr)

    def bwd(res, do):
        return tuple(_fox_bwd_call(name + "_bwd", *res, do))

    fox.defvjp(fwd, bwd)
    return fox


def _relations(x, y, c):
    out = []
    for rel in range(1, N_DEV):
        px = 1 - x if rel & 4 else x
        py = 1 - y if rel & 2 else y
        pc = 1 - c if rel & 1 else c
        out.append((rel, (px, py, pc), 4 * px + 2 * py + pc))
    return out


_HBM_SPEC = pl.BlockSpec(memory_space=pltpu.HBM)
_SEM_SPEC = pl.BlockSpec(memory_space=pltpu.SEMAPHORE)
_DATAFLOW = pltpu.SideEffectType.DATAFLOW_SIDE_EFFECTING


def exchange_start(name, srcs, gather):
    n = len(srcs)
    me = 4 * lax.axis_index("x") + 2 * lax.axis_index("y") + lax.axis_index("c")
    lands = []
    for s in srcs:
        own = s if gather else lax.dynamic_index_in_dim(s, me, 0, keepdims=False)
        land = lax.empty((N_DEV,) + own.shape, s.dtype)
        lands.append(lax.dynamic_update_index_in_dim(land, own, me, 0))

    def body(*refs):
        src_refs, land_refs = refs[:n], refs[n:2 * n]
        send_sems, recv_sems, token = refs[2 * n], refs[2 * n + 1], refs[-1]
        x, y, c = lax.axis_index("x"), lax.axis_index("y"), lax.axis_index("c")
        mine = 4 * x + 2 * y + c
        for k in range(n):
            for rel, peer, peer_blk in _relations(x, y, c):
                pltpu.make_async_remote_copy(
                    src_ref=src_refs[k] if gather else src_refs[k].at[peer_blk], dst_ref=land_refs[k].at[mine],
                    send_sem=send_sems.at[7 * k + rel - 1], recv_sem=recv_sems.at[7 * k + rel - 1],
                    device_id=peer, device_id_type=pl.DeviceIdType.MESH).start()
        token[...] = jnp.zeros_like(token)

    sem = pltpu.SemaphoreType.DMA((7 * n,))
    hbm = lambda t: pltpu.HBM(t.shape, t.dtype)
    outs = pl.pallas_call(
        body, name=name,
        out_shape=(sem, sem, *[hbm(s) for s in srcs], *[hbm(l) for l in lands], jax.ShapeDtypeStruct((8, LANE), F32)),
        in_specs=[_HBM_SPEC] * (2 * n),
        out_specs=(_SEM_SPEC, _SEM_SPEC, *[_HBM_SPEC] * (2 * n), pl.BlockSpec(memory_space=pltpu.VMEM)),
        input_output_aliases={i: 2 + i for i in range(2 * n)},
        compiler_params=pltpu.CompilerParams(has_side_effects=_DATAFLOW),
    )(*[pltpu.with_memory_space_constraint(t, pltpu.HBM) for t in list(srcs) + lands])
    return dict(name=name, gather=gather, send=outs[0], recv=outs[1], srcs=list(outs[2:2 + n]), lands=list(outs[2 + n:2 + 2 * n]), token=outs[-1])


def exchange_wait(handle, k, after):
    gather = handle['gather']

    def body(src_ref, land_ref, send_sems, recv_sems, after_ref, src_out, land_out):
        x, y, c = lax.axis_index("x"), lax.axis_index("y"), lax.axis_index("c")
        for rel, peer, peer_blk in _relations(x, y, c):
            copy = pltpu.make_async_remote_copy(
                src_ref=src_ref if gather else src_ref.at[peer_blk], dst_ref=land_ref.at[peer_blk],
                send_sem=send_sems.at[7 * k + rel - 1], recv_sem=recv_sems.at[7 * k + rel - 1],
                device_id=peer, device_id_type=pl.DeviceIdType.MESH)
            copy.wait_send()
            copy.wait_recv()

    src, land = handle['srcs'][k], handle['lands'][k]
    return pl.pallas_call(
        body, name=f"{handle['name']}_wait{k}", out_shape=(pltpu.HBM(src.shape, src.dtype), pltpu.HBM(land.shape, land.dtype)),
        in_specs=[_HBM_SPEC, _HBM_SPEC, _SEM_SPEC, _SEM_SPEC, pl.BlockSpec(memory_space=pl.ANY)], out_specs=(_HBM_SPEC, _HBM_SPEC),
        input_output_aliases={0: 0, 1: 1}, compiler_params=pltpu.CompilerParams(has_side_effects=_DATAFLOW),
    )(src, land, handle['send'], handle['recv'], after)[1]


def adamw_reduce(name, partials, w, m, v):
    n_part, n_rows, n_cols = partials.shape
    br = n_rows
    for cand in (512, 256, 128, 64, 32, 16, 8):
        if n_rows % cand == 0 and n_part * cand * n_cols * partials.dtype.itemsize <= (4 << 20):
            br = cand
            break

    def body(p_ref, w_ref, m_ref, v_ref, g_ref, d_ref, nm_ref, nv_ref):
        g = p_ref[0].astype(F32)
        for s in range(1, n_part):
            g = g + p_ref[s].astype(F32)
        m_new = ADAM_B1 * m_ref[...] + (1.0 - ADAM_B1) * g
        v_new = ADAM_B2 * v_ref[...] + (1.0 - ADAM_B2) * jnp.square(g)
        m_hat = m_new / (1.0 - ADAM_B1 ** ADAM_STEP)
        v_hat = v_new / (1.0 - ADAM_B2 ** ADAM_STEP)
        g_ref[...] = g
        d_ref[...] = -ADAM_LR * (m_hat / (jnp.sqrt(v_hat) + ADAM_EPS) + ADAM_WD * w_ref[...])
        nm_ref[...] = m_new
        nv_ref[...] = v_new

    spec = pl.BlockSpec((br, n_cols), lambda i: (i, 0))
    return pl.pallas_call(
        body, name=name, grid=(n_rows // br,), in_specs=[pl.BlockSpec((n_part, br, n_cols), lambda i: (0, i, 0)), spec, spec, spec],
        out_specs=[spec] * 4, out_shape=[jax.ShapeDtypeStruct((n_rows, n_cols), F32)] * 4, compiler_params=_cparams(("parallel",)),
    )(partials, w, m, v)


def _pack(arrays, n_rows):
    flat = jnp.concatenate([a.reshape(-1) for a in arrays])
    return jnp.pad(flat, (0, n_rows * LANE - flat.shape[0])).reshape(n_rows, LANE)


def _unpack(packed, shapes):
    flat, out, off = packed.reshape(-1), [], 0
    for s in shapes:
        n = math.prod(s)
        out.append(flat[off:off + n].reshape(s))
        off += n
    return out


def _packed_rows(shapes):
    n = sum(math.prod(s) for s in shapes)
    return -(-n // (LANE * 512)) * 512


GATHER_ORDER = ['small', 'in_a', 'mem_kv0', 'glu', 'out0', 'kv', 'in_b', 'mem_kv1', 'out1']


BLOCKED = ('in_a', 'kv', 'in_b')


def local_step(p, gathered, x, mem, tgt, start_early, start_late):
    n_rows, d_model = x.shape
    main_w = 3 * d_model // 4
    mem_w = d_model - main_w
    n_groups = main_w // SSM_GROUP
    n_heads = main_w // HEAD_DIM
    n_hi = n_rows // (SSM_T * SSM_LO)
    row = lambda a: a.reshape(1, -1)
    tape, pending = {}, {}

    def mm_fwd(name, act, after=None):
        w = exchange_wait(gathered, GATHER_ORDER.index(name), act if after is None else after)
        if name not in BLOCKED:
            w = w.reshape(-1, w.shape[2])
        tape[name] = (act, w)
        return (_mm_nn_cb if name in BLOCKED else _mm_nn)(name, act, w)

    def mm_dw(name, dy, after=None):
        act, w = tape[name]
        if name in BLOCKED:
            dw = _mm_tn_cb(name + "_dw", act, dy, w.shape[2], GRAD_DTYPE, after)
        else:
            dw = _mm_tn(name + "_dw", act, dy, GRAD_DTYPE, after)
            dw = dw.reshape(N_DEV, dw.shape[0] // N_DEV, dw.shape[1])
        pending[name] = exchange_start("rs_" + name, [dw], gather=False)

    def mm_da(name, dy, after=None):
        return (_mm_nt_cb if name in BLOCKED else _mm_nt)(name + "_da", dy, tape[name][1], after)

    def mm_bwd(name, dy):
        mm_dw(name, dy)
        return mm_da(name, dy, pending[name]['token'])

    def split(proj):
        return proj[:, :main_w], proj[:, main_w:2 * main_w], proj[:, 2 * main_w:2 * main_w + mem_w], proj[:, 2 * main_w + mem_w:]

    def mem_attn(i, qm, zm, kvm):
        memo, = make_rowop(f_memattn, f"mem_attn{i}", 512, mem_w // HEAD_DIM, [ACT_DTYPE])((qm, zm), (kvm[:, :mem_w], kvm[:, mem_w:]))
        return memo

    def seg_norms(x_, pre_g0, mem_g0, mem_g1):
        hn, = make_rowop(f_norm, "pre_norm0", 256, out_dtypes=[ACT_DTYPE])((x_,), (row(pre_g0),))
        memn0, = make_rowop(f_norm, "mem_norm0", 256, out_dtypes=[ACT_DTYPE])((mem,), (row(mem_g0),))
        memn1, = make_rowop(f_norm, "mem_norm1", 256, out_dtypes=[ACT_DTYPE])((mem,), (row(mem_g1),))
        return hn, memn0, memn1

    def seg_a1(proj, kvm, d_skip, *ops):
        u, z, qm, zm = split(proj)
        ug = u.astype(BF16).reshape(n_hi, SSM_LO, SSM_T, n_groups, SSM_GROUP).transpose(3, 1, 0, 2, 4).reshape(n_groups, n_hi * SSM_LO, SSM_T * SSM_GROUP)
        yg = make_groupop(f_s5, "s5", SSM_GB)(ug, *ops)
        y = yg.reshape(n_groups, SSM_LO, n_hi, SSM_T, SSM_GROUP).transpose(2, 1, 3, 0, 4).reshape(n_rows, main_w)
        ygelu, = make_rowop(f_gate_a1, "gate_a1", 256, out_dtypes=[ACT_DTYPE])((y, u), (row(d_skip),))
        return ygelu, z, mem_attn(0, qm, zm, kvm)

    def seg_a2(ygelu, t, z, memo, b_glu):
        main, = make_rowop(f_gate_a2, "gate_a2", 256, out_dtypes=[ACT_DTYPE])((ygelu, t, z), (row(b_glu),))
        return jnp.concatenate([main, memo], axis=1)

    def seg_post_a(x_, o, post_g0, kv_g, pre_g1):
        return make_rowop(f_post_a, "post_a", 128, out_dtypes=[F32, ACT_DTYPE, ACT_DTYPE])((x_, o), (row(post_g0), row(kv_g), row(pre_g1)))

    def seg_b(proj, kv, kv_in, kvm, w_fgate, b_fgate):
        w_fg = jnp.pad(w_fgate, ((0, 0), (0, LANE - n_heads)))
        b_fg = jnp.pad(b_fgate, (0, LANE - n_heads)).reshape(1, LANE)
        logf, = make_rowop(f_logf, "logf", 512)((make_mm("fgate")(kv_in, w_fg),), (b_fg,))
        fcum = make_cumsum("fcum")(logf)[:, :n_heads].T
        q, z, qm, zm = split(proj)
        att = make_fox("fox")(q, kv[:, :main_w], kv[:, main_w:], fcum[:, None, :])
        main, = make_rowop(f_gate_b, "gate_b", 256, out_dtypes=[ACT_DTYPE])((att, z), ())
        return jnp.concatenate([main, mem_attn(1, qm, zm, kvm)], axis=1)

    def seg_final(h1, o, post_g1):
        rowloss, = make_rowop(f_final, "final", 128)((h1, o, tgt), (row(post_g1),))
        return jnp.sum(rowloss)

    (hn0, memn0, memn1), vjp_norms = jax.vjp(seg_norms, x, p['pre_norm_g'][0], p['mem_norm_g'][0], p['mem_norm_g'][1])
    s5_names = ('lam_re', 'lam_im', 'log_step', 'b_re', 'b_im', 'c_re', 'c_im')
    ops, vjp_ops = jax.vjp(lambda *a: s5_operators(*a, n_hi), *[p[n] for n in s5_names])
    proj_a = mm_fwd("in_a", hn0, ops[0])
    kvm0 = mm_fwd("mem_kv0", memn0, proj_a)
    (ygelu, z_a, memo0), vjp_a1 = jax.vjp(seg_a1, proj_a, kvm0, p['d_skip'], *ops)
    t = mm_fwd("glu", ygelu)
    cat0, vjp_a2 = jax.vjp(seg_a2, ygelu, t, z_a, memo0, p['b_glu'])
    o0 = mm_fwd("out0", cat0)
    (h1, kv_in, hn1), vjp_post_a = jax.vjp(seg_post_a, x, o0, p['post_norm_g'][0], p['kv_norm_g'], p['pre_norm_g'][1])
    kv = mm_fwd("kv", kv_in)
    proj_b = mm_fwd("in_b", hn1)
    kvm1 = mm_fwd("mem_kv1", memn1, proj_b)
    cat1, vjp_b = jax.vjp(seg_b, proj_b, kv, kv_in, kvm1, p['w_fgate'], p['b_fgate'])
    o1 = mm_fwd("out1", cat1)
    loss, vjp_final = jax.vjp(seg_final, h1, o1, p['post_norm_g'][1])

    g = {}
    d_h1, d_o1, g_post_g1 = vjp_final(jnp.ones((), F32))
    d_proj_b, d_kv, d_kv_in, d_kvm1, g['w_fgate'], g['b_fgate'] = vjp_b(mm_bwd("out1", d_o1))
    d_memn1 = mm_bwd("mem_kv1", d_kvm1)
    d_hn1 = mm_bwd("in_b", d_proj_b)
    d_kv_in = d_kv_in + mm_bwd("kv", d_kv)
    d_x, d_o0, g_post_g0, g['kv_norm_g'], g_pre_g1 = vjp_post_a((d_h1, d_kv_in, d_hn1))
    d_ygelu, d_t, d_z, d_memo0, g['b_glu'] = vjp_a2(mm_bwd("out0", d_o0))
    d_ygelu = d_ygelu + mm_bwd("glu", d_t)
    d_proj_a, d_kvm0, g['d_skip'], *d_ops = vjp_a1((d_ygelu, d_z, d_memo0))
    g.update(zip(s5_names, vjp_ops(tuple(d_ops))))
    g['pre_norm_g'] = jnp.stack([jnp.zeros_like(g_pre_g1), g_pre_g1])
    g['post_norm_g'] = jnp.stack([g_post_g0, g_post_g1])
    g['mem_norm_g'] = jnp.zeros_like(p['mem_norm_g'])
    early = start_early(g)
    d_memn0 = mm_bwd("mem_kv0", d_kvm0)
    d_x2, *g_late = vjp_norms((mm_da("in_a", d_proj_a, early['token']), d_memn0, d_memn1))
    late = start_late(g_late)
    mm_dw("in_a", d_proj_a, late['token'])
    return loss, d_x + d_x2, pending, early, late


def kernel(x, mem, pre_norm_g, post_norm_g, w_in_a, lam_re, lam_im, log_step, b_re, b_im, c_re, c_im, d_skip, w_glu, b_glu, kv_norm_g, w_kv, w_fgate, b_fgate, w_in_b, mem_norm_g, w_mem_kv, w_out, loss_target, m_pre_norm_g, m_post_norm_g, m_w_in_a, m_lam_re, m_lam_im, m_log_step, m_b_re, m_b_im, m_c_re, m_c_im, m_d_skip, m_w_glu, m_b_glu, m_kv_norm_g, m_w_kv, m_w_fgate, m_b_fgate, m_w_in_b, m_mem_norm_g, m_w_mem_kv, m_w_out, v_pre_norm_g, v_post_norm_g, v_w_in_a, v_lam_re, v_lam_im, v_log_step, v_b_re, v_b_im, v_c_re, v_c_im, v_d_skip, v_w_glu, v_b_glu, v_kv_norm_g, v_w_kv, v_w_fgate, v_b_fgate, v_w_in_b, v_mem_norm_g, v_w_mem_kv, v_w_out):
    a = dict(zip(INPUTS, (x, mem, pre_norm_g, post_norm_g, w_in_a, lam_re, lam_im, log_step, b_re, b_im, c_re, c_im, d_skip, w_glu, b_glu, kv_norm_g, w_kv, w_fgate, b_fgate, w_in_b, mem_norm_g, w_mem_kv, w_out, loss_target, m_pre_norm_g, m_post_norm_g, m_w_in_a, m_lam_re, m_lam_im, m_log_step, m_b_re, m_b_im, m_c_re, m_c_im, m_d_skip, m_w_glu, m_b_glu, m_kv_norm_g, m_w_kv, m_w_fgate, m_b_fgate, m_w_in_b, m_mem_norm_g, m_w_mem_kv, m_w_out, v_pre_norm_g, v_post_norm_g, v_w_in_a, v_lam_re, v_lam_im, v_log_step, v_b_re, v_b_im, v_c_re, v_c_im, v_d_skip, v_w_glu, v_b_glu, v_kv_norm_g, v_w_kv, v_w_fgate, v_b_fgate, v_w_in_b, v_mem_norm_g, v_w_mem_kv, v_w_out)))
    me = 4 * lax.axis_index("x") + 2 * lax.axis_index("y") + lax.axis_index("c")
    n_layers = w_out.shape[0]

    small_shapes = [a[n].shape for n in SMALL_SHARDED]
    small_rows = -(-sum(math.prod(s) for s in small_shapes) // (LANE * 8)) * 8
    shards = dict(in_a=w_in_a[0], mem_kv0=w_mem_kv[0], glu=w_glu[0], out0=w_out[0], kv=w_kv, in_b=w_in_b[0], mem_kv1=w_mem_kv[1], out1=w_out[1])
    operands = [_pack([a[n] for n in SMALL_SHARDED], small_rows)] + [shards[n].astype(BF16) for n in GATHER_ORDER[1:]]
    gathered = exchange_start("ag_weights", operands, gather=True)
    small = exchange_wait(gathered, 0, gathered['token'])
    small = [jnp.stack(parts) for parts in zip(*[_unpack(small[b], small_shapes) for b in range(N_DEV)])]
    p = {n: a[n] for n in REPLICATED}
    for n in ('lam_re', 'lam_im', 'log_step', 'b_re', 'b_im', 'c_re', 'c_im'):
        p[n] = p[n][0]
    p['d_skip'] = small[0].reshape(-1)
    p['b_glu'] = small[1].reshape(-1)
    p['w_fgate'] = small[2].reshape(-1, w_fgate.shape[1])
    p['pre_norm_g'] = p['pre_norm_g'] + gathered['token'][0, 0]

    full_shapes = [a[n].shape for n in REPLICATED] + [(1, N_DEV * d_skip.shape[1]), (1, N_DEV * b_glu.shape[1]), (N_DEV * w_fgate.shape[0], w_fgate.shape[1])]
    rows = _packed_rows(full_shapes)
    late_rows = 3 * pre_norm_g.shape[1] // LANE
    start_early = lambda g: exchange_start("ag_grads", [_pack([g[n] for n in REPLICATED + SMALL_SHARDED], rows)], gather=True)
    start_late = lambda g_late: exchange_start("ag_grads_late", [_pack(g_late, late_rows)], gather=True)

    loss_local, grad_x, pending, early, late = local_step(p, gathered, x[0], mem[0], loss_target[0], start_early, start_late)
    loss = lax.psum(loss_local, MESH_AXES)

    out = {}

    def update(tag, name, layer=None):
        pick = (lambda t: t) if layer is None else (lambda t: t[layer])
        parts = exchange_wait(pending[tag], 0, grad_x)
        two_d = lambda t: pick(t).reshape(parts.shape[1:])
        res = adamw_reduce("adamw_" + tag, parts, two_d(a[name]), two_d(a['m_' + name]), two_d(a['v_' + name]))
        return [r.reshape(pick(a[name]).shape) for r in res]

    per_layer = {n: [None] * n_layers for n in ('w_out', 'w_mem_kv')}
    per_layer['w_out'][1] = update("out1", 'w_out', 1)
    per_layer['w_mem_kv'][1] = update("mem_kv1", 'w_mem_kv', 1)
    out['w_in_b'] = [r[None] for r in update("in_b", 'w_in_b', 0)]
    out['w_kv'] = update("kv", 'w_kv')
    per_layer['w_out'][0] = update("out0", 'w_out', 0)
    per_layer['w_mem_kv'][0] = update("mem_kv0", 'w_mem_kv', 0)
    out['w_glu'] = [r[None] for r in update("glu", 'w_glu', 0)]
    for name in per_layer:
        out[name] = [jnp.stack(t) for t in zip(*per_layer[name])]

    zeros = [jnp.zeros(s, F32) for s in full_shapes[len(REPLICATED):]]
    packed = lambda pre: _pack([a[pre + n] for n in REPLICATED] + zeros, rows)
    res = [_unpack(r, full_shapes) for r in adamw_reduce("adamw_small", exchange_wait(early, 0, grad_x), packed(''), packed('m_'), packed('v_'))]
    for i, n in enumerate(REPLICATED):
        out[n] = [r[i] for r in res]
    g_full = res[0][len(REPLICATED):]
    g_shard = [lax.dynamic_slice_in_dim(g_full[0], me * d_skip.shape[1], d_skip.shape[1], 1),
               lax.dynamic_slice_in_dim(g_full[1], me * b_glu.shape[1], b_glu.shape[1], 1),
               lax.dynamic_slice_in_dim(g_full[2], me * w_fgate.shape[0], w_fgate.shape[0], 0)]
    packed = lambda pre: _pack([a[pre + n] for n in SMALL_SHARDED], small_rows)
    res = [_unpack(r, small_shapes) for r in adamw_reduce("adamw_small_sharded", _pack(g_shard, small_rows)[None], packed(''), packed('m_'), packed('v_'))]
    for i, n in enumerate(SMALL_SHARDED):
        out[n] = [r[i] for r in res]
    packed = lambda pre: _pack([a[pre + 'pre_norm_g'][0], a[pre + 'mem_norm_g'][0], a[pre + 'mem_norm_g'][1]], late_rows)
    res = [_unpack(r, [pre_norm_g.shape[1:]] * 3) for r in adamw_reduce("adamw_small_late", exchange_wait(late, 0, grad_x), packed(''), packed('m_'), packed('v_'))]
    out['pre_norm_g'] = [jnp.stack([r[0], o[1]]) for r, o in zip(res, out['pre_norm_g'])]
    out['mem_norm_g'] = [jnp.stack([r[1], r[2]]) for r in res]
    out['w_in_a'] = [r[None] for r in update("in_a", 'w_in_a', 0)]

    return (loss, grad_x[None], *[out[n][k] for k in range(4) for n in WEIGHTS])
```

```python
import functools
import math

import jax
import jax.numpy as jnp
from jax import lax
from jax.experimental import pallas as pl
from jax.experimental.pallas import tpu as pltpu

F32 = jnp.float32
BF16 = jnp.bfloat16
HP = lax.Precision.HIGHEST
MESH_AXES = ("x", "y", "c")
N_DEV = 8
V7X_VMEM_LIMIT = 56 * 1024 * 1024
LANE = 128

EPS = 1e-6
HEAD_DIM = 128
SSM_GROUP = 16
SSM_STATE = 64
SSM_T = 8
SSM_LO = 16
SSM_GB = 8
FOX_BQ = 256
FOX_STRIP = 16
GRAD_DTYPE = BF16
ACT_DTYPE = BF16
ADAM_LR = 0.001
ADAM_B1 = 0.9
ADAM_B2 = 0.999
ADAM_EPS = 1e-08
ADAM_WD = 0.01
ADAM_STEP = 10

WEIGHTS = ['pre_norm_g', 'post_norm_g', 'w_in_a', 'lam_re', 'lam_im', 'log_step', 'b_re', 'b_im', 'c_re', 'c_im', 'd_skip',
           'w_glu', 'b_glu', 'kv_norm_g', 'w_kv', 'w_fgate', 'b_fgate', 'w_in_b', 'mem_norm_g', 'w_mem_kv', 'w_out']
INPUTS = ['x', 'mem'] + WEIGHTS + ['loss_target'] + ['m_' + n for n in WEIGHTS] + ['v_' + n for n in WEIGHTS]
REPLICATED = ['pre_norm_g', 'post_norm_g', 'lam_re', 'lam_im', 'log_step', 'b_re', 'b_im', 'c_re', 'c_im', 'kv_norm_g',
              'b_fgate', 'mem_norm_g']
SMALL_SHARDED = ['d_skip', 'b_glu', 'w_fgate']


def _cparams(sem=None):
    return pltpu.CompilerParams(dimension_semantics=sem, vmem_limit_bytes=V7X_VMEM_LIMIT)


def _tile(n, cap):
    if n <= cap:
        return n
    best = None
    for t in range(LANE, cap + 1, LANE):
        if n % t == 0:
            best = t
    assert best is not None, (n, cap)
    return best


def _matmul(name, a, b, a_spec, b_spec, o_spec, out_shape, grid, dims, nk, after=None):
    def body(a_ref, b_ref, *rest):
        o_ref, acc_ref = rest[-2:]
        k = pl.program_id(2)
        part = lax.dot_general(a_ref[...].astype(BF16), b_ref[...].astype(BF16), dims, preferred_element_type=F32)

        @pl.when(k == 0)
        def _():
            acc_ref[...] = part

        @pl.when(k > 0)
        def _():
            acc_ref[...] += part

        @pl.when(k == nk - 1)
        def _():
            o_ref[...] = acc_ref[...].astype(o_ref.dtype)

    acc_shape = tuple(d for d in o_spec.block_shape if d is not None)
    extra = [] if after is None else [after]
    return pl.pallas_call(
        body, name=name, grid=grid, in_specs=[a_spec, b_spec] + [pl.BlockSpec(memory_space=pl.ANY)] * len(extra), out_specs=o_spec,
        out_shape=out_shape, scratch_shapes=[pltpu.VMEM(acc_shape, F32)],
        compiler_params=_cparams(("parallel", "parallel", "arbitrary")),
    )(a, b, *extra)


NN = (((1,), (0,)), ((), ()))
NT = (((1,), (1,)), ((), ()))
TN = (((0,), (0,)), ((), ()))


def _mm_nn(name, a, b, out_dtype=ACT_DTYPE):
    (m, k), (_, n) = a.shape, b.shape
    bm, bn, bk = _tile(m, 1024), _tile(n, 512), _tile(k, 2048)
    return _matmul(name, a, b, pl.BlockSpec((bm, bk), lambda i, j, kk: (i, kk)), pl.BlockSpec((bk, bn), lambda i, j, kk: (kk, j)),
                   pl.BlockSpec((bm, bn), lambda i, j, kk: (i, j)), jax.ShapeDtypeStruct((m, n), out_dtype), (m // bm, n // bn, k // bk), NN, k // bk)


def _mm_nt(name, a, b, after=None, out_dtype=ACT_DTYPE):
    (m, c), (n, _) = a.shape, b.shape
    bm, bn, bk = _tile(m, 1024), _tile(n, 512), _tile(c, 2048)
    return _matmul(name, a, b, pl.BlockSpec((bm, bk), lambda i, j, kk: (i, kk)), pl.BlockSpec((bn, bk), lambda i, j, kk: (j, kk)),
                   pl.BlockSpec((bm, bn), lambda i, j, kk: (i, j)), jax.ShapeDtypeStruct((m, n), out_dtype), (m // bm, n // bn, c // bk), NT, c // bk, after)


def _mm_tn(name, a, b, out_dtype=F32, after=None):
    (c, m), (_, n) = a.shape, b.shape
    bm, bn, bk = _tile(m, 1024), _tile(n, 512), _tile(c, 2048)
    return _matmul(name, a, b, pl.BlockSpec((bk, bm), lambda i, j, kk: (kk, i)), pl.BlockSpec((bk, bn), lambda i, j, kk: (kk, j)),
                   pl.BlockSpec((bm, bn), lambda i, j, kk: (i, j)), jax.ShapeDtypeStruct((m, n), out_dtype), (m // bm, n // bn, c // bk), TN, c // bk, after)


def _mm_nn_cb(name, a, bb):
    (m, k), (nb, _, ns) = a.shape, bb.shape
    bm, bk = _tile(m, 1024), _tile(k, 2048)
    return _matmul(name, a, bb, pl.BlockSpec((bm, bk), lambda i, j, kk: (i, kk)), pl.BlockSpec((None, bk, ns), lambda i, j, kk: (j, kk, 0)),
                   pl.BlockSpec((bm, ns), lambda i, j, kk: (i, j)), jax.ShapeDtypeStruct((m, nb * ns), ACT_DTYPE), (m // bm, nb, k // bk), NN, k // bk)


def _mm_nt_cb(name, dy, bb, after=None):
    m, (nb, k, ns) = dy.shape[0], bb.shape
    bm, bn = _tile(m, 1024), _tile(k, 512)

    def body(dy_ref, b_ref, *rest):
        o_ref = rest[-1]
        acc = lax.dot_general(dy_ref[:, 0:ns], b_ref[0], NT, preferred_element_type=F32)
        for j in range(1, nb):
            acc += lax.dot_general(dy_ref[:, j * ns:(j + 1) * ns], b_ref[j], NT, preferred_element_type=F32)
        o_ref[...] = acc.astype(o_ref.dtype)

    extra = [] if after is None else [after]
    return pl.pallas_call(
        body, name=name, grid=(m // bm, k // bn),
        in_specs=[pl.BlockSpec((bm, nb * ns), lambda i, j: (i, 0)), pl.BlockSpec((nb, bn, ns), lambda i, j: (0, j, 0))]
        + [pl.BlockSpec(memory_space=pl.ANY)] * len(extra),
        out_specs=pl.BlockSpec((bm, bn), lambda i, j: (i, j)), out_shape=jax.ShapeDtypeStruct((m, k), ACT_DTYPE),
        compiler_params=_cparams(("parallel", "parallel")),
    )(dy, bb, *extra)


def _mm_tn_cb(name, a, dy, ns, out_dtype=F32, after=None):
    (c, k), nb = a.shape, dy.shape[1] // ns
    bm = _tile(k, 1024)
    return _matmul(name, a, dy, pl.BlockSpec((c, bm), lambda i, j, kk: (0, i)), pl.BlockSpec((c, ns), lambda i, j, kk: (0, j)),
                   pl.BlockSpec((None, bm, ns), lambda i, j, kk: (j, i, 0)), jax.ShapeDtypeStruct((nb, k, ns), out_dtype), (k // bm, nb, 1), TN, 1, after)


def make_mm(name):
    @jax.custom_vjp
    def mm(a, b):
        return _mm_nn(name, a, b, F32)

    def fwd(a, b):
        return mm(a, b), (a, b)

    def bwd(res, dy):
        a, b = res
        return _mm_nt(name + "_da", dy, b, out_dtype=a.dtype), _mm_tn(name + "_dw", a, dy, b.dtype)

    mm.defvjp(fwd, bwd)
    return mm


def _rowop_specs(rows, params, bm, nl):
    row_specs = [pl.BlockSpec((bm, r.shape[1] // nl), lambda j, i: (i, j)) for r in rows]
    par_specs = [pl.BlockSpec((p.shape[0], p.shape[1] // nl), lambda j, i: (0, j)) for p in params]
    row_blk = [jax.ShapeDtypeStruct((bm, r.shape[1] // nl), r.dtype) for r in rows]
    par_blk = [jax.ShapeDtypeStruct((p.shape[0], p.shape[1] // nl), p.dtype) for p in params]
    return row_specs, par_specs, row_blk, par_blk


def make_rowop(f, name, bm, nl=1, out_dtypes=None):
    def loaded(refs):
        return [r[...].astype(F32) for r in refs]

    def fwd_call(rows, params):
        n_rows = rows[0].shape[0]
        b = min(bm, n_rows)
        row_specs, par_specs, row_blk, par_blk = _rowop_specs(rows, params, b, nl)
        out_blk = jax.eval_shape(f, *[jax.ShapeDtypeStruct(t.shape, F32) for t in row_blk + par_blk])
        dts = out_dtypes or [F32] * len(out_blk)
        nr, npar = len(rows), len(params)

        def body(*refs):
            for o_ref, o in zip(refs[nr + npar:], f(*loaded(refs[:nr + npar]))):
                o_ref[...] = o.astype(o_ref.dtype)

        return pl.pallas_call(
            body, name=name, grid=(nl, n_rows // b), in_specs=row_specs + par_specs,
            out_specs=[pl.BlockSpec(o.shape, lambda j, i: (i, j)) for o in out_blk],
            out_shape=[jax.ShapeDtypeStruct((n_rows, o.shape[1] * nl), dt) for o, dt in zip(out_blk, dts)],
            compiler_params=_cparams(("parallel", "parallel")),
        )(*rows, *params)

    def bwd_call(rows, params, cts):
        n_rows = rows[0].shape[0]
        b = min(bm, n_rows)
        row_specs, par_specs, row_blk, par_blk = _rowop_specs(rows, params, b, nl)
        ct_specs = [pl.BlockSpec((b, c.shape[1] // nl), lambda j, i: (i, j)) for c in cts]
        nr, npar, nct = len(rows), len(params), len(cts)

        def body(*refs):
            i = pl.program_id(1)
            _, vjp = jax.vjp(lambda *v: tuple(f(*v)), *loaded(refs[:nr + npar]))
            grads = vjp(tuple(loaded(refs[nr + npar:nr + npar + nct])))
            outs = refs[nr + npar + nct:]
            for o_ref, g in zip(outs[:nr], grads[:nr]):
                o_ref[...] = g.astype(o_ref.dtype)
            for o_ref, g in zip(outs[nr:], grads[nr:]):
                @pl.when(i == 0)
                def _(o_ref=o_ref, g=g):
                    o_ref[...] = g

                @pl.when(i > 0)
                def _(o_ref=o_ref, g=g):
                    o_ref[...] += g

        outs = pl.pallas_call(
            body, name=name + "_bwd", grid=(nl, n_rows // b), in_specs=row_specs + par_specs + ct_specs,
            out_specs=row_specs + par_specs,
            out_shape=[jax.ShapeDtypeStruct(a.shape, a.dtype) for a in rows] + [jax.ShapeDtypeStruct(a.shape, F32) for a in params],
            compiler_params=_cparams(("arbitrary", "arbitrary")),
        )(*rows, *params, *cts)
        return tuple(outs[:nr]), tuple(g.astype(a.dtype) for g, a in zip(outs[nr:], params))

    @jax.custom_vjp
    def op(rows, params):
        return tuple(fwd_call(rows, params))

    def fwd(rows, params):
        return op(rows, params), (rows, params)

    def bwd(res, cts):
        rows, params = res
        return bwd_call(rows, params, tuple(cts))

    op.defvjp(fwd, bwd)
    return op


def make_groupop(f, name, gb):
    def specs(arrs):
        return [pl.BlockSpec((gb,) + a.shape[1:], lambda g: (g, 0, 0)) for a in arrs]

    def fwd_call(arrs):
        g_n = arrs[0].shape[0]
        out_blk = jax.eval_shape(f, *[jax.ShapeDtypeStruct((gb,) + a.shape[1:], a.dtype) for a in arrs])[0]
        n = len(arrs)

        def body(*refs):
            for i, o in enumerate(f(*[r[...] for r in refs[:n]])):
                refs[n][i] = o.astype(refs[n].dtype)

        return pl.pallas_call(
            body, name=name, grid=(g_n // gb,), in_specs=specs(arrs),
            out_specs=pl.BlockSpec((gb,) + out_blk.shape, lambda g: (g, 0, 0)),
            out_shape=jax.ShapeDtypeStruct((g_n,) + out_blk.shape, ACT_DTYPE),
            compiler_params=_cparams(("parallel",)),
        )(*arrs)

    def bwd_call(arrs, ct):
        g_n = arrs[0].shape[0]
        n = len(arrs)

        def body(*refs):
            _, vjp = jax.vjp(lambda *v: tuple(f(*v)), *[r[...] for r in refs[:n]])
            for o_ref, g in zip(refs[n + 1:], vjp(tuple(refs[n][i].astype(F32) for i in range(gb)))):
                o_ref[...] = g.astype(o_ref.dtype)

        return pl.pallas_call(
            body, name=name + "_bwd", grid=(g_n // gb,), in_specs=specs(arrs) + specs([ct]), out_specs=specs(arrs),
            out_shape=[jax.ShapeDtypeStruct(a.shape, a.dtype) for a in arrs],
            compiler_params=_cparams(("parallel",)),
        )(*arrs, ct)

    @jax.custom_vjp
    def op(*arrs):
        return fwd_call(arrs)

    def fwd(*arrs):
        return op(*arrs), arrs

    def bwd(arrs, ct):
        return tuple(bwd_call(arrs, ct))

    op.defvjp(fwd, bwd)
    return op


def _rms(x, g):
    return x * lax.rsqrt(jnp.mean(x * x, axis=-1, keepdims=True) + EPS) * g


def _silu(z):
    return z * jax.nn.sigmoid(z)


def _log_sigmoid(x):
    return jnp.minimum(x, 0.0) - jnp.log(1.0 + jnp.exp(-jnp.abs(x)))


def f_norm(x, g):
    return (_rms(x, g),)


def f_gate_a1(y, u, d):
    return (jax.nn.gelu(y + d * u),)


def f_gate_a2(yg, t, z, b):
    return (yg * jax.nn.sigmoid(t + b) * _silu(z),)


def f_gate_b(att, z):
    return (att * _silu(z),)


def f_post_a(h, o, post_g, kv_g, pre_g):
    h1 = h + _rms(o, post_g)
    return h1, _rms(h1, kv_g), _rms(h1, pre_g)


def f_final(h, o, tgt, post_g):
    err = h + _rms(o, post_g) - tgt
    return (0.5 * jnp.mean(err * err, axis=-1, keepdims=True),)


def f_logf(gl, b):
    return (_log_sigmoid(gl + b),)


def _mxu(a, b, dims):
    return lax.dot_general(a.astype(BF16), b.astype(BF16), dims, preferred_element_type=F32)


@jax.custom_vjp
def _mxu_nn(a, b):
    return _mxu(a, b, NN)


def _mxu_nn_bwd(res, g):
    a, b = res
    return _mxu(g, b, NT).astype(a.dtype), _mxu(a, g, TN).astype(b.dtype)


_mxu_nn.defvjp(lambda a, b: (_mxu(a, b, NN), (a, b)), _mxu_nn_bwd)


@jax.custom_vjp
def _mxu_nt(a, b):
    return _mxu(a, b, NT)


def _mxu_nt_bwd(res, g):
    a, b = res
    return _mxu(g, b, NN).astype(a.dtype), _mxu(g, a, TN).astype(b.dtype)


_mxu_nt.defvjp(lambda a, b: (_mxu(a, b, NT), (a, b)), _mxu_nt_bwd)


def f_memattn(q, zm, km, vm):
    o = _mxu_nn(_softmax(_mxu_nt(q * (HEAD_DIM ** -0.5), km)), vm)
    return (o * _silu(zm),)


@jax.custom_vjp
def _softmax(s):
    e = jnp.exp(s - jnp.max(s, axis=-1, keepdims=True))
    return e * (1.0 / jnp.sum(e, axis=-1, keepdims=True))


def _softmax_fwd(s):
    p = _softmax(s)
    return p, p


def _softmax_bwd(p, dp):
    return (p * (dp - jnp.sum(dp * p, axis=-1, keepdims=True)),)


_softmax.defvjp(_softmax_fwd, _softmax_bwd)


def _fox_probs(q_ref, k_ref, fr_ref, s_ref, p_ref, n_keys):
    bq = q_ref.shape[0]
    q = (q_ref[...].astype(F32) * (HEAD_DIM ** -0.5)).astype(BF16)
    s_ref[:, :n_keys] = lax.dot_general(q, k_ref[:n_keys, :], NT, preferred_element_type=F32)
    fr = fr_ref[:, :n_keys]
    ahead = (lax.broadcasted_iota(jnp.int32, (FOX_STRIP, n_keys), 1) - (n_keys - bq)
             - lax.broadcasted_iota(jnp.int32, (FOX_STRIP, n_keys), 0))

    def strip(r, carry):
        rows = pl.ds(pl.multiple_of(r * FOX_STRIP, FOX_STRIP), FOX_STRIP)
        s = jnp.where(ahead <= r * FOX_STRIP, s_ref[rows, :n_keys] - fr, -1e30)
        e = jnp.exp(s - jnp.max(s, axis=-1, keepdims=True))
        p_ref[rows, :n_keys] = (e * (1.0 / jnp.sum(e, axis=-1, keepdims=True))).astype(p_ref.dtype)
        return carry

    lax.fori_loop(0, bq // FOX_STRIP, strip, 0)
    return q


def _cmul(ar, ai, xr, xi):
    return ar * xr - ai * xi, ar * xi + ai * xr


def _hp_dot(a, b):
    return jnp.dot(a, b, precision=HP, preferred_element_type=F32)


_bf_dot = _mxu_nn


def f_s5(u, toep, win_r, win_i, wout_r, wout_i, coef):
    gb = u.shape[0]
    n_hi = u.shape[1] // SSM_LO
    sr = [_bf_dot(u[i], win_r[i]) for i in range(gb)]
    si = [_bf_dot(u[i], win_i[i]) for i in range(gb)]

    def stacked(parts, lo):
        return jnp.concatenate([a[lo * n_hi:(lo + 1) * n_hi] for a in parts], axis=0)

    def coef_rows(k):
        return jnp.concatenate([jnp.broadcast_to(coef[i, k:k + 1, :], (n_hi, coef.shape[2])) for i in range(gb)], axis=0)

    at_r, at_i = coef_rows(0), coef_rows(1)
    pr, pi = stacked(sr, 0), stacked(si, 0)
    for lo in range(1, SSM_LO):
        dr, di = _cmul(at_r, at_i, pr, pi)
        pr, pi = stacked(sr, lo) + dr, stacked(si, lo) + di
    n = 2 * gb * n_hi
    ri = lax.broadcasted_iota(jnp.int32, (n, n), 0)
    ci = lax.broadcasted_iota(jnp.int32, (n, n), 1)
    same = ri // n_hi == ci // n_hi

    def shifted(d, zr, zi):
        z = _hp_dot((same & (ri - ci == d)).astype(F32), jnp.concatenate([zr, zi], axis=0))
        return z[:n // 2], z[n // 2:]

    d, step = 1, 0
    while d < n_hi:
        dr, di = _cmul(coef_rows(2 + 2 * step), coef_rows(3 + 2 * step), *shifted(d, pr, pi))
        pr, pi = pr + dr, pi + di
        d, step = 2 * d, step + 1
    er, ei = shifted(1, pr, pi)
    xr, xi = [er], [ei]
    for lo in range(1, SSM_LO):
        dr, di = _cmul(at_r, at_i, xr[-1], xi[-1])
        xr.append(stacked(sr, lo - 1) + dr)
        xi.append(stacked(si, lo - 1) + di)

    def group_rows(parts, i):
        return jnp.concatenate([a[i * n_hi:(i + 1) * n_hi] for a in parts], axis=0)

    return tuple(_bf_dot(u[i], toep[i]) + _bf_dot(group_rows(xr, i), wout_r[i]) + _bf_dot(group_rows(xi, i), wout_i[i])
                 for i in range(gb))


def s5_operators(lam_re, lam_im, log_step, b_re, b_im, c_re, c_im, n_hi):
    t_n = SSM_T
    lr, li = lam_re, lam_im
    dt = jnp.exp(log_step)[:, None]
    mag = jnp.exp(lr * dt)
    ar, ai = mag * jnp.cos(li * dt), mag * jnp.sin(li * dt)
    den = lr * lr + li * li
    cr = ((ar - 1.0) * lr + ai * li) / den
    ci = (ai * lr - (ar - 1.0) * li) / den
    bbr = cr[..., None] * b_re - ci[..., None] * b_im
    bbi = cr[..., None] * b_im + ci[..., None] * b_re
    k = jnp.arange(t_n + 1, dtype=F32)[:, None, None]
    pm, ang = jnp.exp(k * (lr * dt)), k * (li * dt)
    pr, pi = pm * jnp.cos(ang), pm * jnp.sin(ang)
    abr = pr[..., None] * bbr - pi[..., None] * bbi
    abi = pr[..., None] * bbi + pi[..., None] * bbr
    kk = (jnp.einsum('ghp,kgpj->kghj', c_re, abr[:t_n], precision=HP)
          - jnp.einsum('ghp,kgpj->kghj', c_im, abi[:t_n], precision=HP))
    lag = jnp.arange(t_n)[None, :] - jnp.arange(t_n)[:, None]
    onehot = (lag[None] == jnp.arange(t_n)[:, None, None]).astype(F32)
    g_n, h_n = c_re.shape[0], c_re.shape[1]
    toep = jnp.einsum('kst,kghj->gsjth', onehot, kk, precision=HP).reshape(g_n, t_n * h_n, t_n * h_n)
    win_r = abr[:t_n][::-1].transpose(1, 0, 3, 2).reshape(g_n, t_n * h_n, -1)
    win_i = abi[:t_n][::-1].transpose(1, 0, 3, 2).reshape(g_n, t_n * h_n, -1)
    p1r, p1i = pr[1:, :, None, :], pi[1:, :, None, :]
    wout_r = (c_re[None] * p1r - c_im[None] * p1i).transpose(1, 3, 0, 2).reshape(g_n, -1, t_n * h_n)
    wout_i = (-(c_re[None] * p1i + c_im[None] * p1r)).transpose(1, 3, 0, 2).reshape(g_n, -1, t_n * h_n)
    rows = [pr[t_n], pi[t_n]]
    qr, qi = pr[t_n], pi[t_n]
    for _ in range(int(math.log2(SSM_LO))):
        qr, qi = qr * qr - qi * qi, 2.0 * qr * qi
    d = 1
    while d < n_hi:
        rows += [qr, qi]
        qr, qi = qr * qr - qi * qi, 2.0 * qr * qi
        d *= 2
    coef = jnp.stack(rows, axis=1)
    return toep, win_r, win_i, wout_r, wout_i, coef


def _cumsum_call(name, x, reverse):
    n_rows, w = x.shape
    bm = min(256, n_rows)
    nb = n_rows // bm

    def body(x_ref, o_ref, carry_ref):
        i = pl.program_id(0)

        @pl.when(i == 0)
        def _():
            carry_ref[...] = jnp.zeros_like(carry_ref)

        ri = lax.broadcasted_iota(jnp.int32, (bm, bm), 0)
        ci = lax.broadcasted_iota(jnp.int32, (bm, bm), 1)
        tri = ((ri <= ci) if reverse else (ri >= ci)).astype(F32)
        xb = x_ref[...]
        o_ref[...] = _hp_dot(tri, xb) + carry_ref[...]
        carry_ref[...] += jnp.sum(xb, axis=0, keepdims=True)

    idx = (lambda i: (nb - 1 - i, 0)) if reverse else (lambda i: (i, 0))
    return pl.pallas_call(
        body, name=name, grid=(nb,), in_specs=[pl.BlockSpec((bm, w), idx)], out_specs=pl.BlockSpec((bm, w), idx),
        out_shape=jax.ShapeDtypeStruct(x.shape, F32), scratch_shapes=[pltpu.VMEM((1, w), F32)],
        compiler_params=_cparams(("arbitrary",)),
    )(x)


def make_cumsum(name):
    @jax.custom_vjp
    def cs(x):
        return _cumsum_call(name, x, False)

    def fwd(x):
        return cs(x), None

    def bwd(_, dy):
        return (_cumsum_call(name + "_bwd", dy, True),)

    cs.defvjp(fwd, bwd)
    return cs


def _fox_specs(n_rows, bq):
    q_spec = pl.BlockSpec((bq, HEAD_DIM), lambda h, i: (i, h))
    kv_spec = pl.BlockSpec((n_rows, HEAD_DIM), lambda h, i: (0, h))
    fr_spec = pl.BlockSpec((None, 1, n_rows), lambda h, i: (h, 0, 0))
    return q_spec, kv_spec, fr_spec


def _fox_fwd_call(name, q, k, v, fr):
    n_rows, width = q.shape
    bq = min(FOX_BQ, n_rows)
    nq = n_rows // bq
    q_spec, kv_spec, fr_spec = _fox_specs(n_rows, bq)

    def body(q_ref, k_ref, v_ref, fr_ref, o_ref, s_ref, p_ref):
        i = pl.program_id(1)
        for p in range(nq):
            n_keys = (p + 1) * bq

            @pl.when(i == p)
            def _(n_keys=n_keys):
                _fox_probs(q_ref, k_ref, fr_ref, s_ref, p_ref, n_keys)
                o_ref[...] = jnp.dot(p_ref[:, :n_keys], v_ref[:n_keys, :], preferred_element_type=F32).astype(o_ref.dtype)

    return pl.pallas_call(
        body, name=name, grid=(width // HEAD_DIM, nq), in_specs=[q_spec, kv_spec, kv_spec, fr_spec], out_specs=q_spec,
        out_shape=jax.ShapeDtypeStruct(q.shape, ACT_DTYPE),
        scratch_shapes=[pltpu.VMEM((bq, n_rows), F32), pltpu.VMEM((bq, n_rows), BF16)],
        compiler_params=_cparams(("parallel", "parallel")),
    )(q, k, v, fr)


def _fox_bwd_call(name, q, k, v, fr, do):
    n_rows, width = q.shape
    bq = min(FOX_BQ, n_rows)
    nq = n_rows // bq
    q_spec, kv_spec, fr_spec = _fox_specs(n_rows, bq)

    def body(q_ref, k_ref, v_ref, fr_ref, do_ref, dq_ref, dk_ref, dv_ref, dfr_ref, s_ref, p_ref, dp_ref, ds_ref, dfr_acc, dk_acc, dv_acc):
        i = pl.program_id(1)

        @pl.when(i == 0)
        def _():
            dk_acc[...] = jnp.zeros_like(dk_acc)
            dv_acc[...] = jnp.zeros_like(dv_acc)
            dfr_ref[...] = jnp.zeros_like(dfr_ref)

        for p in range(nq):
            n_keys = (p + 1) * bq

            @pl.when(i == p)
            def _(n_keys=n_keys):
                q = _fox_probs(q_ref, k_ref, fr_ref, s_ref, p_ref, n_keys)
                do = do_ref[...]
                dp_ref[:, :n_keys] = lax.dot_general(do, v_ref[:n_keys, :], NT, preferred_element_type=F32)
                dfr_acc[:, :n_keys] = jnp.zeros((FOX_STRIP, n_keys), F32)

                def strip(r, carry):
                    rows = pl.ds(pl.multiple_of(r * FOX_STRIP, FOX_STRIP), FOX_STRIP)
                    pr, dp = p_ref[rows, :n_keys].astype(F32), dp_ref[rows, :n_keys]
                    mean = jnp.sum(dp * pr, axis=-1, keepdims=True) / jnp.sum(pr, axis=-1, keepdims=True)
                    ds = pr * (dp - mean)
                    ds_ref[rows, :n_keys] = ds.astype(ds_ref.dtype)
                    dfr_acc[:, :n_keys] += ds
                    return carry

                lax.fori_loop(0, bq // FOX_STRIP, strip, 0)
                ds = ds_ref[:, :n_keys]
                dq = jnp.dot(ds, k_ref[:n_keys, :], preferred_element_type=F32) * (HEAD_DIM ** -0.5)
                dq_ref[...] = dq.astype(dq_ref.dtype)
                dk_acc[:n_keys, :] += lax.dot_general(ds, q, TN, preferred_element_type=F32)
                dv_acc[:n_keys, :] += lax.dot_general(p_ref[:, :n_keys], do, TN, preferred_element_type=F32)
                dfr_ref[:, :n_keys] -= jnp.sum(dfr_acc[:, :n_keys], axis=0, keepdims=True)

        @pl.when(i == nq - 1)
        def _():
            dk_ref[...] = dk_acc[...].astype(dk_ref.dtype)
            dv_ref[...] = dv_acc[...].astype(dv_ref.dtype)

    return pl.pallas_call(
        body, name=name, grid=(width // HEAD_DIM, nq), in_specs=[q_spec, kv_spec, kv_spec, fr_spec, q_spec],
        out_specs=[q_spec, kv_spec, kv_spec, fr_spec], out_shape=[jax.ShapeDtypeStruct(a.shape, a.dtype) for a in (q, k, v, fr)],
        scratch_shapes=[pltpu.VMEM((bq, n_rows), F32), pltpu.VMEM((bq, n_rows), BF16), pltpu.VMEM((bq, n_rows), F32),
                        pltpu.VMEM((bq, n_rows), BF16), pltpu.VMEM((FOX_STRIP, n_rows), F32),
                        pltpu.VMEM((n_rows, HEAD_DIM), F32), pltpu.VMEM((n_rows, HEAD_DIM), F32)],
        compiler_params=_cparams(("parallel", "arbitrary")),
    )(q, k, v, fr, do)


def make_fox(name):
    @jax.custom_vjp
    def fox(q, k, v, fr):
        return _fox_fwd_call(name, q, k, v, fr)

    def fwd(q, k, v, fr):
        return fox(q, k, v, fr), (q, k, v, fr)

    def bwd(res, do):
        return tuple(_fox_bwd_call(name + "_bwd", *res, do))

    fox.defvjp(fwd, bwd)
    return fox


def _relations(x, y, c):
    out = []
    for rel in range(1, N_DEV):
        px = 1 - x if rel & 4 else x
        py = 1 - y if rel & 2 else y
        pc = 1 - c if rel & 1 else c
        out.append((rel, (px, py, pc), 4 * px + 2 * py + pc))
    return out


_HBM_SPEC = pl.BlockSpec(memory_space=pltpu.HBM)
_SEM_SPEC = pl.BlockSpec(memory_space=pltpu.SEMAPHORE)
_DATAFLOW = pltpu.SideEffectType.DATAFLOW_SIDE_EFFECTING


def exchange_start(name, srcs, gather):
    n = len(srcs)
    me = 4 * lax.axis_index("x") + 2 * lax.axis_index("y") + lax.axis_index("c")
    lands = []
    for s in srcs:
        own = s if gather else lax.dynamic_index_in_dim(s, me, 0, keepdims=False)
        land = lax.empty((N_DEV,) + own.shape, s.dtype)
        lands.append(lax.dynamic_update_index_in_dim(land, own, me, 0))

    def body(*refs):
        src_refs, land_refs = refs[:n], refs[n:2 * n]
        send_sems, recv_sems, token = refs[2 * n], refs[2 * n + 1], refs[-1]
        x, y, c = lax.axis_index("x"), lax.axis_index("y"), lax.axis_index("c")
        mine = 4 * x + 2 * y + c
        for k in range(n):
            for rel, peer, peer_blk in _relations(x, y, c):
                pltpu.make_async_remote_copy(
                    src_ref=src_refs[k] if gather else src_refs[k].at[peer_blk], dst_ref=land_refs[k].at[mine],
                    send_sem=send_sems.at[7 * k + rel - 1], recv_sem=recv_sems.at[7 * k + rel - 1],
                    device_id=peer, device_id_type=pl.DeviceIdType.MESH).start()
        token[...] = jnp.zeros_like(token)

    sem = pltpu.SemaphoreType.DMA((7 * n,))
    hbm = lambda t: pltpu.HBM(t.shape, t.dtype)
    outs = pl.pallas_call(
        body, name=name,
        out_shape=(sem, sem, *[hbm(s) for s in srcs], *[hbm(l) for l in lands], jax.ShapeDtypeStruct((8, LANE), F32)),
        in_specs=[_HBM_SPEC] * (2 * n),
        out_specs=(_SEM_SPEC, _SEM_SPEC, *[_HBM_SPEC] * (2 * n), pl.BlockSpec(memory_space=pltpu.VMEM)),
        input_output_aliases={i: 2 + i for i in range(2 * n)},
        compiler_params=pltpu.CompilerParams(has_side_effects=_DATAFLOW),
    )(*[pltpu.with_memory_space_constraint(t, pltpu.HBM) for t in list(srcs) + lands])
    return dict(name=name, gather=gather, send=outs[0], recv=outs[1], srcs=list(outs[2:2 + n]), lands=list(outs[2 + n:2 + 2 * n]), token=outs[-1])


def exchange_wait(handle, k, after):
    gather = handle['gather']

    def body(src_ref, land_ref, send_sems, recv_sems, after_ref, src_out, land_out):
        x, y, c = lax.axis_index("x"), lax.axis_index("y"), lax.axis_index("c")
        for rel, peer, peer_blk in _relations(x, y, c):
            copy = pltpu.make_async_remote_copy(
                src_ref=src_ref if gather else src_ref.at[peer_blk], dst_ref=land_ref.at[peer_blk],
                send_sem=send_sems.at[7 * k + rel - 1], recv_sem=recv_sems.at[7 * k + rel - 1],
                device_id=peer, device_id_type=pl.DeviceIdType.MESH)
            copy.wait_send()
            copy.wait_recv()

    src, land = handle['srcs'][k], handle['lands'][k]
    return pl.pallas_call(
        body, name=f"{handle['name']}_wait{k}", out_shape=(pltpu.HBM(src.shape, src.dtype), pltpu.HBM(land.shape, land.dtype)),
        in_specs=[_HBM_SPEC, _HBM_SPEC, _SEM_SPEC, _SEM_SPEC, pl.BlockSpec(memory_space=pl.ANY)], out_specs=(_HBM_SPEC, _HBM_SPEC),
        input_output_aliases={0: 0, 1: 1}, compiler_params=pltpu.CompilerParams(has_side_effects=_DATAFLOW),
    )(src, land, handle['send'], handle['recv'], after)[1]


def adamw_reduce(name, partials, w, m, v):
    n_part, n_rows, n_cols = partials.shape
    br = n_rows
    for cand in (512, 256, 128, 64, 32, 16, 8):
        if n_rows % cand == 0 and n_part * cand * n_cols * partials.dtype.itemsize <= (4 << 20):
            br = cand
            break

    def body(p_ref, w_ref, m_ref, v_ref, g_ref, d_ref, nm_ref, nv_ref):
        g = p_ref[0].astype(F32)
        for s in range(1, n_part):
            g = g + p_ref[s].astype(F32)
        m_new = ADAM_B1 * m_ref[...] + (1.0 - ADAM_B1) * g
        v_new = ADAM_B2 * v_ref[...] + (1.0 - ADAM_B2) * jnp.square(g)
        m_hat = m_new / (1.0 - ADAM_B1 ** ADAM_STEP)
        v_hat = v_new / (1.0 - ADAM_B2 ** ADAM_STEP)
        g_ref[...] = g
        d_ref[...] = -ADAM_LR * (m_hat / (jnp.sqrt(v_hat) + ADAM_EPS) + ADAM_WD * w_ref[...])
        nm_ref[...] = m_new
        nv_ref[...] = v_new

    spec = pl.BlockSpec((br, n_cols), lambda i: (i, 0))
    return pl.pallas_call(
        body, name=name, grid=(n_rows // br,), in_specs=[pl.BlockSpec((n_part, br, n_cols), lambda i: (0, i, 0)), spec, spec, spec],
        out_specs=[spec] * 4, out_shape=[jax.ShapeDtypeStruct((n_rows, n_cols), F32)] * 4, compiler_params=_cparams(("parallel",)),
    )(partials, w, m, v)


def _pack(arrays, n_rows):
    flat = jnp.concatenate([a.reshape(-1) for a in arrays])
    return jnp.pad(flat, (0, n_rows * LANE - flat.shape[0])).reshape(n_rows, LANE)


def _unpack(packed, shapes):
    flat, out, off = packed.reshape(-1), [], 0
    for s in shapes:
        n = math.prod(s)
        out.append(flat[off:off + n].reshape(s))
        off += n
    return out


def _packed_rows(shapes):
    n = sum(math.prod(s) for s in shapes)
    return -(-n // (LANE * 512)) * 512


GATHER_ORDER = ['small', 'in_a', 'mem_kv0', 'glu', 'out0', 'kv', 'in_b', 'mem_kv1', 'out1']


BLOCKED = ('in_a', 'kv', 'in_b')


def local_step(p, gathered, x, mem, tgt, start_early, start_late):
    n_rows, d_model = x.shape
    main_w = 3 * d_model // 4
    mem_w = d_model - main_w
    n_groups = main_w // SSM_GROUP
    n_heads = main_w // HEAD_DIM
    n_hi = n_rows // (SSM_T * SSM_LO)
    row = lambda a: a.reshape(1, -1)
    tape, pending = {}, {}

    def mm_fwd(name, act, after=None):
        w = exchange_wait(gathered, GATHER_ORDER.index(name), act if after is None else after)
        if name not in BLOCKED:
            w = w.reshape(-1, w.shape[2])
        tape[name] = (act, w)
        return (_mm_nn_cb if name in BLOCKED else _mm_nn)(name, act, w)

    def mm_dw(name, dy, after=None):
        act, w = tape[name]
        if name in BLOCKED:
            dw = _mm_tn_cb(name + "_dw", act, dy, w.shape[2], GRAD_DTYPE, after)
        else:
            dw = _mm_tn(name + "_dw", act, dy, GRAD_DTYPE, after)
            dw = dw.reshape(N_DEV, dw.shape[0] // N_DEV, dw.shape[1])
        pending[name] = exchange_start("rs_" + name, [dw], gather=False)

    def mm_da(name, dy, after=None):
        return (_mm_nt_cb if name in BLOCKED else _mm_nt)(name + "_da", dy, tape[name][1], after)

    def mm_bwd(name, dy):
        mm_dw(name, dy)
        return mm_da(name, dy, pending[name]['token'])

    def split(proj):
        return proj[:, :main_w], proj[:, main_w:2 * main_w], proj[:, 2 * main_w:2 * main_w + mem_w], proj[:, 2 * main_w + mem_w:]

    def mem_attn(i, qm, zm, kvm):
        memo, = make_rowop(f_memattn, f"mem_attn{i}", 512, mem_w // HEAD_DIM, [ACT_DTYPE])((qm, zm), (kvm[:, :mem_w], kvm[:, mem_w:]))
        return memo

    def seg_norms(x_, pre_g0, mem_g0, mem_g1):
        hn, = make_rowop(f_norm, "pre_norm0", 256, out_dtypes=[ACT_DTYPE])((x_,), (row(pre_g0),))
        memn0, = make_rowop(f_norm, "mem_norm0", 256, out_dtypes=[ACT_DTYPE])((mem,), (row(mem_g0),))
        memn1, = make_rowop(f_norm, "mem_norm1", 256, out_dtypes=[ACT_DTYPE])((mem,), (row(mem_g1),))
        return hn, memn0, memn1

    def seg_a1(proj, kvm, d_skip, *ops):
        u, z, qm, zm = split(proj)
        ug = u.astype(BF16).reshape(n_hi, SSM_LO, SSM_T, n_groups, SSM_GROUP).transpose(3, 1, 0, 2, 4).reshape(n_groups, n_hi * SSM_LO, SSM_T * SSM_GROUP)
        yg = make_groupop(f_s5, "s5", SSM_GB)(ug, *ops)
        y = yg.reshape(n_groups, SSM_LO, n_hi, SSM_T, SSM_GROUP).transpose(2, 1, 3, 0, 4).reshape(n_rows, main_w)
        ygelu, = make_rowop(f_gate_a1, "gate_a1", 256, out_dtypes=[ACT_DTYPE])((y, u), (row(d_skip),))
        return ygelu, z, mem_attn(0, qm, zm, kvm)

    def seg_a2(ygelu, t, z, memo, b_glu):
        main, = make_rowop(f_gate_a2, "gate_a2", 256, out_dtypes=[ACT_DTYPE])((ygelu, t, z), (row(b_glu),))
        return jnp.concatenate([main, memo], axis=1)

    def seg_post_a(x_, o, post_g0, kv_g, pre_g1):
        return make_rowop(f_post_a, "post_a", 128, out_dtypes=[F32, ACT_DTYPE, ACT_DTYPE])((x_, o), (row(post_g0), row(kv_g), row(pre_g1)))

    def seg_b(proj, kv, kv_in, kvm, w_fgate, b_fgate):
        w_fg = jnp.pad(w_fgate, ((0, 0), (0, LANE - n_heads)))
        b_fg = jnp.pad(b_fgate, (0, LANE - n_heads)).reshape(1, LANE)
        logf, = make_rowop(f_logf, "logf", 512)((make_mm("fgate")(kv_in, w_fg),), (b_fg,))
        fcum = make_cumsum("fcum")(logf)[:, :n_heads].T
        q, z, qm, zm = split(proj)
        att = make_fox("fox")(q, kv[:, :main_w], kv[:, main_w:], fcum[:, None, :])
        main, = make_rowop(f_gate_b, "gate_b", 256, out_dtypes=[ACT_DTYPE])((att, z), ())
        return jnp.concatenate([main, mem_attn(1, qm, zm, kvm)], axis=1)

    def seg_final(h1, o, post_g1):
        rowloss, = make_rowop(f_final, "final", 128)((h1, o, tgt), (row(post_g1),))
        return jnp.sum(rowloss)

    (hn0, memn0, memn1), vjp_norms = jax.vjp(seg_norms, x, p['pre_norm_g'][0], p['mem_norm_g'][0], p['mem_norm_g'][1])
    s5_names = ('lam_re', 'lam_im', 'log_step', 'b_re', 'b_im', 'c_re', 'c_im')
    ops, vjp_ops = jax.vjp(lambda *a: s5_operators(*a, n_hi), *[p[n] for n in s5_names])
    proj_a = mm_fwd("in_a", hn0, ops[0])
    kvm0 = mm_fwd("mem_kv0", memn0, proj_a)
    (ygelu, z_a, memo0), vjp_a1 = jax.vjp(seg_a1, proj_a, kvm0, p['d_skip'], *ops)
    t = mm_fwd("glu", ygelu)
    cat0, vjp_a2 = jax.vjp(seg_a2, ygelu, t, z_a, memo0, p['b_glu'])
    o0 = mm_fwd("out0", cat0)
    (h1, kv_in, hn1), vjp_post_a = jax.vjp(seg_post_a, x, o0, p['post_norm_g'][0], p['kv_norm_g'], p['pre_norm_g'][1])
    kv = mm_fwd("kv", kv_in)
    proj_b = mm_fwd("in_b", hn1)
    kvm1 = mm_fwd("mem_kv1", memn1, proj_b)
    cat1, vjp_b = jax.vjp(seg_b, proj_b, kv, kv_in, kvm1, p['w_fgate'], p['b_fgate'])
    o1 = mm_fwd("out1", cat1)
    loss, vjp_final = jax.vjp(seg_final, h1, o1, p['post_norm_g'][1])

    g = {}
    d_h1, d_o1, g_post_g1 = vjp_final(jnp.ones((), F32))
    d_proj_b, d_kv, d_kv_in, d_kvm1, g['w_fgate'], g['b_fgate'] = vjp_b(mm_bwd("out1", d_o1))
    d_memn1 = mm_bwd("mem_kv1", d_kvm1)
    d_hn1 = mm_bwd("in_b", d_proj_b)
    d_kv_in = d_kv_in + mm_bwd("kv", d_kv)
    d_x, d_o0, g_post_g0, g['kv_norm_g'], g_pre_g1 = vjp_post_a((d_h1, d_kv_in, d_hn1))
    d_ygelu, d_t, d_z, d_memo0, g['b_glu'] = vjp_a2(mm_bwd("out0", d_o0))
    d_ygelu = d_ygelu + mm_bwd("glu", d_t)
    d_proj_a, d_kvm0, g['d_skip'], *d_ops = vjp_a1((d_ygelu, d_z, d_memo0))
    g.update(zip(s5_names, vjp_ops(tuple(d_ops))))
    g['pre_norm_g'] = jnp.stack([jnp.zeros_like(g_pre_g1), g_pre_g1])
    g['post_norm_g'] = jnp.stack([g_post_g0, g_post_g1])
    g['mem_norm_g'] = jnp.zeros_like(p['mem_norm_g'])
    early = start_early(g)
    d_memn0 = mm_bwd("mem_kv0", d_kvm0)
    d_x2, *g_late = vjp_norms((mm_da("in_a", d_proj_a, early['token']), d_memn0, d_memn1))
    late = start_late(g_late)
    mm_dw("in_a", d_proj_a, late['token'])
    return loss, d_x + d_x2, pending, early, late


def kernel(x, mem, pre_norm_g, post_norm_g, w_in_a, lam_re, lam_im, log_step, b_re, b_im, c_re, c_im, d_skip, w_glu, b_glu, kv_norm_g, w_kv, w_fgate, b_fgate, w_in_b, mem_norm_g, w_mem_kv, w_out, loss_target, m_pre_norm_g, m_post_norm_g, m_w_in_a, m_lam_re, m_lam_im, m_log_step, m_b_re, m_b_im, m_c_re, m_c_im, m_d_skip, m_w_glu, m_b_glu, m_kv_norm_g, m_w_kv, m_w_fgate, m_b_fgate, m_w_in_b, m_mem_norm_g, m_w_mem_kv, m_w_out, v_pre_norm_g, v_post_norm_g, v_w_in_a, v_lam_re, v_lam_im, v_log_step, v_b_re, v_b_im, v_c_re, v_c_im, v_d_skip, v_w_glu, v_b_glu, v_kv_norm_g, v_w_kv, v_w_fgate, v_b_fgate, v_w_in_b, v_mem_norm_g, v_w_mem_kv, v_w_out):
    a = dict(zip(INPUTS, (x, mem, pre_norm_g, post_norm_g, w_in_a, lam_re, lam_im, log_step, b_re, b_im, c_re, c_im, d_skip, w_glu, b_glu, kv_norm_g, w_kv, w_fgate, b_fgate, w_in_b, mem_norm_g, w_mem_kv, w_out, loss_target, m_pre_norm_g, m_post_norm_g, m_w_in_a, m_lam_re, m_lam_im, m_log_step, m_b_re, m_b_im, m_c_re, m_c_im, m_d_skip, m_w_glu, m_b_glu, m_kv_norm_g, m_w_kv, m_w_fgate, m_b_fgate, m_w_in_b, m_mem_norm_g, m_w_mem_kv, m_w_out, v_pre_norm_g, v_post_norm_g, v_w_in_a, v_lam_re, v_lam_im, v_log_step, v_b_re, v_b_im, v_c_re, v_c_im, v_d_skip, v_w_glu, v_b_glu, v_kv_norm_g, v_w_kv, v_w_fgate, v_b_fgate, v_w_in_b, v_mem_norm_g, v_w_mem_kv, v_w_out)))
    me = 4 * lax.axis_index("x") + 2 * lax.axis_index("y") + lax.axis_index("c")
    n_layers = w_out.shape[0]

    small_shapes = [a[n].shape for n in SMALL_SHARDED]
    small_rows = -(-sum(math.prod(s) for s in small_shapes) // (LANE * 8)) * 8
    shards = dict(in_a=w_in_a[0], mem_kv0=w_mem_kv[0], glu=w_glu[0], out0=w_out[0], kv=w_kv, in_b=w_in_b[0], mem_kv1=w_mem_kv[1], out1=w_out[1])
    operands = [_pack([a[n] for n in SMALL_SHARDED], small_rows)] + [shards[n].astype(BF16) for n in GATHER_ORDER[1:]]
    gathered = exchange_start("ag_weights", operands, gather=True)
    small = exchange_wait(gathered, 0, gathered['token'])
    small = [jnp.stack(parts) for parts in zip(*[_unpack(small[b], small_shapes) for b in range(N_DEV)])]
    p = {n: a[n] for n in REPLICATED}
    for n in ('lam_re', 'lam_im', 'log_step', 'b_re', 'b_im', 'c_re', 'c_im'):
        p[n] = p[n][0]
    p['d_skip'] = small[0].reshape(-1)
    p['b_glu'] = small[1].reshape(-1)
    p['w_fgate'] = small[2].reshape(-1, w_fgate.shape[1])
    p['pre_norm_g'] = p['pre_norm_g'] + gathered['token'][0, 0]

    full_shapes = [a[n].shape for n in REPLICATED] + [(1, N_DEV * d_skip.shape[1]), (1, N_DEV * b_glu.shape[1]), (N_DEV * w_fgate.shape[0], w_fgate.shape[1])]
    rows = _packed_rows(full_shapes)
    late_rows = 3 * pre_norm_g.shape[1] // LANE
    start_early = lambda g: exchange_start("ag_grads", [_pack([g[n] for n in REPLICATED + SMALL_SHARDED], rows)], gather=True)
    start_late = lambda g_late: exchange_start("ag_grads_late", [_pack(g_late, late_rows)], gather=True)

    loss_local, grad_x, pending, early, late = local_step(p, gathered, x[0], mem[0], loss_target[0], start_early, start_late)
    loss = lax.psum(loss_local, MESH_AXES)

    out = {}

    def update(tag, name, layer=None):
        pick = (lambda t: t) if layer is None else (lambda t: t[layer])
        parts = exchange_wait(pending[tag], 0, grad_x)
        two_d = lambda t: pick(t).reshape(parts.shape[1:])
        res = adamw_reduce("adamw_" + tag, parts, two_d(a[name]), two_d(a['m_' + name]), two_d(a['v_' + name]))
        return [r.reshape(pick(a[name]).shape) for r in res]

    per_layer = {n: [None] * n_layers for n in ('w_out', 'w_mem_kv')}
    per_layer['w_out'][1] = update("out1", 'w_out', 1)
    per_layer['w_mem_kv'][1] = update("mem_kv1", 'w_mem_kv', 1)
    out['w_in_b'] = [r[None] for r in update("in_b", 'w_in_b', 0)]
    out['w_kv'] = update("kv", 'w_kv')
    per_layer['w_out'][0] = update("out0", 'w_out', 0)
    per_layer['w_mem_kv'][0] = update("mem_kv0", 'w_mem_kv', 0)
    out['w_glu'] = [r[None] for r in update("glu", 'w_glu', 0)]
    for name in per_layer:
        out[name] = [jnp.stack(t) for t in zip(*per_layer[name])]

    zeros = [jnp.zeros(s, F32) for s in full_shapes[len(REPLICATED):]]
    packed = lambda pre: _pack([a[pre + n] for n in REPLICATED] + zeros, rows)
    res = [_unpack(r, full_shapes) for r in adamw_reduce("adamw_small", exchange_wait(early, 0, grad_x), packed(''), packed('m_'), packed('v_'))]
    for i, n in enumerate(REPLICATED):
        out[n] = [r[i] for r in res]
    g_full = res[0][len(REPLICATED):]
    g_shard = [lax.dynamic_slice_in_dim(g_full[0], me * d_skip.shape[1], d_skip.shape[1], 1),
               lax.dynamic_slice_in_dim(g_full[1], me * b_glu.shape[1], b_glu.shape[1], 1),
               lax.dynamic_slice_in_dim(g_full[2], me * w_fgate.shape[0], w_fgate.shape[0], 0)]
    packed = lambda pre: _pack([a[pre + n] for n in SMALL_SHARDED], small_rows)
    res = [_unpack(r, small_shapes) for r in adamw_reduce("adamw_small_sharded", _pack(g_shard, small_rows)[None], packed(''), packed('m_'), packed('v_'))]
    for i, n in enumerate(SMALL_SHARDED):
        out[n] = [r[i] for r in res]
    packed = lambda pre: _pack([a[pre + 'pre_norm_g'][0], a[pre + 'mem_norm_g'][0], a[pre + 'mem_norm_g'][1]], late_rows)
    res = [_unpack(r, [pre_norm_g.shape[1:]] * 3) for r in adamw_reduce("adamw_small_late", exchange_wait(late, 0, grad_x), packed(''), packed('m_'), packed('v_'))]
    out['pre_norm_g'] = [jnp.stack([r[0], o[1]]) for r, o in zip(res, out['pre_norm_g'])]
    out['mem_norm_g'] = [jnp.stack([r[1], r[2]]) for r in res]
    out['w_in_a'] = [r[None] for r in update("in_a", 'w_in_a', 0)]

    return (loss, grad_x[None], *[out[n][k] for k in range(4) for n in WEIGHTS])
```

```python
import functools
import math

import jax
import jax.numpy as jnp
from jax import lax
from jax.experimental import pallas as pl
from jax.experimental.pallas import tpu as pltpu

F32 = jnp.float32
BF16 = jnp.bfloat16
HP = lax.Precision.HIGHEST
MESH_AXES = ("x", "y", "c")
N_DEV = 8
V7X_VMEM_LIMIT = 56 * 1024 * 1024
LANE = 128

EPS = 1e-6
HEAD_DIM = 128
SSM_GROUP = 16
SSM_STATE = 64
SSM_T = 8
SSM_LO = 16
SSM_GB = 12
FOX_BQ = 256
GRAD_DTYPE = BF16
ACT_DTYPE = BF16
ADAM_LR = 0.001
ADAM_B1 = 0.9
ADAM_B2 = 0.999
ADAM_EPS = 1e-08
ADAM_WD = 0.01
ADAM_STEP = 10

WEIGHTS = ['pre_norm_g', 'post_norm_g', 'w_in_a', 'lam_re', 'lam_im', 'log_step', 'b_re', 'b_im', 'c_re', 'c_im', 'd_skip',
           'w_glu', 'b_glu', 'kv_norm_g', 'w_kv', 'w_fgate', 'b_fgate', 'w_in_b', 'mem_norm_g', 'w_mem_kv', 'w_out']
INPUTS = ['x', 'mem'] + WEIGHTS + ['loss_target'] + ['m_' + n for n in WEIGHTS] + ['v_' + n for n in WEIGHTS]
REPLICATED = ['pre_norm_g', 'post_norm_g', 'lam_re', 'lam_im', 'log_step', 'b_re', 'b_im', 'c_re', 'c_im', 'kv_norm_g',
              'b_fgate', 'mem_norm_g']
SMALL_SHARDED = ['d_skip', 'b_glu', 'w_fgate']


def _cparams(sem=None):
    return pltpu.CompilerParams(dimension_semantics=sem, vmem_limit_bytes=V7X_VMEM_LIMIT)


def _tile(n, cap):
    if n <= cap:
        return n
    best = None
    for t in range(LANE, cap + 1, LANE):
        if n % t == 0:
            best = t
    assert best is not None, (n, cap)
    return best


def _matmul(name, a, b, a_spec, b_spec, o_spec, out_shape, grid, dims, nk, after=None):
    def body(a_ref, b_ref, *rest):
        o_ref, acc_ref = rest[-2:]
        k = pl.program_id(2)
        part = lax.dot_general(a_ref[...].astype(BF16), b_ref[...].astype(BF16), dims, preferred_element_type=F32)

        @pl.when(k == 0)
        def _():
            acc_ref[...] = part

        @pl.when(k > 0)
        def _():
            acc_ref[...] += part

        @pl.when(k == nk - 1)
        def _():
            o_ref[...] = acc_ref[...].astype(o_ref.dtype)

    acc_shape = tuple(d for d in o_spec.block_shape if d is not None)
    extra = [] if after is None else [after]
    return pl.pallas_call(
        body, name=name, grid=grid, in_specs=[a_spec, b_spec] + [pl.BlockSpec(memory_space=pl.ANY)] * len(extra), out_specs=o_spec,
        out_shape=out_shape, scratch_shapes=[pltpu.VMEM(acc_shape, F32)],
        compiler_params=_cparams(("parallel", "parallel", "arbitrary")),
    )(a, b, *extra)


NN = (((1,), (0,)), ((), ()))
NT = (((1,), (1,)), ((), ()))
TN = (((0,), (0,)), ((), ()))


def _mm_nn(name, a, b, out_dtype=ACT_DTYPE):
    (m, k), (_, n) = a.shape, b.shape
    bm, bn, bk = _tile(m, 1024), _tile(n, 512), _tile(k, 2048)
    return _matmul(name, a, b, pl.BlockSpec((bm, bk), lambda i, j, kk: (i, kk)), pl.BlockSpec((bk, bn), lambda i, j, kk: (kk, j)),
                   pl.BlockSpec((bm, bn), lambda i, j, kk: (i, j)), jax.ShapeDtypeStruct((m, n), out_dtype), (m // bm, n // bn, k // bk), NN, k // bk)


def _mm_nt(name, a, b, after=None, out_dtype=ACT_DTYPE):
    (m, c), (n, _) = a.shape, b.shape
    bm, bn, bk = _tile(m, 1024), _tile(n, 512), _tile(c, 2048)
    return _matmul(name, a, b, pl.BlockSpec((bm, bk), lambda i, j, kk: (i, kk)), pl.BlockSpec((bn, bk), lambda i, j, kk: (j, kk)),
                   pl.BlockSpec((bm, bn), lambda i, j, kk: (i, j)), jax.ShapeDtypeStruct((m, n), out_dtype), (m // bm, n // bn, c // bk), NT, c // bk, after)


def _mm_tn(name, a, b, out_dtype=F32, after=None):
    (c, m), (_, n) = a.shape, b.shape
    bm, bn, bk = _tile(m, 1024), _tile(n, 512), _tile(c, 2048)
    return _matmul(name, a, b, pl.BlockSpec((bk, bm), lambda i, j, kk: (kk, i)), pl.BlockSpec((bk, bn), lambda i, j, kk: (kk, j)),
                   pl.BlockSpec((bm, bn), lambda i, j, kk: (i, j)), jax.ShapeDtypeStruct((m, n), out_dtype), (m // bm, n // bn, c // bk), TN, c // bk, after)


def _mm_nn_cb(name, a, bb):
    (m, k), (nb, _, ns) = a.shape, bb.shape
    bm, bk = _tile(m, 1024), _tile(k, 2048)
    return _matmul(name, a, bb, pl.BlockSpec((bm, bk), lambda i, j, kk: (i, kk)), pl.BlockSpec((None, bk, ns), lambda i, j, kk: (j, kk, 0)),
                   pl.BlockSpec((bm, ns), lambda i, j, kk: (i, j)), jax.ShapeDtypeStruct((m, nb * ns), ACT_DTYPE), (m // bm, nb, k // bk), NN, k // bk)


def _mm_nt_cb(name, dy, bb, after=None):
    m, (nb, k, ns) = dy.shape[0], bb.shape
    bm, bn = _tile(m, 1024), _tile(k, 512)

    def body(dy_ref, b_ref, *rest):
        o_ref = rest[-1]
        acc = lax.dot_general(dy_ref[:, 0:ns], b_ref[0], NT, preferred_element_type=F32)
        for j in range(1, nb):
            acc += lax.dot_general(dy_ref[:, j * ns:(j + 1) * ns], b_ref[j], NT, preferred_element_type=F32)
        o_ref[...] = acc.astype(o_ref.dtype)

    extra = [] if after is None else [after]
    return pl.pallas_call(
        body, name=name, grid=(m // bm, k // bn),
        in_specs=[pl.BlockSpec((bm, nb * ns), lambda i, j: (i, 0)), pl.BlockSpec((nb, bn, ns), lambda i, j: (0, j, 0))]
        + [pl.BlockSpec(memory_space=pl.ANY)] * len(extra),
        out_specs=pl.BlockSpec((bm, bn), lambda i, j: (i, j)), out_shape=jax.ShapeDtypeStruct((m, k), ACT_DTYPE),
        compiler_params=_cparams(("parallel", "parallel")),
    )(dy, bb, *extra)


def _mm_tn_cb(name, a, dy, ns, out_dtype=F32, after=None):
    (c, k), nb = a.shape, dy.shape[1] // ns
    bm = _tile(k, 1024)
    return _matmul(name, a, dy, pl.BlockSpec((c, bm), lambda i, j, kk: (0, i)), pl.BlockSpec((c, ns), lambda i, j, kk: (0, j)),
                   pl.BlockSpec((None, bm, ns), lambda i, j, kk: (j, i, 0)), jax.ShapeDtypeStruct((nb, k, ns), out_dtype), (k // bm, nb, 1), TN, 1, after)


def make_mm(name):
    @jax.custom_vjp
    def mm(a, b):
        return _mm_nn(name, a, b, F32)

    def fwd(a, b):
        return mm(a, b), (a, b)

    def bwd(res, dy):
        a, b = res
        return _mm_nt(name + "_da", dy, b, out_dtype=a.dtype), _mm_tn(name + "_dw", a, dy, b.dtype)

    mm.defvjp(fwd, bwd)
    return mm


def _rowop_specs(rows, params, bm, nl):
    row_specs = [pl.BlockSpec((bm, r.shape[1] // nl), lambda j, i: (i, j)) for r in rows]
    par_specs = [pl.BlockSpec((p.shape[0], p.shape[1] // nl), lambda j, i: (0, j)) for p in params]
    row_blk = [jax.ShapeDtypeStruct((bm, r.shape[1] // nl), r.dtype) for r in rows]
    par_blk = [jax.ShapeDtypeStruct((p.shape[0], p.shape[1] // nl), p.dtype) for p in params]
    return row_specs, par_specs, row_blk, par_blk


def make_rowop(f, name, bm, nl=1, out_dtypes=None):
    def loaded(refs):
        return [r[...].astype(F32) for r in refs]

    def fwd_call(rows, params):
        n_rows = rows[0].shape[0]
        b = min(bm, n_rows)
        row_specs, par_specs, row_blk, par_blk = _rowop_specs(rows, params, b, nl)
        out_blk = jax.eval_shape(f, *[jax.ShapeDtypeStruct(t.shape, F32) for t in row_blk + par_blk])
        dts = out_dtypes or [F32] * len(out_blk)
        nr, npar = len(rows), len(params)

        def body(*refs):
            for o_ref, o in zip(refs[nr + npar:], f(*loaded(refs[:nr + npar]))):
                o_ref[...] = o.astype(o_ref.dtype)

        return pl.pallas_call(
            body, name=name, grid=(nl, n_rows // b), in_specs=row_specs + par_specs,
            out_specs=[pl.BlockSpec(o.shape, lambda j, i: (i, j)) for o in out_blk],
            out_shape=[jax.ShapeDtypeStruct((n_rows, o.shape[1] * nl), dt) for o, dt in zip(out_blk, dts)],
            compiler_params=_cparams(("parallel", "parallel")),
        )(*rows, *params)

    def bwd_call(rows, params, cts):
        n_rows = rows[0].shape[0]
        b = min(bm, n_rows)
        row_specs, par_specs, row_blk, par_blk = _rowop_specs(rows, params, b, nl)
        ct_specs = [pl.BlockSpec((b, c.shape[1] // nl), lambda j, i: (i, j)) for c in cts]
        nr, npar, nct = len(rows), len(params), len(cts)

        def body(*refs):
            i = pl.program_id(1)
            _, vjp = jax.vjp(lambda *v: tuple(f(*v)), *loaded(refs[:nr + npar]))
            grads = vjp(tuple(loaded(refs[nr + npar:nr + npar + nct])))
            outs = refs[nr + npar + nct:]
            for o_ref, g in zip(outs[:nr], grads[:nr]):
                o_ref[...] = g.astype(o_ref.dtype)
            for o_ref, g in zip(outs[nr:], grads[nr:]):
                @pl.when(i == 0)
                def _(o_ref=o_ref, g=g):
                    o_ref[...] = g

                @pl.when(i > 0)
                def _(o_ref=o_ref, g=g):
                    o_ref[...] += g

        outs = pl.pallas_call(
            body, name=name + "_bwd", grid=(nl, n_rows // b), in_specs=row_specs + par_specs + ct_specs,
            out_specs=row_specs + par_specs,
            out_shape=[jax.ShapeDtypeStruct(a.shape, a.dtype) for a in rows] + [jax.ShapeDtypeStruct(a.shape, F32) for a in params],
            compiler_params=_cparams(("arbitrary", "arbitrary")),
        )(*rows, *params, *cts)
        return tuple(outs[:nr]), tuple(g.astype(a.dtype) for g, a in zip(outs[nr:], params))

    @jax.custom_vjp
    def op(rows, params):
        return tuple(fwd_call(rows, params))

    def fwd(rows, params):
        return op(rows, params), (rows, params)

    def bwd(res, cts):
        rows, params = res
        return bwd_call(rows, params, tuple(cts))

    op.defvjp(fwd, bwd)
    return op


def make_groupop(f, name, gb):
    def specs(arrs):
        return [pl.BlockSpec((gb,) + a.shape[1:], lambda g: (g, 0, 0)) for a in arrs]

    def fwd_call(arrs):
        g_n = arrs[0].shape[0]
        out_blk = jax.eval_shape(f, *[jax.ShapeDtypeStruct((gb,) + a.shape[1:], a.dtype) for a in arrs])[0]
        n = len(arrs)

        def body(*refs):
            for i, o in enumerate(f(*[r[...] for r in refs[:n]])):
                refs[n][i] = o.astype(refs[n].dtype)

        return pl.pallas_call(
            body, name=name, grid=(g_n // gb,), in_specs=specs(arrs),
            out_specs=pl.BlockSpec((gb,) + out_blk.shape, lambda g: (g, 0, 0)),
            out_shape=jax.ShapeDtypeStruct((g_n,) + out_blk.shape, ACT_DTYPE),
            compiler_params=_cparams(("parallel",)),
        )(*arrs)

    def bwd_call(arrs, ct):
        g_n = arrs[0].shape[0]
        n = len(arrs)

        def body(*refs):
            _, vjp = jax.vjp(lambda *v: tuple(f(*v)), *[r[...] for r in refs[:n]])
            for o_ref, g in zip(refs[n + 1:], vjp(tuple(refs[n][i].astype(F32) for i in range(gb)))):
                o_ref[...] = g.astype(o_ref.dtype)

        return pl.pallas_call(
            body, name=name + "_bwd", grid=(g_n // gb,), in_specs=specs(arrs) + specs([ct]), out_specs=specs(arrs),
            out_shape=[jax.ShapeDtypeStruct(a.shape, a.dtype) for a in arrs],
            compiler_params=_cparams(("parallel",)),
        )(*arrs, ct)

    @jax.custom_vjp
    def op(*arrs):
        return fwd_call(arrs)

    def fwd(*arrs):
        return op(*arrs), arrs

    def bwd(arrs, ct):
        return tuple(bwd_call(arrs, ct))

    op.defvjp(fwd, bwd)
    return op


def _rms(x, g):
    return x * lax.rsqrt(jnp.mean(x * x, axis=-1, keepdims=True) + EPS) * g


def _silu(z):
    return z * jax.nn.sigmoid(z)


def _log_sigmoid(x):
    return jnp.minimum(x, 0.0) - jnp.log(1.0 + jnp.exp(-jnp.abs(x)))


def f_norm(x, g):
    return (_rms(x, g),)


def f_gate_a1(y, u, d):
    return (jax.nn.gelu(y + d * u),)


def f_gate_a2(yg, t, z, b):
    return (yg * jax.nn.sigmoid(t + b) * _silu(z),)


def f_gate_b(att, z):
    return (att * _silu(z),)


def f_post_a(h, o, post_g, kv_g, pre_g):
    h1 = h + _rms(o, post_g)
    return h1, _rms(h1, kv_g), _rms(h1, pre_g)


def f_final(h, o, tgt, post_g):
    err = h + _rms(o, post_g) - tgt
    return (0.5 * jnp.mean(err * err, axis=-1, keepdims=True),)


def f_logf(gl, b):
    return (_log_sigmoid(gl + b),)


def _mxu(a, b, dims):
    return lax.dot_general(a.astype(BF16), b.astype(BF16), dims, preferred_element_type=F32)


@jax.custom_vjp
def _mxu_nn(a, b):
    return _mxu(a, b, NN)


def _mxu_nn_bwd(res, g):
    a, b = res
    return _mxu(g, b, NT).astype(a.dtype), _mxu(a, g, TN).astype(b.dtype)


_mxu_nn.defvjp(lambda a, b: (_mxu(a, b, NN), (a, b)), _mxu_nn_bwd)


@jax.custom_vjp
def _mxu_nt(a, b):
    return _mxu(a, b, NT)


def _mxu_nt_bwd(res, g):
    a, b = res
    return _mxu(g, b, NN).astype(a.dtype), _mxu(g, a, TN).astype(b.dtype)


_mxu_nt.defvjp(lambda a, b: (_mxu(a, b, NT), (a, b)), _mxu_nt_bwd)


def f_memattn(q, zm, km, vm):
    o = _mxu_nn(_softmax(_mxu_nt(q * (HEAD_DIM ** -0.5), km)), vm)
    return (o * _silu(zm),)


@jax.custom_vjp
def _softmax(s):
    e = jnp.exp(s - jnp.max(s, axis=-1, keepdims=True))
    return e * (1.0 / jnp.sum(e, axis=-1, keepdims=True))


def _softmax_fwd(s):
    p = _softmax(s)
    return p, p


def _softmax_bwd(p, dp):
    return (p * (dp - jnp.sum(dp * p, axis=-1, keepdims=True)),)


_softmax.defvjp(_softmax_fwd, _softmax_bwd)


def _fox_block(q, k, v, fr):
    bq = q.shape[0]
    s = _mxu_nt(q * (HEAD_DIM ** -0.5), k) - fr
    tri = lax.broadcasted_iota(jnp.int32, (bq, bq), 0) >= lax.broadcasted_iota(jnp.int32, (bq, bq), 1)
    diag = jnp.where(tri, s[:, -bq:], -1e30)
    s = diag if k.shape[0] == bq else jnp.concatenate([s[:, :-bq], diag], axis=1)
    return _mxu_nn(_softmax(s), v)


def _cmul(ar, ai, xr, xi):
    return ar * xr - ai * xi, ar * xi + ai * xr


def _hp_dot(a, b):
    return jnp.dot(a, b, precision=HP, preferred_element_type=F32)


_bf_dot = _mxu_nn


def f_s5(u, toep, win_r, win_i, wout_r, wout_i, coef):
    gb = u.shape[0]
    n_hi = u.shape[1] // SSM_LO
    sr = [_bf_dot(u[i], win_r[i]) for i in range(gb)]
    si = [_bf_dot(u[i], win_i[i]) for i in range(gb)]

    def stacked(parts, lo):
        return jnp.concatenate([a[lo * n_hi:(lo + 1) * n_hi] for a in parts], axis=0)

    def coef_rows(k):
        return jnp.concatenate([jnp.broadcast_to(coef[i, k:k + 1, :], (n_hi, coef.shape[2])) for i in range(gb)], axis=0)

    at_r, at_i = coef_rows(0), coef_rows(1)
    pr, pi = stacked(sr, 0), stacked(si, 0)
    for lo in range(1, SSM_LO):
        dr, di = _cmul(at_r, at_i, pr, pi)
        pr, pi = stacked(sr, lo) + dr, stacked(si, lo) + di
    n = 2 * gb * n_hi
    ri = lax.broadcasted_iota(jnp.int32, (n, n), 0)
    ci = lax.broadcasted_iota(jnp.int32, (n, n), 1)
    same = ri // n_hi == ci // n_hi

    def shifted(d, zr, zi):
        z = _hp_dot((same & (ri - ci == d)).astype(F32), jnp.concatenate([zr, zi], axis=0))
        return z[:n // 2], z[n // 2:]

    d, step = 1, 0
    while d < n_hi:
        dr, di = _cmul(coef_rows(2 + 2 * step), coef_rows(3 + 2 * step), *shifted(d, pr, pi))
        pr, pi = pr + dr, pi + di
        d, step = 2 * d, step + 1
    er, ei = shifted(1, pr, pi)
    xr, xi = [er], [ei]
    for lo in range(1, SSM_LO):
        dr, di = _cmul(at_r, at_i, xr[-1], xi[-1])
        xr.append(stacked(sr, lo - 1) + dr)
        xi.append(stacked(si, lo - 1) + di)

    def group_rows(parts, i):
        return jnp.concatenate([a[i * n_hi:(i + 1) * n_hi] for a in parts], axis=0)

    return tuple(_bf_dot(u[i], toep[i]) + _bf_dot(group_rows(xr, i), wout_r[i]) + _bf_dot(group_rows(xi, i), wout_i[i])
                 for i in range(gb))


def s5_operators(lam_re, lam_im, log_step, b_re, b_im, c_re, c_im, n_hi):
    t_n = SSM_T
    lr, li = lam_re, lam_im
    dt = jnp.exp(log_step)[:, None]
    mag = jnp.exp(lr * dt)
    ar, ai = mag * jnp.cos(li * dt), mag * jnp.sin(li * dt)
    den = lr * lr + li * li
    cr = ((ar - 1.0) * lr + ai * li) / den
    ci = (ai * lr - (ar - 1.0) * li) / den
    bbr = cr[..., None] * b_re - ci[..., None] * b_im
    bbi = cr[..., None] * b_im + ci[..., None] * b_re
    k = jnp.arange(t_n + 1, dtype=F32)[:, None, None]
    pm, ang = jnp.exp(k * (lr * dt)), k * (li * dt)
    pr, pi = pm * jnp.cos(ang), pm * jnp.sin(ang)
    abr = pr[..., None] * bbr - pi[..., None] * bbi
    abi = pr[..., None] * bbi + pi[..., None] * bbr
    kk = (jnp.einsum('ghp,kgpj->kghj', c_re, abr[:t_n], precision=HP)
          - jnp.einsum('ghp,kgpj->kghj', c_im, abi[:t_n], precision=HP))
    lag = jnp.arange(t_n)[None, :] - jnp.arange(t_n)[:, None]
    onehot = (lag[None] == jnp.arange(t_n)[:, None, None]).astype(F32)
    g_n, h_n = c_re.shape[0], c_re.shape[1]
    toep = jnp.einsum('kst,kghj->gsjth', onehot, kk, precision=HP).reshape(g_n, t_n * h_n, t_n * h_n)
    win_r = abr[:t_n][::-1].transpose(1, 0, 3, 2).reshape(g_n, t_n * h_n, -1)
    win_i = abi[:t_n][::-1].transpose(1, 0, 3, 2).reshape(g_n, t_n * h_n, -1)
    p1r, p1i = pr[1:, :, None, :], pi[1:, :, None, :]
    wout_r = (c_re[None] * p1r - c_im[None] * p1i).transpose(1, 3, 0, 2).reshape(g_n, -1, t_n * h_n)
    wout_i = (-(c_re[None] * p1i + c_im[None] * p1r)).transpose(1, 3, 0, 2).reshape(g_n, -1, t_n * h_n)
    rows = [pr[t_n], pi[t_n]]
    qr, qi = pr[t_n], pi[t_n]
    for _ in range(int(math.log2(SSM_LO))):
        qr, qi = qr * qr - qi * qi, 2.0 * qr * qi
    d = 1
    while d < n_hi:
        rows += [qr, qi]
        qr, qi = qr * qr - qi * qi, 2.0 * qr * qi
        d *= 2
    coef = jnp.stack(rows, axis=1)
    return toep, win_r, win_i, wout_r, wout_i, coef


def _cumsum_call(name, x, reverse):
    n_rows, w = x.shape
    bm = min(256, n_rows)
    nb = n_rows // bm

    def body(x_ref, o_ref, carry_ref):
        i = pl.program_id(0)

        @pl.when(i == 0)
        def _():
            carry_ref[...] = jnp.zeros_like(carry_ref)

        ri = lax.broadcasted_iota(jnp.int32, (bm, bm), 0)
        ci = lax.broadcasted_iota(jnp.int32, (bm, bm), 1)
        tri = ((ri <= ci) if reverse else (ri >= ci)).astype(F32)
        xb = x_ref[...]
        o_ref[...] = _hp_dot(tri, xb) + carry_ref[...]
        carry_ref[...] += jnp.sum(xb, axis=0, keepdims=True)

    idx = (lambda i: (nb - 1 - i, 0)) if reverse else (lambda i: (i, 0))
    return pl.pallas_call(
        body, name=name, grid=(nb,), in_specs=[pl.BlockSpec((bm, w), idx)], out_specs=pl.BlockSpec((bm, w), idx),
        out_shape=jax.ShapeDtypeStruct(x.shape, F32), scratch_shapes=[pltpu.VMEM((1, w), F32)],
        compiler_params=_cparams(("arbitrary",)),
    )(x)


def make_cumsum(name):
    @jax.custom_vjp
    def cs(x):
        return _cumsum_call(name, x, False)

    def fwd(x):
        return cs(x), None

    def bwd(_, dy):
        return (_cumsum_call(name + "_bwd", dy, True),)

    cs.defvjp(fwd, bwd)
    return cs


def _fox_specs(n_rows, bq):
    q_spec = pl.BlockSpec((bq, HEAD_DIM), lambda h, i: (i, h))
    kv_spec = pl.BlockSpec((n_rows, HEAD_DIM), lambda h, i: (0, h))
    fr_spec = pl.BlockSpec((None, 1, n_rows), lambda h, i: (h, 0, 0))
    return q_spec, kv_spec, fr_spec


def _fox_fwd_call(name, q, k, v, fr):
    n_rows, width = q.shape
    bq = min(FOX_BQ, n_rows)
    nq = n_rows // bq
    q_spec, kv_spec, fr_spec = _fox_specs(n_rows, bq)

    def body(q_ref, k_ref, v_ref, fr_ref, o_ref):
        i = pl.program_id(1)
        for p in range(nq):
            n_keys = (p + 1) * bq

            @pl.when(i == p)
            def _(n_keys=n_keys):
                o = _fox_block(q_ref[...].astype(F32), k_ref[:n_keys, :], v_ref[:n_keys, :], fr_ref[:, :n_keys])
                o_ref[...] = o.astype(o_ref.dtype)

    return pl.pallas_call(
        body, name=name, grid=(width // HEAD_DIM, nq), in_specs=[q_spec, kv_spec, kv_spec, fr_spec], out_specs=q_spec,
        out_shape=jax.ShapeDtypeStruct(q.shape, ACT_DTYPE), compiler_params=_cparams(("parallel", "parallel")),
    )(q, k, v, fr)


def _fox_bwd_call(name, q, k, v, fr, do):
    n_rows, width = q.shape
    bq = min(FOX_BQ, n_rows)
    nq = n_rows // bq
    q_spec, kv_spec, fr_spec = _fox_specs(n_rows, bq)

    def body(q_ref, k_ref, v_ref, fr_ref, do_ref, dq_ref, dk_ref, dv_ref, dfr_ref, dk_acc, dv_acc):
        i = pl.program_id(1)

        @pl.when(i == 0)
        def _():
            dk_acc[...] = jnp.zeros_like(dk_acc)
            dv_acc[...] = jnp.zeros_like(dv_acc)
            dfr_ref[...] = jnp.zeros_like(dfr_ref)

        for p in range(nq):
            n_keys = (p + 1) * bq

            @pl.when(i == p)
            def _(n_keys=n_keys):
                _, vjp = jax.vjp(_fox_block, q_ref[...].astype(F32), k_ref[:n_keys, :].astype(F32), v_ref[:n_keys, :].astype(F32),
                                 fr_ref[:, :n_keys])
                dq, dk, dv, dfr = vjp(do_ref[...].astype(F32))
                dq_ref[...] = dq.astype(dq_ref.dtype)
                dk_acc[:n_keys, :] += dk
                dv_acc[:n_keys, :] += dv
                dfr_ref[:, :n_keys] += dfr

        @pl.when(i == nq - 1)
        def _():
            dk_ref[...] = dk_acc[...].astype(dk_ref.dtype)
            dv_ref[...] = dv_acc[...].astype(dv_ref.dtype)

    return pl.pallas_call(
        body, name=name, grid=(width // HEAD_DIM, nq), in_specs=[q_spec, kv_spec, kv_spec, fr_spec, q_spec],
        out_specs=[q_spec, kv_spec, kv_spec, fr_spec], out_shape=[jax.ShapeDtypeStruct(a.shape, a.dtype) for a in (q, k, v, fr)],
        scratch_shapes=[pltpu.VMEM((n_rows, HEAD_DIM), F32), pltpu.VMEM((n_rows, HEAD_DIM), F32)],
        compiler_params=_cparams(("parallel", "arbitrary")),
    )(q, k, v, fr, do)


def make_fox(name):
    @jax.custom_vjp
    def fox(q, k, v, fr):
        return _fox_fwd_call(name, q, k, v, fr)

    def fwd(q, k, v, fr):
        return fox(q, k, v, fr), (q, k, v, fr)

    def bwd(res, do):
        return tuple(_fox_bwd_call(name + "_bwd", *res, do))

    fox.defvjp(fwd, bwd)
    return fox


def _relations(x, y, c):
    out = []
    for rel in range(1, N_DEV):
        px = 1 - x if rel & 4 else x
        py = 1 - y if rel & 2 else y
        pc = 1 - c if rel & 1 else c
        out.append((rel, (px, py, pc), 4 * px + 2 * py + pc))
    return out


_HBM_SPEC = pl.BlockSpec(memory_space=pltpu.HBM)
_SEM_SPEC = pl.BlockSpec(memory_space=pltpu.SEMAPHORE)
_DATAFLOW = pltpu.SideEffectType.DATAFLOW_SIDE_EFFECTING


def exchange_start(name, srcs, gather):
    n = len(srcs)
    me = 4 * lax.axis_index("x") + 2 * lax.axis_index("y") + lax.axis_index("c")
    lands = []
    for s in srcs:
        own = s if gather else lax.dynamic_index_in_dim(s, me, 0, keepdims=False)
        land = lax.empty((N_DEV,) + own.shape, s.dtype)
        lands.append(lax.dynamic_update_index_in_dim(land, own, me, 0))

    def body(*refs):
        src_refs, land_refs = refs[:n], refs[n:2 * n]
        send_sems, recv_sems, token = refs[2 * n], refs[2 * n + 1], refs[-1]
        x, y, c = lax.axis_index("x"), lax.axis_index("y"), lax.axis_index("c")
        mine = 4 * x + 2 * y + c
        for k in range(n):
            for rel, peer, peer_blk in _relations(x, y, c):
                pltpu.make_async_remote_copy(
                    src_ref=src_refs[k] if gather else src_refs[k].at[peer_blk], dst_ref=land_refs[k].at[mine],
                    send_sem=send_sems.at[7 * k + rel - 1], recv_sem=recv_sems.at[7 * k + rel - 1],
                    device_id=peer, device_id_type=pl.DeviceIdType.MESH).start()
        token[...] = jnp.zeros_like(token)

    sem = pltpu.SemaphoreType.DMA((7 * n,))
    hbm = lambda t: pltpu.HBM(t.shape, t.dtype)
    outs = pl.pallas_call(
        body, name=name,
        out_shape=(sem, sem, *[hbm(s) for s in srcs], *[hbm(l) for l in lands], jax.ShapeDtypeStruct((8, LANE), F32)),
        in_specs=[_HBM_SPEC] * (2 * n),
        out_specs=(_SEM_SPEC, _SEM_SPEC, *[_HBM_SPEC] * (2 * n), pl.BlockSpec(memory_space=pltpu.VMEM)),
        input_output_aliases={i: 2 + i for i in range(2 * n)},
        compiler_params=pltpu.CompilerParams(has_side_effects=_DATAFLOW),
    )(*[pltpu.with_memory_space_constraint(t, pltpu.HBM) for t in list(srcs) + lands])
    return dict(name=name, gather=gather, send=outs[0], recv=outs[1], srcs=list(outs[2:2 + n]), lands=list(outs[2 + n:2 + 2 * n]), token=outs[-1])


def exchange_wait(handle, k, after):
    gather = handle['gather']

    def body(src_ref, land_ref, send_sems, recv_sems, after_ref, src_out, land_out):
        x, y, c = lax.axis_index("x"), lax.axis_index("y"), lax.axis_index("c")
        for rel, peer, peer_blk in _relations(x, y, c):
            copy = pltpu.make_async_remote_copy(
                src_ref=src_ref if gather else src_ref.at[peer_blk], dst_ref=land_ref.at[peer_blk],
                send_sem=send_sems.at[7 * k + rel - 1], recv_sem=recv_sems.at[7 * k + rel - 1],
                device_id=peer, device_id_type=pl.DeviceIdType.MESH)
            copy.wait_send()
            copy.wait_recv()

    src, land = handle['srcs'][k], handle['lands'][k]
    return pl.pallas_call(
        body, name=f"{handle['name']}_wait{k}", out_shape=(pltpu.HBM(src.shape, src.dtype), pltpu.HBM(land.shape, land.dtype)),
        in_specs=[_HBM_SPEC, _HBM_SPEC, _SEM_SPEC, _SEM_SPEC, pl.BlockSpec(memory_space=pl.ANY)], out_specs=(_HBM_SPEC, _HBM_SPEC),
        input_output_aliases={0: 0, 1: 1}, compiler_params=pltpu.CompilerParams(has_side_effects=_DATAFLOW),
    )(src, land, handle['send'], handle['recv'], after)[1]


def adamw_reduce(name, partials, w, m, v):
    n_part, n_rows, n_cols = partials.shape
    br = n_rows
    for cand in (512, 256, 128, 64, 32, 16, 8):
        if n_rows % cand == 0 and n_part * cand * n_cols * partials.dtype.itemsize <= (4 << 20):
            br = cand
            break

    def body(p_ref, w_ref, m_ref, v_ref, g_ref, d_ref, nm_ref, nv_ref):
        g = p_ref[0].astype(F32)
        for s in range(1, n_part):
            g = g + p_ref[s].astype(F32)
        m_new = ADAM_B1 * m_ref[...] + (1.0 - ADAM_B1) * g
        v_new = ADAM_B2 * v_ref[...] + (1.0 - ADAM_B2) * jnp.square(g)
        m_hat = m_new / (1.0 - ADAM_B1 ** ADAM_STEP)
        v_hat = v_new / (1.0 - ADAM_B2 ** ADAM_STEP)
        g_ref[...] = g
        d_ref[...] = -ADAM_LR * (m_hat / (jnp.sqrt(v_hat) + ADAM_EPS) + ADAM_WD * w_ref[...])
        nm_ref[...] = m_new
        nv_ref[...] = v_new

    spec = pl.BlockSpec((br, n_cols), lambda i: (i, 0))
    return pl.pallas_call(
        body, name=name, grid=(n_rows // br,), in_specs=[pl.BlockSpec((n_part, br, n_cols), lambda i: (0, i, 0)), spec, spec, spec],
        out_specs=[spec] * 4, out_shape=[jax.ShapeDtypeStruct((n_rows, n_cols), F32)] * 4, compiler_params=_cparams(("parallel",)),
    )(partials, w, m, v)


def _pack(arrays, n_rows):
    flat = jnp.concatenate([a.reshape(-1) for a in arrays])
    return jnp.pad(flat, (0, n_rows * LANE - flat.shape[0])).reshape(n_rows, LANE)


def _unpack(packed, shapes):
    flat, out, off = packed.reshape(-1), [], 0
    for s in shapes:
        n = math.prod(s)
        out.append(flat[off:off + n].reshape(s))
        off += n
    return out


def _packed_rows(shapes):
    n = sum(math.prod(s) for s in shapes)
    return -(-n // (LANE * 512)) * 512


GATHER_ORDER = ['small', 'in_a', 'mem_kv0', 'glu', 'out0', 'kv', 'in_b', 'mem_kv1', 'out1']


BLOCKED = ('in_a', 'kv', 'in_b')


def local_step(p, gathered, x, mem, tgt, start_early, start_late):
    n_rows, d_model = x.shape
    main_w = 3 * d_model // 4
    mem_w = d_model - main_w
    n_groups = main_w // SSM_GROUP
    n_heads = main_w // HEAD_DIM
    n_hi = n_rows // (SSM_T * SSM_LO)
    row = lambda a: a.reshape(1, -1)
    tape, pending = {}, {}

    def mm_fwd(name, act, after=None):
        w = exchange_wait(gathered, GATHER_ORDER.index(name), act if after is None else after)
        if name not in BLOCKED:
            w = w.reshape(-1, w.shape[2])
        tape[name] = (act, w)
        return (_mm_nn_cb if name in BLOCKED else _mm_nn)(name, act, w)

    def mm_dw(name, dy, after=None):
        act, w = tape[name]
        if name in BLOCKED:
            dw = _mm_tn_cb(name + "_dw", act, dy, w.shape[2], GRAD_DTYPE, after)
        else:
            dw = _mm_tn(name + "_dw", act, dy, GRAD_DTYPE, after)
            dw = dw.reshape(N_DEV, dw.shape[0] // N_DEV, dw.shape[1])
        pending[name] = exchange_start("rs_" + name, [dw], gather=False)

    def mm_da(name, dy, after=None):
        return (_mm_nt_cb if name in BLOCKED else _mm_nt)(name + "_da", dy, tape[name][1], after)

    def mm_bwd(name, dy):
        mm_dw(name, dy)
        return mm_da(name, dy, pending[name]['token'])

    def split(proj):
        return proj[:, :main_w], proj[:, main_w:2 * main_w], proj[:, 2 * main_w:2 * main_w + mem_w], proj[:, 2 * main_w + mem_w:]

    def mem_attn(i, qm, zm, kvm):
        memo, = make_rowop(f_memattn, f"mem_attn{i}", 512, mem_w // HEAD_DIM, [ACT_DTYPE])((qm, zm), (kvm[:, :mem_w], kvm[:, mem_w:]))
        return memo

    def seg_norms(x_, pre_g0, mem_g0, mem_g1):
        hn, = make_rowop(f_norm, "pre_norm0", 256, out_dtypes=[ACT_DTYPE])((x_,), (row(pre_g0),))
        memn0, = make_rowop(f_norm, "mem_norm0", 256, out_dtypes=[ACT_DTYPE])((mem,), (row(mem_g0),))
        memn1, = make_rowop(f_norm, "mem_norm1", 256, out_dtypes=[ACT_DTYPE])((mem,), (row(mem_g1),))
        return hn, memn0, memn1

    def seg_a1(u, qm, zm, kvm, d_skip, *ops):
        ug = u.astype(BF16).reshape(n_hi, SSM_LO, SSM_T, n_groups, SSM_GROUP).transpose(3, 1, 0, 2, 4).reshape(n_groups, n_hi * SSM_LO, SSM_T * SSM_GROUP)
        yg = make_groupop(f_s5, "s5", SSM_GB)(ug, *ops)
        y = yg.reshape(n_groups, SSM_LO, n_hi, SSM_T, SSM_GROUP).transpose(2, 1, 3, 0, 4).reshape(n_rows, main_w)
        ygelu, = make_rowop(f_gate_a1, "gate_a1", 256, out_dtypes=[ACT_DTYPE])((y, u), (row(d_skip),))
        return ygelu, mem_attn(0, qm, zm, kvm)

    def seg_a2(ygelu, t, z, memo, b_glu):
        main, = make_rowop(f_gate_a2, "gate_a2", 256, out_dtypes=[ACT_DTYPE])((ygelu, t, z), (row(b_glu),))
        return jnp.concatenate([main, memo], axis=1)

    def seg_post_a(x_, o, post_g0, kv_g, pre_g1):
        return make_rowop(f_post_a, "post_a", 128, out_dtypes=[F32, ACT_DTYPE, ACT_DTYPE])((x_, o), (row(post_g0), row(kv_g), row(pre_g1)))

    def seg_b(q, z, qm, zm, k, v, kv_in, kvm, w_fgate, b_fgate):
        w_fg = jnp.pad(w_fgate, ((0, 0), (0, LANE - n_heads)))
        b_fg = jnp.pad(b_fgate, (0, LANE - n_heads)).reshape(1, LANE)
        logf, = make_rowop(f_logf, "logf", 512)((make_mm("fgate")(kv_in, w_fg),), (b_fg,))
        fcum = make_cumsum("fcum")(logf)[:, :n_heads].T
        att = make_fox("fox")(q, k, v, fcum[:, None, :])
        main, = make_rowop(f_gate_b, "gate_b", 256, out_dtypes=[ACT_DTYPE])((att, z), ())
        return jnp.concatenate([main, mem_attn(1, qm, zm, kvm)], axis=1)

    def seg_final(h1, o, post_g1):
        rowloss, = make_rowop(f_final, "final", 128)((h1, o, tgt), (row(post_g1),))
        return jnp.sum(rowloss)

    (hn0, memn0, memn1), vjp_norms = jax.vjp(seg_norms, x, p['pre_norm_g'][0], p['mem_norm_g'][0], p['mem_norm_g'][1])
    s5_names = ('lam_re', 'lam_im', 'log_step', 'b_re', 'b_im', 'c_re', 'c_im')
    ops, vjp_ops = jax.vjp(lambda *a: s5_operators(*a, n_hi), *[p[n] for n in s5_names])
    proj_a = mm_fwd("in_a", hn0, ops[0])
    kvm0 = mm_fwd("mem_kv0", memn0, proj_a)
    u_a, z_a, qm_a, zm_a = split(proj_a)
    (ygelu, memo0), vjp_a1 = jax.vjp(seg_a1, u_a, qm_a, zm_a, kvm0, p['d_skip'], *ops)
    t = mm_fwd("glu", ygelu)
    cat0, vjp_a2 = jax.vjp(seg_a2, ygelu, t, z_a, memo0, p['b_glu'])
    o0 = mm_fwd("out0", cat0)
    (h1, kv_in, hn1), vjp_post_a = jax.vjp(seg_post_a, x, o0, p['post_norm_g'][0], p['kv_norm_g'], p['pre_norm_g'][1])
    kv = mm_fwd("kv", kv_in)
    proj_b = mm_fwd("in_b", hn1)
    kvm1 = mm_fwd("mem_kv1", memn1, proj_b)
    cat1, vjp_b = jax.vjp(seg_b, *split(proj_b), kv[:, :main_w], kv[:, main_w:], kv_in, kvm1, p['w_fgate'], p['b_fgate'])
    o1 = mm_fwd("out1", cat1)
    loss, vjp_final = jax.vjp(seg_final, h1, o1, p['post_norm_g'][1])

    g = {}
    d_h1, d_o1, g_post_g1 = vjp_final(jnp.ones((), F32))
    *d_proj_b, d_k, d_v, d_kv_in, d_kvm1, g['w_fgate'], g['b_fgate'] = vjp_b(mm_bwd("out1", d_o1))
    d_memn1 = mm_bwd("mem_kv1", d_kvm1)
    d_hn1 = mm_bwd("in_b", jnp.concatenate(d_proj_b, axis=1))
    d_kv_in = d_kv_in + mm_bwd("kv", jnp.concatenate([d_k, d_v], axis=1))
    d_x, d_o0, g_post_g0, g['kv_norm_g'], g_pre_g1 = vjp_post_a((d_h1, d_kv_in, d_hn1))
    d_ygelu, d_t, d_z, d_memo0, g['b_glu'] = vjp_a2(mm_bwd("out0", d_o0))
    d_ygelu = d_ygelu + mm_bwd("glu", d_t)
    d_u, d_qm, d_zm, d_kvm0, g['d_skip'], *d_ops = vjp_a1((d_ygelu, d_memo0))
    d_proj_a = jnp.concatenate([d_u, d_z, d_qm, d_zm], axis=1)
    g.update(zip(s5_names, vjp_ops(tuple(d_ops))))
    g['pre_norm_g'] = jnp.stack([jnp.zeros_like(g_pre_g1), g_pre_g1])
    g['post_norm_g'] = jnp.stack([g_post_g0, g_post_g1])
    g['mem_norm_g'] = jnp.zeros_like(p['mem_norm_g'])
    early = start_early(g)
    d_memn0 = mm_bwd("mem_kv0", d_kvm0)
    d_x2, *g_late = vjp_norms((mm_da("in_a", d_proj_a, early['token']), d_memn0, d_memn1))
    late = start_late(g_late)
    mm_dw("in_a", d_proj_a, late['token'])
    return loss, d_x + d_x2, pending, early, late


def kernel(x, mem, pre_norm_g, post_norm_g, w_in_a, lam_re, lam_im, log_step, b_re, b_im, c_re, c_im, d_skip, w_glu, b_glu, kv_norm_g, w_kv, w_fgate, b_fgate, w_in_b, mem_norm_g, w_mem_kv, w_out, loss_target, m_pre_norm_g, m_post_norm_g, m_w_in_a, m_lam_re, m_lam_im, m_log_step, m_b_re, m_b_im, m_c_re, m_c_im, m_d_skip, m_w_glu, m_b_glu, m_kv_norm_g, m_w_kv, m_w_fgate, m_b_fgate, m_w_in_b, m_mem_norm_g, m_w_mem_kv, m_w_out, v_pre_norm_g, v_post_norm_g, v_w_in_a, v_lam_re, v_lam_im, v_log_step, v_b_re, v_b_im, v_c_re, v_c_im, v_d_skip, v_w_glu, v_b_glu, v_kv_norm_g, v_w_kv, v_w_fgate, v_b_fgate, v_w_in_b, v_mem_norm_g, v_w_mem_kv, v_w_out):
    a = dict(zip(INPUTS, (x, mem, pre_norm_g, post_norm_g, w_in_a, lam_re, lam_im, log_step, b_re, b_im, c_re, c_im, d_skip, w_glu, b_glu, kv_norm_g, w_kv, w_fgate, b_fgate, w_in_b, mem_norm_g, w_mem_kv, w_out, loss_target, m_pre_norm_g, m_post_norm_g, m_w_in_a, m_lam_re, m_lam_im, m_log_step, m_b_re, m_b_im, m_c_re, m_c_im, m_d_skip, m_w_glu, m_b_glu, m_kv_norm_g, m_w_kv, m_w_fgate, m_b_fgate, m_w_in_b, m_mem_norm_g, m_w_mem_kv, m_w_out, v_pre_norm_g, v_post_norm_g, v_w_in_a, v_lam_re, v_lam_im, v_log_step, v_b_re, v_b_im, v_c_re, v_c_im, v_d_skip, v_w_glu, v_b_glu, v_kv_norm_g, v_w_kv, v_w_fgate, v_b_fgate, v_w_in_b, v_mem_norm_g, v_w_mem_kv, v_w_out)))
    me = 4 * lax.axis_index("x") + 2 * lax.axis_index("y") + lax.axis_index("c")
    n_layers = w_out.shape[0]

    small_shapes = [a[n].shape for n in SMALL_SHARDED]
    small_rows = -(-sum(math.prod(s) for s in small_shapes) // (LANE * 8)) * 8
    shards = dict(in_a=w_in_a[0], mem_kv0=w_mem_kv[0], glu=w_glu[0], out0=w_out[0], kv=w_kv, in_b=w_in_b[0], mem_kv1=w_mem_kv[1], out1=w_out[1])
    operands = [_pack([a[n] for n in SMALL_SHARDED], small_rows)] + [shards[n].astype(BF16) for n in GATHER_ORDER[1:]]
    gathered = exchange_start("ag_weights", operands, gather=True)
    small = exchange_wait(gathered, 0, gathered['token'])
    small = [jnp.stack(parts) for parts in zip(*[_unpack(small[b], small_shapes) for b in range(N_DEV)])]
    p = {n: a[n] for n in REPLICATED}
    for n in ('lam_re', 'lam_im', 'log_step', 'b_re', 'b_im', 'c_re', 'c_im'):
        p[n] = p[n][0]
    p['d_skip'] = small[0].reshape(-1)
    p['b_glu'] = small[1].reshape(-1)
    p['w_fgate'] = small[2].reshape(-1, w_fgate.shape[1])
    p['pre_norm_g'] = p['pre_norm_g'] + gathered['token'][0, 0]

    full_shapes = [a[n].shape for n in REPLICATED] + [(1, N_DEV * d_skip.shape[1]), (1, N_DEV * b_glu.shape[1]), (N_DEV * w_fgate.shape[0], w_fgate.shape[1])]
    rows = _packed_rows(full_shapes)
    late_rows = 3 * pre_norm_g.shape[1] // LANE
    start_early = lambda g: exchange_start("ag_grads", [_pack([g[n] for n in REPLICATED + SMALL_SHARDED], rows)], gather=True)
    start_late = lambda g_late: exchange_start("ag_grads_late", [_pack(g_late, late_rows)], gather=True)

    loss_local, grad_x, pending, early, late = local_step(p, gathered, x[0], mem[0], loss_target[0], start_early, start_late)
    loss = lax.psum(loss_local, MESH_AXES)

    out = {}

    def update(tag, name, layer=None):
        pick = (lambda t: t) if layer is None else (lambda t: t[layer])
        parts = exchange_wait(pending[tag], 0, grad_x)
        two_d = lambda t: pick(t).reshape(parts.shape[1:])
        res = adamw_reduce("adamw_" + tag, parts, two_d(a[name]), two_d(a['m_' + name]), two_d(a['v_' + name]))
        return [r.reshape(pick(a[name]).shape) for r in res]

    per_layer = {n: [None] * n_layers for n in ('w_out', 'w_mem_kv')}
    per_layer['w_out'][1] = update("out1", 'w_out', 1)
    per_layer['w_mem_kv'][1] = update("mem_kv1", 'w_mem_kv', 1)
    out['w_in_b'] = [r[None] for r in update("in_b", 'w_in_b', 0)]
    out['w_kv'] = update("kv", 'w_kv')
    per_layer['w_out'][0] = update("out0", 'w_out', 0)
    per_layer['w_mem_kv'][0] = update("mem_kv0", 'w_mem_kv', 0)
    out['w_glu'] = [r[None] for r in update("glu", 'w_glu', 0)]
    for name in per_layer:
        out[name] = [jnp.stack(t) for t in zip(*per_layer[name])]

    zeros = [jnp.zeros(s, F32) for s in full_shapes[len(REPLICATED):]]
    packed = lambda pre: _pack([a[pre + n] for n in REPLICATED] + zeros, rows)
    res = [_unpack(r, full_shapes) for r in adamw_reduce("adamw_small", exchange_wait(early, 0, grad_x), packed(''), packed('m_'), packed('v_'))]
    for i, n in enumerate(REPLICATED):
        out[n] = [r[i] for r in res]
    g_full = res[0][len(REPLICATED):]
    g_shard = [lax.dynamic_slice_in_dim(g_full[0], me * d_skip.shape[1], d_skip.shape[1], 1),
               lax.dynamic_slice_in_dim(g_full[1], me * b_glu.shape[1], b_glu.shape[1], 1),
               lax.dynamic_slice_in_dim(g_full[2], me * w_fgate.shape[0], w_fgate.shape[0], 0)]
    packed = lambda pre: _pack([a[pre + n] for n in SMALL_SHARDED], small_rows)
    res = [_unpack(r, small_shapes) for r in adamw_reduce("adamw_small_sharded", _pack(g_shard, small_rows)[None], packed(''), packed('m_'), packed('v_'))]
    for i, n in enumerate(SMALL_SHARDED):
        out[n] = [r[i] for r in res]
    packed = lambda pre: _pack([a[pre + 'pre_norm_g'][0], a[pre + 'mem_norm_g'][0], a[pre + 'mem_norm_g'][1]], late_rows)
    res = [_unpack(r, [pre_norm_g.shape[1:]] * 3) for r in adamw_reduce("adamw_small_late", exchange_wait(late, 0, grad_x), packed(''), packed('m_'), packed('v_'))]
    out['pre_norm_g'] = [jnp.stack([r[0], o[1]]) for r, o in zip(res, out['pre_norm_g'])]
    out['mem_norm_g'] = [jnp.stack([r[1], r[2]]) for r in res]
    out['w_in_a'] = [r[None] for r in update("in_a", 'w_in_a', 0)]

    return (loss, grad_x[None], *[out[n][k] for k in range(4) for n in WEIGHTS])
```

```python
import functools
import math

import jax
import jax.numpy as jnp
from jax import lax
from jax.experimental import pallas as pl
from jax.experimental.pallas import tpu as pltpu

F32 = jnp.float32
BF16 = jnp.bfloat16
HP = lax.Precision.HIGHEST
MESH_AXES = ("x", "y", "c")
N_DEV = 8
V7X_VMEM_LIMIT = 56 * 1024 * 1024
LANE = 128

EPS = 1e-6
HEAD_DIM = 128
SSM_GROUP = 16
SSM_STATE = 64
SSM_T = 8
SSM_LO = 16
SSM_GB = 12
FOX_BQ = 256
GRAD_DTYPE = BF16
ACT_DTYPE = BF16
ADAM_LR = 0.001
ADAM_B1 = 0.9
ADAM_B2 = 0.999
ADAM_EPS = 1e-08
ADAM_WD = 0.01
ADAM_STEP = 10

WEIGHTS = ['pre_norm_g', 'post_norm_g', 'w_in_a', 'lam_re', 'lam_im', 'log_step', 'b_re', 'b_im', 'c_re', 'c_im', 'd_skip',
           'w_glu', 'b_glu', 'kv_norm_g', 'w_kv', 'w_fgate', 'b_fgate', 'w_in_b', 'mem_norm_g', 'w_mem_kv', 'w_out']
INPUTS = ['x', 'mem'] + WEIGHTS + ['loss_target'] + ['m_' + n for n in WEIGHTS] + ['v_' + n for n in WEIGHTS]
REPLICATED = ['pre_norm_g', 'post_norm_g', 'lam_re', 'lam_im', 'log_step', 'b_re', 'b_im', 'c_re', 'c_im', 'kv_norm_g',
              'b_fgate', 'mem_norm_g']
SMALL_SHARDED = ['d_skip', 'b_glu', 'w_fgate']


def _cparams(sem=None):
    return pltpu.CompilerParams(dimension_semantics=sem, vmem_limit_bytes=V7X_VMEM_LIMIT)


def _tile(n, cap):
    if n <= cap:
        return n
    best = None
    for t in range(LANE, cap + 1, LANE):
        if n % t == 0:
            best = t
    assert best is not None, (n, cap)
    return best


def _matmul(name, a, b, a_spec, b_spec, o_spec, out_shape, grid, dims, nk, after=None):
    def body(a_ref, b_ref, *rest):
        o_ref, acc_ref = rest[-2:]
        k = pl.program_id(2)
        part = lax.dot_general(a_ref[...].astype(BF16), b_ref[...].astype(BF16), dims, preferred_element_type=F32)

        @pl.when(k == 0)
        def _():
            acc_ref[...] = part

        @pl.when(k > 0)
        def _():
            acc_ref[...] += part

        @pl.when(k == nk - 1)
        def _():
            o_ref[...] = acc_ref[...].astype(o_ref.dtype)

    acc_shape = tuple(d for d in o_spec.block_shape if d is not None)
    extra = [] if after is None else [after]
    return pl.pallas_call(
        body, name=name, grid=grid, in_specs=[a_spec, b_spec] + [pl.BlockSpec(memory_space=pl.ANY)] * len(extra), out_specs=o_spec,
        out_shape=out_shape, scratch_shapes=[pltpu.VMEM(acc_shape, F32)],
        compiler_params=_cparams(("parallel", "parallel", "arbitrary")),
    )(a, b, *extra)


NN = (((1,), (0,)), ((), ()))
NT = (((1,), (1,)), ((), ()))
TN = (((0,), (0,)), ((), ()))


def _mm_nn(name, a, b, out_dtype=ACT_DTYPE):
    (m, k), (_, n) = a.shape, b.shape
    bm, bn, bk = _tile(m, 1024), _tile(n, 512), _tile(k, 2048)
    return _matmul(name, a, b, pl.BlockSpec((bm, bk), lambda i, j, kk: (i, kk)), pl.BlockSpec((bk, bn), lambda i, j, kk: (kk, j)),
                   pl.BlockSpec((bm, bn), lambda i, j, kk: (i, j)), jax.ShapeDtypeStruct((m, n), out_dtype), (m // bm, n // bn, k // bk), NN, k // bk)


def _mm_nt(name, a, b, after=None, out_dtype=ACT_DTYPE):
    (m, c), (n, _) = a.shape, b.shape
    bm, bn, bk = _tile(m, 1024), _tile(n, 512), _tile(c, 2048)
    return _matmul(name, a, b, pl.BlockSpec((bm, bk), lambda i, j, kk: (i, kk)), pl.BlockSpec((bn, bk), lambda i, j, kk: (j, kk)),
                   pl.BlockSpec((bm, bn), lambda i, j, kk: (i, j)), jax.ShapeDtypeStruct((m, n), out_dtype), (m // bm, n // bn, c // bk), NT, c // bk, after)


def _mm_tn(name, a, b, out_dtype=F32, after=None):
    (c, m), (_, n) = a.shape, b.shape
    bm, bn, bk = _tile(m, 1024), _tile(n, 512), _tile(c, 2048)
    return _matmul(name, a, b, pl.BlockSpec((bk, bm), lambda i, j, kk: (kk, i)), pl.BlockSpec((bk, bn), lambda i, j, kk: (kk, j)),
                   pl.BlockSpec((bm, bn), lambda i, j, kk: (i, j)), jax.ShapeDtypeStruct((m, n), out_dtype), (m // bm, n // bn, c // bk), TN, c // bk, after)


def _mm_nn_cb(name, a, bb):
    (m, k), (nb, _, ns) = a.shape, bb.shape
    bm, bk = _tile(m, 1024), _tile(k, 2048)
    return _matmul(name, a, bb, pl.BlockSpec((bm, bk), lambda i, j, kk: (i, kk)), pl.BlockSpec((None, bk, ns), lambda i, j, kk: (j, kk, 0)),
                   pl.BlockSpec((bm, ns), lambda i, j, kk: (i, j)), jax.ShapeDtypeStruct((m, nb * ns), ACT_DTYPE), (m // bm, nb, k // bk), NN, k // bk)


def _mm_nt_cb(name, dy, bb, after=None):
    m, (nb, k, ns) = dy.shape[0], bb.shape
    bm, bn = _tile(m, 1024), _tile(k, 512)

    def body(dy_ref, b_ref, *rest):
        o_ref = rest[-1]
        acc = lax.dot_general(dy_ref[:, 0:ns], b_ref[0], NT, preferred_element_type=F32)
        for j in range(1, nb):
            acc += lax.dot_general(dy_ref[:, j * ns:(j + 1) * ns], b_ref[j], NT, preferred_element_type=F32)
        o_ref[...] = acc.astype(o_ref.dtype)

    extra = [] if after is None else [after]
    return pl.pallas_call(
        body, name=name, grid=(m // bm, k // bn),
        in_specs=[pl.BlockSpec((bm, nb * ns), lambda i, j: (i, 0)), pl.BlockSpec((nb, bn, ns), lambda i, j: (0, j, 0))]
        + [pl.BlockSpec(memory_space=pl.ANY)] * len(extra),
        out_specs=pl.BlockSpec((bm, bn), lambda i, j: (i, j)), out_shape=jax.ShapeDtypeStruct((m, k), ACT_DTYPE),
        compiler_params=_cparams(("parallel", "parallel")),
    )(dy, bb, *extra)


def _mm_tn_cb(name, a, dy, ns, out_dtype=F32, after=None):
    (c, k), nb = a.shape, dy.shape[1] // ns
    bm = _tile(k, 1024)
    return _matmul(name, a, dy, pl.BlockSpec((c, bm), lambda i, j, kk: (0, i)), pl.BlockSpec((c, ns), lambda i, j, kk: (0, j)),
                   pl.BlockSpec((None, bm, ns), lambda i, j, kk: (j, i, 0)), jax.ShapeDtypeStruct((nb, k, ns), out_dtype), (k // bm, nb, 1), TN, 1, after)


def make_mm(name):
    @jax.custom_vjp
    def mm(a, b):
        return _mm_nn(name, a, b, F32)

    def fwd(a, b):
        return mm(a, b), (a, b)

    def bwd(res, dy):
        a, b = res
        return _mm_nt(name + "_da", dy, b, out_dtype=a.dtype), _mm_tn(name + "_dw", a, dy, b.dtype)

    mm.defvjp(fwd, bwd)
    return mm


def _rowop_specs(rows, params, bm, nl):
    row_specs = [pl.BlockSpec((bm, r.shape[1] // nl), lambda j, i: (i, j)) for r in rows]
    par_specs = [pl.BlockSpec((p.shape[0], p.shape[1] // nl), lambda j, i: (0, j)) for p in params]
    row_blk = [jax.ShapeDtypeStruct((bm, r.shape[1] // nl), r.dtype) for r in rows]
    par_blk = [jax.ShapeDtypeStruct((p.shape[0], p.shape[1] // nl), p.dtype) for p in params]
    return row_specs, par_specs, row_blk, par_blk


def make_rowop(f, name, bm, nl=1, out_dtypes=None):
    def loaded(refs):
        return [r[...].astype(F32) for r in refs]

    def fwd_call(rows, params):
        n_rows = rows[0].shape[0]
        b = min(bm, n_rows)
        row_specs, par_specs, row_blk, par_blk = _rowop_specs(rows, params, b, nl)
        out_blk = jax.eval_shape(f, *[jax.ShapeDtypeStruct(t.shape, F32) for t in row_blk + par_blk])
        dts = out_dtypes or [F32] * len(out_blk)
        nr, npar = len(rows), len(params)

        def body(*refs):
            for o_ref, o in zip(refs[nr + npar:], f(*loaded(refs[:nr + npar]))):
                o_ref[...] = o.astype(o_ref.dtype)

        return pl.pallas_call(
            body, name=name, grid=(nl, n_rows // b), in_specs=row_specs + par_specs,
            out_specs=[pl.BlockSpec(o.shape, lambda j, i: (i, j)) for o in out_blk],
            out_shape=[jax.ShapeDtypeStruct((n_rows, o.shape[1] * nl), dt) for o, dt in zip(out_blk, dts)],
            compiler_params=_cparams(("parallel", "parallel")),
        )(*rows, *params)

    def bwd_call(rows, params, cts):
        n_rows = rows[0].shape[0]
        b = min(bm, n_rows)
        row_specs, par_specs, row_blk, par_blk = _rowop_specs(rows, params, b, nl)
        ct_specs = [pl.BlockSpec((b, c.shape[1] // nl), lambda j, i: (i, j)) for c in cts]
        nr, npar, nct = len(rows), len(params), len(cts)

        def body(*refs):
            i = pl.program_id(1)
            _, vjp = jax.vjp(lambda *v: tuple(f(*v)), *loaded(refs[:nr + npar]))
            grads = vjp(tuple(loaded(refs[nr + npar:nr + npar + nct])))
            outs = refs[nr + npar + nct:]
            for o_ref, g in zip(outs[:nr], grads[:nr]):
                o_ref[...] = g.astype(o_ref.dtype)
            for o_ref, g in zip(outs[nr:], grads[nr:]):
                @pl.when(i == 0)
                def _(o_ref=o_ref, g=g):
                    o_ref[...] = g

                @pl.when(i > 0)
                def _(o_ref=o_ref, g=g):
                    o_ref[...] += g

        outs = pl.pallas_call(
            body, name=name + "_bwd", grid=(nl, n_rows // b), in_specs=row_specs + par_specs + ct_specs,
            out_specs=row_specs + par_specs,
            out_shape=[jax.ShapeDtypeStruct(a.shape, a.dtype) for a in rows] + [jax.ShapeDtypeStruct(a.shape, F32) for a in params],
            compiler_params=_cparams(("arbitrary", "arbitrary")),
        )(*rows, *params, *cts)
        return tuple(outs[:nr]), tuple(g.astype(a.dtype) for g, a in zip(outs[nr:], params))

    @jax.custom_vjp
    def op(rows, params):
        return tuple(fwd_call(rows, params))

    def fwd(rows, params):
        return op(rows, params), (rows, params)

    def bwd(res, cts):
        rows, params = res
        return bwd_call(rows, params, tuple(cts))

    op.defvjp(fwd, bwd)
    return op


def make_groupop(f, name, gb):
    def specs(arrs):
        return [pl.BlockSpec((gb,) + a.shape[1:], lambda g: (g, 0, 0)) for a in arrs]

    def fwd_call(arrs):
        g_n = arrs[0].shape[0]
        out_blk = jax.eval_shape(f, *[jax.ShapeDtypeStruct((gb,) + a.shape[1:], a.dtype) for a in arrs])[0]
        n = len(arrs)

        def body(*refs):
            for i, o in enumerate(f(*[r[...] for r in refs[:n]])):
                refs[n][i] = o.astype(refs[n].dtype)

        return pl.pallas_call(
            body, name=name, grid=(g_n // gb,), in_specs=specs(arrs),
            out_specs=pl.BlockSpec((gb,) + out_blk.shape, lambda g: (g, 0, 0)),
            out_shape=jax.ShapeDtypeStruct((g_n,) + out_blk.shape, ACT_DTYPE),
            compiler_params=_cparams(("parallel",)),
        )(*arrs)

    def bwd_call(arrs, ct):
        g_n = arrs[0].shape[0]
        n = len(arrs)

        def body(*refs):
            _, vjp = jax.vjp(lambda *v: tuple(f(*v)), *[r[...] for r in refs[:n]])
            for o_ref, g in zip(refs[n + 1:], vjp(tuple(refs[n][i].astype(F32) for i in range(gb)))):
                o_ref[...] = g.astype(o_ref.dtype)

        return pl.pallas_call(
            body, name=name + "_bwd", grid=(g_n // gb,), in_specs=specs(arrs) + specs([ct]), out_specs=specs(arrs),
            out_shape=[jax.ShapeDtypeStruct(a.shape, a.dtype) for a in arrs],
            compiler_params=_cparams(("parallel",)),
        )(*arrs, ct)

    @jax.custom_vjp
    def op(*arrs):
        return fwd_call(arrs)

    def fwd(*arrs):
        return op(*arrs), arrs

    def bwd(arrs, ct):
        return tuple(bwd_call(arrs, ct))

    op.defvjp(fwd, bwd)
    return op


@jax.custom_vjp
def _rms(x, g):
    return x * lax.rsqrt(jnp.mean(x * x, axis=-1, keepdims=True) + EPS) * g


def _rms_fwd(x, g):
    r = lax.rsqrt(jnp.mean(x * x, axis=-1, keepdims=True) + EPS)
    return x * r * g, (x, g, r)


def _rms_bwd(res, dy):
    x, g, r = res
    n, t = x * r, dy * g
    return r * (t - n * jnp.mean(n * t, axis=-1, keepdims=True)), jnp.sum(dy * n, axis=0, keepdims=True)


_rms.defvjp(_rms_fwd, _rms_bwd)


def _silu(z):
    return z * jax.nn.sigmoid(z)


def _log_sigmoid(x):
    return jnp.minimum(x, 0.0) - jnp.log(1.0 + jnp.exp(-jnp.abs(x)))


def f_norm(x, g):
    return (_rms(x, g),)


def f_gate_a1(y, u, d):
    return (jax.nn.gelu(y + d * u),)


def f_gate_a2(yg, t, z, b):
    return (yg * jax.nn.sigmoid(t + b) * _silu(z),)


def f_gate_b(att, z):
    return (att * _silu(z),)


def f_post_a(h, o, post_g, kv_g, pre_g):
    h1 = h + _rms(o, post_g)
    return h1, _rms(h1, kv_g), _rms(h1, pre_g)


def f_final(h, o, tgt, post_g):
    err = h + _rms(o, post_g) - tgt
    return (0.5 * jnp.mean(err * err, axis=-1, keepdims=True),)


def f_logf(gl, b):
    return (_log_sigmoid(gl + b),)


def _mxu(a, b, dims):
    return lax.dot_general(a.astype(BF16), b.astype(BF16), dims, preferred_element_type=F32)


@jax.custom_vjp
def _mxu_nn(a, b):
    return _mxu(a, b, NN)


def _mxu_nn_bwd(res, g):
    a, b = res
    return _mxu(g, b, NT).astype(a.dtype), _mxu(a, g, TN).astype(b.dtype)


_mxu_nn.defvjp(lambda a, b: (_mxu(a, b, NN), (a, b)), _mxu_nn_bwd)


@jax.custom_vjp
def _mxu_nt(a, b):
    return _mxu(a, b, NT)


def _mxu_nt_bwd(res, g):
    a, b = res
    return _mxu(g, b, NN).astype(a.dtype), _mxu(g, a, TN).astype(b.dtype)


_mxu_nt.defvjp(lambda a, b: (_mxu(a, b, NT), (a, b)), _mxu_nt_bwd)


def f_memattn(q, zm, km, vm):
    o = _mxu_nn(_softmax(_mxu_nt(q * (HEAD_DIM ** -0.5), km)), vm)
    return (o * _silu(zm),)


@jax.custom_vjp
def _softmax(s):
    e = jnp.exp(s - jnp.max(s, axis=-1, keepdims=True))
    return e * (1.0 / jnp.sum(e, axis=-1, keepdims=True))


def _softmax_fwd(s):
    p = _softmax(s)
    return p, p


def _softmax_bwd(p, dp):
    return (p * (dp - jnp.sum(dp * p, axis=-1, keepdims=True)),)


_softmax.defvjp(_softmax_fwd, _softmax_bwd)


def _fox_block(q, k, v, fr):
    bq = q.shape[0]
    s = _mxu_nt(q * (HEAD_DIM ** -0.5), k) - fr
    tri = lax.broadcasted_iota(jnp.int32, (bq, bq), 0) >= lax.broadcasted_iota(jnp.int32, (bq, bq), 1)
    diag = jnp.where(tri, s[:, -bq:], -1e30)
    s = diag if k.shape[0] == bq else jnp.concatenate([s[:, :-bq], diag], axis=1)
    return _mxu_nn(_softmax(s), v)


def _cmul(ar, ai, xr, xi):
    return ar * xr - ai * xi, ar * xi + ai * xr


def _hp_dot(a, b):
    return jnp.dot(a, b, precision=HP, preferred_element_type=F32)


_bf_dot = _mxu_nn


def f_s5(u, toep, win_r, win_i, wout_r, wout_i, coef):
    gb = u.shape[0]
    n_hi = u.shape[1] // SSM_LO
    sr = [_bf_dot(u[i], win_r[i]) for i in range(gb)]
    si = [_bf_dot(u[i], win_i[i]) for i in range(gb)]

    def stacked(parts, lo):
        return jnp.concatenate([a[lo * n_hi:(lo + 1) * n_hi] for a in parts], axis=0)

    def coef_rows(k):
        return jnp.concatenate([jnp.broadcast_to(coef[i, k:k + 1, :], (n_hi, coef.shape[2])) for i in range(gb)], axis=0)

    at_r, at_i = coef_rows(0), coef_rows(1)
    pr, pi = stacked(sr, 0), stacked(si, 0)
    for lo in range(1, SSM_LO):
        dr, di = _cmul(at_r, at_i, pr, pi)
        pr, pi = stacked(sr, lo) + dr, stacked(si, lo) + di
    n = 2 * gb * n_hi
    ri = lax.broadcasted_iota(jnp.int32, (n, n), 0)
    ci = lax.broadcasted_iota(jnp.int32, (n, n), 1)
    same = ri // n_hi == ci // n_hi

    def shifted(d, zr, zi):
        z = _hp_dot((same & (ri - ci == d)).astype(F32), jnp.concatenate([zr, zi], axis=0))
        return z[:n // 2], z[n // 2:]

    d, step = 1, 0
    while d < n_hi:
        dr, di = _cmul(coef_rows(2 + 2 * step), coef_rows(3 + 2 * step), *shifted(d, pr, pi))
        pr, pi = pr + dr, pi + di
        d, step = 2 * d, step + 1
    er, ei = shifted(1, pr, pi)
    xr, xi = [er], [ei]
    for lo in range(1, SSM_LO):
        dr, di = _cmul(at_r, at_i, xr[-1], xi[-1])
        xr.append(stacked(sr, lo - 1) + dr)
        xi.append(stacked(si, lo - 1) + di)

    def group_rows(parts, i):
        return jnp.concatenate([a[i * n_hi:(i + 1) * n_hi] for a in parts], axis=0)

    return tuple(_bf_dot(u[i], toep[i]) + _bf_dot(group_rows(xr, i), wout_r[i]) + _bf_dot(group_rows(xi, i), wout_i[i])
                 for i in range(gb))


def s5_operators(lam_re, lam_im, log_step, b_re, b_im, c_re, c_im, n_hi):
    t_n = SSM_T
    lr, li = lam_re, lam_im
    dt = jnp.exp(log_step)[:, None]
    mag = jnp.exp(lr * dt)
    ar, ai = mag * jnp.cos(li * dt), mag * jnp.sin(li * dt)
    den = lr * lr + li * li
    cr = ((ar - 1.0) * lr + ai * li) / den
    ci = (ai * lr - (ar - 1.0) * li) / den
    bbr = cr[..., None] * b_re - ci[..., None] * b_im
    bbi = cr[..., None] * b_im + ci[..., None] * b_re
    k = jnp.arange(t_n + 1, dtype=F32)[:, None, None]
    pm, ang = jnp.exp(k * (lr * dt)), k * (li * dt)
    pr, pi = pm * jnp.cos(ang), pm * jnp.sin(ang)
    abr = pr[..., None] * bbr - pi[..., None] * bbi
    abi = pr[..., None] * bbi + pi[..., None] * bbr
    kk = (jnp.einsum('ghp,kgpj->kghj', c_re, abr[:t_n], precision=HP)
          - jnp.einsum('ghp,kgpj->kghj', c_im, abi[:t_n], precision=HP))
    lag = jnp.arange(t_n)[None, :] - jnp.arange(t_n)[:, None]
    onehot = (lag[None] == jnp.arange(t_n)[:, None, None]).astype(F32)
    g_n, h_n = c_re.shape[0], c_re.shape[1]
    toep = jnp.einsum('kst,kghj->gsjth', onehot, kk, precision=HP).reshape(g_n, t_n * h_n, t_n * h_n)
    win_r = abr[:t_n][::-1].transpose(1, 0, 3, 2).reshape(g_n, t_n * h_n, -1)
    win_i = abi[:t_n][::-1].transpose(1, 0, 3, 2).reshape(g_n, t_n * h_n, -1)
    p1r, p1i = pr[1:, :, None, :], pi[1:, :, None, :]
    wout_r = (c_re[None] * p1r - c_im[None] * p1i).transpose(1, 3, 0, 2).reshape(g_n, -1, t_n * h_n)
    wout_i = (-(c_re[None] * p1i + c_im[None] * p1r)).transpose(1, 3, 0, 2).reshape(g_n, -1, t_n * h_n)
    rows = [pr[t_n], pi[t_n]]
    qr, qi = pr[t_n], pi[t_n]
    for _ in range(int(math.log2(SSM_LO))):
        qr, qi = qr * qr - qi * qi, 2.0 * qr * qi
    d = 1
    while d < n_hi:
        rows += [qr, qi]
        qr, qi = qr * qr - qi * qi, 2.0 * qr * qi
        d *= 2
    coef = jnp.stack(rows, axis=1)
    return toep, win_r, win_i, wout_r, wout_i, coef


def _cumsum_call(name, x, reverse):
    n_rows, w = x.shape
    bm = min(256, n_rows)
    nb = n_rows // bm

    def body(x_ref, o_ref, carry_ref):
        i = pl.program_id(0)

        @pl.when(i == 0)
        def _():
            carry_ref[...] = jnp.zeros_like(carry_ref)

        ri = lax.broadcasted_iota(jnp.int32, (bm, bm), 0)
        ci = lax.broadcasted_iota(jnp.int32, (bm, bm), 1)
        tri = ((ri <= ci) if reverse else (ri >= ci)).astype(F32)
        xb = x_ref[...]
        o_ref[...] = _hp_dot(tri, xb) + carry_ref[...]
        carry_ref[...] += jnp.sum(xb, axis=0, keepdims=True)

    idx = (lambda i: (nb - 1 - i, 0)) if reverse else (lambda i: (i, 0))
    return pl.pallas_call(
        body, name=name, grid=(nb,), in_specs=[pl.BlockSpec((bm, w), idx)], out_specs=pl.BlockSpec((bm, w), idx),
        out_shape=jax.ShapeDtypeStruct(x.shape, F32), scratch_shapes=[pltpu.VMEM((1, w), F32)],
        compiler_params=_cparams(("arbitrary",)),
    )(x)


def make_cumsum(name):
    @jax.custom_vjp
    def cs(x):
        return _cumsum_call(name, x, False)

    def fwd(x):
        return cs(x), None

    def bwd(_, dy):
        return (_cumsum_call(name + "_bwd", dy, True),)

    cs.defvjp(fwd, bwd)
    return cs


def _fox_specs(n_rows, bq):
    q_spec = pl.BlockSpec((bq, HEAD_DIM), lambda h, i: (i, h))
    kv_spec = pl.BlockSpec((n_rows, HEAD_DIM), lambda h, i: (0, h))
    fr_spec = pl.BlockSpec((None, 1, n_rows), lambda h, i: (h, 0, 0))
    return q_spec, kv_spec, fr_spec


def _fox_fwd_call(name, q, k, v, fr):
    n_rows, width = q.shape
    bq = min(FOX_BQ, n_rows)
    nq = n_rows // bq
    q_spec, kv_spec, fr_spec = _fox_specs(n_rows, bq)

    def body(q_ref, k_ref, v_ref, fr_ref, o_ref):
        i = pl.program_id(1)
        for p in range(nq):
            n_keys = (p + 1) * bq

            @pl.when(i == p)
            def _(n_keys=n_keys):
                o = _fox_block(q_ref[...].astype(F32), k_ref[:n_keys, :], v_ref[:n_keys, :], fr_ref[:, :n_keys])
                o_ref[...] = o.astype(o_ref.dtype)

    return pl.pallas_call(
        body, name=name, grid=(width // HEAD_DIM, nq), in_specs=[q_spec, kv_spec, kv_spec, fr_spec], out_specs=q_spec,
        out_shape=jax.ShapeDtypeStruct(q.shape, ACT_DTYPE), compiler_params=_cparams(("parallel", "parallel")),
    )(q, k, v, fr)


def _fox_bwd_call(name, q, k, v, fr, do):
    n_rows, width = q.shape
    bq = min(FOX_BQ, n_rows)
    nq = n_rows // bq
    q_spec, kv_spec, fr_spec = _fox_specs(n_rows, bq)

    def body(q_ref, k_ref, v_ref, fr_ref, do_ref, dq_ref, dk_ref, dv_ref, dfr_ref, dk_acc, dv_acc):
        i = pl.program_id(1)

        @pl.when(i == 0)
        def _():
            dk_acc[...] = jnp.zeros_like(dk_acc)
            dv_acc[...] = jnp.zeros_like(dv_acc)
            dfr_ref[...] = jnp.zeros_like(dfr_ref)

        for p in range(nq):
            n_keys = (p + 1) * bq

            @pl.when(i == p)
            def _(n_keys=n_keys):
                _, vjp = jax.vjp(_fox_block, q_ref[...].astype(F32), k_ref[:n_keys, :].astype(F32), v_ref[:n_keys, :].astype(F32),
                                 fr_ref[:, :n_keys])
                dq, dk, dv, dfr = vjp(do_ref[...].astype(F32))
                dq_ref[...] = dq.astype(dq_ref.dtype)
                dk_acc[:n_keys, :] += dk
                dv_acc[:n_keys, :] += dv
                dfr_ref[:, :n_keys] += dfr

        @pl.when(i == nq - 1)
        def _():
            dk_ref[...] = dk_acc[...].astype(dk_ref.dtype)
            dv_ref[...] = dv_acc[...].astype(dv_ref.dtype)

    return pl.pallas_call(
        body, name=name, grid=(width // HEAD_DIM, nq), in_specs=[q_spec, kv_spec, kv_spec, fr_spec, q_spec],
        out_specs=[q_spec, kv_spec, kv_spec, fr_spec], out_shape=[jax.ShapeDtypeStruct(a.shape, a.dtype) for a in (q, k, v, fr)],
        scratch_shapes=[pltpu.VMEM((n_rows, HEAD_DIM), F32), pltpu.VMEM((n_rows, HEAD_DIM), F32)],
        compiler_params=_cparams(("parallel", "arbitrary")),
    )(q, k, v, fr, do)


def make_fox(name):
    @jax.custom_vjp
    def fox(q, k, v, fr):
        return _fox_fwd_call(name, q, k, v, fr)

    def fwd(q, k, v, fr):
        return fox(q, k, v, fr), (q, k, v, fr)

    def bwd(res, do):
        return tuple(_fox_bwd_call(name + "_bwd", *res, do))

    fox.defvjp(fwd, bwd)
    return fox


def _relations(x, y, c):
    out = []
    for rel in range(1, N_DEV):
        px = 1 - x if rel & 4 else x
        py = 1 - y if rel & 2 else y
        pc = 1 - c if rel & 1 else c
        out.append((rel, (px, py, pc), 4 * px + 2 * py + pc))
    return out


_HBM_SPEC = pl.BlockSpec(memory_space=pltpu.HBM)
_SEM_SPEC = pl.BlockSpec(memory_space=pltpu.SEMAPHORE)
_DATAFLOW = pltpu.SideEffectType.DATAFLOW_SIDE_EFFECTING


def exchange_start(name, srcs, gather):
    n = len(srcs)
    me = 4 * lax.axis_index("x") + 2 * lax.axis_index("y") + lax.axis_index("c")
    lands = []
    for s in srcs:
        own = s if gather else lax.dynamic_index_in_dim(s, me, 0, keepdims=False)
        land = lax.empty((N_DEV,) + own.shape, s.dtype)
        lands.append(lax.dynamic_update_index_in_dim(land, own, me, 0))

    def body(*refs):
        src_refs, land_refs = refs[:n], refs[n:2 * n]
        send_sems, recv_sems, token = refs[2 * n], refs[2 * n + 1], refs[-1]
        x, y, c = lax.axis_index("x"), lax.axis_index("y"), lax.axis_index("c")
        mine = 4 * x + 2 * y + c
        for k in range(n):
            for rel, peer, peer_blk in _relations(x, y, c):
                pltpu.make_async_remote_copy(
                    src_ref=src_refs[k] if gather else src_refs[k].at[peer_blk], dst_ref=land_refs[k].at[mine],
                    send_sem=send_sems.at[7 * k + rel - 1], recv_sem=recv_sems.at[7 * k + rel - 1],
                    device_id=peer, device_id_type=pl.DeviceIdType.MESH).start()
        token[...] = jnp.zeros_like(token)

    sem = pltpu.SemaphoreType.DMA((7 * n,))
    hbm = lambda t: pltpu.HBM(t.shape, t.dtype)
    outs = pl.pallas_call(
        body, name=name,
        out_shape=(sem, sem, *[hbm(s) for s in srcs], *[hbm(l) for l in lands], jax.ShapeDtypeStruct((8, LANE), F32)),
        in_specs=[_HBM_SPEC] * (2 * n),
        out_specs=(_SEM_SPEC, _SEM_SPEC, *[_HBM_SPEC] * (2 * n), pl.BlockSpec(memory_space=pltpu.VMEM)),
        input_output_aliases={i: 2 + i for i in range(2 * n)},
        compiler_params=pltpu.CompilerParams(has_side_effects=_DATAFLOW),
    )(*[pltpu.with_memory_space_constraint(t, pltpu.HBM) for t in list(srcs) + lands])
    return dict(name=name, gather=gather, send=outs[0], recv=outs[1], srcs=list(outs[2:2 + n]), lands=list(outs[2 + n:2 + 2 * n]), token=outs[-1])


def exchange_wait(handle, k, after):
    gather = handle['gather']

    def body(src_ref, land_ref, send_sems, recv_sems, after_ref, src_out, land_out):
        x, y, c = lax.axis_index("x"), lax.axis_index("y"), lax.axis_index("c")
        for rel, peer, peer_blk in _relations(x, y, c):
            copy = pltpu.make_async_remote_copy(
                src_ref=src_ref if gather else src_ref.at[peer_blk], dst_ref=land_ref.at[peer_blk],
                send_sem=send_sems.at[7 * k + rel - 1], recv_sem=recv_sems.at[7 * k + rel - 1],
                device_id=peer, device_id_type=pl.DeviceIdType.MESH)
            copy.wait_send()
            copy.wait_recv()

    src, land = handle['srcs'][k], handle['lands'][k]
    return pl.pallas_call(
        body, name=f"{handle['name']}_wait{k}", out_shape=(pltpu.HBM(src.shape, src.dtype), pltpu.HBM(land.shape, land.dtype)),
        in_specs=[_HBM_SPEC, _HBM_SPEC, _SEM_SPEC, _SEM_SPEC, pl.BlockSpec(memory_space=pl.ANY)], out_specs=(_HBM_SPEC, _HBM_SPEC),
        input_output_aliases={0: 0, 1: 1}, compiler_params=pltpu.CompilerParams(has_side_effects=_DATAFLOW),
    )(src, land, handle['send'], handle['recv'], after)[1]


def adamw_reduce(name, partials, w, m, v):
    n_part, n_rows, n_cols = partials.shape
    br = n_rows
    for cand in (512, 256, 128, 64, 32, 16, 8):
        if n_rows % cand == 0 and n_part * cand * n_cols * partials.dtype.itemsize <= (4 << 20):
            br = cand
            break

    def body(p_ref, w_ref, m_ref, v_ref, g_ref, d_ref, nm_ref, nv_ref):
        g = p_ref[0].astype(F32)
        for s in range(1, n_part):
            g = g + p_ref[s].astype(F32)
        m_new = ADAM_B1 * m_ref[...] + (1.0 - ADAM_B1) * g
        v_new = ADAM_B2 * v_ref[...] + (1.0 - ADAM_B2) * jnp.square(g)
        m_hat = m_new / (1.0 - ADAM_B1 ** ADAM_STEP)
        v_hat = v_new / (1.0 - ADAM_B2 ** ADAM_STEP)
        g_ref[...] = g
        d_ref[...] = -ADAM_LR * (m_hat / (jnp.sqrt(v_hat) + ADAM_EPS) + ADAM_WD * w_ref[...])
        nm_ref[...] = m_new
        nv_ref[...] = v_new

    spec = pl.BlockSpec((br, n_cols), lambda i: (i, 0))
    return pl.pallas_call(
        body, name=name, grid=(n_rows // br,), in_specs=[pl.BlockSpec((n_part, br, n_cols), lambda i: (0, i, 0)), spec, spec, spec],
        out_specs=[spec] * 4, out_shape=[jax.ShapeDtypeStruct((n_rows, n_cols), F32)] * 4, compiler_params=_cparams(("parallel",)),
    )(partials, w, m, v)


def _pack(arrays, n_rows):
    flat = jnp.concatenate([a.reshape(-1) for a in arrays])
    return jnp.pad(flat, (0, n_rows * LANE - flat.shape[0])).reshape(n_rows, LANE)


def _unpack(packed, shapes):
    flat, out, off = packed.reshape(-1), [], 0
    for s in shapes:
        n = math.prod(s)
        out.append(flat[off:off + n].reshape(s))
        off += n
    return out


def _packed_rows(shapes):
    n = sum(math.prod(s) for s in shapes)
    return -(-n // (LANE * 512)) * 512


GATHER_ORDER = ['small', 'in_a', 'mem_kv0', 'glu', 'out0', 'kv', 'in_b', 'mem_kv1', 'out1']


BLOCKED = ('in_a', 'kv', 'in_b')


def local_step(p, gathered, x, mem, tgt, start_early, start_late):
    n_rows, d_model = x.shape
    main_w = 3 * d_model // 4
    mem_w = d_model - main_w
    n_groups = main_w // SSM_GROUP
    n_heads = main_w // HEAD_DIM
    n_hi = n_rows // (SSM_T * SSM_LO)
    row = lambda a: a.reshape(1, -1)
    tape, pending = {}, {}

    def mm_fwd(name, act, after=None):
        w = exchange_wait(gathered, GATHER_ORDER.index(name), act if after is None else after)
        if name not in BLOCKED:
            w = w.reshape(-1, w.shape[2])
        tape[name] = (act, w)
        return (_mm_nn_cb if name in BLOCKED else _mm_nn)(name, act, w)

    def mm_dw(name, dy, after=None):
        act, w = tape[name]
        if name in BLOCKED:
            dw = _mm_tn_cb(name + "_dw", act, dy, w.shape[2], GRAD_DTYPE, after)
        else:
            dw = _mm_tn(name + "_dw", act, dy, GRAD_DTYPE, after)
            dw = dw.reshape(N_DEV, dw.shape[0] // N_DEV, dw.shape[1])
        pending[name] = exchange_start("rs_" + name, [dw], gather=False)

    def mm_da(name, dy, after=None):
        return (_mm_nt_cb if name in BLOCKED else _mm_nt)(name + "_da", dy, tape[name][1], after)

    def mm_bwd(name, dy):
        mm_dw(name, dy)
        return mm_da(name, dy, pending[name]['token'])

    def split(proj):
        return proj[:, :main_w], proj[:, main_w:2 * main_w], proj[:, 2 * main_w:2 * main_w + mem_w], proj[:, 2 * main_w + mem_w:]

    def mem_attn(i, qm, zm, kvm):
        memo, = make_rowop(f_memattn, f"mem_attn{i}", 512, mem_w // HEAD_DIM, [ACT_DTYPE])((qm, zm), (kvm[:, :mem_w], kvm[:, mem_w:]))
        return memo

    def seg_norms(x_, pre_g0, mem_g0, mem_g1):
        hn, = make_rowop(f_norm, "pre_norm0", 256, out_dtypes=[ACT_DTYPE])((x_,), (row(pre_g0),))
        memn0, = make_rowop(f_norm, "mem_norm0", 256, out_dtypes=[ACT_DTYPE])((mem,), (row(mem_g0),))
        memn1, = make_rowop(f_norm, "mem_norm1", 256, out_dtypes=[ACT_DTYPE])((mem,), (row(mem_g1),))
        return hn, memn0, memn1

    def seg_a1(u, qm, zm, kvm, d_skip, *ops):
        ug = u.astype(BF16).reshape(n_hi, SSM_LO, SSM_T, n_groups, SSM_GROUP).transpose(3, 1, 0, 2, 4).reshape(n_groups, n_hi * SSM_LO, SSM_T * SSM_GROUP)
        yg = make_groupop(f_s5, "s5", SSM_GB)(ug, *ops)
        y = yg.reshape(n_groups, SSM_LO, n_hi, SSM_T, SSM_GROUP).transpose(2, 1, 3, 0, 4).reshape(n_rows, main_w)
        ygelu, = make_rowop(f_gate_a1, "gate_a1", 256, out_dtypes=[ACT_DTYPE])((y, u), (row(d_skip),))
        return ygelu, mem_attn(0, qm, zm, kvm)

    def seg_a2(ygelu, t, z, memo, b_glu):
        main, = make_rowop(f_gate_a2, "gate_a2", 256, out_dtypes=[ACT_DTYPE])((ygelu, t, z), (row(b_glu),))
        return jnp.concatenate([main, memo], axis=1)

    def seg_post_a(x_, o, post_g0, kv_g, pre_g1):
        return make_rowop(f_post_a, "post_a", 128, out_dtypes=[F32, ACT_DTYPE, ACT_DTYPE])((x_, o), (row(post_g0), row(kv_g), row(pre_g1)))

    def seg_b(q, z, qm, zm, k, v, kv_in, kvm, w_fgate, b_fgate):
        w_fg = jnp.pad(w_fgate, ((0, 0), (0, LANE - n_heads)))
        b_fg = jnp.pad(b_fgate, (0, LANE - n_heads)).reshape(1, LANE)
        logf, = make_rowop(f_logf, "logf", 512)((make_mm("fgate")(kv_in, w_fg),), (b_fg,))
        fcum = make_cumsum("fcum")(logf)[:, :n_heads].T
        att = make_fox("fox")(q, k, v, fcum[:, None, :])
        main, = make_rowop(f_gate_b, "gate_b", 256, out_dtypes=[ACT_DTYPE])((att, z), ())
        return jnp.concatenate([main, mem_attn(1, qm, zm, kvm)], axis=1)

    def seg_final(h1, o, post_g1):
        rowloss, = make_rowop(f_final, "final", 128)((h1, o, tgt), (row(post_g1),))
        return jnp.sum(rowloss)

    (hn0, memn0, memn1), vjp_norms = jax.vjp(seg_norms, x, p['pre_norm_g'][0], p['mem_norm_g'][0], p['mem_norm_g'][1])
    s5_names = ('lam_re', 'lam_im', 'log_step', 'b_re', 'b_im', 'c_re', 'c_im')
    ops, vjp_ops = jax.vjp(lambda *a: s5_operators(*a, n_hi), *[p[n] for n in s5_names])
    proj_a = mm_fwd("in_a", hn0, ops[0])
    kvm0 = mm_fwd("mem_kv0", memn0, proj_a)
    u_a, z_a, qm_a, zm_a = split(proj_a)
    (ygelu, memo0), vjp_a1 = jax.vjp(seg_a1, u_a, qm_a, zm_a, kvm0, p['d_skip'], *ops)
    t = mm_fwd("glu", ygelu)
    cat0, vjp_a2 = jax.vjp(seg_a2, ygelu, t, z_a, memo0, p['b_glu'])
    o0 = mm_fwd("out0", cat0)
    (h1, kv_in, hn1), vjp_post_a = jax.vjp(seg_post_a, x, o0, p['post_norm_g'][0], p['kv_norm_g'], p['pre_norm_g'][1])
    kv = mm_fwd("kv", kv_in)
    proj_b = mm_fwd("in_b", hn1)
    kvm1 = mm_fwd("mem_kv1", memn1, proj_b)
    cat1, vjp_b = jax.vjp(seg_b, *split(proj_b), kv[:, :main_w], kv[:, main_w:], kv_in, kvm1, p['w_fgate'], p['b_fgate'])
    o1 = mm_fwd("out1", cat1)
    loss, vjp_final = jax.vjp(seg_final, h1, o1, p['post_norm_g'][1])

    g = {}
    d_h1, d_o1, g_post_g1 = vjp_final(jnp.ones((), F32))
    *d_proj_b, d_k, d_v, d_kv_in, d_kvm1, g['w_fgate'], g['b_fgate'] = vjp_b(mm_bwd("out1", d_o1))
    d_memn1 = mm_bwd("mem_kv1", d_kvm1)
    d_hn1 = mm_bwd("in_b", jnp.concatenate(d_proj_b, axis=1))
    d_kv_in = d_kv_in + mm_bwd("kv", jnp.concatenate([d_k, d_v], axis=1))
    d_x, d_o0, g_post_g0, g['kv_norm_g'], g_pre_g1 = vjp_post_a((d_h1, d_kv_in, d_hn1))
    d_ygelu, d_t, d_z, d_memo0, g['b_glu'] = vjp_a2(mm_bwd("out0", d_o0))
    d_ygelu = d_ygelu + mm_bwd("glu", d_t)
    d_u, d_qm, d_zm, d_kvm0, g['d_skip'], *d_ops = vjp_a1((d_ygelu, d_memo0))
    d_proj_a = jnp.concatenate([d_u, d_z, d_qm, d_zm], axis=1)
    g.update(zip(s5_names, vjp_ops(tuple(d_ops))))
    g['pre_norm_g'] = jnp.stack([jnp.zeros_like(g_pre_g1), g_pre_g1])
    g['post_norm_g'] = jnp.stack([g_post_g0, g_post_g1])
    g['mem_norm_g'] = jnp.zeros_like(p['mem_norm_g'])
    early = start_early(g)
    d_memn0 = mm_bwd("mem_kv0", d_kvm0)
    d_x2, *g_late = vjp_norms((mm_da("in_a", d_proj_a, early['token']), d_memn0, d_memn1))
    late = start_late(g_late)
    mm_dw("in_a", d_proj_a, late['token'])
    return loss, d_x + d_x2, pending, early, late


def kernel(x, mem, pre_norm_g, post_norm_g, w_in_a, lam_re, lam_im, log_step, b_re, b_im, c_re, c_im, d_skip, w_glu, b_glu, kv_norm_g, w_kv, w_fgate, b_fgate, w_in_b, mem_norm_g, w_mem_kv, w_out, loss_target, m_pre_norm_g, m_post_norm_g, m_w_in_a, m_lam_re, m_lam_im, m_log_step, m_b_re, m_b_im, m_c_re, m_c_im, m_d_skip, m_w_glu, m_b_glu, m_kv_norm_g, m_w_kv, m_w_fgate, m_b_fgate, m_w_in_b, m_mem_norm_g, m_w_mem_kv, m_w_out, v_pre_norm_g, v_post_norm_g, v_w_in_a, v_lam_re, v_lam_im, v_log_step, v_b_re, v_b_im, v_c_re, v_c_im, v_d_skip, v_w_glu, v_b_glu, v_kv_norm_g, v_w_kv, v_w_fgate, v_b_fgate, v_w_in_b, v_mem_norm_g, v_w_mem_kv, v_w_out):
    a = dict(zip(INPUTS, (x, mem, pre_norm_g, post_norm_g, w_in_a, lam_re, lam_im, log_step, b_re, b_im, c_re, c_im, d_skip, w_glu, b_glu, kv_norm_g, w_kv, w_fgate, b_fgate, w_in_b, mem_norm_g, w_mem_kv, w_out, loss_target, m_pre_norm_g, m_post_norm_g, m_w_in_a, m_lam_re, m_lam_im, m_log_step, m_b_re, m_b_im, m_c_re, m_c_im, m_d_skip, m_w_glu, m_b_glu, m_kv_norm_g, m_w_kv, m_w_fgate, m_b_fgate, m_w_in_b, m_mem_norm_g, m_w_mem_kv, m_w_out, v_pre_norm_g, v_post_norm_g, v_w_in_a, v_lam_re, v_lam_im, v_log_step, v_b_re, v_b_im, v_c_re, v_c_im, v_d_skip, v_w_glu, v_b_glu, v_kv_norm_g, v_w_kv, v_w_fgate, v_b_fgate, v_w_in_b, v_mem_norm_g, v_w_mem_kv, v_w_out)))
    me = 4 * lax.axis_index("x") + 2 * lax.axis_index("y") + lax.axis_index("c")
    n_layers = w_out.shape[0]

    small_shapes = [a[n].shape for n in SMALL_SHARDED]
    small_rows = -(-sum(math.prod(s) for s in small_shapes) // (LANE * 8)) * 8
    shards = dict(in_a=w_in_a[0], mem_kv0=w_mem_kv[0], glu=w_glu[0], out0=w_out[0], kv=w_kv, in_b=w_in_b[0], mem_kv1=w_mem_kv[1], out1=w_out[1])
    operands = [_pack([a[n] for n in SMALL_SHARDED], small_rows)] + [shards[n].astype(BF16) for n in GATHER_ORDER[1:]]
    gathered = exchange_start("ag_weights", operands, gather=True)
    small = exchange_wait(gathered, 0, gathered['token'])
    small = [jnp.stack(parts) for parts in zip(*[_unpack(small[b], small_shapes) for b in range(N_DEV)])]
    p = {n: a[n] for n in REPLICATED}
    for n in ('lam_re', 'lam_im', 'log_step', 'b_re', 'b_im', 'c_re', 'c_im'):
        p[n] = p[n][0]
    p['d_skip'] = small[0].reshape(-1)
    p['b_glu'] = small[1].reshape(-1)
    p['w_fgate'] = small[2].reshape(-1, w_fgate.shape[1])
    p['pre_norm_g'] = p['pre_norm_g'] + gathered['token'][0, 0]

    full_shapes = [a[n].shape for n in REPLICATED] + [(1, N_DEV * d_skip.shape[1]), (1, N_DEV * b_glu.shape[1]), (N_DEV * w_fgate.shape[0], w_fgate.shape[1])]
    rows = _packed_rows(full_shapes)
    late_rows = 3 * pre_norm_g.shape[1] // LANE
    start_early = lambda g: exchange_start("ag_grads", [_pack([g[n] for n in REPLICATED + SMALL_SHARDED], rows)], gather=True)
    start_late = lambda g_late: exchange_start("ag_grads_late", [_pack(g_late, late_rows)], gather=True)

    loss_local, grad_x, pending, early, late = local_step(p, gathered, x[0], mem[0], loss_target[0], start_early, start_late)
    loss = lax.psum(loss_local, MESH_AXES)

    out = {}

    def update(tag, name, layer=None):
        pick = (lambda t: t) if layer is None else (lambda t: t[layer])
        parts = exchange_wait(pending[tag], 0, grad_x)
        two_d = lambda t: pick(t).reshape(parts.shape[1:])
        res = adamw_reduce("adamw_" + tag, parts, two_d(a[name]), two_d(a['m_' + name]), two_d(a['v_' + name]))
        return [r.reshape(pick(a[name]).shape) for r in res]

    per_layer = {n: [None] * n_layers for n in ('w_out', 'w_mem_kv')}
    per_layer['w_out'][1] = update("out1", 'w_out', 1)
    per_layer['w_mem_kv'][1] = update("mem_kv1", 'w_mem_kv', 1)
    out['w_in_b'] = [r[None] for r in update("in_b", 'w_in_b', 0)]
    out['w_kv'] = update("kv", 'w_kv')
    per_layer['w_out'][0] = update("out0", 'w_out', 0)
    per_layer['w_mem_kv'][0] = update("mem_kv0", 'w_mem_kv', 0)
    out['w_glu'] = [r[None] for r in update("glu", 'w_glu', 0)]
    for name in per_layer:
        out[name] = [jnp.stack(t) for t in zip(*per_layer[name])]

    zeros = [jnp.zeros(s, F32) for s in full_shapes[len(REPLICATED):]]
    packed = lambda pre: _pack([a[pre + n] for n in REPLICATED] + zeros, rows)
    res = [_unpack(r, full_shapes) for r in adamw_reduce("adamw_small", exchange_wait(early, 0, grad_x), packed(''), packed('m_'), packed('v_'))]
    for i, n in enumerate(REPLICATED):
        out[n] = [r[i] for r in res]
    g_full = res[0][len(REPLICATED):]
    g_shard = [lax.dynamic_slice_in_dim(g_full[0], me * d_skip.shape[1], d_skip.shape[1], 1),
               lax.dynamic_slice_in_dim(g_full[1], me * b_glu.shape[1], b_glu.shape[1], 1),
               lax.dynamic_slice_in_dim(g_full[2], me * w_fgate.shape[0], w_fgate.shape[0], 0)]
    packed = lambda pre: _pack([a[pre + n] for n in SMALL_SHARDED], small_rows)
    res = [_unpack(r, small_shapes) for r in adamw_reduce("adamw_small_sharded", _pack(g_shard, small_rows)[None], packed(''), packed('m_'), packed('v_'))]
    for i, n in enumerate(SMALL_SHARDED):
        out[n] = [r[i] for r in res]
    packed = lambda pre: _pack([a[pre + 'pre_norm_g'][0], a[pre + 'mem_norm_g'][0], a[pre + 'mem_norm_g'][1]], late_rows)
    res = [_unpack(r, [pre_norm_g.shape[1:]] * 3) for r in adamw_reduce("adamw_small_late", exchange_wait(late, 0, grad_x), packed(''), packed('m_'), packed('v_'))]
    out['pre_norm_g'] = [jnp.stack([r[0], o[1]]) for r, o in zip(res, out['pre_norm_g'])]
    out['mem_norm_g'] = [jnp.stack([r[1], r[2]]) for r in res]
    out['w_in_a'] = [r[None] for r in update("in_a", 'w_in_a', 0)]

    return (loss, grad_x[None], *[out[n][k] for k in range(4) for n in WEIGHTS])
```

```python
import functools
import math

import jax
import jax.numpy as jnp
from jax import lax
from jax.experimental import pallas as pl
from jax.experimental.pallas import tpu as pltpu

F32 = jnp.float32
BF16 = jnp.bfloat16
HP = lax.Precision.HIGHEST
MESH_AXES = ("x", "y", "c")
N_DEV = 8
V7X_VMEM_LIMIT = 56 * 1024 * 1024
LANE = 128

EPS = 1e-6
HEAD_DIM = 128
SSM_GROUP = 16
SSM_STATE = 64
SSM_T = 8
SSM_LO = 16
SSM_GB = 12
FOX_BQ = 256
GRAD_DTYPE = BF16
ACT_DTYPE = BF16
ADAM_LR = 0.001
ADAM_B1 = 0.9
ADAM_B2 = 0.999
ADAM_EPS = 1e-08
ADAM_WD = 0.01
ADAM_STEP = 10

WEIGHTS = ['pre_norm_g', 'post_norm_g', 'w_in_a', 'lam_re', 'lam_im', 'log_step', 'b_re', 'b_im', 'c_re', 'c_im', 'd_skip',
           'w_glu', 'b_glu', 'kv_norm_g', 'w_kv', 'w_fgate', 'b_fgate', 'w_in_b', 'mem_norm_g', 'w_mem_kv', 'w_out']
INPUTS = ['x', 'mem'] + WEIGHTS + ['loss_target'] + ['m_' + n for n in WEIGHTS] + ['v_' + n for n in WEIGHTS]
REPLICATED = ['pre_norm_g', 'post_norm_g', 'lam_re', 'lam_im', 'log_step', 'b_re', 'b_im', 'c_re', 'c_im', 'kv_norm_g',
              'b_fgate', 'mem_norm_g']
SMALL_SHARDED = ['d_skip', 'b_glu', 'w_fgate']


def _cparams(sem=None):
    return pltpu.CompilerParams(dimension_semantics=sem, vmem_limit_bytes=V7X_VMEM_LIMIT)


def _tile(n, cap):
    if n <= cap:
        return n
    best = None
    for t in range(LANE, cap + 1, LANE):
        if n % t == 0:
            best = t
    assert best is not None, (n, cap)
    return best


def _matmul(name, a, b, a_spec, b_spec, o_spec, out_shape, grid, dims, nk, after=None):
    def body(a_ref, b_ref, *rest):
        o_ref, acc_ref = rest[-2:]
        k = pl.program_id(2)
        part = lax.dot_general(a_ref[...].astype(BF16), b_ref[...].astype(BF16), dims, preferred_element_type=F32)

        @pl.when(k == 0)
        def _():
            acc_ref[...] = part

        @pl.when(k > 0)
        def _():
            acc_ref[...] += part

        @pl.when(k == nk - 1)
        def _():
            o_ref[...] = acc_ref[...].astype(o_ref.dtype)

    acc_shape = tuple(d for d in o_spec.block_shape if d is not None)
    extra = [] if after is None else [after]
    return pl.pallas_call(
        body, name=name, grid=grid, in_specs=[a_spec, b_spec] + [pl.BlockSpec(memory_space=pl.ANY)] * len(extra), out_specs=o_spec,
        out_shape=out_shape, scratch_shapes=[pltpu.VMEM(acc_shape, F32)],
        compiler_params=_cparams(("parallel", "parallel", "arbitrary")),
    )(a, b, *extra)


NN = (((1,), (0,)), ((), ()))
NT = (((1,), (1,)), ((), ()))
TN = (((0,), (0,)), ((), ()))


def _mm_nn(name, a, b, out_dtype=ACT_DTYPE):
    (m, k), (_, n) = a.shape, b.shape
    bm, bn, bk = _tile(m, 1024), _tile(n, 512), _tile(k, 2048)
    return _matmul(name, a, b, pl.BlockSpec((bm, bk), lambda i, j, kk: (i, kk)), pl.BlockSpec((bk, bn), lambda i, j, kk: (kk, j)),
                   pl.BlockSpec((bm, bn), lambda i, j, kk: (i, j)), jax.ShapeDtypeStruct((m, n), out_dtype), (m // bm, n // bn, k // bk), NN, k // bk)


def _mm_nt(name, a, b, after=None, out_dtype=ACT_DTYPE):
    (m, c), (n, _) = a.shape, b.shape
    bm, bn, bk = _tile(m, 1024), _tile(n, 512), _tile(c, 2048)
    return _matmul(name, a, b, pl.BlockSpec((bm, bk), lambda i, j, kk: (i, kk)), pl.BlockSpec((bn, bk), lambda i, j, kk: (j, kk)),
                   pl.BlockSpec((bm, bn), lambda i, j, kk: (i, j)), jax.ShapeDtypeStruct((m, n), out_dtype), (m // bm, n // bn, c // bk), NT, c // bk, after)


def _mm_tn(name, a, b, out_dtype=F32, after=None):
    (c, m), (_, n) = a.shape, b.shape
    bm, bn, bk = _tile(m, 1024), _tile(n, 512), _tile(c, 2048)
    return _matmul(name, a, b, pl.BlockSpec((bk, bm), lambda i, j, kk: (kk, i)), pl.BlockSpec((bk, bn), lambda i, j, kk: (kk, j)),
                   pl.BlockSpec((bm, bn), lambda i, j, kk: (i, j)), jax.ShapeDtypeStruct((m, n), out_dtype), (m // bm, n // bn, c // bk), TN, c // bk, after)


def _mm_nn_cb(name, a, bb):
    (m, k), (nb, _, ns) = a.shape, bb.shape
    bm, bk = _tile(m, 1024), _tile(k, 2048)
    return _matmul(name, a, bb, pl.BlockSpec((bm, bk), lambda i, j, kk: (i, kk)), pl.BlockSpec((None, bk, ns), lambda i, j, kk: (j, kk, 0)),
                   pl.BlockSpec((bm, ns), lambda i, j, kk: (i, j)), jax.ShapeDtypeStruct((m, nb * ns), ACT_DTYPE), (m // bm, nb, k // bk), NN, k // bk)


def _mm_nt_cb(name, dy, bb, after=None):
    m, (nb, k, ns) = dy.shape[0], bb.shape
    bm, bn = _tile(m, 1024), _tile(k, 512)

    def body(dy_ref, b_ref, *rest):
        o_ref = rest[-1]
        acc = lax.dot_general(dy_ref[:, 0:ns], b_ref[0], NT, preferred_element_type=F32)
        for j in range(1, nb):
            acc += lax.dot_general(dy_ref[:, j * ns:(j + 1) * ns], b_ref[j], NT, preferred_element_type=F32)
        o_ref[...] = acc.astype(o_ref.dtype)

    extra = [] if after is None else [after]
    return pl.pallas_call(
        body, name=name, grid=(m // bm, k // bn),
        in_specs=[pl.BlockSpec((bm, nb * ns), lambda i, j: (i, 0)), pl.BlockSpec((nb, bn, ns), lambda i, j: (0, j, 0))]
        + [pl.BlockSpec(memory_space=pl.ANY)] * len(extra),
        out_specs=pl.BlockSpec((bm, bn), lambda i, j: (i, j)), out_shape=jax.ShapeDtypeStruct((m, k), ACT_DTYPE),
        compiler_params=_cparams(("parallel", "parallel")),
    )(dy, bb, *extra)


def _mm_tn_cb(name, a, dy, ns, out_dtype=F32, after=None):
    (c, k), nb = a.shape, dy.shape[1] // ns
    bm = _tile(k, 1024)
    return _matmul(name, a, dy, pl.BlockSpec((c, bm), lambda i, j, kk: (0, i)), pl.BlockSpec((c, ns), lambda i, j, kk: (0, j)),
                   pl.BlockSpec((None, bm, ns), lambda i, j, kk: (j, i, 0)), jax.ShapeDtypeStruct((nb, k, ns), out_dtype), (k // bm, nb, 1), TN, 1, after)


def make_mm(name):
    @jax.custom_vjp
    def mm(a, b):
        return _mm_nn(name, a, b, F32)

    def fwd(a, b):
        return mm(a, b), (a, b)

    def bwd(res, dy):
        a, b = res
        return _mm_nt(name + "_da", dy, b, out_dtype=a.dtype), _mm_tn(name + "_dw", a, dy, b.dtype)

    mm.defvjp(fwd, bwd)
    return mm


def _rowop_specs(rows, params, bm, nl):
    row_specs = [pl.BlockSpec((bm, r.shape[1] // nl), lambda j, i: (i, j)) for r in rows]
    par_specs = [pl.BlockSpec((p.shape[0], p.shape[1] // nl), lambda j, i: (0, j)) for p in params]
    row_blk = [jax.ShapeDtypeStruct((bm, r.shape[1] // nl), r.dtype) for r in rows]
    par_blk = [jax.ShapeDtypeStruct((p.shape[0], p.shape[1] // nl), p.dtype) for p in params]
    return row_specs, par_specs, row_blk, par_blk


def make_rowop(f, name, bm, nl=1, out_dtypes=None):
    def loaded(refs):
        return [r[...].astype(F32) for r in refs]

    def fwd_call(rows, params):
        n_rows = rows[0].shape[0]
        b = min(bm, n_rows)
        row_specs, par_specs, row_blk, par_blk = _rowop_specs(rows, params, b, nl)
        out_blk = jax.eval_shape(f, *[jax.ShapeDtypeStruct(t.shape, F32) for t in row_blk + par_blk])
        dts = out_dtypes or [F32] * len(out_blk)
        nr, npar = len(rows), len(params)

        def body(*refs):
            for o_ref, o in zip(refs[nr + npar:], f(*loaded(refs[:nr + npar]))):
                o_ref[...] = o.astype(o_ref.dtype)

        return pl.pallas_call(
            body, name=name, grid=(nl, n_rows // b), in_specs=row_specs + par_specs,
            out_specs=[pl.BlockSpec(o.shape, lambda j, i: (i, j)) for o in out_blk],
            out_shape=[jax.ShapeDtypeStruct((n_rows, o.shape[1] * nl), dt) for o, dt in zip(out_blk, dts)],
            compiler_params=_cparams(("parallel", "parallel")),
        )(*rows, *params)

    def bwd_call(rows, params, cts):
        n_rows = rows[0].shape[0]
        b = min(bm, n_rows)
        row_specs, par_specs, row_blk, par_blk = _rowop_specs(rows, params, b, nl)
        ct_specs = [pl.BlockSpec((b, c.shape[1] // nl), lambda j, i: (i, j)) for c in cts]
        nr, npar, nct = len(rows), len(params), len(cts)

        def body(*refs):
            i = pl.program_id(1)
            _, vjp = jax.vjp(lambda *v: tuple(f(*v)), *loaded(refs[:nr + npar]))
            grads = vjp(tuple(loaded(refs[nr + npar:nr + npar + nct])))
            outs = refs[nr + npar + nct:]
            for o_ref, g in zip(outs[:nr], grads[:nr]):
                o_ref[...] = g.astype(o_ref.dtype)
            for o_ref, g in zip(outs[nr:], grads[nr:]):
                @pl.when(i == 0)
                def _(o_ref=o_ref, g=g):
                    o_ref[...] = g

                @pl.when(i > 0)
                def _(o_ref=o_ref, g=g):
                    o_ref[...] += g

        outs = pl.pallas_call(
            body, name=name + "_bwd", grid=(nl, n_rows // b), in_specs=row_specs + par_specs + ct_specs,
            out_specs=row_specs + par_specs,
            out_shape=[jax.ShapeDtypeStruct(a.shape, a.dtype) for a in rows] + [jax.ShapeDtypeStruct(a.shape, F32) for a in params],
            compiler_params=_cparams(("arbitrary", "arbitrary")),
        )(*rows, *params, *cts)
        return tuple(outs[:nr]), tuple(g.astype(a.dtype) for g, a in zip(outs[nr:], params))

    @jax.custom_vjp
    def op(rows, params):
        return tuple(fwd_call(rows, params))

    def fwd(rows, params):
        return op(rows, params), (rows, params)

    def bwd(res, cts):
        rows, params = res
        return bwd_call(rows, params, tuple(cts))

    op.defvjp(fwd, bwd)
    return op


def loss_head(name, h, o, tgt, post_g, bm=128):
    n_rows, d = h.shape
    b = min(bm, n_rows)
    row_spec = pl.BlockSpec((b, d), lambda i: (i, 0))
    par_spec = pl.BlockSpec((1, d), lambda i: (0, 0))

    def body(h_ref, o_ref, t_ref, g_ref, loss_ref, dh_ref, do_ref, dg_ref):
        i = pl.program_id(0)
        tgt_blk = t_ref[...]
        (rowloss,), vjp = jax.vjp(lambda hv, ov, gv: f_final(hv, ov, tgt_blk, gv), h_ref[...], o_ref[...].astype(F32), g_ref[...])
        dh, do, dg = vjp((jnp.ones_like(rowloss),))
        loss_ref[...] = rowloss
        dh_ref[...] = dh
        do_ref[...] = do.astype(do_ref.dtype)

        @pl.when(i == 0)
        def _():
            dg_ref[...] = dg

        @pl.when(i > 0)
        def _():
            dg_ref[...] += dg

    return pl.pallas_call(
        body, name=name, grid=(n_rows // b,), in_specs=[row_spec, row_spec, row_spec, par_spec],
        out_specs=[pl.BlockSpec((b, 1), lambda i: (i, 0)), row_spec, row_spec, par_spec],
        out_shape=[jax.ShapeDtypeStruct((n_rows, 1), F32), jax.ShapeDtypeStruct(h.shape, h.dtype), jax.ShapeDtypeStruct(o.shape, o.dtype),
                   jax.ShapeDtypeStruct((1, d), F32)],
        compiler_params=_cparams(("arbitrary",)),
    )(h, o, tgt, post_g)


def make_groupop(f, name, gb):
    def specs(arrs):
        return [pl.BlockSpec((gb,) + a.shape[1:], lambda g: (g, 0, 0)) for a in arrs]

    def fwd_call(arrs):
        g_n = arrs[0].shape[0]
        out_blk = jax.eval_shape(f, *[jax.ShapeDtypeStruct((gb,) + a.shape[1:], a.dtype) for a in arrs])[0]
        n = len(arrs)

        def body(*refs):
            for i, o in enumerate(f(*[r[...] for r in refs[:n]])):
                refs[n][i] = o.astype(refs[n].dtype)

        return pl.pallas_call(
            body, name=name, grid=(g_n // gb,), in_specs=specs(arrs),
            out_specs=pl.BlockSpec((gb,) + out_blk.shape, lambda g: (g, 0, 0)),
            out_shape=jax.ShapeDtypeStruct((g_n,) + out_blk.shape, ACT_DTYPE),
            compiler_params=_cparams(("parallel",)),
        )(*arrs)

    def bwd_call(arrs, ct):
        g_n = arrs[0].shape[0]
        n = len(arrs)

        def body(*refs):
            _, vjp = jax.vjp(lambda *v: tuple(f(*v)), *[r[...] for r in refs[:n]])
            for o_ref, g in zip(refs[n + 1:], vjp(tuple(refs[n][i].astype(F32) for i in range(gb)))):
                o_ref[...] = g.astype(o_ref.dtype)

        return pl.pallas_call(
            body, name=name + "_bwd", grid=(g_n // gb,), in_specs=specs(arrs) + specs([ct]), out_specs=specs(arrs),
            out_shape=[jax.ShapeDtypeStruct(a.shape, a.dtype) for a in arrs],
            compiler_params=_cparams(("parallel",)),
        )(*arrs, ct)

    @jax.custom_vjp
    def op(*arrs):
        return fwd_call(arrs)

    def fwd(*arrs):
        return op(*arrs), arrs

    def bwd(arrs, ct):
        return tuple(bwd_call(arrs, ct))

    op.defvjp(fwd, bwd)
    return op


@jax.custom_vjp
def _rms(x, g):
    return x * lax.rsqrt(jnp.mean(x * x, axis=-1, keepdims=True) + EPS) * g


def _rms_fwd(x, g):
    r = lax.rsqrt(jnp.mean(x * x, axis=-1, keepdims=True) + EPS)
    return x * r * g, (x, g, r)


def _rms_bwd(res, dy):
    x, g, r = res
    n, t = x * r, dy * g
    return r * (t - n * jnp.mean(n * t, axis=-1, keepdims=True)), jnp.sum(dy * n, axis=0, keepdims=True)


_rms.defvjp(_rms_fwd, _rms_bwd)


def _silu(z):
    return z * jax.nn.sigmoid(z)


def _log_sigmoid(x):
    return jnp.minimum(x, 0.0) - jnp.log(1.0 + jnp.exp(-jnp.abs(x)))


def f_norm(x, g):
    return (_rms(x, g),)


def f_gate_a1(y, u, d):
    return (jax.nn.gelu(y + d * u),)


def f_gate_a2(yg, t, z, b):
    return (yg * jax.nn.sigmoid(t + b) * _silu(z),)


def f_gate_b(att, z):
    return (att * _silu(z),)


def f_post_a(h, o, post_g, kv_g, pre_g):
    h1 = h + _rms(o, post_g)
    return h1, _rms(h1, kv_g), _rms(h1, pre_g)


def f_final(h, o, tgt, post_g):
    err = h + _rms(o, post_g) - tgt
    return (0.5 * jnp.mean(err * err, axis=-1, keepdims=True),)


def f_logf(gl, b):
    return (_log_sigmoid(gl + b),)


def _mxu(a, b, dims):
    return lax.dot_general(a.astype(BF16), b.astype(BF16), dims, preferred_element_type=F32)


@jax.custom_vjp
def _mxu_nn(a, b):
    return _mxu(a, b, NN)


def _mxu_nn_bwd(res, g):
    a, b = res
    return _mxu(g, b, NT).astype(a.dtype), _mxu(a, g, TN).astype(b.dtype)


_mxu_nn.defvjp(lambda a, b: (_mxu(a, b, NN), (a, b)), _mxu_nn_bwd)


@jax.custom_vjp
def _mxu_nt(a, b):
    return _mxu(a, b, NT)


def _mxu_nt_bwd(res, g):
    a, b = res
    return _mxu(g, b, NN).astype(a.dtype), _mxu(g, a, TN).astype(b.dtype)


_mxu_nt.defvjp(lambda a, b: (_mxu(a, b, NT), (a, b)), _mxu_nt_bwd)


def f_memattn(q, zm, km, vm):
    o = _mxu_nn(_softmax(_mxu_nt(q * (HEAD_DIM ** -0.5), km)), vm)
    return (o * _silu(zm),)


@jax.custom_vjp
def _softmax(s):
    e = jnp.exp(s - jnp.max(s, axis=-1, keepdims=True))
    return e * (1.0 / jnp.sum(e, axis=-1, keepdims=True))


def _softmax_fwd(s):
    p = _softmax(s)
    return p, p


def _softmax_bwd(p, dp):
    return (p * (dp - jnp.sum(dp * p, axis=-1, keepdims=True)),)


_softmax.defvjp(_softmax_fwd, _softmax_bwd)


def _fox_block(q, k, v, fr):
    bq = q.shape[0]
    s = _mxu_nt(q * (HEAD_DIM ** -0.5), k) - fr
    tri = lax.broadcasted_iota(jnp.int32, (bq, bq), 0) >= lax.broadcasted_iota(jnp.int32, (bq, bq), 1)
    diag = jnp.where(tri, s[:, -bq:], -1e30)
    s = diag if k.shape[0] == bq else jnp.concatenate([s[:, :-bq], diag], axis=1)
    return _mxu_nn(_softmax(s), v)


def _cmul(ar, ai, xr, xi):
    return ar * xr - ai * xi, ar * xi + ai * xr


def _hp_dot(a, b):
    return jnp.dot(a, b, precision=HP, preferred_element_type=F32)


_bf_dot = _mxu_nn


def f_s5(u, toep, win_r, win_i, wout_r, wout_i, coef):
    gb = u.shape[0]
    n_hi = u.shape[1] // SSM_LO
    sr = [_bf_dot(u[i], win_r[i]) for i in range(gb)]
    si = [_bf_dot(u[i], win_i[i]) for i in range(gb)]

    def stacked(parts, lo):
        return jnp.concatenate([a[lo * n_hi:(lo + 1) * n_hi] for a in parts], axis=0)

    def coef_rows(k):
        return jnp.concatenate([jnp.broadcast_to(coef[i, k:k + 1, :], (n_hi, coef.shape[2])) for i in range(gb)], axis=0)

    at_r, at_i = coef_rows(0), coef_rows(1)
    pr, pi = stacked(sr, 0), stacked(si, 0)
    for lo in range(1, SSM_LO):
        dr, di = _cmul(at_r, at_i, pr, pi)
        pr, pi = stacked(sr, lo) + dr, stacked(si, lo) + di
    n = 2 * gb * n_hi
    ri = lax.broadcasted_iota(jnp.int32, (n, n), 0)
    ci = lax.broadcasted_iota(jnp.int32, (n, n), 1)
    same = ri // n_hi == ci // n_hi

    def shifted(d, zr, zi):
        z = _hp_dot((same & (ri - ci == d)).astype(F32), jnp.concatenate([zr, zi], axis=0))
        return z[:n // 2], z[n // 2:]

    d, step = 1, 0
    while d < n_hi:
        dr, di = _cmul(coef_rows(2 + 2 * step), coef_rows(3 + 2 * step), *shifted(d, pr, pi))
        pr, pi = pr + dr, pi + di
        d, step = 2 * d, step + 1
    er, ei = shifted(1, pr, pi)
    xr, xi = [er], [ei]
    for lo in range(1, SSM_LO):
        dr, di = _cmul(at_r, at_i, xr[-1], xi[-1])
        xr.append(stacked(sr, lo - 1) + dr)
        xi.append(stacked(si, lo - 1) + di)

    def group_rows(parts, i):
        return jnp.concatenate([a[i * n_hi:(i + 1) * n_hi] for a in parts], axis=0)

    return tuple(_bf_dot(u[i], toep[i]) + _bf_dot(group_rows(xr, i), wout_r[i]) + _bf_dot(group_rows(xi, i), wout_i[i])
                 for i in range(gb))


def s5_operators(lam_re, lam_im, log_step, b_re, b_im, c_re, c_im, n_hi):
    t_n = SSM_T
    lr, li = lam_re, lam_im
    dt = jnp.exp(log_step)[:, None]
    mag = jnp.exp(lr * dt)
    ar, ai = mag * jnp.cos(li * dt), mag * jnp.sin(li * dt)
    den = lr * lr + li * li
    cr = ((ar - 1.0) * lr + ai * li) / den
    ci = (ai * lr - (ar - 1.0) * li) / den
    bbr = cr[..., None] * b_re - ci[..., None] * b_im
    bbi = cr[..., None] * b_im + ci[..., None] * b_re
    k = jnp.arange(t_n + 1, dtype=F32)[:, None, None]
    pm, ang = jnp.exp(k * (lr * dt)), k * (li * dt)
    pr, pi = pm * jnp.cos(ang), pm * jnp.sin(ang)
    abr = pr[..., None] * bbr - pi[..., None] * bbi
    abi = pr[..., None] * bbi + pi[..., None] * bbr
    kk = (jnp.einsum('ghp,kgpj->kghj', c_re, abr[:t_n], precision=HP)
          - jnp.einsum('ghp,kgpj->kghj', c_im, abi[:t_n], precision=HP))
    lag = jnp.arange(t_n)[None, :] - jnp.arange(t_n)[:, None]
    onehot = (lag[None] == jnp.arange(t_n)[:, None, None]).astype(F32)
    g_n, h_n = c_re.shape[0], c_re.shape[1]
    toep = jnp.einsum('kst,kghj->gsjth', onehot, kk, precision=HP).reshape(g_n, t_n * h_n, t_n * h_n)
    win_r = abr[:t_n][::-1].transpose(1, 0, 3, 2).reshape(g_n, t_n * h_n, -1)
    win_i = abi[:t_n][::-1].transpose(1, 0, 3, 2).reshape(g_n, t_n * h_n, -1)
    p1r, p1i = pr[1:, :, None, :], pi[1:, :, None, :]
    wout_r = (c_re[None] * p1r - c_im[None] * p1i).transpose(1, 3, 0, 2).reshape(g_n, -1, t_n * h_n)
    wout_i = (-(c_re[None] * p1i + c_im[None] * p1r)).transpose(1, 3, 0, 2).reshape(g_n, -1, t_n * h_n)
    rows = [pr[t_n], pi[t_n]]
    qr, qi = pr[t_n], pi[t_n]
    for _ in range(int(math.log2(SSM_LO))):
        qr, qi = qr * qr - qi * qi, 2.0 * qr * qi
    d = 1
    while d < n_hi:
        rows += [qr, qi]
        qr, qi = qr * qr - qi * qi, 2.0 * qr * qi
        d *= 2
    coef = jnp.stack(rows, axis=1)
    return toep, win_r, win_i, wout_r, wout_i, coef


def _cumsum_call(name, x, reverse):
    n_rows, w = x.shape
    bm = min(256, n_rows)
    nb = n_rows // bm

    def body(x_ref, o_ref, carry_ref):
        i = pl.program_id(0)

        @pl.when(i == 0)
        def _():
            carry_ref[...] = jnp.zeros_like(carry_ref)

        ri = lax.broadcasted_iota(jnp.int32, (bm, bm), 0)
        ci = lax.broadcasted_iota(jnp.int32, (bm, bm), 1)
        tri = ((ri <= ci) if reverse else (ri >= ci)).astype(F32)
        xb = x_ref[...]
        o_ref[...] = _hp_dot(tri, xb) + carry_ref[...]
        carry_ref[...] += jnp.sum(xb, axis=0, keepdims=True)

    idx = (lambda i: (nb - 1 - i, 0)) if reverse else (lambda i: (i, 0))
    return pl.pallas_call(
        body, name=name, grid=(nb,), in_specs=[pl.BlockSpec((bm, w), idx)], out_specs=pl.BlockSpec((bm, w), idx),
        out_shape=jax.ShapeDtypeStruct(x.shape, F32), scratch_shapes=[pltpu.VMEM((1, w), F32)],
        compiler_params=_cparams(("arbitrary",)),
    )(x)


def make_cumsum(name):
    @jax.custom_vjp
    def cs(x):
        return _cumsum_call(name, x, False)

    def fwd(x):
        return cs(x), None

    def bwd(_, dy):
        return (_cumsum_call(name + "_bwd", dy, True),)

    cs.defvjp(fwd, bwd)
    return cs


def _fox_specs(n_rows, bq):
    q_spec = pl.BlockSpec((bq, HEAD_DIM), lambda h, i: (i, h))
    kv_spec = pl.BlockSpec((n_rows, HEAD_DIM), lambda h, i: (0, h))
    fr_spec = pl.BlockSpec((None, 1, n_rows), lambda h, i: (h, 0, 0))
    return q_spec, kv_spec, fr_spec


def _fox_fwd_call(name, q, k, v, fr):
    n_rows, width = q.shape
    bq = min(FOX_BQ, n_rows)
    nq = n_rows // bq
    q_spec, kv_spec, fr_spec = _fox_specs(n_rows, bq)

    def body(q_ref, k_ref, v_ref, fr_ref, o_ref):
        i = pl.program_id(1)
        for p in range(nq):
            n_keys = (p + 1) * bq

            @pl.when(i == p)
            def _(n_keys=n_keys):
                o = _fox_block(q_ref[...].astype(F32), k_ref[:n_keys, :], v_ref[:n_keys, :], fr_ref[:, :n_keys])
                o_ref[...] = o.astype(o_ref.dtype)

    return pl.pallas_call(
        body, name=name, grid=(width // HEAD_DIM, nq), in_specs=[q_spec, kv_spec, kv_spec, fr_spec], out_specs=q_spec,
        out_shape=jax.ShapeDtypeStruct(q.shape, ACT_DTYPE), compiler_params=_cparams(("parallel", "parallel")),
    )(q, k, v, fr)


def _fox_bwd_call(name, q, k, v, fr, do):
    n_rows, width = q.shape
    bq = min(FOX_BQ, n_rows)
    nq = n_rows // bq
    q_spec, kv_spec, fr_spec = _fox_specs(n_rows, bq)

    def body(q_ref, k_ref, v_ref, fr_ref, do_ref, dq_ref, dk_ref, dv_ref, dfr_ref, dk_acc, dv_acc):
        i = pl.program_id(1)

        @pl.when(i == 0)
        def _():
            dk_acc[...] = jnp.zeros_like(dk_acc)
            dv_acc[...] = jnp.zeros_like(dv_acc)
            dfr_ref[...] = jnp.zeros_like(dfr_ref)

        for p in range(nq):
            n_keys = (p + 1) * bq

            @pl.when(i == p)
            def _(n_keys=n_keys):
                _, vjp = jax.vjp(_fox_block, q_ref[...].astype(F32), k_ref[:n_keys, :].astype(F32), v_ref[:n_keys, :].astype(F32),
                                 fr_ref[:, :n_keys])
                dq, dk, dv, dfr = vjp(do_ref[...].astype(F32))
                dq_ref[...] = dq.astype(dq_ref.dtype)
                dk_acc[:n_keys, :] += dk
                dv_acc[:n_keys, :] += dv
                dfr_ref[:, :n_keys] += dfr

        @pl.when(i == nq - 1)
        def _():
            dk_ref[...] = dk_acc[...].astype(dk_ref.dtype)
            dv_ref[...] = dv_acc[...].astype(dv_ref.dtype)

    return pl.pallas_call(
        body, name=name, grid=(width // HEAD_DIM, nq), in_specs=[q_spec, kv_spec, kv_spec, fr_spec, q_spec],
        out_specs=[q_spec, kv_spec, kv_spec, fr_spec], out_shape=[jax.ShapeDtypeStruct(a.shape, a.dtype) for a in (q, k, v, fr)],
        scratch_shapes=[pltpu.VMEM((n_rows, HEAD_DIM), F32), pltpu.VMEM((n_rows, HEAD_DIM), F32)],
        compiler_params=_cparams(("parallel", "arbitrary")),
    )(q, k, v, fr, do)


def make_fox(name):
    @jax.custom_vjp
    def fox(q, k, v, fr):
        return _fox_fwd_call(name, q, k, v, fr)

    def fwd(q, k, v, fr):
        return fox(q, k, v, fr), (q, k, v, fr)

    def bwd(res, do):
        return tuple(_fox_bwd_call(name + "_bwd", *res, do))

    fox.defvjp(fwd, bwd)
    return fox


def _relations(x, y, c):
    out = []
    for rel in range(1, N_DEV):
        px = 1 - x if rel & 4 else x
        py = 1 - y if rel & 2 else y
        pc = 1 - c if rel & 1 else c
        out.append((rel, (px, py, pc), 4 * px + 2 * py + pc))
    return out


_HBM_SPEC = pl.BlockSpec(memory_space=pltpu.HBM)
_SEM_SPEC = pl.BlockSpec(memory_space=pltpu.SEMAPHORE)
_DATAFLOW = pltpu.SideEffectType.DATAFLOW_SIDE_EFFECTING


def exchange_start(name, srcs, gather):
    n = len(srcs)
    me = 4 * lax.axis_index("x") + 2 * lax.axis_index("y") + lax.axis_index("c")
    lands = []
    for s in srcs:
        own = s if gather else lax.dynamic_index_in_dim(s, me, 0, keepdims=False)
        land = lax.empty((N_DEV,) + own.shape, s.dtype)
        lands.append(lax.dynamic_update_index_in_dim(land, own, me, 0))

    def body(*refs):
        src_refs, land_refs = refs[:n], refs[n:2 * n]
        send_sems, recv_sems, token = refs[2 * n], refs[2 * n + 1], refs[-1]
        x, y, c = lax.axis_index("x"), lax.axis_index("y"), lax.axis_index("c")
        mine = 4 * x + 2 * y + c
        for k in range(n):
            for rel, peer, peer_blk in _relations(x, y, c):
                pltpu.make_async_remote_copy(
                    src_ref=src_refs[k] if gather else src_refs[k].at[peer_blk], dst_ref=land_refs[k].at[mine],
                    send_sem=send_sems.at[7 * k + rel - 1], recv_sem=recv_sems.at[7 * k + rel - 1],
                    device_id=peer, device_id_type=pl.DeviceIdType.MESH).start()
        token[...] = jnp.zeros_like(token)

    sem = pltpu.SemaphoreType.DMA((7 * n,))
    hbm = lambda t: pltpu.HBM(t.shape, t.dtype)
    outs = pl.pallas_call(
        body, name=name,
        out_shape=(sem, sem, *[hbm(s) for s in srcs], *[hbm(l) for l in lands], jax.ShapeDtypeStruct((8, LANE), F32)),
        in_specs=[_HBM_SPEC] * (2 * n),
        out_specs=(_SEM_SPEC, _SEM_SPEC, *[_HBM_SPEC] * (2 * n), pl.BlockSpec(memory_space=pltpu.VMEM)),
        input_output_aliases={i: 2 + i for i in range(2 * n)},
        compiler_params=pltpu.CompilerParams(has_side_effects=_DATAFLOW),
    )(*[pltpu.with_memory_space_constraint(t, pltpu.HBM) for t in list(srcs) + lands])
    return dict(name=name, gather=gather, send=outs[0], recv=outs[1], srcs=list(outs[2:2 + n]), lands=list(outs[2 + n:2 + 2 * n]), token=outs[-1])


def exchange_wait(handle, k, after):
    gather = handle['gather']

    def body(src_ref, land_ref, send_sems, recv_sems, after_ref, src_out, land_out):
        x, y, c = lax.axis_index("x"), lax.axis_index("y"), lax.axis_index("c")
        for rel, peer, peer_blk in _relations(x, y, c):
            copy = pltpu.make_async_remote_copy(
                src_ref=src_ref if gather else src_ref.at[peer_blk], dst_ref=land_ref.at[peer_blk],
                send_sem=send_sems.at[7 * k + rel - 1], recv_sem=recv_sems.at[7 * k + rel - 1],
                device_id=peer, device_id_type=pl.DeviceIdType.MESH)
            copy.wait_send()
            copy.wait_recv()

    src, land = handle['srcs'][k], handle['lands'][k]
    return pl.pallas_call(
        body, name=f"{handle['name']}_wait{k}", out_shape=(pltpu.HBM(src.shape, src.dtype), pltpu.HBM(land.shape, land.dtype)),
        in_specs=[_HBM_SPEC, _HBM_SPEC, _SEM_SPEC, _SEM_SPEC, pl.BlockSpec(memory_space=pl.ANY)], out_specs=(_HBM_SPEC, _HBM_SPEC),
        input_output_aliases={0: 0, 1: 1}, compiler_params=pltpu.CompilerParams(has_side_effects=_DATAFLOW),
    )(src, land, handle['send'], handle['recv'], after)[1]


def adamw_reduce(name, partials, w, m, v):
    n_part, n_rows, n_cols = partials.shape
    br = n_rows
    for cand in (512, 256, 128, 64, 32, 16, 8):
        if n_rows % cand == 0 and n_part * cand * n_cols * partials.dtype.itemsize <= (4 << 20):
            br = cand
            break

    def body(p_ref, w_ref, m_ref, v_ref, g_ref, d_ref, nm_ref, nv_ref):
        g = p_ref[0].astype(F32)
        for s in range(1, n_part):
            g = g + p_ref[s].astype(F32)
        m_new = ADAM_B1 * m_ref[...] + (1.0 - ADAM_B1) * g
        v_new = ADAM_B2 * v_ref[...] + (1.0 - ADAM_B2) * jnp.square(g)
        m_hat = m_new / (1.0 - ADAM_B1 ** ADAM_STEP)
        v_hat = v_new / (1.0 - ADAM_B2 ** ADAM_STEP)
        g_ref[...] = g
        d_ref[...] = -ADAM_LR * (m_hat / (jnp.sqrt(v_hat) + ADAM_EPS) + ADAM_WD * w_ref[...])
        nm_ref[...] = m_new
        nv_ref[...] = v_new

    spec = pl.BlockSpec((br, n_cols), lambda i: (i, 0))
    return pl.pallas_call(
        body, name=name, grid=(n_rows // br,), in_specs=[pl.BlockSpec((n_part, br, n_cols), lambda i: (0, i, 0)), spec, spec, spec],
        out_specs=[spec] * 4, out_shape=[jax.ShapeDtypeStruct((n_rows, n_cols), F32)] * 4, compiler_params=_cparams(("parallel",)),
    )(partials, w, m, v)


def _pack(arrays, n_rows):
    flat = jnp.concatenate([a.reshape(-1) for a in arrays])
    return jnp.pad(flat, (0, n_rows * LANE - flat.shape[0])).reshape(n_rows, LANE)


def _unpack(packed, shapes):
    flat, out, off = packed.reshape(-1), [], 0
    for s in shapes:
        n = math.prod(s)
        out.append(flat[off:off + n].reshape(s))
        off += n
    return out


def _packed_rows(shapes):
    n = sum(math.prod(s) for s in shapes)
    return -(-n // (LANE * 512)) * 512


GATHER_ORDER = ['small', 'in_a', 'mem_kv0', 'glu', 'out0', 'kv', 'in_b', 'mem_kv1', 'out1']


BLOCKED = ('in_a', 'kv', 'in_b')


def local_step(p, gathered, x, mem, tgt, start_early, start_late):
    n_rows, d_model = x.shape
    main_w = 3 * d_model // 4
    mem_w = d_model - main_w
    n_groups = main_w // SSM_GROUP
    n_heads = main_w // HEAD_DIM
    n_hi = n_rows // (SSM_T * SSM_LO)
    row = lambda a: a.reshape(1, -1)
    tape, pending = {}, {}

    def mm_fwd(name, act, after=None):
        w = exchange_wait(gathered, GATHER_ORDER.index(name), act if after is None else after)
        if name not in BLOCKED:
            w = w.reshape(-1, w.shape[2])
        tape[name] = (act, w)
        return (_mm_nn_cb if name in BLOCKED else _mm_nn)(name, act, w)

    def mm_dw(name, dy, after=None):
        act, w = tape[name]
        if name in BLOCKED:
            dw = _mm_tn_cb(name + "_dw", act, dy, w.shape[2], GRAD_DTYPE, after)
        else:
            dw = _mm_tn(name + "_dw", act, dy, GRAD_DTYPE, after)
            dw = dw.reshape(N_DEV, dw.shape[0] // N_DEV, dw.shape[1])
        pending[name] = exchange_start("rs_" + name, [dw], gather=False)

    def mm_da(name, dy, after=None):
        return (_mm_nt_cb if name in BLOCKED else _mm_nt)(name + "_da", dy, tape[name][1], after)

    def mm_bwd(name, dy):
        mm_dw(name, dy)
        return mm_da(name, dy, pending[name]['token'])

    def split(proj):
        return proj[:, :main_w], proj[:, main_w:2 * main_w], proj[:, 2 * main_w:2 * main_w + mem_w], proj[:, 2 * main_w + mem_w:]

    def mem_attn(i, qm, zm, kvm):
        memo, = make_rowop(f_memattn, f"mem_attn{i}", 512, mem_w // HEAD_DIM, [ACT_DTYPE])((qm, zm), (kvm[:, :mem_w], kvm[:, mem_w:]))
        return memo

    def seg_norms(x_, pre_g0, mem_g0, mem_g1):
        hn, = make_rowop(f_norm, "pre_norm0", 256, out_dtypes=[ACT_DTYPE])((x_,), (row(pre_g0),))
        memn0, = make_rowop(f_norm, "mem_norm0", 256, out_dtypes=[ACT_DTYPE])((mem,), (row(mem_g0),))
        memn1, = make_rowop(f_norm, "mem_norm1", 256, out_dtypes=[ACT_DTYPE])((mem,), (row(mem_g1),))
        return hn, memn0, memn1

    def seg_a1(u, qm, zm, kvm, d_skip, *ops):
        ug = u.astype(BF16).reshape(n_hi, SSM_LO, SSM_T, n_groups, SSM_GROUP).transpose(3, 1, 0, 2, 4).reshape(n_groups, n_hi * SSM_LO, SSM_T * SSM_GROUP)
        yg = make_groupop(f_s5, "s5", SSM_GB)(ug, *ops)
        y = yg.reshape(n_groups, SSM_LO, n_hi, SSM_T, SSM_GROUP).transpose(2, 1, 3, 0, 4).reshape(n_rows, main_w)
        ygelu, = make_rowop(f_gate_a1, "gate_a1", 256, out_dtypes=[ACT_DTYPE])((y, u), (row(d_skip),))
        return ygelu, mem_attn(0, qm, zm, kvm)

    def seg_a2(ygelu, t, z, memo, b_glu):
        main, = make_rowop(f_gate_a2, "gate_a2", 256, out_dtypes=[ACT_DTYPE])((ygelu, t, z), (row(b_glu),))
        return jnp.concatenate([main, memo], axis=1)

    def seg_post_a(x_, o, post_g0, kv_g, pre_g1):
        return make_rowop(f_post_a, "post_a", 128, out_dtypes=[F32, ACT_DTYPE, ACT_DTYPE])((x_, o), (row(post_g0), row(kv_g), row(pre_g1)))

    def seg_b(q, z, qm, zm, k, v, kv_in, kvm, w_fgate, b_fgate):
        w_fg = jnp.pad(w_fgate, ((0, 0), (0, LANE - n_heads)))
        b_fg = jnp.pad(b_fgate, (0, LANE - n_heads)).reshape(1, LANE)
        logf, = make_rowop(f_logf, "logf", 512)((make_mm("fgate")(kv_in, w_fg),), (b_fg,))
        fcum = make_cumsum("fcum")(logf)[:, :n_heads].T
        att = make_fox("fox")(q, k, v, fcum[:, None, :])
        main, = make_rowop(f_gate_b, "gate_b", 256, out_dtypes=[ACT_DTYPE])((att, z), ())
        return jnp.concatenate([main, mem_attn(1, qm, zm, kvm)], axis=1)


    (hn0, memn0, memn1), vjp_norms = jax.vjp(seg_norms, x, p['pre_norm_g'][0], p['mem_norm_g'][0], p['mem_norm_g'][1])
    s5_names = ('lam_re', 'lam_im', 'log_step', 'b_re', 'b_im', 'c_re', 'c_im')
    ops, vjp_ops = jax.vjp(lambda *a: s5_operators(*a, n_hi), *[p[n] for n in s5_names])
    proj_a = mm_fwd("in_a", hn0, ops[0])
    kvm0 = mm_fwd("mem_kv0", memn0, proj_a)
    u_a, z_a, qm_a, zm_a = split(proj_a)
    (ygelu, memo0), vjp_a1 = jax.vjp(seg_a1, u_a, qm_a, zm_a, kvm0, p['d_skip'], *ops)
    t = mm_fwd("glu", ygelu)
    cat0, vjp_a2 = jax.vjp(seg_a2, ygelu, t, z_a, memo0, p['b_glu'])
    o0 = mm_fwd("out0", cat0)
    (h1, kv_in, hn1), vjp_post_a = jax.vjp(seg_post_a, x, o0, p['post_norm_g'][0], p['kv_norm_g'], p['pre_norm_g'][1])
    kv = mm_fwd("kv", kv_in)
    proj_b = mm_fwd("in_b", hn1)
    kvm1 = mm_fwd("mem_kv1", memn1, proj_b)
    cat1, vjp_b = jax.vjp(seg_b, *split(proj_b), kv[:, :main_w], kv[:, main_w:], kv_in, kvm1, p['w_fgate'], p['b_fgate'])
    o1 = mm_fwd("out1", cat1)
    rowloss, d_h1, d_o1, g_post_g1 = loss_head("final", h1, o1, tgt, row(p['post_norm_g'][1]))
    loss = jnp.sum(rowloss)

    g = {}
    g_post_g1 = g_post_g1.reshape(-1)
    *d_proj_b, d_k, d_v, d_kv_in, d_kvm1, g['w_fgate'], g['b_fgate'] = vjp_b(mm_bwd("out1", d_o1))
    d_memn1 = mm_bwd("mem_kv1", d_kvm1)
    d_hn1 = mm_bwd("in_b", jnp.concatenate(d_proj_b, axis=1))
    d_kv_in = d_kv_in + mm_bwd("kv", jnp.concatenate([d_k, d_v], axis=1))
    d_x, d_o0, g_post_g0, g['kv_norm_g'], g_pre_g1 = vjp_post_a((d_h1, d_kv_in, d_hn1))
    d_ygelu, d_t, d_z, d_memo0, g['b_glu'] = vjp_a2(mm_bwd("out0", d_o0))
    d_ygelu = d_ygelu + mm_bwd("glu", d_t)
    d_u, d_qm, d_zm, d_kvm0, g['d_skip'], *d_ops = vjp_a1((d_ygelu, d_memo0))
    d_proj_a = jnp.concatenate([d_u, d_z, d_qm, d_zm], axis=1)
    g.update(zip(s5_names, vjp_ops(tuple(d_ops))))
    g['pre_norm_g'] = jnp.stack([jnp.zeros_like(g_pre_g1), g_pre_g1])
    g['post_norm_g'] = jnp.stack([g_post_g0, g_post_g1])
    g['mem_norm_g'] = jnp.zeros_like(p['mem_norm_g'])
    early = start_early(g)
    d_memn0 = mm_bwd("mem_kv0", d_kvm0)
    d_x2, *g_late = vjp_norms((mm_da("in_a", d_proj_a, early['token']), d_memn0, d_memn1))
    late = start_late(g_late)
    mm_dw("in_a", d_proj_a, late['token'])
    return loss, d_x + d_x2, pending, early, late


def kernel(x, mem, pre_norm_g, post_norm_g, w_in_a, lam_re, lam_im, log_step, b_re, b_im, c_re, c_im, d_skip, w_glu, b_glu, kv_norm_g, w_kv, w_fgate, b_fgate, w_in_b, mem_norm_g, w_mem_kv, w_out, loss_target, m_pre_norm_g, m_post_norm_g, m_w_in_a, m_lam_re, m_lam_im, m_log_step, m_b_re, m_b_im, m_c_re, m_c_im, m_d_skip, m_w_glu, m_b_glu, m_kv_norm_g, m_w_kv, m_w_fgate, m_b_fgate, m_w_in_b, m_mem_norm_g, m_w_mem_kv, m_w_out, v_pre_norm_g, v_post_norm_g, v_w_in_a, v_lam_re, v_lam_im, v_log_step, v_b_re, v_b_im, v_c_re, v_c_im, v_d_skip, v_w_glu, v_b_glu, v_kv_norm_g, v_w_kv, v_w_fgate, v_b_fgate, v_w_in_b, v_mem_norm_g, v_w_mem_kv, v_w_out):
    a = dict(zip(INPUTS, (x, mem, pre_norm_g, post_norm_g, w_in_a, lam_re, lam_im, log_step, b_re, b_im, c_re, c_im, d_skip, w_glu, b_glu, kv_norm_g, w_kv, w_fgate, b_fgate, w_in_b, mem_norm_g, w_mem_kv, w_out, loss_target, m_pre_norm_g, m_post_norm_g, m_w_in_a, m_lam_re, m_lam_im, m_log_step, m_b_re, m_b_im, m_c_re, m_c_im, m_d_skip, m_w_glu, m_b_glu, m_kv_norm_g, m_w_kv, m_w_fgate, m_b_fgate, m_w_in_b, m_mem_norm_g, m_w_mem_kv, m_w_out, v_pre_norm_g, v_post_norm_g, v_w_in_a, v_lam_re, v_lam_im, v_log_step, v_b_re, v_b_im, v_c_re, v_c_im, v_d_skip, v_w_glu, v_b_glu, v_kv_norm_g, v_w_kv, v_w_fgate, v_b_fgate, v_w_in_b, v_mem_norm_g, v_w_mem_kv, v_w_out)))
    me = 4 * lax.axis_index("x") + 2 * lax.axis_index("y") + lax.axis_index("c")
    n_layers = w_out.shape[0]

    small_shapes = [a[n].shape for n in SMALL_SHARDED]
    small_rows = -(-sum(math.prod(s) for s in small_shapes) // (LANE * 8)) * 8
    shards = dict(in_a=w_in_a[0], mem_kv0=w_mem_kv[0], glu=w_glu[0], out0=w_out[0], kv=w_kv, in_b=w_in_b[0], mem_kv1=w_mem_kv[1], out1=w_out[1])
    operands = [_pack([a[n] for n in SMALL_SHARDED], small_rows)] + [shards[n].astype(BF16) for n in GATHER_ORDER[1:]]
    gathered = exchange_start("ag_weights", operands, gather=True)
    small = exchange_wait(gathered, 0, gathered['token'])
    small = [jnp.stack(parts) for parts in zip(*[_unpack(small[b], small_shapes) for b in range(N_DEV)])]
    p = {n: a[n] for n in REPLICATED}
    for n in ('lam_re', 'lam_im', 'log_step', 'b_re', 'b_im', 'c_re', 'c_im'):
        p[n] = p[n][0]
    p['d_skip'] = small[0].reshape(-1)
    p['b_glu'] = small[1].reshape(-1)
    p['w_fgate'] = small[2].reshape(-1, w_fgate.shape[1])
    p['pre_norm_g'] = p['pre_norm_g'] + gathered['token'][0, 0]

    full_shapes = [a[n].shape for n in REPLICATED] + [(1, N_DEV * d_skip.shape[1]), (1, N_DEV * b_glu.shape[1]), (N_DEV * w_fgate.shape[0], w_fgate.shape[1])]
    rows = _packed_rows(full_shapes)
    late_rows = 3 * pre_norm_g.shape[1] // LANE
    start_early = lambda g: exchange_start("ag_grads", [_pack([g[n] for n in REPLICATED + SMALL_SHARDED], rows)], gather=True)
    start_late = lambda g_late: exchange_start("ag_grads_late", [_pack(g_late, late_rows)], gather=True)

    loss_local, grad_x, pending, early, late = local_step(p, gathered, x[0], mem[0], loss_target[0], start_early, start_late)
    loss = lax.psum(loss_local, MESH_AXES)

    out = {}

    def update(tag, name, layer=None):
        pick = (lambda t: t) if layer is None else (lambda t: t[layer])
        parts = exchange_wait(pending[tag], 0, grad_x)
        two_d = lambda t: pick(t).reshape(parts.shape[1:])
        res = adamw_reduce("adamw_" + tag, parts, two_d(a[name]), two_d(a['m_' + name]), two_d(a['v_' + name]))
        return [r.reshape(pick(a[name]).shape) for r in res]

    per_layer = {n: [None] * n_layers for n in ('w_out', 'w_mem_kv')}
    per_layer['w_out'][1] = update("out1", 'w_out', 1)
    per_layer['w_mem_kv'][1] = update("mem_kv1", 'w_mem_kv', 1)
    out['w_in_b'] = [r[None] for r in update("in_b", 'w_in_b', 0)]
    out['w_kv'] = update("kv", 'w_kv')
    per_layer['w_out'][0] = update("out0", 'w_out', 0)
    per_layer['w_mem_kv'][0] = update("mem_kv0", 'w_mem_kv', 0)
    out['w_glu'] = [r[None] for r in update("glu", 'w_glu', 0)]
    for name in per_layer:
        out[name] = [jnp.stack(t) for t in zip(*per_layer[name])]

    zeros = [jnp.zeros(s, F32) for s in full_shapes[len(REPLICATED):]]
    packed = lambda pre: _pack([a[pre + n] for n in REPLICATED] + zeros, rows)
    res = [_unpack(r, full_shapes) for r in adamw_reduce("adamw_small", exchange_wait(early, 0, grad_x), packed(''), packed('m_'), packed('v_'))]
    for i, n in enumerate(REPLICATED):
        out[n] = [r[i] for r in res]
    g_full = res[0][len(REPLICATED):]
    g_shard = [lax.dynamic_slice_in_dim(g_full[0], me * d_skip.shape[1], d_skip.shape[1], 1),
               lax.dynamic_slice_in_dim(g_full[1], me * b_glu.shape[1], b_glu.shape[1], 1),
               lax.dynamic_slice_in_dim(g_full[2], me * w_fgate.shape[0], w_fgate.shape[0], 0)]
    packed = lambda pre: _pack([a[pre + n] for n in SMALL_SHARDED], small_rows)
    res = [_unpack(r, small_shapes) for r in adamw_reduce("adamw_small_sharded", _pack(g_shard, small_rows)[None], packed(''), packed('m_'), packed('v_'))]
    for i, n in enumerate(SMALL_SHARDED):
        out[n] = [r[i] for r in res]
    packed = lambda pre: _pack([a[pre + 'pre_norm_g'][0], a[pre + 'mem_norm_g'][0], a[pre + 'mem_norm_g'][1]], late_rows)
    res = [_unpack(r, [pre_norm_g.shape[1:]] * 3) for r in adamw_reduce("adamw_small_late", exchange_wait(late, 0, grad_x), packed(''), packed('m_'), packed('v_'))]
    out['pre_norm_g'] = [jnp.stack([r[0], o[1]]) for r, o in zip(res, out['pre_norm_g'])]
    out['mem_norm_g'] = [jnp.stack([r[1], r[2]]) for r in res]
    out['w_in_a'] = [r[None] for r in update("in_a", 'w_in_a', 0)]

    return (loss, grad_x[None], *[out[n][k] for k in range(4) for n in WEIGHTS])
```

```python
import functools
import math

import jax
import jax.numpy as jnp
from jax import lax
from jax.experimental import pallas as pl
from jax.experimental.pallas import tpu as pltpu

F32 = jnp.float32
BF16 = jnp.bfloat16
HP = lax.Precision.HIGHEST
MESH_AXES = ("x", "y", "c")
N_DEV = 8
V7X_VMEM_LIMIT = 56 * 1024 * 1024
LANE = 128

EPS = 1e-6
HEAD_DIM = 128
SSM_GROUP = 16
SSM_STATE = 64
SSM_T = 8
SSM_LO = 16
SSM_GB = 12
FOX_BQ = 256
GRAD_DTYPE = BF16
ACT_DTYPE = BF16
ADAM_LR = 0.001
ADAM_B1 = 0.9
ADAM_B2 = 0.999
ADAM_EPS = 1e-08
ADAM_WD = 0.01
ADAM_STEP = 10

WEIGHTS = ['pre_norm_g', 'post_norm_g', 'w_in_a', 'lam_re', 'lam_im', 'log_step', 'b_re', 'b_im', 'c_re', 'c_im', 'd_skip',
           'w_glu', 'b_glu', 'kv_norm_g', 'w_kv', 'w_fgate', 'b_fgate', 'w_in_b', 'mem_norm_g', 'w_mem_kv', 'w_out']
INPUTS = ['x', 'mem'] + WEIGHTS + ['loss_target'] + ['m_' + n for n in WEIGHTS] + ['v_' + n for n in WEIGHTS]
REPLICATED = ['pre_norm_g', 'post_norm_g', 'lam_re', 'lam_im', 'log_step', 'b_re', 'b_im', 'c_re', 'c_im', 'kv_norm_g',
              'b_fgate', 'mem_norm_g']
SMALL_SHARDED = ['d_skip', 'b_glu', 'w_fgate']


def _cparams(sem=None):
    return pltpu.CompilerParams(dimension_semantics=sem, vmem_limit_bytes=V7X_VMEM_LIMIT)


def _tile(n, cap):
    if n <= cap:
        return n
    best = None
    for t in range(LANE, cap + 1, LANE):
        if n % t == 0:
            best = t
    assert best is not None, (n, cap)
    return best


def _matmul(name, a, b, a_spec, b_spec, o_spec, out_shape, grid, dims, nk, after=None):
    def body(a_ref, b_ref, *rest):
        o_ref, acc_ref = rest[-2:]
        k = pl.program_id(2)
        part = lax.dot_general(a_ref[...].astype(BF16), b_ref[...].astype(BF16), dims, preferred_element_type=F32)

        @pl.when(k == 0)
        def _():
            acc_ref[...] = part

        @pl.when(k > 0)
        def _():
            acc_ref[...] += part

        @pl.when(k == nk - 1)
        def _():
            o_ref[...] = acc_ref[...].astype(o_ref.dtype)

    acc_shape = tuple(d for d in o_spec.block_shape if d is not None)
    extra = [] if after is None else [after]
    return pl.pallas_call(
        body, name=name, grid=grid, in_specs=[a_spec, b_spec] + [pl.BlockSpec(memory_space=pl.ANY)] * len(extra), out_specs=o_spec,
        out_shape=out_shape, scratch_shapes=[pltpu.VMEM(acc_shape, F32)],
        compiler_params=_cparams(("parallel", "parallel", "arbitrary")),
    )(a, b, *extra)


NN = (((1,), (0,)), ((), ()))
NT = (((1,), (1,)), ((), ()))
TN = (((0,), (0,)), ((), ()))


def _mm_nn(name, a, b, out_dtype=ACT_DTYPE):
    (m, k), (_, n) = a.shape, b.shape
    bm, bn, bk = _tile(m, 1024), _tile(n, 512), _tile(k, 2048)
    return _matmul(name, a, b, pl.BlockSpec((bm, bk), lambda i, j, kk: (i, kk)), pl.BlockSpec((bk, bn), lambda i, j, kk: (kk, j)),
                   pl.BlockSpec((bm, bn), lambda i, j, kk: (i, j)), jax.ShapeDtypeStruct((m, n), out_dtype), (m // bm, n // bn, k // bk), NN, k // bk)


def _mm_nt(name, a, b, after=None, out_dtype=ACT_DTYPE):
    (m, c), (n, _) = a.shape, b.shape
    bm, bn, bk = _tile(m, 1024), _tile(n, 512), _tile(c, 2048)
    return _matmul(name, a, b, pl.BlockSpec((bm, bk), lambda i, j, kk: (i, kk)), pl.BlockSpec((bn, bk), lambda i, j, kk: (j, kk)),
                   pl.BlockSpec((bm, bn), lambda i, j, kk: (i, j)), jax.ShapeDtypeStruct((m, n), out_dtype), (m // bm, n // bn, c // bk), NT, c // bk, after)


def _mm_tn(name, a, b, out_dtype=F32, after=None):
    (c, m), (_, n) = a.shape, b.shape
    bm, bn, bk = _tile(m, 1024), _tile(n, 512), _tile(c, 2048)
    return _matmul(name, a, b, pl.BlockSpec((bk, bm), lambda i, j, kk: (kk, i)), pl.BlockSpec((bk, bn), lambda i, j, kk: (kk, j)),
                   pl.BlockSpec((bm, bn), lambda i, j, kk: (i, j)), jax.ShapeDtypeStruct((m, n), out_dtype), (m // bm, n // bn, c // bk), TN, c // bk, after)


def _mm_nn_cb_split(name, a, bb, widths):
    (m, k), (nb, _, ns) = a.shape, bb.shape
    bm = _tile(m, 1024)
    his = [sum(widths[:p + 1]) // ns for p in range(len(widths))]
    los = [0] + his[:-1]
    assert his[-1] == nb and all(w % ns == 0 for w in widths), (widths, ns, nb)

    def body(a_ref, b_ref, *o_refs):
        j = pl.program_id(1)
        res = jnp.dot(a_ref[...].astype(BF16), b_ref[...].astype(BF16), preferred_element_type=F32)
        for o_ref, lo, hi in zip(o_refs, los, his):
            @pl.when((j >= lo) & (j < hi))
            def _(o_ref=o_ref):
                o_ref[...] = res.astype(o_ref.dtype)

    return pl.pallas_call(
        body, name=name, grid=(m // bm, nb),
        in_specs=[pl.BlockSpec((bm, k), lambda i, j: (i, 0)), pl.BlockSpec((None, k, ns), lambda i, j: (j, 0, 0))],
        out_specs=[pl.BlockSpec((bm, ns), lambda i, j, lo=lo, n=hi - lo: (i, jnp.clip(j - lo, 0, n - 1))) for lo, hi in zip(los, his)],
        out_shape=[jax.ShapeDtypeStruct((m, w), ACT_DTYPE) for w in widths],
        compiler_params=_cparams(("parallel", "arbitrary")),
    )(a, bb)


def _mm_nt_cb(name, dy, bb, after=None):
    m, (nb, k, ns) = dy.shape[0], bb.shape
    bm, bn = _tile(m, 1024), _tile(k, 512)

    def body(dy_ref, b_ref, *rest):
        o_ref = rest[-1]
        acc = lax.dot_general(dy_ref[:, 0:ns], b_ref[0], NT, preferred_element_type=F32)
        for j in range(1, nb):
            acc += lax.dot_general(dy_ref[:, j * ns:(j + 1) * ns], b_ref[j], NT, preferred_element_type=F32)
        o_ref[...] = acc.astype(o_ref.dtype)

    extra = [] if after is None else [after]
    return pl.pallas_call(
        body, name=name, grid=(m // bm, k // bn),
        in_specs=[pl.BlockSpec((bm, nb * ns), lambda i, j: (i, 0)), pl.BlockSpec((nb, bn, ns), lambda i, j: (0, j, 0))]
        + [pl.BlockSpec(memory_space=pl.ANY)] * len(extra),
        out_specs=pl.BlockSpec((bm, bn), lambda i, j: (i, j)), out_shape=jax.ShapeDtypeStruct((m, k), ACT_DTYPE),
        compiler_params=_cparams(("parallel", "parallel")),
    )(dy, bb, *extra)


def _mm_tn_cb(name, a, dy, ns, out_dtype=F32, after=None):
    (c, k), nb = a.shape, dy.shape[1] // ns
    bm = _tile(k, 1024)
    return _matmul(name, a, dy, pl.BlockSpec((c, bm), lambda i, j, kk: (0, i)), pl.BlockSpec((c, ns), lambda i, j, kk: (0, j)),
                   pl.BlockSpec((None, bm, ns), lambda i, j, kk: (j, i, 0)), jax.ShapeDtypeStruct((nb, k, ns), out_dtype), (k // bm, nb, 1), TN, 1, after)


def make_mm(name):
    @jax.custom_vjp
    def mm(a, b):
        return _mm_nn(name, a, b, F32)

    def fwd(a, b):
        return mm(a, b), (a, b)

    def bwd(res, dy):
        a, b = res
        return _mm_nt(name + "_da", dy, b, out_dtype=a.dtype), _mm_tn(name + "_dw", a, dy, b.dtype)

    mm.defvjp(fwd, bwd)
    return mm


def _rowop_specs(rows, params, bm, nl):
    row_specs = [pl.BlockSpec((bm, r.shape[1] // nl), lambda j, i: (i, j)) for r in rows]
    par_specs = [pl.BlockSpec((p.shape[0], p.shape[1] // nl), lambda j, i: (0, j)) for p in params]
    row_blk = [jax.ShapeDtypeStruct((bm, r.shape[1] // nl), r.dtype) for r in rows]
    par_blk = [jax.ShapeDtypeStruct((p.shape[0], p.shape[1] // nl), p.dtype) for p in params]
    return row_specs, par_specs, row_blk, par_blk


def make_rowop(f, name, bm, nl=1, out_dtypes=None):
    def loaded(refs):
        return [r[...].astype(F32) for r in refs]

    def fwd_call(rows, params):
        n_rows = rows[0].shape[0]
        b = min(bm, n_rows)
        row_specs, par_specs, row_blk, par_blk = _rowop_specs(rows, params, b, nl)
        out_blk = jax.eval_shape(f, *[jax.ShapeDtypeStruct(t.shape, F32) for t in row_blk + par_blk])
        dts = out_dtypes or [F32] * len(out_blk)
        nr, npar = len(rows), len(params)

        def body(*refs):
            for o_ref, o in zip(refs[nr + npar:], f(*loaded(refs[:nr + npar]))):
                o_ref[...] = o.astype(o_ref.dtype)

        return pl.pallas_call(
            body, name=name, grid=(nl, n_rows // b), in_specs=row_specs + par_specs,
            out_specs=[pl.BlockSpec(o.shape, lambda j, i: (i, j)) for o in out_blk],
            out_shape=[jax.ShapeDtypeStruct((n_rows, o.shape[1] * nl), dt) for o, dt in zip(out_blk, dts)],
            compiler_params=_cparams(("parallel", "parallel")),
        )(*rows, *params)

    def bwd_call(rows, params, cts):
        n_rows = rows[0].shape[0]
        b = min(bm, n_rows)
        row_specs, par_specs, row_blk, par_blk = _rowop_specs(rows, params, b, nl)
        ct_specs = [pl.BlockSpec((b, c.shape[1] // nl), lambda j, i: (i, j)) for c in cts]
        nr, npar, nct = len(rows), len(params), len(cts)

        def body(*refs):
            i = pl.program_id(1)
            _, vjp = jax.vjp(lambda *v: tuple(f(*v)), *loaded(refs[:nr + npar]))
            grads = vjp(tuple(loaded(refs[nr + npar:nr + npar + nct])))
            outs = refs[nr + npar + nct:]
            for o_ref, g in zip(outs[:nr], grads[:nr]):
                o_ref[...] = g.astype(o_ref.dtype)
            for o_ref, g in zip(outs[nr:], grads[nr:]):
                @pl.when(i == 0)
                def _(o_ref=o_ref, g=g):
                    o_ref[...] = g

                @pl.when(i > 0)
                def _(o_ref=o_ref, g=g):
                    o_ref[...] += g

        outs = pl.pallas_call(
            body, name=name + "_bwd", grid=(nl, n_rows // b), in_specs=row_specs + par_specs + ct_specs,
            out_specs=row_specs + par_specs,
            out_shape=[jax.ShapeDtypeStruct(a.shape, a.dtype) for a in rows] + [jax.ShapeDtypeStruct(a.shape, F32) for a in params],
            compiler_params=_cparams(("arbitrary", "arbitrary")),
        )(*rows, *params, *cts)
        return tuple(outs[:nr]), tuple(g.astype(a.dtype) for g, a in zip(outs[nr:], params))

    @jax.custom_vjp
    def op(rows, params):
        return tuple(fwd_call(rows, params))

    def fwd(rows, params):
        return op(rows, params), (rows, params)

    def bwd(res, cts):
        rows, params = res
        return bwd_call(rows, params, tuple(cts))

    op.defvjp(fwd, bwd)
    return op


def make_groupop(f, name, gb):
    def specs(arrs):
        return [pl.BlockSpec((gb,) + a.shape[1:], lambda g: (g, 0, 0)) for a in arrs]

    def fwd_call(arrs):
        g_n = arrs[0].shape[0]
        out_blk = jax.eval_shape(f, *[jax.ShapeDtypeStruct((gb,) + a.shape[1:], a.dtype) for a in arrs])[0]
        n = len(arrs)

        def body(*refs):
            for i, o in enumerate(f(*[r[...] for r in refs[:n]])):
                refs[n][i] = o.astype(refs[n].dtype)

        return pl.pallas_call(
            body, name=name, grid=(g_n // gb,), in_specs=specs(arrs),
            out_specs=pl.BlockSpec((gb,) + out_blk.shape, lambda g: (g, 0, 0)),
            out_shape=jax.ShapeDtypeStruct((g_n,) + out_blk.shape, ACT_DTYPE),
            compiler_params=_cparams(("parallel",)),
        )(*arrs)

    def bwd_call(arrs, ct):
        g_n = arrs[0].shape[0]
        n = len(arrs)

        def body(*refs):
            _, vjp = jax.vjp(lambda *v: tuple(f(*v)), *[r[...] for r in refs[:n]])
            for o_ref, g in zip(refs[n + 1:], vjp(tuple(refs[n][i].astype(F32) for i in range(gb)))):
                o_ref[...] = g.astype(o_ref.dtype)

        return pl.pallas_call(
            body, name=name + "_bwd", grid=(g_n // gb,), in_specs=specs(arrs) + specs([ct]), out_specs=specs(arrs),
            out_shape=[jax.ShapeDtypeStruct(a.shape, a.dtype) for a in arrs],
            compiler_params=_cparams(("parallel",)),
        )(*arrs, ct)

    @jax.custom_vjp
    def op(*arrs):
        return fwd_call(arrs)

    def fwd(*arrs):
        return op(*arrs), arrs

    def bwd(arrs, ct):
        return tuple(bwd_call(arrs, ct))

    op.defvjp(fwd, bwd)
    return op


@jax.custom_vjp
def _rms(x, g):
    return x * lax.rsqrt(jnp.mean(x * x, axis=-1, keepdims=True) + EPS) * g


def _rms_fwd(x, g):
    r = lax.rsqrt(jnp.mean(x * x, axis=-1, keepdims=True) + EPS)
    return x * r * g, (x, g, r)


def _rms_bwd(res, dy):
    x, g, r = res
    n, t = x * r, dy * g
    return r * (t - n * jnp.mean(n * t, axis=-1, keepdims=True)), jnp.sum(dy * n, axis=0, keepdims=True)


_rms.defvjp(_rms_fwd, _rms_bwd)


def _silu(z):
    return z * jax.nn.sigmoid(z)


def _log_sigmoid(x):
    return jnp.minimum(x, 0.0) - jnp.log(1.0 + jnp.exp(-jnp.abs(x)))


def f_norm(x, g):
    return (_rms(x, g),)


def f_gate_a1(y, u, d):
    return (jax.nn.gelu(y + d * u),)


def f_gate_a2(yg, t, z, b):
    return (yg * jax.nn.sigmoid(t + b) * _silu(z),)


def f_gate_b(att, z):
    return (att * _silu(z),)


def f_post_a(h, o, post_g, kv_g, pre_g):
    h1 = h + _rms(o, post_g)
    return h1, _rms(h1, kv_g), _rms(h1, pre_g)


def f_final(h, o, tgt, post_g):
    err = h + _rms(o, post_g) - tgt
    return (0.5 * jnp.mean(err * err, axis=-1, keepdims=True),)


def f_logf(gl, b):
    return (_log_sigmoid(gl + b),)


def _mxu(a, b, dims):
    return lax.dot_general(a.astype(BF16), b.astype(BF16), dims, preferred_element_type=F32)


@jax.custom_vjp
def _mxu_nn(a, b):
    return _mxu(a, b, NN)


def _mxu_nn_bwd(res, g):
    a, b = res
    return _mxu(g, b, NT).astype(a.dtype), _mxu(a, g, TN).astype(b.dtype)


_mxu_nn.defvjp(lambda a, b: (_mxu(a, b, NN), (a, b)), _mxu_nn_bwd)


@jax.custom_vjp
def _mxu_nt(a, b):
    return _mxu(a, b, NT)


def _mxu_nt_bwd(res, g):
    a, b = res
    return _mxu(g, b, NN).astype(a.dtype), _mxu(g, a, TN).astype(b.dtype)


_mxu_nt.defvjp(lambda a, b: (_mxu(a, b, NT), (a, b)), _mxu_nt_bwd)


def f_memattn(q, zm, km, vm):
    o = _mxu_nn(_softmax(_mxu_nt(q * (HEAD_DIM ** -0.5), km)), vm)
    return (o * _silu(zm),)


@jax.custom_vjp
def _softmax(s):
    e = jnp.exp(s - jnp.max(s, axis=-1, keepdims=True))
    return e * (1.0 / jnp.sum(e, axis=-1, keepdims=True))


def _softmax_fwd(s):
    p = _softmax(s)
    return p, p


def _softmax_bwd(p, dp):
    return (p * (dp - jnp.sum(dp * p, axis=-1, keepdims=True)),)


_softmax.defvjp(_softmax_fwd, _softmax_bwd)


def _fox_block(q, k, v, fr):
    bq = q.shape[0]
    s = _mxu_nt(q * (HEAD_DIM ** -0.5), k) - fr
    tri = lax.broadcasted_iota(jnp.int32, (bq, bq), 0) >= lax.broadcasted_iota(jnp.int32, (bq, bq), 1)
    diag = jnp.where(tri, s[:, -bq:], -1e30)
    s = diag if k.shape[0] == bq else jnp.concatenate([s[:, :-bq], diag], axis=1)
    return _mxu_nn(_softmax(s), v)


def _cmul(ar, ai, xr, xi):
    return ar * xr - ai * xi, ar * xi + ai * xr


def _hp_dot(a, b):
    return jnp.dot(a, b, precision=HP, preferred_element_type=F32)


_bf_dot = _mxu_nn


def f_s5(u, toep, win_r, win_i, wout_r, wout_i, coef):
    gb = u.shape[0]
    n_hi = u.shape[1] // SSM_LO
    sr = [_bf_dot(u[i], win_r[i]) for i in range(gb)]
    si = [_bf_dot(u[i], win_i[i]) for i in range(gb)]

    def stacked(parts, lo):
        return jnp.concatenate([a[lo * n_hi:(lo + 1) * n_hi] for a in parts], axis=0)

    def coef_rows(k):
        return jnp.concatenate([jnp.broadcast_to(coef[i, k:k + 1, :], (n_hi, coef.shape[2])) for i in range(gb)], axis=0)

    at_r, at_i = coef_rows(0), coef_rows(1)
    pr, pi = stacked(sr, 0), stacked(si, 0)
    for lo in range(1, SSM_LO):
        dr, di = _cmul(at_r, at_i, pr, pi)
        pr, pi = stacked(sr, lo) + dr, stacked(si, lo) + di
    n = 2 * gb * n_hi
    ri = lax.broadcasted_iota(jnp.int32, (n, n), 0)
    ci = lax.broadcasted_iota(jnp.int32, (n, n), 1)
    same = ri // n_hi == ci // n_hi

    def shifted(d, zr, zi):
        z = _hp_dot((same & (ri - ci == d)).astype(F32), jnp.concatenate([zr, zi], axis=0))
        return z[:n // 2], z[n // 2:]

    d, step = 1, 0
    while d < n_hi:
        dr, di = _cmul(coef_rows(2 + 2 * step), coef_rows(3 + 2 * step), *shifted(d, pr, pi))
        pr, pi = pr + dr, pi + di
        d, step = 2 * d, step + 1
    er, ei = shifted(1, pr, pi)
    xr, xi = [er], [ei]
    for lo in range(1, SSM_LO):
        dr, di = _cmul(at_r, at_i, xr[-1], xi[-1])
        xr.append(stacked(sr, lo - 1) + dr)
        xi.append(stacked(si, lo - 1) + di)

    def group_rows(parts, i):
        return jnp.concatenate([a[i * n_hi:(i + 1) * n_hi] for a in parts], axis=0)

    return tuple(_bf_dot(u[i], toep[i]) + _bf_dot(group_rows(xr, i), wout_r[i]) + _bf_dot(group_rows(xi, i), wout_i[i])
                 for i in range(gb))


def s5_operators(lam_re, lam_im, log_step, b_re, b_im, c_re, c_im, n_hi):
    t_n = SSM_T
    lr, li = lam_re, lam_im
    dt = jnp.exp(log_step)[:, None]
    mag = jnp.exp(lr * dt)
    ar, ai = mag * jnp.cos(li * dt), mag * jnp.sin(li * dt)
    den = lr * lr + li * li
    cr = ((ar - 1.0) * lr + ai * li) / den
    ci = (ai * lr - (ar - 1.0) * li) / den
    bbr = cr[..., None] * b_re - ci[..., None] * b_im
    bbi = cr[..., None] * b_im + ci[..., None] * b_re
    k = jnp.arange(t_n + 1, dtype=F32)[:, None, None]
    pm, ang = jnp.exp(k * (lr * dt)), k * (li * dt)
    pr, pi = pm * jnp.cos(ang), pm * jnp.sin(ang)
    abr = pr[..., None] * bbr - pi[..., None] * bbi
    abi = pr[..., None] * bbi + pi[..., None] * bbr
    kk = (jnp.einsum('ghp,kgpj->kghj', c_re, abr[:t_n], precision=HP)
          - jnp.einsum('ghp,kgpj->kghj', c_im, abi[:t_n], precision=HP))
    lag = jnp.arange(t_n)[None, :] - jnp.arange(t_n)[:, None]
    onehot = (lag[None] == jnp.arange(t_n)[:, None, None]).astype(F32)
    g_n, h_n = c_re.shape[0], c_re.shape[1]
    toep = jnp.einsum('kst,kghj->gsjth', onehot, kk, precision=HP).reshape(g_n, t_n * h_n, t_n * h_n)
    win_r = abr[:t_n][::-1].transpose(1, 0, 3, 2).reshape(g_n, t_n * h_n, -1)
    win_i = abi[:t_n][::-1].transpose(1, 0, 3, 2).reshape(g_n, t_n * h_n, -1)
    p1r, p1i = pr[1:, :, None, :], pi[1:, :, None, :]
    wout_r = (c_re[None] * p1r - c_im[None] * p1i).transpose(1, 3, 0, 2).reshape(g_n, -1, t_n * h_n)
    wout_i = (-(c_re[None] * p1i + c_im[None] * p1r)).transpose(1, 3, 0, 2).reshape(g_n, -1, t_n * h_n)
    rows = [pr[t_n], pi[t_n]]
    qr, qi = pr[t_n], pi[t_n]
    for _ in range(int(math.log2(SSM_LO))):
        qr, qi = qr * qr - qi * qi, 2.0 * qr * qi
    d = 1
    while d < n_hi:
        rows += [qr, qi]
        qr, qi = qr * qr - qi * qi, 2.0 * qr * qi
        d *= 2
    coef = jnp.stack(rows, axis=1)
    return toep, win_r, win_i, wout_r, wout_i, coef


def _cumsum_call(name, x, reverse):
    n_rows, w = x.shape
    bm = min(256, n_rows)
    nb = n_rows // bm

    def body(x_ref, o_ref, carry_ref):
        i = pl.program_id(0)

        @pl.when(i == 0)
        def _():
            carry_ref[...] = jnp.zeros_like(carry_ref)

        ri = lax.broadcasted_iota(jnp.int32, (bm, bm), 0)
        ci = lax.broadcasted_iota(jnp.int32, (bm, bm), 1)
        tri = ((ri <= ci) if reverse else (ri >= ci)).astype(F32)
        xb = x_ref[...]
        o_ref[...] = _hp_dot(tri, xb) + carry_ref[...]
        carry_ref[...] += jnp.sum(xb, axis=0, keepdims=True)

    idx = (lambda i: (nb - 1 - i, 0)) if reverse else (lambda i: (i, 0))
    return pl.pallas_call(
        body, name=name, grid=(nb,), in_specs=[pl.BlockSpec((bm, w), idx)], out_specs=pl.BlockSpec((bm, w), idx),
        out_shape=jax.ShapeDtypeStruct(x.shape, F32), scratch_shapes=[pltpu.VMEM((1, w), F32)],
        compiler_params=_cparams(("arbitrary",)),
    )(x)


def make_cumsum(name):
    @jax.custom_vjp
    def cs(x):
        return _cumsum_call(name, x, False)

    def fwd(x):
        return cs(x), None

    def bwd(_, dy):
        return (_cumsum_call(name + "_bwd", dy, True),)

    cs.defvjp(fwd, bwd)
    return cs


def _fox_specs(n_rows, bq):
    q_spec = pl.BlockSpec((bq, HEAD_DIM), lambda h, i: (i, h))
    kv_spec = pl.BlockSpec((n_rows, HEAD_DIM), lambda h, i: (0, h))
    fr_spec = pl.BlockSpec((None, 1, n_rows), lambda h, i: (h, 0, 0))
    return q_spec, kv_spec, fr_spec


def _fox_fwd_call(name, q, k, v, fr):
    n_rows, width = q.shape
    bq = min(FOX_BQ, n_rows)
    nq = n_rows // bq
    q_spec, kv_spec, fr_spec = _fox_specs(n_rows, bq)

    def body(q_ref, k_ref, v_ref, fr_ref, o_ref):
        i = pl.program_id(1)
        for p in range(nq):
            n_keys = (p + 1) * bq

            @pl.when(i == p)
            def _(n_keys=n_keys):
                o = _fox_block(q_ref[...].astype(F32), k_ref[:n_keys, :], v_ref[:n_keys, :], fr_ref[:, :n_keys])
                o_ref[...] = o.astype(o_ref.dtype)

    return pl.pallas_call(
        body, name=name, grid=(width // HEAD_DIM, nq), in_specs=[q_spec, kv_spec, kv_spec, fr_spec], out_specs=q_spec,
        out_shape=jax.ShapeDtypeStruct(q.shape, ACT_DTYPE), compiler_params=_cparams(("parallel", "parallel")),
    )(q, k, v, fr)


def _fox_bwd_call(name, q, k, v, fr, do):
    n_rows, width = q.shape
    bq = min(FOX_BQ, n_rows)
    nq = n_rows // bq
    q_spec, kv_spec, fr_spec = _fox_specs(n_rows, bq)

    def body(q_ref, k_ref, v_ref, fr_ref, do_ref, dq_ref, dk_ref, dv_ref, dfr_ref, dk_acc, dv_acc):
        i = pl.program_id(1)

        @pl.when(i == 0)
        def _():
            dk_acc[...] = jnp.zeros_like(dk_acc)
            dv_acc[...] = jnp.zeros_like(dv_acc)
            dfr_ref[...] = jnp.zeros_like(dfr_ref)

        for p in range(nq):
            n_keys = (p + 1) * bq

            @pl.when(i == p)
            def _(n_keys=n_keys):
                _, vjp = jax.vjp(_fox_block, q_ref[...].astype(F32), k_ref[:n_keys, :].astype(F32), v_ref[:n_keys, :].astype(F32),
                                 fr_ref[:, :n_keys])
                dq, dk, dv, dfr = vjp(do_ref[...].astype(F32))
                dq_ref[...] = dq.astype(dq_ref.dtype)
                dk_acc[:n_keys, :] += dk
                dv_acc[:n_keys, :] += dv
                dfr_ref[:, :n_keys] += dfr

        @pl.when(i == nq - 1)
        def _():
            dk_ref[...] = dk_acc[...].astype(dk_ref.dtype)
            dv_ref[...] = dv_acc[...].astype(dv_ref.dtype)

    return pl.pallas_call(
        body, name=name, grid=(width // HEAD_DIM, nq), in_specs=[q_spec, kv_spec, kv_spec, fr_spec, q_spec],
        out_specs=[q_spec, kv_spec, kv_spec, fr_spec], out_shape=[jax.ShapeDtypeStruct(a.shape, a.dtype) for a in (q, k, v, fr)],
        scratch_shapes=[pltpu.VMEM((n_rows, HEAD_DIM), F32), pltpu.VMEM((n_rows, HEAD_DIM), F32)],
        compiler_params=_cparams(("parallel", "arbitrary")),
    )(q, k, v, fr, do)


def make_fox(name):
    @jax.custom_vjp
    def fox(q, k, v, fr):
        return _fox_fwd_call(name, q, k, v, fr)

    def fwd(q, k, v, fr):
        return fox(q, k, v, fr), (q, k, v, fr)

    def bwd(res, do):
        return tuple(_fox_bwd_call(name + "_bwd", *res, do))

    fox.defvjp(fwd, bwd)
    return fox


def _relations(x, y, c):
    out = []
    for rel in range(1, N_DEV):
        px = 1 - x if rel & 4 else x
        py = 1 - y if rel & 2 else y
        pc = 1 - c if rel & 1 else c
        out.append((rel, (px, py, pc), 4 * px + 2 * py + pc))
    return out


_HBM_SPEC = pl.BlockSpec(memory_space=pltpu.HBM)
_SEM_SPEC = pl.BlockSpec(memory_space=pltpu.SEMAPHORE)
_DATAFLOW = pltpu.SideEffectType.DATAFLOW_SIDE_EFFECTING


def exchange_start(name, srcs, gather):
    n = len(srcs)
    me = 4 * lax.axis_index("x") + 2 * lax.axis_index("y") + lax.axis_index("c")
    lands = []
    for s in srcs:
        own = s if gather else lax.dynamic_index_in_dim(s, me, 0, keepdims=False)
        land = lax.empty((N_DEV,) + own.shape, s.dtype)
        lands.append(lax.dynamic_update_index_in_dim(land, own, me, 0))

    def body(*refs):
        src_refs, land_refs = refs[:n], refs[n:2 * n]
        send_sems, recv_sems, token = refs[2 * n], refs[2 * n + 1], refs[-1]
        x, y, c = lax.axis_index("x"), lax.axis_index("y"), lax.axis_index("c")
        mine = 4 * x + 2 * y + c
        for k in range(n):
            for rel, peer, peer_blk in _relations(x, y, c):
                pltpu.make_async_remote_copy(
                    src_ref=src_refs[k] if gather else src_refs[k].at[peer_blk], dst_ref=land_refs[k].at[mine],
                    send_sem=send_sems.at[7 * k + rel - 1], recv_sem=recv_sems.at[7 * k + rel - 1],
                    device_id=peer, device_id_type=pl.DeviceIdType.MESH).start()
        token[...] = jnp.zeros_like(token)

    sem = pltpu.SemaphoreType.DMA((7 * n,))
    hbm = lambda t: pltpu.HBM(t.shape, t.dtype)
    outs = pl.pallas_call(
        body, name=name,
        out_shape=(sem, sem, *[hbm(s) for s in srcs], *[hbm(l) for l in lands], jax.ShapeDtypeStruct((8, LANE), F32)),
        in_specs=[_HBM_SPEC] * (2 * n),
        out_specs=(_SEM_SPEC, _SEM_SPEC, *[_HBM_SPEC] * (2 * n), pl.BlockSpec(memory_space=pltpu.VMEM)),
        input_output_aliases={i: 2 + i for i in range(2 * n)},
        compiler_params=pltpu.CompilerParams(has_side_effects=_DATAFLOW),
    )(*[pltpu.with_memory_space_constraint(t, pltpu.HBM) for t in list(srcs) + lands])
    return dict(name=name, gather=gather, send=outs[0], recv=outs[1], srcs=list(outs[2:2 + n]), lands=list(outs[2 + n:2 + 2 * n]), token=outs[-1])


def exchange_wait(handle, k, after):
    gather = handle['gather']

    def body(src_ref, land_ref, send_sems, recv_sems, after_ref, src_out, land_out):
        x, y, c = lax.axis_index("x"), lax.axis_index("y"), lax.axis_index("c")
        for rel, peer, peer_blk in _relations(x, y, c):
            copy = pltpu.make_async_remote_copy(
                src_ref=src_ref if gather else src_ref.at[peer_blk], dst_ref=land_ref.at[peer_blk],
                send_sem=send_sems.at[7 * k + rel - 1], recv_sem=recv_sems.at[7 * k + rel - 1],
                device_id=peer, device_id_type=pl.DeviceIdType.MESH)
            copy.wait_send()
            copy.wait_recv()

    src, land = handle['srcs'][k], handle['lands'][k]
    return pl.pallas_call(
        body, name=f"{handle['name']}_wait{k}", out_shape=(pltpu.HBM(src.shape, src.dtype), pltpu.HBM(land.shape, land.dtype)),
        in_specs=[_HBM_SPEC, _HBM_SPEC, _SEM_SPEC, _SEM_SPEC, pl.BlockSpec(memory_space=pl.ANY)], out_specs=(_HBM_SPEC, _HBM_SPEC),
        input_output_aliases={0: 0, 1: 1}, compiler_params=pltpu.CompilerParams(has_side_effects=_DATAFLOW),
    )(src, land, handle['send'], handle['recv'], after)[1]


def adamw_reduce(name, partials, w, m, v):
    n_part, n_rows, n_cols = partials.shape
    br = n_rows
    for cand in (512, 256, 128, 64, 32, 16, 8):
        if n_rows % cand == 0 and n_part * cand * n_cols * partials.dtype.itemsize <= (4 << 20):
            br = cand
            break

    def body(p_ref, w_ref, m_ref, v_ref, g_ref, d_ref, nm_ref, nv_ref):
        g = p_ref[0].astype(F32)
        for s in range(1, n_part):
            g = g + p_ref[s].astype(F32)
        m_new = ADAM_B1 * m_ref[...] + (1.0 - ADAM_B1) * g
        v_new = ADAM_B2 * v_ref[...] + (1.0 - ADAM_B2) * jnp.square(g)
        m_hat = m_new / (1.0 - ADAM_B1 ** ADAM_STEP)
        v_hat = v_new / (1.0 - ADAM_B2 ** ADAM_STEP)
        g_ref[...] = g
        d_ref[...] = -ADAM_LR * (m_hat / (jnp.sqrt(v_hat) + ADAM_EPS) + ADAM_WD * w_ref[...])
        nm_ref[...] = m_new
        nv_ref[...] = v_new

    spec = pl.BlockSpec((br, n_cols), lambda i: (i, 0))
    return pl.pallas_call(
        body, name=name, grid=(n_rows // br,), in_specs=[pl.BlockSpec((n_part, br, n_cols), lambda i: (0, i, 0)), spec, spec, spec],
        out_specs=[spec] * 4, out_shape=[jax.ShapeDtypeStruct((n_rows, n_cols), F32)] * 4, compiler_params=_cparams(("parallel",)),
    )(partials, w, m, v)


def _pack(arrays, n_rows):
    flat = jnp.concatenate([a.reshape(-1) for a in arrays])
    return jnp.pad(flat, (0, n_rows * LANE - flat.shape[0])).reshape(n_rows, LANE)


def _unpack(packed, shapes):
    flat, out, off = packed.reshape(-1), [], 0
    for s in shapes:
        n = math.prod(s)
        out.append(flat[off:off + n].reshape(s))
        off += n
    return out


def _packed_rows(shapes):
    n = sum(math.prod(s) for s in shapes)
    return -(-n // (LANE * 512)) * 512


GATHER_ORDER = ['small', 'in_a', 'mem_kv0', 'glu', 'out0', 'kv', 'in_b', 'mem_kv1', 'out1']


BLOCKED = ('in_a', 'kv', 'in_b')


def local_step(p, gathered, x, mem, tgt, start_early, start_late):
    n_rows, d_model = x.shape
    main_w = 3 * d_model // 4
    mem_w = d_model - main_w
    n_groups = main_w // SSM_GROUP
    n_heads = main_w // HEAD_DIM
    n_hi = n_rows // (SSM_T * SSM_LO)
    row = lambda a: a.reshape(1, -1)
    tape, pending = {}, {}

    def mm_fwd(name, act, after=None, widths=None):
        w = exchange_wait(gathered, GATHER_ORDER.index(name), act if after is None else after)
        if name not in BLOCKED:
            w = w.reshape(-1, w.shape[2])
        tape[name] = (act, w)
        return _mm_nn_cb_split(name, act, w, widths) if name in BLOCKED else _mm_nn(name, act, w)

    proj_widths = [main_w, main_w, mem_w, mem_w]

    def mm_dw(name, dy, after=None):
        act, w = tape[name]
        if name in BLOCKED:
            dw = _mm_tn_cb(name + "_dw", act, dy, w.shape[2], GRAD_DTYPE, after)
        else:
            dw = _mm_tn(name + "_dw", act, dy, GRAD_DTYPE, after)
            dw = dw.reshape(N_DEV, dw.shape[0] // N_DEV, dw.shape[1])
        pending[name] = exchange_start("rs_" + name, [dw], gather=False)

    def mm_da(name, dy, after=None):
        return (_mm_nt_cb if name in BLOCKED else _mm_nt)(name + "_da", dy, tape[name][1], after)

    def mm_bwd(name, dy):
        mm_dw(name, dy)
        return mm_da(name, dy, pending[name]['token'])

    def mem_attn(i, qm, zm, kvm):
        memo, = make_rowop(f_memattn, f"mem_attn{i}", 512, mem_w // HEAD_DIM, [ACT_DTYPE])((qm, zm), (kvm[:, :mem_w], kvm[:, mem_w:]))
        return memo

    def seg_norms(x_, pre_g0, mem_g0, mem_g1):
        hn, = make_rowop(f_norm, "pre_norm0", 256, out_dtypes=[ACT_DTYPE])((x_,), (row(pre_g0),))
        memn0, = make_rowop(f_norm, "mem_norm0", 256, out_dtypes=[ACT_DTYPE])((mem,), (row(mem_g0),))
        memn1, = make_rowop(f_norm, "mem_norm1", 256, out_dtypes=[ACT_DTYPE])((mem,), (row(mem_g1),))
        return hn, memn0, memn1

    def seg_a1(u, qm, zm, kvm, d_skip, *ops):
        ug = u.astype(BF16).reshape(n_hi, SSM_LO, SSM_T, n_groups, SSM_GROUP).transpose(3, 1, 0, 2, 4).reshape(n_groups, n_hi * SSM_LO, SSM_T * SSM_GROUP)
        yg = make_groupop(f_s5, "s5", SSM_GB)(ug, *ops)
        y = yg.reshape(n_groups, SSM_LO, n_hi, SSM_T, SSM_GROUP).transpose(2, 1, 3, 0, 4).reshape(n_rows, main_w)
        ygelu, = make_rowop(f_gate_a1, "gate_a1", 256, out_dtypes=[ACT_DTYPE])((y, u), (row(d_skip),))
        return ygelu, mem_attn(0, qm, zm, kvm)

    def seg_a2(ygelu, t, z, memo, b_glu):
        main, = make_rowop(f_gate_a2, "gate_a2", 256, out_dtypes=[ACT_DTYPE])((ygelu, t, z), (row(b_glu),))
        return jnp.concatenate([main, memo], axis=1)

    def seg_post_a(x_, o, post_g0, kv_g, pre_g1):
        return make_rowop(f_post_a, "post_a", 128, out_dtypes=[F32, ACT_DTYPE, ACT_DTYPE])((x_, o), (row(post_g0), row(kv_g), row(pre_g1)))

    def seg_b(q, z, qm, zm, k, v, kv_in, kvm, w_fgate, b_fgate):
        w_fg = jnp.pad(w_fgate, ((0, 0), (0, LANE - n_heads)))
        b_fg = jnp.pad(b_fgate, (0, LANE - n_heads)).reshape(1, LANE)
        logf, = make_rowop(f_logf, "logf", 512)((make_mm("fgate")(kv_in, w_fg),), (b_fg,))
        fcum = make_cumsum("fcum")(logf)[:, :n_heads].T
        att = make_fox("fox")(q, k, v, fcum[:, None, :])
        main, = make_rowop(f_gate_b, "gate_b", 256, out_dtypes=[ACT_DTYPE])((att, z), ())
        return jnp.concatenate([main, mem_attn(1, qm, zm, kvm)], axis=1)

    def seg_final(h1, o, post_g1):
        rowloss, = make_rowop(f_final, "final", 128)((h1, o, tgt), (row(post_g1),))
        return jnp.sum(rowloss)

    (hn0, memn0, memn1), vjp_norms = jax.vjp(seg_norms, x, p['pre_norm_g'][0], p['mem_norm_g'][0], p['mem_norm_g'][1])
    s5_names = ('lam_re', 'lam_im', 'log_step', 'b_re', 'b_im', 'c_re', 'c_im')
    ops, vjp_ops = jax.vjp(lambda *a: s5_operators(*a, n_hi), *[p[n] for n in s5_names])
    u_a, z_a, qm_a, zm_a = mm_fwd("in_a", hn0, ops[0], proj_widths)
    kvm0 = mm_fwd("mem_kv0", memn0, u_a)
    (ygelu, memo0), vjp_a1 = jax.vjp(seg_a1, u_a, qm_a, zm_a, kvm0, p['d_skip'], *ops)
    t = mm_fwd("glu", ygelu)
    cat0, vjp_a2 = jax.vjp(seg_a2, ygelu, t, z_a, memo0, p['b_glu'])
    o0 = mm_fwd("out0", cat0)
    (h1, kv_in, hn1), vjp_post_a = jax.vjp(seg_post_a, x, o0, p['post_norm_g'][0], p['kv_norm_g'], p['pre_norm_g'][1])
    k_sh, v_sh = mm_fwd("kv", kv_in, widths=[main_w, main_w])
    proj_b = mm_fwd("in_b", hn1, widths=proj_widths)
    kvm1 = mm_fwd("mem_kv1", memn1, proj_b[0])
    cat1, vjp_b = jax.vjp(seg_b, *proj_b, k_sh, v_sh, kv_in, kvm1, p['w_fgate'], p['b_fgate'])
    o1 = mm_fwd("out1", cat1)
    loss, vjp_final = jax.vjp(seg_final, h1, o1, p['post_norm_g'][1])

    g = {}
    d_h1, d_o1, g_post_g1 = vjp_final(jnp.ones((), F32))
    *d_proj_b, d_k, d_v, d_kv_in, d_kvm1, g['w_fgate'], g['b_fgate'] = vjp_b(mm_bwd("out1", d_o1))
    d_memn1 = mm_bwd("mem_kv1", d_kvm1)
    d_hn1 = mm_bwd("in_b", jnp.concatenate(d_proj_b, axis=1))
    d_kv_in = d_kv_in + mm_bwd("kv", jnp.concatenate([d_k, d_v], axis=1))
    d_x, d_o0, g_post_g0, g['kv_norm_g'], g_pre_g1 = vjp_post_a((d_h1, d_kv_in, d_hn1))
    d_ygelu, d_t, d_z, d_memo0, g['b_glu'] = vjp_a2(mm_bwd("out0", d_o0))
    d_ygelu = d_ygelu + mm_bwd("glu", d_t)
    d_u, d_qm, d_zm, d_kvm0, g['d_skip'], *d_ops = vjp_a1((d_ygelu, d_memo0))
    d_proj_a = jnp.concatenate([d_u, d_z, d_qm, d_zm], axis=1)
    g.update(zip(s5_names, vjp_ops(tuple(d_ops))))
    g['pre_norm_g'] = jnp.stack([jnp.zeros_like(g_pre_g1), g_pre_g1])
    g['post_norm_g'] = jnp.stack([g_post_g0, g_post_g1])
    g['mem_norm_g'] = jnp.zeros_like(p['mem_norm_g'])
    early = start_early(g)
    d_memn0 = mm_bwd("mem_kv0", d_kvm0)
    d_x2, *g_late = vjp_norms((mm_da("in_a", d_proj_a, early['token']), d_memn0, d_memn1))
    late = start_late(g_late)
    mm_dw("in_a", d_proj_a, late['token'])
    return loss, d_x + d_x2, pending, early, late


def kernel(x, mem, pre_norm_g, post_norm_g, w_in_a, lam_re, lam_im, log_step, b_re, b_im, c_re, c_im, d_skip, w_glu, b_glu, kv_norm_g, w_kv, w_fgate, b_fgate, w_in_b, mem_norm_g, w_mem_kv, w_out, loss_target, m_pre_norm_g, m_post_norm_g, m_w_in_a, m_lam_re, m_lam_im, m_log_step, m_b_re, m_b_im, m_c_re, m_c_im, m_d_skip, m_w_glu, m_b_glu, m_kv_norm_g, m_w_kv, m_w_fgate, m_b_fgate, m_w_in_b, m_mem_norm_g, m_w_mem_kv, m_w_out, v_pre_norm_g, v_post_norm_g, v_w_in_a, v_lam_re, v_lam_im, v_log_step, v_b_re, v_b_im, v_c_re, v_c_im, v_d_skip, v_w_glu, v_b_glu, v_kv_norm_g, v_w_kv, v_w_fgate, v_b_fgate, v_w_in_b, v_mem_norm_g, v_w_mem_kv, v_w_out):
    a = dict(zip(INPUTS, (x, mem, pre_norm_g, post_norm_g, w_in_a, lam_re, lam_im, log_step, b_re, b_im, c_re, c_im, d_skip, w_glu, b_glu, kv_norm_g, w_kv, w_fgate, b_fgate, w_in_b, mem_norm_g, w_mem_kv, w_out, loss_target, m_pre_norm_g, m_post_norm_g, m_w_in_a, m_lam_re, m_lam_im, m_log_step, m_b_re, m_b_im, m_c_re, m_c_im, m_d_skip, m_w_glu, m_b_glu, m_kv_norm_g, m_w_kv, m_w_fgate, m_b_fgate, m_w_in_b, m_mem_norm_g, m_w_mem_kv, m_w_out, v_pre_norm_g, v_post_norm_g, v_w_in_a, v_lam_re, v_lam_im, v_log_step, v_b_re, v_b_im, v_c_re, v_c_im, v_d_skip, v_w_glu, v_b_glu, v_kv_norm_g, v_w_kv, v_w_fgate, v_b_fgate, v_w_in_b, v_mem_norm_g, v_w_mem_kv, v_w_out)))
    me = 4 * lax.axis_index("x") + 2 * lax.axis_index("y") + lax.axis_index("c")
    n_layers = w_out.shape[0]

    small_shapes = [a[n].shape for n in SMALL_SHARDED]
    small_rows = -(-sum(math.prod(s) for s in small_shapes) // (LANE * 8)) * 8
    shards = dict(in_a=w_in_a[0], mem_kv0=w_mem_kv[0], glu=w_glu[0], out0=w_out[0], kv=w_kv, in_b=w_in_b[0], mem_kv1=w_mem_kv[1], out1=w_out[1])
    operands = [_pack([a[n] for n in SMALL_SHARDED], small_rows)] + [shards[n].astype(BF16) for n in GATHER_ORDER[1:]]
    gathered = exchange_start("ag_weights", operands, gather=True)
    small = exchange_wait(gathered, 0, gathered['token'])
    small = [jnp.stack(parts) for parts in zip(*[_unpack(small[b], small_shapes) for b in range(N_DEV)])]
    p = {n: a[n] for n in REPLICATED}
    for n in ('lam_re', 'lam_im', 'log_step', 'b_re', 'b_im', 'c_re', 'c_im'):
        p[n] = p[n][0]
    p['d_skip'] = small[0].reshape(-1)
    p['b_glu'] = small[1].reshape(-1)
    p['w_fgate'] = small[2].reshape(-1, w_fgate.shape[1])
    p['pre_norm_g'] = p['pre_norm_g'] + gathered['token'][0, 0]

    full_shapes = [a[n].shape for n in REPLICATED] + [(1, N_DEV * d_skip.shape[1]), (1, N_DEV * b_glu.shape[1]), (N_DEV * w_fgate.shape[0], w_fgate.shape[1])]
    rows = _packed_rows(full_shapes)
    late_rows = 3 * pre_norm_g.shape[1] // LANE
    start_early = lambda g: exchange_start("ag_grads", [_pack([g[n] for n in REPLICATED + SMALL_SHARDED], rows)], gather=True)
    start_late = lambda g_late: exchange_start("ag_grads_late", [_pack(g_late, late_rows)], gather=True)

    loss_local, grad_x, pending, early, late = local_step(p, gathered, x[0], mem[0], loss_target[0], start_early, start_late)
    loss = lax.psum(loss_local, MESH_AXES)

    out = {}

    def update(tag, name, layer=None):
        pick = (lambda t: t) if layer is None else (lambda t: t[layer])
        parts = exchange_wait(pending[tag], 0, grad_x)
        two_d = lambda t: pick(t).reshape(parts.shape[1:])
        res = adamw_reduce("adamw_" + tag, parts, two_d(a[name]), two_d(a['m_' + name]), two_d(a['v_' + name]))
        return [r.reshape(pick(a[name]).shape) for r in res]

    per_layer = {n: [None] * n_layers for n in ('w_out', 'w_mem_kv')}
    per_layer['w_out'][1] = update("out1", 'w_out', 1)
    per_layer['w_mem_kv'][1] = update("mem_kv1", 'w_mem_kv', 1)
    out['w_in_b'] = [r[None] for r in update("in_b", 'w_in_b', 0)]
    out['w_kv'] = update("kv", 'w_kv')
    per_layer['w_out'][0] = update("out0", 'w_out', 0)
    per_layer['w_mem_kv'][0] = update("mem_kv0", 'w_mem_kv', 0)
    out['w_glu'] = [r[None] for r in update("glu", 'w_glu', 0)]
    for name in per_layer:
        out[name] = [jnp.stack(t) for t in zip(*per_layer[name])]

    zeros = [jnp.zeros(s, F32) for s in full_shapes[len(REPLICATED):]]
    packed = lambda pre: _pack([a[pre + n] for n in REPLICATED] + zeros, rows)
    res = [_unpack(r, full_shapes) for r in adamw_reduce("adamw_small", exchange_wait(early, 0, grad_x), packed(''), packed('m_'), packed('v_'))]
    for i, n in enumerate(REPLICATED):
        out[n] = [r[i] for r in res]
    g_full = res[0][len(REPLICATED):]
    g_shard = [lax.dynamic_slice_in_dim(g_full[0], me * d_skip.shape[1], d_skip.shape[1], 1),
               lax.dynamic_slice_in_dim(g_full[1], me * b_glu.shape[1], b_glu.shape[1], 1),
               lax.dynamic_slice_in_dim(g_full[2], me * w_fgate.shape[0], w_fgate.shape[0], 0)]
    packed = lambda pre: _pack([a[pre + n] for n in SMALL_SHARDED], small_rows)
    res = [_unpack(r, small_shapes) for r in adamw_reduce("adamw_small_sharded", _pack(g_shard, small_rows)[None], packed(''), packed('m_'), packed('v_'))]
    for i, n in enumerate(SMALL_SHARDED):
        out[n] = [r[i] for r in res]
    packed = lambda pre: _pack([a[pre + 'pre_norm_g'][0], a[pre + 'mem_norm_g'][0], a[pre + 'mem_norm_g'][1]], late_rows)
    res = [_unpack(r, [pre_norm_g.shape[1:]] * 3) for r in adamw_reduce("adamw_small_late", exchange_wait(late, 0, grad_x), packed(''), packed('m_'), packed('v_'))]
    out['pre_norm_g'] = [jnp.stack([r[0], o[1]]) for r, o in zip(res, out['pre_norm_g'])]
    out['mem_norm_g'] = [jnp.stack([r[1], r[2]]) for r in res]
    out['w_in_a'] = [r[None] for r in update("in_a", 'w_in_a', 0)]

    return (loss, grad_x[None], *[out[n][k] for k in range(4) for n in WEIGHTS])
```

```python
import functools
import math

import jax
import jax.numpy as jnp
from jax import lax
from jax.experimental import pallas as pl
from jax.experimental.pallas import tpu as pltpu

F32 = jnp.float32
BF16 = jnp.bfloat16
HP = lax.Precision.HIGHEST
MESH_AXES = ("x", "y", "c")
N_DEV = 8
V7X_VMEM_LIMIT = 56 * 1024 * 1024
LANE = 128

EPS = 1e-6
HEAD_DIM = 128
SSM_GROUP = 16
SSM_STATE = 64
SSM_T = 8
SSM_LO = 16
SSM_GB = 12
FOX_BQ = 256
GRAD_DTYPE = BF16
ACT_DTYPE = BF16
ADAM_LR = 0.001
ADAM_B1 = 0.9
ADAM_B2 = 0.999
ADAM_EPS = 1e-08
ADAM_WD = 0.01
ADAM_STEP = 10

WEIGHTS = ['pre_norm_g', 'post_norm_g', 'w_in_a', 'lam_re', 'lam_im', 'log_step', 'b_re', 'b_im', 'c_re', 'c_im', 'd_skip',
           'w_glu', 'b_glu', 'kv_norm_g', 'w_kv', 'w_fgate', 'b_fgate', 'w_in_b', 'mem_norm_g', 'w_mem_kv', 'w_out']
INPUTS = ['x', 'mem'] + WEIGHTS + ['loss_target'] + ['m_' + n for n in WEIGHTS] + ['v_' + n for n in WEIGHTS]
REPLICATED = ['pre_norm_g', 'post_norm_g', 'lam_re', 'lam_im', 'log_step', 'b_re', 'b_im', 'c_re', 'c_im', 'kv_norm_g',
              'b_fgate', 'mem_norm_g']
SMALL_SHARDED = ['d_skip', 'b_glu', 'w_fgate']


def _cparams(sem=None):
    return pltpu.CompilerParams(dimension_semantics=sem, vmem_limit_bytes=V7X_VMEM_LIMIT)


def _tile(n, cap):
    if n <= cap:
        return n
    best = None
    for t in range(LANE, cap + 1, LANE):
        if n % t == 0:
            best = t
    assert best is not None, (n, cap)
    return best


def _matmul(name, a, b, a_spec, b_spec, o_spec, out_shape, grid, dims, nk, after=None):
    def body(a_ref, b_ref, *rest):
        o_ref, acc_ref = rest[-2:]
        k = pl.program_id(2)
        part = lax.dot_general(a_ref[...].astype(BF16), b_ref[...].astype(BF16), dims, preferred_element_type=F32)

        @pl.when(k == 0)
        def _():
            acc_ref[...] = part

        @pl.when(k > 0)
        def _():
            acc_ref[...] += part

        @pl.when(k == nk - 1)
        def _():
            o_ref[...] = acc_ref[...].astype(o_ref.dtype)

    acc_shape = tuple(d for d in o_spec.block_shape if d is not None)
    extra = [] if after is None else [after]
    return pl.pallas_call(
        body, name=name, grid=grid, in_specs=[a_spec, b_spec] + [pl.BlockSpec(memory_space=pl.ANY)] * len(extra), out_specs=o_spec,
        out_shape=out_shape, scratch_shapes=[pltpu.VMEM(acc_shape, F32)],
        compiler_params=_cparams(("parallel", "parallel", "arbitrary")),
    )(a, b, *extra)


NN = (((1,), (0,)), ((), ()))
NT = (((1,), (1,)), ((), ()))
TN = (((0,), (0,)), ((), ()))


def _mm_nn(name, a, b, out_dtype=ACT_DTYPE):
    (m, k), (_, n) = a.shape, b.shape
    bm, bn, bk = _tile(m, 1024), _tile(n, 512), _tile(k, 2048)
    return _matmul(name, a, b, pl.BlockSpec((bm, bk), lambda i, j, kk: (i, kk)), pl.BlockSpec((bk, bn), lambda i, j, kk: (kk, j)),
                   pl.BlockSpec((bm, bn), lambda i, j, kk: (i, j)), jax.ShapeDtypeStruct((m, n), out_dtype), (m // bm, n // bn, k // bk), NN, k // bk)


def _mm_nt(name, a, b, after=None, out_dtype=ACT_DTYPE):
    (m, c), (n, _) = a.shape, b.shape
    bm, bn, bk = _tile(m, 1024), _tile(n, 512), _tile(c, 2048)
    return _matmul(name, a, b, pl.BlockSpec((bm, bk), lambda i, j, kk: (i, kk)), pl.BlockSpec((bn, bk), lambda i, j, kk: (j, kk)),
                   pl.BlockSpec((bm, bn), lambda i, j, kk: (i, j)), jax.ShapeDtypeStruct((m, n), out_dtype), (m // bm, n // bn, c // bk), NT, c // bk, after)


def _mm_tn(name, a, b, out_dtype=F32, after=None):
    (c, m), (_, n) = a.shape, b.shape
    bm, bn, bk = _tile(m, 1024), _tile(n, 512), _tile(c, 2048)
    return _matmul(name, a, b, pl.BlockSpec((bk, bm), lambda i, j, kk: (kk, i)), pl.BlockSpec((bk, bn), lambda i, j, kk: (kk, j)),
                   pl.BlockSpec((bm, bn), lambda i, j, kk: (i, j)), jax.ShapeDtypeStruct((m, n), out_dtype), (m // bm, n // bn, c // bk), TN, c // bk, after)


def _mm_nn_cb_split(name, a, bb, widths):
    (m, k), (nb, _, ns) = a.shape, bb.shape
    bm = _tile(m, 1024)
    his = [sum(widths[:p + 1]) // ns for p in range(len(widths))]
    los = [0] + his[:-1]
    assert his[-1] == nb and all(w % ns == 0 for w in widths), (widths, ns, nb)

    def body(a_ref, b_ref, *o_refs):
        j = pl.program_id(1)
        res = jnp.dot(a_ref[...].astype(BF16), b_ref[...].astype(BF16), preferred_element_type=F32)
        for o_ref, lo, hi in zip(o_refs, los, his):
            @pl.when((j >= lo) & (j < hi))
            def _(o_ref=o_ref):
                o_ref[...] = res.astype(o_ref.dtype)

    return pl.pallas_call(
        body, name=name, grid=(m // bm, nb),
        in_specs=[pl.BlockSpec((bm, k), lambda i, j: (i, 0)), pl.BlockSpec((None, k, ns), lambda i, j: (j, 0, 0))],
        out_specs=[pl.BlockSpec((bm, ns), lambda i, j, lo=lo, n=hi - lo: (i, jnp.clip(j - lo, 0, n - 1))) for lo, hi in zip(los, his)],
        out_shape=[jax.ShapeDtypeStruct((m, w), ACT_DTYPE) for w in widths],
        compiler_params=_cparams(("parallel", "arbitrary")),
    )(a, bb)


def _piece_blocks(dys, ns):
    his = [sum(d.shape[1] for d in dys[:p + 1]) // ns for p in range(len(dys))]
    return list(zip([0] + his[:-1], his))


def _mm_nt_cb(name, dys, bb, after=None):
    m, (nb, k, ns) = dys[0].shape[0], bb.shape
    bm, bn = _tile(m, 1024), _tile(k, 512)
    where = [(p, j - lo) for p, (lo, hi) in enumerate(_piece_blocks(dys, ns)) for j in range(lo, hi)]
    n_p = len(dys)

    def body(*refs):
        b_ref, o_ref = refs[n_p], refs[-1]
        acc = None
        for j, (p, l) in enumerate(where):
            part = lax.dot_general(refs[p][:, l * ns:(l + 1) * ns], b_ref[j], NT, preferred_element_type=F32)
            acc = part if acc is None else acc + part
        o_ref[...] = acc.astype(o_ref.dtype)

    extra = [] if after is None else [after]
    return pl.pallas_call(
        body, name=name, grid=(m // bm, k // bn),
        in_specs=[pl.BlockSpec((bm, d.shape[1]), lambda i, j: (i, 0)) for d in dys] + [pl.BlockSpec((nb, bn, ns), lambda i, j: (0, j, 0))]
        + [pl.BlockSpec(memory_space=pl.ANY)] * len(extra),
        out_specs=pl.BlockSpec((bm, bn), lambda i, j: (i, j)), out_shape=jax.ShapeDtypeStruct((m, k), ACT_DTYPE),
        compiler_params=_cparams(("parallel", "parallel")),
    )(*dys, bb, *extra)


def _mm_tn_cb(name, a, dys, ns, out_dtype=F32, after=None):
    (c, k), blocks = a.shape, _piece_blocks(dys, ns)
    nb, bm, n_p = blocks[-1][1], _tile(k, 1024), len(dys)

    def body(a_ref, *refs):
        j = pl.program_id(1)
        o_ref = refs[-1]
        for p_ref, (lo, hi) in zip(refs[:n_p], blocks):
            @pl.when((j >= lo) & (j < hi))
            def _(p_ref=p_ref):
                o_ref[...] = lax.dot_general(a_ref[...], p_ref[...], TN, preferred_element_type=F32).astype(o_ref.dtype)

    extra = [] if after is None else [after]
    return pl.pallas_call(
        body, name=name, grid=(k // bm, nb),
        in_specs=[pl.BlockSpec((c, bm), lambda i, j: (0, i))]
        + [pl.BlockSpec((c, ns), lambda i, j, lo=lo, n=hi - lo: (0, jnp.clip(j - lo, 0, n - 1))) for lo, hi in blocks]
        + [pl.BlockSpec(memory_space=pl.ANY)] * len(extra),
        out_specs=pl.BlockSpec((None, bm, ns), lambda i, j: (j, i, 0)), out_shape=jax.ShapeDtypeStruct((nb, k, ns), out_dtype),
        compiler_params=_cparams(("parallel", "arbitrary")),
    )(a, *dys, *extra)


def make_mm(name):
    @jax.custom_vjp
    def mm(a, b):
        return _mm_nn(name, a, b, F32)

    def fwd(a, b):
        return mm(a, b), (a, b)

    def bwd(res, dy):
        a, b = res
        return _mm_nt(name + "_da", dy, b, out_dtype=a.dtype), _mm_tn(name + "_dw", a, dy, b.dtype)

    mm.defvjp(fwd, bwd)
    return mm


def _rowop_specs(rows, params, bm, nl):
    row_specs = [pl.BlockSpec((bm, r.shape[1] // nl), lambda j, i: (i, j)) for r in rows]
    par_specs = [pl.BlockSpec((p.shape[0], p.shape[1] // nl), lambda j, i: (0, j)) for p in params]
    row_blk = [jax.ShapeDtypeStruct((bm, r.shape[1] // nl), r.dtype) for r in rows]
    par_blk = [jax.ShapeDtypeStruct((p.shape[0], p.shape[1] // nl), p.dtype) for p in params]
    return row_specs, par_specs, row_blk, par_blk


def make_rowop(f, name, bm, nl=1, out_dtypes=None):
    def loaded(refs):
        return [r[...].astype(F32) for r in refs]

    def fwd_call(rows, params):
        n_rows = rows[0].shape[0]
        b = min(bm, n_rows)
        row_specs, par_specs, row_blk, par_blk = _rowop_specs(rows, params, b, nl)
        out_blk = jax.eval_shape(f, *[jax.ShapeDtypeStruct(t.shape, F32) for t in row_blk + par_blk])
        dts = out_dtypes or [F32] * len(out_blk)
        nr, npar = len(rows), len(params)

        def body(*refs):
            for o_ref, o in zip(refs[nr + npar:], f(*loaded(refs[:nr + npar]))):
                o_ref[...] = o.astype(o_ref.dtype)

        return pl.pallas_call(
            body, name=name, grid=(nl, n_rows // b), in_specs=row_specs + par_specs,
            out_specs=[pl.BlockSpec(o.shape, lambda j, i: (i, j)) for o in out_blk],
            out_shape=[jax.ShapeDtypeStruct((n_rows, o.shape[1] * nl), dt) for o, dt in zip(out_blk, dts)],
            compiler_params=_cparams(("parallel", "parallel")),
        )(*rows, *params)

    def bwd_call(rows, params, cts):
        n_rows = rows[0].shape[0]
        b = min(bm, n_rows)
        row_specs, par_specs, row_blk, par_blk = _rowop_specs(rows, params, b, nl)
        ct_specs = [pl.BlockSpec((b, c.shape[1] // nl), lambda j, i: (i, j)) for c in cts]
        nr, npar, nct = len(rows), len(params), len(cts)

        def body(*refs):
            i = pl.program_id(1)
            _, vjp = jax.vjp(lambda *v: tuple(f(*v)), *loaded(refs[:nr + npar]))
            grads = vjp(tuple(loaded(refs[nr + npar:nr + npar + nct])))
            outs = refs[nr + npar + nct:]
            for o_ref, g in zip(outs[:nr], grads[:nr]):
                o_ref[...] = g.astype(o_ref.dtype)
            for o_ref, g in zip(outs[nr:], grads[nr:]):
                @pl.when(i == 0)
                def _(o_ref=o_ref, g=g):
                    o_ref[...] = g

                @pl.when(i > 0)
                def _(o_ref=o_ref, g=g):
                    o_ref[...] += g

        outs = pl.pallas_call(
            body, name=name + "_bwd", grid=(nl, n_rows // b), in_specs=row_specs + par_specs + ct_specs,
            out_specs=row_specs + par_specs,
            out_shape=[jax.ShapeDtypeStruct(a.shape, a.dtype) for a in rows] + [jax.ShapeDtypeStruct(a.shape, F32) for a in params],
            compiler_params=_cparams(("arbitrary", "arbitrary")),
        )(*rows, *params, *cts)
        return tuple(outs[:nr]), tuple(g.astype(a.dtype) for g, a in zip(outs[nr:], params))

    @jax.custom_vjp
    def op(rows, params):
        return tuple(fwd_call(rows, params))

    def fwd(rows, params):
        return op(rows, params), (rows, params)

    def bwd(res, cts):
        rows, params = res
        return bwd_call(rows, params, tuple(cts))

    op.defvjp(fwd, bwd)
    return op


def make_groupop(f, name, gb):
    def specs(arrs):
        return [pl.BlockSpec((gb,) + a.shape[1:], lambda g: (g, 0, 0)) for a in arrs]

    def fwd_call(arrs):
        g_n = arrs[0].shape[0]
        out_blk = jax.eval_shape(f, *[jax.ShapeDtypeStruct((gb,) + a.shape[1:], a.dtype) for a in arrs])[0]
        n = len(arrs)

        def body(*refs):
            for i, o in enumerate(f(*[r[...] for r in refs[:n]])):
                refs[n][i] = o.astype(refs[n].dtype)

        return pl.pallas_call(
            body, name=name, grid=(g_n // gb,), in_specs=specs(arrs),
            out_specs=pl.BlockSpec((gb,) + out_blk.shape, lambda g: (g, 0, 0)),
            out_shape=jax.ShapeDtypeStruct((g_n,) + out_blk.shape, ACT_DTYPE),
            compiler_params=_cparams(("parallel",)),
        )(*arrs)

    def bwd_call(arrs, ct):
        g_n = arrs[0].shape[0]
        n = len(arrs)

        def body(*refs):
            _, vjp = jax.vjp(lambda *v: tuple(f(*v)), *[r[...] for r in refs[:n]])
            for o_ref, g in zip(refs[n + 1:], vjp(tuple(refs[n][i].astype(F32) for i in range(gb)))):
                o_ref[...] = g.astype(o_ref.dtype)

        return pl.pallas_call(
            body, name=name + "_bwd", grid=(g_n // gb,), in_specs=specs(arrs) + specs([ct]), out_specs=specs(arrs),
            out_shape=[jax.ShapeDtypeStruct(a.shape, a.dtype) for a in arrs],
            compiler_params=_cparams(("parallel",)),
        )(*arrs, ct)

    @jax.custom_vjp
    def op(*arrs):
        return fwd_call(arrs)

    def fwd(*arrs):
        return op(*arrs), arrs

    def bwd(arrs, ct):
        return tuple(bwd_call(arrs, ct))

    op.defvjp(fwd, bwd)
    return op


@jax.custom_vjp
def _rms(x, g):
    return x * lax.rsqrt(jnp.mean(x * x, axis=-1, keepdims=True) + EPS) * g


def _rms_fwd(x, g):
    r = lax.rsqrt(jnp.mean(x * x, axis=-1, keepdims=True) + EPS)
    return x * r * g, (x, g, r)


def _rms_bwd(res, dy):
    x, g, r = res
    n, t = x * r, dy * g
    return r * (t - n * jnp.mean(n * t, axis=-1, keepdims=True)), jnp.sum(dy * n, axis=0, keepdims=True)


_rms.defvjp(_rms_fwd, _rms_bwd)


def _silu(z):
    return z * jax.nn.sigmoid(z)


def _log_sigmoid(x):
    return jnp.minimum(x, 0.0) - jnp.log(1.0 + jnp.exp(-jnp.abs(x)))


def f_norm(x, g):
    return (_rms(x, g),)


def f_gate_a1(y, u, d):
    return (jax.nn.gelu(y + d * u),)


def f_gate_a2(yg, t, z, b):
    return (yg * jax.nn.sigmoid(t + b) * _silu(z),)


def f_gate_b(att, z):
    return (att * _silu(z),)


def f_post_a(h, o, post_g, kv_g, pre_g):
    h1 = h + _rms(o, post_g)
    return h1, _rms(h1, kv_g), _rms(h1, pre_g)


def f_final(h, o, tgt, post_g):
    err = h + _rms(o, post_g) - tgt
    return (0.5 * jnp.mean(err * err, axis=-1, keepdims=True),)


def f_logf(gl, b):
    return (_log_sigmoid(gl + b),)


def _mxu(a, b, dims):
    return lax.dot_general(a.astype(BF16), b.astype(BF16), dims, preferred_element_type=F32)


@jax.custom_vjp
def _mxu_nn(a, b):
    return _mxu(a, b, NN)


def _mxu_nn_bwd(res, g):
    a, b = res
    return _mxu(g, b, NT).astype(a.dtype), _mxu(a, g, TN).astype(b.dtype)


_mxu_nn.defvjp(lambda a, b: (_mxu(a, b, NN), (a, b)), _mxu_nn_bwd)


@jax.custom_vjp
def _mxu_nt(a, b):
    return _mxu(a, b, NT)


def _mxu_nt_bwd(res, g):
    a, b = res
    return _mxu(g, b, NN).astype(a.dtype), _mxu(g, a, TN).astype(b.dtype)


_mxu_nt.defvjp(lambda a, b: (_mxu(a, b, NT), (a, b)), _mxu_nt_bwd)


def f_memattn(q, zm, km, vm):
    o = _mxu_nn(_softmax(_mxu_nt(q * (HEAD_DIM ** -0.5), km)), vm)
    return (o * _silu(zm),)


@jax.custom_vjp
def _softmax(s):
    e = jnp.exp(s - jnp.max(s, axis=-1, keepdims=True))
    return e * (1.0 / jnp.sum(e, axis=-1, keepdims=True))


def _softmax_fwd(s):
    p = _softmax(s)
    return p, p


def _softmax_bwd(p, dp):
    return (p * (dp - jnp.sum(dp * p, axis=-1, keepdims=True)),)


_softmax.defvjp(_softmax_fwd, _softmax_bwd)


def _fox_block(q, k, v, fr):
    bq = q.shape[0]
    s = _mxu_nt(q * (HEAD_DIM ** -0.5), k) - fr
    tri = lax.broadcasted_iota(jnp.int32, (bq, bq), 0) >= lax.broadcasted_iota(jnp.int32, (bq, bq), 1)
    diag = jnp.where(tri, s[:, -bq:], -1e30)
    s = diag if k.shape[0] == bq else jnp.concatenate([s[:, :-bq], diag], axis=1)
    return _mxu_nn(_softmax(s), v)


def _cmul(ar, ai, xr, xi):
    return ar * xr - ai * xi, ar * xi + ai * xr


def _hp_dot(a, b):
    return jnp.dot(a, b, precision=HP, preferred_element_type=F32)


_bf_dot = _mxu_nn


def f_s5(u, toep, win_r, win_i, wout_r, wout_i, coef):
    gb = u.shape[0]
    n_hi = u.shape[1] // SSM_LO
    sr = [_bf_dot(u[i], win_r[i]) for i in range(gb)]
    si = [_bf_dot(u[i], win_i[i]) for i in range(gb)]

    def stacked(parts, lo):
        return jnp.concatenate([a[lo * n_hi:(lo + 1) * n_hi] for a in parts], axis=0)

    def coef_rows(k):
        return jnp.concatenate([jnp.broadcast_to(coef[i, k:k + 1, :], (n_hi, coef.shape[2])) for i in range(gb)], axis=0)

    at_r, at_i = coef_rows(0), coef_rows(1)
    pr, pi = stacked(sr, 0), stacked(si, 0)
    for lo in range(1, SSM_LO):
        dr, di = _cmul(at_r, at_i, pr, pi)
        pr, pi = stacked(sr, lo) + dr, stacked(si, lo) + di
    n = 2 * gb * n_hi
    ri = lax.broadcasted_iota(jnp.int32, (n, n), 0)
    ci = lax.broadcasted_iota(jnp.int32, (n, n), 1)
    same = ri // n_hi == ci // n_hi

    def shifted(d, zr, zi):
        z = _hp_dot((same & (ri - ci == d)).astype(F32), jnp.concatenate([zr, zi], axis=0))
        return z[:n // 2], z[n // 2:]

    d, step = 1, 0
    while d < n_hi:
        dr, di = _cmul(coef_rows(2 + 2 * step), coef_rows(3 + 2 * step), *shifted(d, pr, pi))
        pr, pi = pr + dr, pi + di
        d, step = 2 * d, step + 1
    er, ei = shifted(1, pr, pi)
    xr, xi = [er], [ei]
    for lo in range(1, SSM_LO):
        dr, di = _cmul(at_r, at_i, xr[-1], xi[-1])
        xr.append(stacked(sr, lo - 1) + dr)
        xi.append(stacked(si, lo - 1) + di)

    def group_rows(parts, i):
        return jnp.concatenate([a[i * n_hi:(i + 1) * n_hi] for a in parts], axis=0)

    return tuple(_bf_dot(u[i], toep[i]) + _bf_dot(group_rows(xr, i), wout_r[i]) + _bf_dot(group_rows(xi, i), wout_i[i])
                 for i in range(gb))


def s5_operators(lam_re, lam_im, log_step, b_re, b_im, c_re, c_im, n_hi):
    t_n = SSM_T
    lr, li = lam_re, lam_im
    dt = jnp.exp(log_step)[:, None]
    mag = jnp.exp(lr * dt)
    ar, ai = mag * jnp.cos(li * dt), mag * jnp.sin(li * dt)
    den = lr * lr + li * li
    cr = ((ar - 1.0) * lr + ai * li) / den
    ci = (ai * lr - (ar - 1.0) * li) / den
    bbr = cr[..., None] * b_re - ci[..., None] * b_im
    bbi = cr[..., None] * b_im + ci[..., None] * b_re
    k = jnp.arange(t_n + 1, dtype=F32)[:, None, None]
    pm, ang = jnp.exp(k * (lr * dt)), k * (li * dt)
    pr, pi = pm * jnp.cos(ang), pm * jnp.sin(ang)
    abr = pr[..., None] * bbr - pi[..., None] * bbi
    abi = pr[..., None] * bbi + pi[..., None] * bbr
    kk = (jnp.einsum('ghp,kgpj->kghj', c_re, abr[:t_n], precision=HP)
          - jnp.einsum('ghp,kgpj->kghj', c_im, abi[:t_n], precision=HP))
    lag = jnp.arange(t_n)[None, :] - jnp.arange(t_n)[:, None]
    onehot = (lag[None] == jnp.arange(t_n)[:, None, None]).astype(F32)
    g_n, h_n = c_re.shape[0], c_re.shape[1]
    toep = jnp.einsum('kst,kghj->gsjth', onehot, kk, precision=HP).reshape(g_n, t_n * h_n, t_n * h_n)
    win_r = abr[:t_n][::-1].transpose(1, 0, 3, 2).reshape(g_n, t_n * h_n, -1)
    win_i = abi[:t_n][::-1].transpose(1, 0, 3, 2).reshape(g_n, t_n * h_n, -1)
    p1r, p1i = pr[1:, :, None, :], pi[1:, :, None, :]
    wout_r = (c_re[None] * p1r - c_im[None] * p1i).transpose(1, 3, 0, 2).reshape(g_n, -1, t_n * h_n)
    wout_i = (-(c_re[None] * p1i + c_im[None] * p1r)).transpose(1, 3, 0, 2).reshape(g_n, -1, t_n * h_n)
    rows = [pr[t_n], pi[t_n]]
    qr, qi = pr[t_n], pi[t_n]
    for _ in range(int(math.log2(SSM_LO))):
        qr, qi = qr * qr - qi * qi, 2.0 * qr * qi
    d = 1
    while d < n_hi:
        rows += [qr, qi]
        qr, qi = qr * qr - qi * qi, 2.0 * qr * qi
        d *= 2
    coef = jnp.stack(rows, axis=1)
    return toep, win_r, win_i, wout_r, wout_i, coef


def _cumsum_call(name, x, reverse):
    n_rows, w = x.shape
    bm = min(256, n_rows)
    nb = n_rows // bm

    def body(x_ref, o_ref, carry_ref):
        i = pl.program_id(0)

        @pl.when(i == 0)
        def _():
            carry_ref[...] = jnp.zeros_like(carry_ref)

        ri = lax.broadcasted_iota(jnp.int32, (bm, bm), 0)
        ci = lax.broadcasted_iota(jnp.int32, (bm, bm), 1)
        tri = ((ri <= ci) if reverse else (ri >= ci)).astype(F32)
        xb = x_ref[...]
        o_ref[...] = _hp_dot(tri, xb) + carry_ref[...]
        carry_ref[...] += jnp.sum(xb, axis=0, keepdims=True)

    idx = (lambda i: (nb - 1 - i, 0)) if reverse else (lambda i: (i, 0))
    return pl.pallas_call(
        body, name=name, grid=(nb,), in_specs=[pl.BlockSpec((bm, w), idx)], out_specs=pl.BlockSpec((bm, w), idx),
        out_shape=jax.ShapeDtypeStruct(x.shape, F32), scratch_shapes=[pltpu.VMEM((1, w), F32)],
        compiler_params=_cparams(("arbitrary",)),
    )(x)


def make_cumsum(name):
    @jax.custom_vjp
    def cs(x):
        return _cumsum_call(name, x, False)

    def fwd(x):
        return cs(x), None

    def bwd(_, dy):
        return (_cumsum_call(name + "_bwd", dy, True),)

    cs.defvjp(fwd, bwd)
    return cs


def _fox_specs(n_rows, bq):
    q_spec = pl.BlockSpec((bq, HEAD_DIM), lambda h, i: (i, h))
    kv_spec = pl.BlockSpec((n_rows, HEAD_DIM), lambda h, i: (0, h))
    fr_spec = pl.BlockSpec((None, 1, n_rows), lambda h, i: (h, 0, 0))
    return q_spec, kv_spec, fr_spec


def _fox_fwd_call(name, q, k, v, fr):
    n_rows, width = q.shape
    bq = min(FOX_BQ, n_rows)
    nq = n_rows // bq
    q_spec, kv_spec, fr_spec = _fox_specs(n_rows, bq)

    def body(q_ref, k_ref, v_ref, fr_ref, o_ref):
        i = pl.program_id(1)
        for p in range(nq):
            n_keys = (p + 1) * bq

            @pl.when(i == p)
            def _(n_keys=n_keys):
                o = _fox_block(q_ref[...].astype(F32), k_ref[:n_keys, :], v_ref[:n_keys, :], fr_ref[:, :n_keys])
                o_ref[...] = o.astype(o_ref.dtype)

    return pl.pallas_call(
        body, name=name, grid=(width // HEAD_DIM, nq), in_specs=[q_spec, kv_spec, kv_spec, fr_spec], out_specs=q_spec,
        out_shape=jax.ShapeDtypeStruct(q.shape, ACT_DTYPE), compiler_params=_cparams(("parallel", "parallel")),
    )(q, k, v, fr)


def _fox_bwd_call(name, q, k, v, fr, do):
    n_rows, width = q.shape
    bq = min(FOX_BQ, n_rows)
    nq = n_rows // bq
    q_spec, kv_spec, fr_spec = _fox_specs(n_rows, bq)

    def body(q_ref, k_ref, v_ref, fr_ref, do_ref, dq_ref, dk_ref, dv_ref, dfr_ref, dk_acc, dv_acc):
        i = pl.program_id(1)

        @pl.when(i == 0)
        def _():
            dk_acc[...] = jnp.zeros_like(dk_acc)
            dv_acc[...] = jnp.zeros_like(dv_acc)
            dfr_ref[...] = jnp.zeros_like(dfr_ref)

        for p in range(nq):
            n_keys = (p + 1) * bq

            @pl.when(i == p)
            def _(n_keys=n_keys):
                _, vjp = jax.vjp(_fox_block, q_ref[...].astype(F32), k_ref[:n_keys, :].astype(F32), v_ref[:n_keys, :].astype(F32),
                                 fr_ref[:, :n_keys])
                dq, dk, dv, dfr = vjp(do_ref[...].astype(F32))
                dq_ref[...] = dq.astype(dq_ref.dtype)
                dk_acc[:n_keys, :] += dk
                dv_acc[:n_keys, :] += dv
                dfr_ref[:, :n_keys] += dfr

        @pl.when(i == nq - 1)
        def _():
            dk_ref[...] = dk_acc[...].astype(dk_ref.dtype)
            dv_ref[...] = dv_acc[...].astype(dv_ref.dtype)

    return pl.pallas_call(
        body, name=name, grid=(width // HEAD_DIM, nq), in_specs=[q_spec, kv_spec, kv_spec, fr_spec, q_spec],
        out_specs=[q_spec, kv_spec, kv_spec, fr_spec], out_shape=[jax.ShapeDtypeStruct(a.shape, a.dtype) for a in (q, k, v, fr)],
        scratch_shapes=[pltpu.VMEM((n_rows, HEAD_DIM), F32), pltpu.VMEM((n_rows, HEAD_DIM), F32)],
        compiler_params=_cparams(("parallel", "arbitrary")),
    )(q, k, v, fr, do)


def make_fox(name):
    @jax.custom_vjp
    def fox(q, k, v, fr):
        return _fox_fwd_call(name, q, k, v, fr)

    def fwd(q, k, v, fr):
        return fox(q, k, v, fr), (q, k, v, fr)

    def bwd(res, do):
        return tuple(_fox_bwd_call(name + "_bwd", *res, do))

    fox.defvjp(fwd, bwd)
    return fox


def _relations(x, y, c):
    out = []
    for rel in range(1, N_DEV):
        px = 1 - x if rel & 4 else x
        py = 1 - y if rel & 2 else y
        pc = 1 - c if rel & 1 else c
        out.append((rel, (px, py, pc), 4 * px + 2 * py + pc))
    return out


_HBM_SPEC = pl.BlockSpec(memory_space=pltpu.HBM)
_SEM_SPEC = pl.BlockSpec(memory_space=pltpu.SEMAPHORE)
_DATAFLOW = pltpu.SideEffectType.DATAFLOW_SIDE_EFFECTING


def exchange_start(name, srcs, gather):
    n = len(srcs)
    me = 4 * lax.axis_index("x") + 2 * lax.axis_index("y") + lax.axis_index("c")
    lands = []
    for s in srcs:
        own = s if gather else lax.dynamic_index_in_dim(s, me, 0, keepdims=False)
        land = lax.empty((N_DEV,) + own.shape, s.dtype)
        lands.append(lax.dynamic_update_index_in_dim(land, own, me, 0))

    def body(*refs):
        src_refs, land_refs = refs[:n], refs[n:2 * n]
        send_sems, recv_sems, token = refs[2 * n], refs[2 * n + 1], refs[-1]
        x, y, c = lax.axis_index("x"), lax.axis_index("y"), lax.axis_index("c")
        mine = 4 * x + 2 * y + c
        for k in range(n):
            for rel, peer, peer_blk in _relations(x, y, c):
                pltpu.make_async_remote_copy(
                    src_ref=src_refs[k] if gather else src_refs[k].at[peer_blk], dst_ref=land_refs[k].at[mine],
                    send_sem=send_sems.at[7 * k + rel - 1], recv_sem=recv_sems.at[7 * k + rel - 1],
                    device_id=peer, device_id_type=pl.DeviceIdType.MESH).start()
        token[...] = jnp.zeros_like(token)

    sem = pltpu.SemaphoreType.DMA((7 * n,))
    hbm = lambda t: pltpu.HBM(t.shape, t.dtype)
    outs = pl.pallas_call(
        body, name=name,
        out_shape=(sem, sem, *[hbm(s) for s in srcs], *[hbm(l) for l in lands], jax.ShapeDtypeStruct((8, LANE), F32)),
        in_specs=[_HBM_SPEC] * (2 * n),
        out_specs=(_SEM_SPEC, _SEM_SPEC, *[_HBM_SPEC] * (2 * n), pl.BlockSpec(memory_space=pltpu.VMEM)),
        input_output_aliases={i: 2 + i for i in range(2 * n)},
        compiler_params=pltpu.CompilerParams(has_side_effects=_DATAFLOW),
    )(*[pltpu.with_memory_space_constraint(t, pltpu.HBM) for t in list(srcs) + lands])
    return dict(name=name, gather=gather, send=outs[0], recv=outs[1], srcs=list(outs[2:2 + n]), lands=list(outs[2 + n:2 + 2 * n]), token=outs[-1])


def exchange_wait(handle, k, after):
    gather = handle['gather']

    def body(src_ref, land_ref, send_sems, recv_sems, after_ref, src_out, land_out):
        x, y, c = lax.axis_index("x"), lax.axis_index("y"), lax.axis_index("c")
        for rel, peer, peer_blk in _relations(x, y, c):
            copy = pltpu.make_async_remote_copy(
                src_ref=src_ref if gather else src_ref.at[peer_blk], dst_ref=land_ref.at[peer_blk],
                send_sem=send_sems.at[7 * k + rel - 1], recv_sem=recv_sems.at[7 * k + rel - 1],
                device_id=peer, device_id_type=pl.DeviceIdType.MESH)
            copy.wait_send()
            copy.wait_recv()

    src, land = handle['srcs'][k], handle['lands'][k]
    return pl.pallas_call(
        body, name=f"{handle['name']}_wait{k}", out_shape=(pltpu.HBM(src.shape, src.dtype), pltpu.HBM(land.shape, land.dtype)),
        in_specs=[_HBM_SPEC, _HBM_SPEC, _SEM_SPEC, _SEM_SPEC, pl.BlockSpec(memory_space=pl.ANY)], out_specs=(_HBM_SPEC, _HBM_SPEC),
        input_output_aliases={0: 0, 1: 1}, compiler_params=pltpu.CompilerParams(has_side_effects=_DATAFLOW),
    )(src, land, handle['send'], handle['recv'], after)[1]


def adamw_reduce(name, partials, w, m, v):
    n_part, n_rows, n_cols = partials.shape
    br = n_rows
    for cand in (512, 256, 128, 64, 32, 16, 8):
        if n_rows % cand == 0 and n_part * cand * n_cols * partials.dtype.itemsize <= (4 << 20):
            br = cand
            break

    def body(p_ref, w_ref, m_ref, v_ref, g_ref, d_ref, nm_ref, nv_ref):
        g = p_ref[0].astype(F32)
        for s in range(1, n_part):
            g = g + p_ref[s].astype(F32)
        m_new = ADAM_B1 * m_ref[...] + (1.0 - ADAM_B1) * g
        v_new = ADAM_B2 * v_ref[...] + (1.0 - ADAM_B2) * jnp.square(g)
        m_hat = m_new / (1.0 - ADAM_B1 ** ADAM_STEP)
        v_hat = v_new / (1.0 - ADAM_B2 ** ADAM_STEP)
        g_ref[...] = g
        d_ref[...] = -ADAM_LR * (m_hat / (jnp.sqrt(v_hat) + ADAM_EPS) + ADAM_WD * w_ref[...])
        nm_ref[...] = m_new
        nv_ref[...] = v_new

    spec = pl.BlockSpec((br, n_cols), lambda i: (i, 0))
    return pl.pallas_call(
        body, name=name, grid=(n_rows // br,), in_specs=[pl.BlockSpec((n_part, br, n_cols), lambda i: (0, i, 0)), spec, spec, spec],
        out_specs=[spec] * 4, out_shape=[jax.ShapeDtypeStruct((n_rows, n_cols), F32)] * 4, compiler_params=_cparams(("parallel",)),
    )(partials, w, m, v)


def _pack(arrays, n_rows):
    flat = jnp.concatenate([a.reshape(-1) for a in arrays])
    return jnp.pad(flat, (0, n_rows * LANE - flat.shape[0])).reshape(n_rows, LANE)


def _unpack(packed, shapes):
    flat, out, off = packed.reshape(-1), [], 0
    for s in shapes:
        n = math.prod(s)
        out.append(flat[off:off + n].reshape(s))
        off += n
    return out


def _packed_rows(shapes):
    n = sum(math.prod(s) for s in shapes)
    return -(-n // (LANE * 512)) * 512


GATHER_ORDER = ['small', 'in_a', 'mem_kv0', 'glu', 'out0', 'kv', 'in_b', 'mem_kv1', 'out1']


BLOCKED = ('in_a', 'kv', 'in_b')


def local_step(p, gathered, x, mem, tgt, start_early, start_late):
    n_rows, d_model = x.shape
    main_w = 3 * d_model // 4
    mem_w = d_model - main_w
    n_groups = main_w // SSM_GROUP
    n_heads = main_w // HEAD_DIM
    n_hi = n_rows // (SSM_T * SSM_LO)
    row = lambda a: a.reshape(1, -1)
    tape, pending = {}, {}

    def mm_fwd(name, act, after=None, widths=None):
        w = exchange_wait(gathered, GATHER_ORDER.index(name), act if after is None else after)
        if name not in BLOCKED:
            w = w.reshape(-1, w.shape[2])
        tape[name] = (act, w)
        return _mm_nn_cb_split(name, act, w, widths) if name in BLOCKED else _mm_nn(name, act, w)

    proj_widths = [main_w, main_w, mem_w, mem_w]

    def mm_dw(name, dy, after=None):
        act, w = tape[name]
        if name in BLOCKED:
            dw = _mm_tn_cb(name + "_dw", act, dy, w.shape[2], GRAD_DTYPE, after)
        else:
            dw = _mm_tn(name + "_dw", act, dy, GRAD_DTYPE, after)
            dw = dw.reshape(N_DEV, dw.shape[0] // N_DEV, dw.shape[1])
        pending[name] = exchange_start("rs_" + name, [dw], gather=False)

    def mm_da(name, dy, after=None):
        return (_mm_nt_cb if name in BLOCKED else _mm_nt)(name + "_da", dy, tape[name][1], after)

    def mm_bwd(name, dy):
        mm_dw(name, dy)
        return mm_da(name, dy, pending[name]['token'])

    def mem_attn(i, qm, zm, kvm):
        memo, = make_rowop(f_memattn, f"mem_attn{i}", 512, mem_w // HEAD_DIM, [ACT_DTYPE])((qm, zm), (kvm[:, :mem_w], kvm[:, mem_w:]))
        return memo

    def seg_norms(x_, pre_g0, mem_g0, mem_g1):
        hn, = make_rowop(f_norm, "pre_norm0", 256, out_dtypes=[ACT_DTYPE])((x_,), (row(pre_g0),))
        memn0, = make_rowop(f_norm, "mem_norm0", 256, out_dtypes=[ACT_DTYPE])((mem,), (row(mem_g0),))
        memn1, = make_rowop(f_norm, "mem_norm1", 256, out_dtypes=[ACT_DTYPE])((mem,), (row(mem_g1),))
        return hn, memn0, memn1

    def seg_a1(u, qm, zm, kvm, d_skip, *ops):
        ug = u.astype(BF16).reshape(n_hi, SSM_LO, SSM_T, n_groups, SSM_GROUP).transpose(3, 1, 0, 2, 4).reshape(n_groups, n_hi * SSM_LO, SSM_T * SSM_GROUP)
        yg = make_groupop(f_s5, "s5", SSM_GB)(ug, *ops)
        y = yg.reshape(n_groups, SSM_LO, n_hi, SSM_T, SSM_GROUP).transpose(2, 1, 3, 0, 4).reshape(n_rows, main_w)
        ygelu, = make_rowop(f_gate_a1, "gate_a1", 256, out_dtypes=[ACT_DTYPE])((y, u), (row(d_skip),))
        return ygelu, mem_attn(0, qm, zm, kvm)

    def seg_a2(ygelu, t, z, memo, b_glu):
        main, = make_rowop(f_gate_a2, "gate_a2", 256, out_dtypes=[ACT_DTYPE])((ygelu, t, z), (row(b_glu),))
        return jnp.concatenate([main, memo], axis=1)

    def seg_post_a(x_, o, post_g0, kv_g, pre_g1):
        return make_rowop(f_post_a, "post_a", 128, out_dtypes=[F32, ACT_DTYPE, ACT_DTYPE])((x_, o), (row(post_g0), row(kv_g), row(pre_g1)))

    def seg_b(q, z, qm, zm, k, v, kv_in, kvm, w_fgate, b_fgate):
        w_fg = jnp.pad(w_fgate, ((0, 0), (0, LANE - n_heads)))
        b_fg = jnp.pad(b_fgate, (0, LANE - n_heads)).reshape(1, LANE)
        logf, = make_rowop(f_logf, "logf", 512)((make_mm("fgate")(kv_in, w_fg),), (b_fg,))
        fcum = make_cumsum("fcum")(logf)[:, :n_heads].T
        att = make_fox("fox")(q, k, v, fcum[:, None, :])
        main, = make_rowop(f_gate_b, "gate_b", 256, out_dtypes=[ACT_DTYPE])((att, z), ())
        return jnp.concatenate([main, mem_attn(1, qm, zm, kvm)], axis=1)

    def seg_final(h1, o, post_g1):
        rowloss, = make_rowop(f_final, "final", 128)((h1, o, tgt), (row(post_g1),))
        return jnp.sum(rowloss)

    (hn0, memn0, memn1), vjp_norms = jax.vjp(seg_norms, x, p['pre_norm_g'][0], p['mem_norm_g'][0], p['mem_norm_g'][1])
    s5_names = ('lam_re', 'lam_im', 'log_step', 'b_re', 'b_im', 'c_re', 'c_im')
    ops, vjp_ops = jax.vjp(lambda *a: s5_operators(*a, n_hi), *[p[n] for n in s5_names])
    u_a, z_a, qm_a, zm_a = mm_fwd("in_a", hn0, ops[0], proj_widths)
    kvm0 = mm_fwd("mem_kv0", memn0, u_a)
    (ygelu, memo0), vjp_a1 = jax.vjp(seg_a1, u_a, qm_a, zm_a, kvm0, p['d_skip'], *ops)
    t = mm_fwd("glu", ygelu)
    cat0, vjp_a2 = jax.vjp(seg_a2, ygelu, t, z_a, memo0, p['b_glu'])
    o0 = mm_fwd("out0", cat0)
    (h1, kv_in, hn1), vjp_post_a = jax.vjp(seg_post_a, x, o0, p['post_norm_g'][0], p['kv_norm_g'], p['pre_norm_g'][1])
    k_sh, v_sh = mm_fwd("kv", kv_in, widths=[main_w, main_w])
    proj_b = mm_fwd("in_b", hn1, widths=proj_widths)
    kvm1 = mm_fwd("mem_kv1", memn1, proj_b[0])
    cat1, vjp_b = jax.vjp(seg_b, *proj_b, k_sh, v_sh, kv_in, kvm1, p['w_fgate'], p['b_fgate'])
    o1 = mm_fwd("out1", cat1)
    loss, vjp_final = jax.vjp(seg_final, h1, o1, p['post_norm_g'][1])

    g = {}
    d_h1, d_o1, g_post_g1 = vjp_final(jnp.ones((), F32))
    *d_proj_b, d_k, d_v, d_kv_in, d_kvm1, g['w_fgate'], g['b_fgate'] = vjp_b(mm_bwd("out1", d_o1))
    d_memn1 = mm_bwd("mem_kv1", d_kvm1)
    d_hn1 = mm_bwd("in_b", d_proj_b)
    d_kv_in = d_kv_in + mm_bwd("kv", [d_k, d_v])
    d_x, d_o0, g_post_g0, g['kv_norm_g'], g_pre_g1 = vjp_post_a((d_h1, d_kv_in, d_hn1))
    d_ygelu, d_t, d_z, d_memo0, g['b_glu'] = vjp_a2(mm_bwd("out0", d_o0))
    d_ygelu = d_ygelu + mm_bwd("glu", d_t)
    d_u, d_qm, d_zm, d_kvm0, g['d_skip'], *d_ops = vjp_a1((d_ygelu, d_memo0))
    d_proj_a = [d_u, d_z, d_qm, d_zm]
    g.update(zip(s5_names, vjp_ops(tuple(d_ops))))
    g['pre_norm_g'] = jnp.stack([jnp.zeros_like(g_pre_g1), g_pre_g1])
    g['post_norm_g'] = jnp.stack([g_post_g0, g_post_g1])
    g['mem_norm_g'] = jnp.zeros_like(p['mem_norm_g'])
    early = start_early(g)
    d_memn0 = mm_bwd("mem_kv0", d_kvm0)
    d_x2, *g_late = vjp_norms((mm_da("in_a", d_proj_a, early['token']), d_memn0, d_memn1))
    late = start_late(g_late)
    mm_dw("in_a", d_proj_a, late['token'])
    return loss, d_x + d_x2, pending, early, late


def kernel(x, mem, pre_norm_g, post_norm_g, w_in_a, lam_re, lam_im, log_step, b_re, b_im, c_re, c_im, d_skip, w_glu, b_glu, kv_norm_g, w_kv, w_fgate, b_fgate, w_in_b, mem_norm_g, w_mem_kv, w_out, loss_target, m_pre_norm_g, m_post_norm_g, m_w_in_a, m_lam_re, m_lam_im, m_log_step, m_b_re, m_b_im, m_c_re, m_c_im, m_d_skip, m_w_glu, m_b_glu, m_kv_norm_g, m_w_kv, m_w_fgate, m_b_fgate, m_w_in_b, m_mem_norm_g, m_w_mem_kv, m_w_out, v_pre_norm_g, v_post_norm_g, v_w_in_a, v_lam_re, v_lam_im, v_log_step, v_b_re, v_b_im, v_c_re, v_c_im, v_d_skip, v_w_glu, v_b_glu, v_kv_norm_g, v_w_kv, v_w_fgate, v_b_fgate, v_w_in_b, v_mem_norm_g, v_w_mem_kv, v_w_out):
    a = dict(zip(INPUTS, (x, mem, pre_norm_g, post_norm_g, w_in_a, lam_re, lam_im, log_step, b_re, b_im, c_re, c_im, d_skip, w_glu, b_glu, kv_norm_g, w_kv, w_fgate, b_fgate, w_in_b, mem_norm_g, w_mem_kv, w_out, loss_target, m_pre_norm_g, m_post_norm_g, m_w_in_a, m_lam_re, m_lam_im, m_log_step, m_b_re, m_b_im, m_c_re, m_c_im, m_d_skip, m_w_glu, m_b_glu, m_kv_norm_g, m_w_kv, m_w_fgate, m_b_fgate, m_w_in_b, m_mem_norm_g, m_w_mem_kv, m_w_out, v_pre_norm_g, v_post_norm_g, v_w_in_a, v_lam_re, v_lam_im, v_log_step, v_b_re, v_b_im, v_c_re, v_c_im, v_d_skip, v_w_glu, v_b_glu, v_kv_norm_g, v_w_kv, v_w_fgate, v_b_fgate, v_w_in_b, v_mem_norm_g, v_w_mem_kv, v_w_out)))
    me = 4 * lax.axis_index("x") + 2 * lax.axis_index("y") + lax.axis_index("c")
    n_layers = w_out.shape[0]

    small_shapes = [a[n].shape for n in SMALL_SHARDED]
    small_rows = -(-sum(math.prod(s) for s in small_shapes) // (LANE * 8)) * 8
    shards = dict(in_a=w_in_a[0], mem_kv0=w_mem_kv[0], glu=w_glu[0], out0=w_out[0], kv=w_kv, in_b=w_in_b[0], mem_kv1=w_mem_kv[1], out1=w_out[1])
    operands = [_pack([a[n] for n in SMALL_SHARDED], small_rows)] + [shards[n].astype(BF16) for n in GATHER_ORDER[1:]]
    gathered = exchange_start("ag_weights", operands, gather=True)
    small = exchange_wait(gathered, 0, gathered['token'])
    small = [jnp.stack(parts) for parts in zip(*[_unpack(small[b], small_shapes) for b in range(N_DEV)])]
    p = {n: a[n] for n in REPLICATED}
    for n in ('lam_re', 'lam_im', 'log_step', 'b_re', 'b_im', 'c_re', 'c_im'):
        p[n] = p[n][0]
    p['d_skip'] = small[0].reshape(-1)
    p['b_glu'] = small[1].reshape(-1)
    p['w_fgate'] = small[2].reshape(-1, w_fgate.shape[1])
    p['pre_norm_g'] = p['pre_norm_g'] + gathered['token'][0, 0]

    full_shapes = [a[n].shape for n in REPLICATED] + [(1, N_DEV * d_skip.shape[1]), (1, N_DEV * b_glu.shape[1]), (N_DEV * w_fgate.shape[0], w_fgate.shape[1])]
    rows = _packed_rows(full_shapes)
    late_rows = 3 * pre_norm_g.shape[1] // LANE
    start_early = lambda g: exchange_start("ag_grads", [_pack([g[n] for n in REPLICATED + SMALL_SHARDED], rows)], gather=True)
    start_late = lambda g_late: exchange_start("ag_grads_late", [_pack(g_late, late_rows)], gather=True)

    loss_local, grad_x, pending, early, late = local_step(p, gathered, x[0], mem[0], loss_target[0], start_early, start_late)
    loss = lax.psum(loss_local, MESH_AXES)

    out = {}

    def update(tag, name, layer=None):
        pick = (lambda t: t) if layer is None else (lambda t: t[layer])
        parts = exchange_wait(pending[tag], 0, grad_x)
        two_d = lambda t: pick(t).reshape(parts.shape[1:])
        res = adamw_reduce("adamw_" + tag, parts, two_d(a[name]), two_d(a['m_' + name]), two_d(a['v_' + name]))
        return [r.reshape(pick(a[name]).shape) for r in res]

    per_layer = {n: [None] * n_layers for n in ('w_out', 'w_mem_kv')}
    per_layer['w_out'][1] = update("out1", 'w_out', 1)
    per_layer['w_mem_kv'][1] = update("mem_kv1", 'w_mem_kv', 1)
    out['w_in_b'] = [r[None] for r in update("in_b", 'w_in_b', 0)]
    out['w_kv'] = update("kv", 'w_kv')
    per_layer['w_out'][0] = update("out0", 'w_out', 0)
    per_layer['w_mem_kv'][0] = update("mem_kv0", 'w_mem_kv', 0)
    out['w_glu'] = [r[None] for r in update("glu", 'w_glu', 0)]
    for name in per_layer:
        out[name] = [jnp.stack(t) for t in zip(*per_layer[name])]

    zeros = [jnp.zeros(s, F32) for s in full_shapes[len(REPLICATED):]]
    packed = lambda pre: _pack([a[pre + n] for n in REPLICATED] + zeros, rows)
    res = [_unpack(r, full_shapes) for r in adamw_reduce("adamw_small", exchange_wait(early, 0, grad_x), packed(''), packed('m_'), packed('v_'))]
    for i, n in enumerate(REPLICATED):
        out[n] = [r[i] for r in res]
    g_full = res[0][len(REPLICATED):]
    g_shard = [lax.dynamic_slice_in_dim(g_full[0], me * d_skip.shape[1], d_skip.shape[1], 1),
               lax.dynamic_slice_in_dim(g_full[1], me * b_glu.shape[1], b_glu.shape[1], 1),
               lax.dynamic_slice_in_dim(g_full[2], me * w_fgate.shape[0], w_fgate.shape[0], 0)]
    packed = lambda pre: _pack([a[pre + n] for n in SMALL_SHARDED], small_rows)
    res = [_unpack(r, small_shapes) for r in adamw_reduce("adamw_small_sharded", _pack(g_shard, small_rows)[None], packed(''), packed('m_'), packed('v_'))]
    for i, n in enumerate(SMALL_SHARDED):
        out[n] = [r[i] for r in res]
    packed = lambda pre: _pack([a[pre + 'pre_norm_g'][0], a[pre + 'mem_norm_g'][0], a[pre + 'mem_norm_g'][1]], late_rows)
    res = [_unpack(r, [pre_norm_g.shape[1:]] * 3) for r in adamw_reduce("adamw_small_late", exchange_wait(late, 0, grad_x), packed(''), packed('m_'), packed('v_'))]
    out['pre_norm_g'] = [jnp.stack([r[0], o[1]]) for r, o in zip(res, out['pre_norm_g'])]
    out['mem_norm_g'] = [jnp.stack([r[1], r[2]]) for r in res]
    out['w_in_a'] = [r[None] for r in update("in_a", 'w_in_a', 0)]

    return (loss, grad_x[None], *[out[n][k] for k in range(4) for n in WEIGHTS])
```
